```python
import jax, jax.numpy as jnp
from jax import lax
import numpy as np

D_MODEL = 1024
BATCH = 2
SEQ = 8192
DEPTH = 2

N_MIXERS = 2
N_RWKV_LAYERS = (DEPTH + 1) // 2
N_SB_LAYERS = DEPTH // 2
HEAD_DIM = 64
N_HEADS = D_MODEL // HEAD_DIM
D_DECAY_LORA = 64
D_AAA_LORA = 64
D_GATE_LORA = 128
N_TOKEN_SHIFT_MIX = 6
LNX_EPS = 64e-5
Q_BLOCK = 128
N_EXPERTS = 64
TOP_K = 8
D_EXPERT = D_MODEL // 4
D_SHARED = D_MODEL // 4
ROUTED_SCALE = 2.5
EXPERT_BLOCK = 128
RMS_EPS = 1e-6
N_ADA = 6

kernel_name = 'hybrid_rwkv7_stickbreak_moe_block'


def rmsnorm(x, g):
    xf = x.astype(jnp.float32)
    y = xf * lax.rsqrt(jnp.mean(xf * xf, axis=-1, keepdims=True) + RMS_EPS)
    return (y * g.astype(jnp.float32)).astype(x.dtype)


def _wkv7_scan(r, decay, k, v, a_vec, b_vec):
    B, S, H, N = r.shape

    def step(state, inp):
        r_t, w_t, k_t, v_t, a_t, b_t = inp
        sa = jnp.einsum('bhij,bhj->bhi', state, a_t)
        state = (state * w_t[:, :, None, :]
                 + sa[..., None] * b_t[:, :, None, :]
                 + v_t[..., None] * k_t[:, :, None, :])
        return state, jnp.einsum('bhij,bhj->bhi', state, r_t)

    xs = tuple(jnp.moveaxis(t, 1, 0) for t in (r, decay, k, v, a_vec, b_vec))
    s0 = jnp.zeros((B, H, N, N), jnp.float32)
    _, out = lax.scan(step, s0, xs)
    return jnp.moveaxis(out, 0, 1)


def rwkv7_time_mix(h, mu, w_rkv, w_w1, w_w2, w0, w_a1, w_a2, a0, w_g1, w_g2,
                   k_k, k_a, r_k, lnx_g, lnx_b, w_o):
    B, S, D = h.shape
    xx = jnp.pad(h, ((0, 0), (1, 0), (0, 0)))[:, :-1] - h
    xs = h[None] + xx[None] * mu[:, None, None, :]
    rkv = jnp.einsum('nbsd,nde->nbse', xs[:3], w_rkv)
    r, k, v = rkv[0], rkv[1], rkv[2]
    xw, xa, xg = xs[3], xs[4], xs[5]
    w_log = -jax.nn.softplus(-(w0 + jnp.tanh(xw @ w_w1) @ w_w2)) - 0.5
    decay = jnp.exp(-jnp.exp(w_log.astype(jnp.float32)))
    a = jax.nn.sigmoid((a0 + (xa @ w_a1) @ w_a2).astype(jnp.float32))
    g = jax.nn.sigmoid(xg @ w_g1) @ w_g2

    hs = lambda t: t.astype(jnp.float32).reshape(B, S, N_HEADS, HEAD_DIM)
    kk = hs(k * k_k)
    kk = kk / jnp.maximum(jnp.sqrt(jnp.sum(kk * kk, axis=-1, keepdims=True)), 1e-12)
    k_f = k.astype(jnp.float32) * (1.0 + (a - 1.0) * k_a.astype(jnp.float32))
    a_h = hs(a)
    r_h, k_h, v_h, w_h = hs(r), hs(k_f), hs(v), hs(decay)

    o = _wkv7_scan(r_h, w_h, k_h, v_h, -kk, kk * a_h)
    mean = jnp.mean(o, axis=-1, keepdims=True)
    var = jnp.mean(jnp.square(o - mean), axis=-1, keepdims=True)
    o = ((o - mean) * lax.rsqrt(var + LNX_EPS)).reshape(B, S, D)
    o = o * lnx_g.astype(jnp.float32) + lnx_b.astype(jnp.float32)
    bonus = jnp.sum(r_h * k_h * r_k.astype(jnp.float32), axis=-1, keepdims=True) * v_h
    o = o + bonus.reshape(B, S, D)
    return (o.astype(h.dtype) * g) @ w_o


def stick_breaking_attention(h, w_qkv, w_o):
    B, S, D = h.shape
    qkv = (h @ w_qkv).reshape(B, S, 3, N_HEADS, HEAD_DIM)
    q = jnp.transpose(qkv[:, :, 0], (0, 2, 1, 3))
    k = jnp.transpose(qkv[:, :, 1], (0, 2, 1, 3))
    v = jnp.transpose(qkv[:, :, 2], (0, 2, 1, 3))
    n_blk = S // Q_BLOCK
    q_blocks = jnp.moveaxis(q.reshape(B, N_HEADS, n_blk, Q_BLOCK, HEAD_DIM), 2, 0)
    key_pos = jnp.arange(S)
    scale = 1.0 / float(np.sqrt(HEAD_DIM))

    def query_block(args):
        q_i, blk = args
        z = jnp.einsum('bhqd,bhkd->bhqk', q_i, k).astype(jnp.float32) * scale
        q_pos = blk * Q_BLOCK + jnp.arange(Q_BLOCK)
        causal = key_pos[None, :] < q_pos[:, None]
        log_beta = jax.nn.log_sigmoid(z)
        log_1m_beta = jnp.where(causal, jax.nn.log_sigmoid(-z), 0.0)
        tail = lax.cumsum(log_1m_beta, axis=3, reverse=True) - log_1m_beta
        weights = jnp.where(causal, jnp.exp(log_beta + tail), 0.0)
        return jnp.einsum('bhqk,bhkd->bhqd', weights.astype(v.dtype), v)

    o = lax.map(query_block, (q_blocks, jnp.arange(n_blk)))
    o = jnp.transpose(o, (1, 0, 3, 2, 4)).reshape(B, S, D)
    return o @ w_o


def moe_ffn(h, w_router, router_bias, w_gate, w_up, w_down, ws_gate, ws_up, ws_down):
    B, S, D = h.shape
    T = B * S
    xf = h.reshape(T, D)
    scores = jax.nn.sigmoid((xf @ w_router).astype(jnp.float32))
    _, idx = lax.top_k(scores + router_bias.astype(jnp.float32), TOP_K)
    sel = jnp.take_along_axis(scores, idx, axis=1)
    gates = ROUTED_SCALE * sel / jnp.sum(sel, axis=1, keepdims=True)

    A = T * TOP_K
    e_flat = idx.reshape(A).astype(jnp.int32)
    tok_flat = jnp.arange(A, dtype=jnp.int32) // TOP_K
    w_flat = gates.reshape(A)
    order = jnp.argsort(e_flat)
    e_s, tok_s, w_s = e_flat[order], tok_flat[order], w_flat[order]
    counts = jnp.bincount(e_flat, length=N_EXPERTS).astype(jnp.int32)
    padded = ((counts + EXPERT_BLOCK - 1) // EXPERT_BLOCK) * EXPERT_BLOCK
    start = jnp.cumsum(counts) - counts
    pend = jnp.cumsum(padded)
    pstart = pend - padded
    dest = pstart[e_s] + jnp.arange(A, dtype=jnp.int32) - start[e_s]
    n_blocks = (A + N_EXPERTS * (EXPERT_BLOCK - 1) + EXPERT_BLOCK - 1) // EXPERT_BLOCK
    P = n_blocks * EXPERT_BLOCK
    buf_tok = jnp.zeros((P,), jnp.int32).at[dest].set(tok_s)
    buf_w = jnp.zeros((P,), jnp.float32).at[dest].set(w_s)
    blk_start = jnp.arange(n_blocks, dtype=jnp.int32) * EXPERT_BLOCK
    blk_expert = jnp.minimum(jnp.searchsorted(pend, blk_start, side='right'), N_EXPERTS - 1)

    def expert_block(args):
        toks, wts, e = args
        xb = xf[toks]
        hb = jax.nn.silu(xb @ w_gate[e]) * (xb @ w_up[e])
        return (hb @ w_down[e]) * wts[:, None].astype(xb.dtype)

    yb = lax.map(expert_block, (buf_tok.reshape(n_blocks, EXPERT_BLOCK),
                                buf_w.reshape(n_blocks, EXPERT_BLOCK), blk_expert))
    routed = jnp.zeros_like(xf).at[buf_tok].add(yb.reshape(P, D))
    shared = (jax.nn.silu(xf @ ws_gate) * (xf @ ws_up)) @ ws_down
    return (routed + shared).reshape(B, S, D)


def setup_inputs(seed: int = 0) -> dict:
    key = jax.random.key(seed)
    ks = iter(jax.random.split(key, 48))
    nrm = lambda shape, s: jax.random.normal(next(ks), shape, jnp.float32) * s
    D, NR, NS, L = D_MODEL, N_RWKV_LAYERS, N_SB_LAYERS, DEPTH
    E, F, FS = N_EXPERTS, D_EXPERT, D_SHARED
    inv = lambda n: float(n) ** -0.5
    return {
        'x': nrm((BATCH, SEQ, D), 1.0),
        'c': nrm((BATCH, D), 1.0),
        'ada_w': nrm((L, D, N_ADA * D), 0.5 * inv(D)),
        'ada_b': nrm((L, N_ADA * D), 0.02),
        'norm_pre_mix': 1.0 + nrm((L, D), 0.05),
        'norm_post_mix': 1.0 + nrm((L, D), 0.05),
        'norm_pre_ffn': 1.0 + nrm((L, D), 0.05),
        'norm_post_ffn': 1.0 + nrm((L, D), 0.05),
        'rwkv_mu': jax.random.uniform(next(ks), (NR, N_TOKEN_SHIFT_MIX, D), jnp.float32),
        'rwkv_w_rkv': nrm((NR, 3, D, D), inv(D)),
        'rwkv_w_w1': nrm((NR, D, D_DECAY_LORA), inv(D)),
        'rwkv_w_w2': nrm((NR, D_DECAY_LORA, D), 0.5 * inv(D_DECAY_LORA)),
        'rwkv_w0': jax.random.uniform(next(ks), (NR, D), jnp.float32, -6.0, -1.0),
        'rwkv_w_a1': nrm((NR, D, D_AAA_LORA), inv(D)),
        'rwkv_w_a2': nrm((NR, D_AAA_LORA, D), 0.5 * inv(D_AAA_LORA)),
        'rwkv_a0': nrm((NR, D), 0.1),
        'rwkv_w_g1': nrm((NR, D, D_GATE_LORA), inv(D)),
        'rwkv_w_g2': nrm((NR, D_GATE_LORA, D), inv(D_GATE_LORA)),
        'rwkv_k_k': 0.85 + nrm((NR, D), 0.05),
        'rwkv_k_a': 1.0 + nrm((NR, D), 0.05),
        'rwkv_r_k': nrm((NR, N_HEADS, HEAD_DIM), 0.1),
        'rwkv_lnx_g': 1.0 + nrm((NR, D), 0.05),
        'rwkv_lnx_b': nrm((NR, D), 0.01),
        'rwkv_w_o': nrm((NR, D, D), inv(D)),
        'sb_w_qkv': nrm((NS, D, 3 * D), inv(D)),
        'sb_w_o': nrm((NS, D, D), inv(D)),
        'moe_w_router': nrm((L, D, E), inv(D)),
        'moe_router_bias': nrm((L, E), 0.01),
        'moe_w_gate': nrm((L, E, D, F), inv(D)),
        'moe_w_up': nrm((L, E, D, F), inv(D)),
        'moe_w_down': nrm((L, E, F, D), inv(F)),
        'moe_ws_gate': nrm((L, D, FS), inv(D)),
        'moe_ws_up': nrm((L, D, FS), inv(D)),
        'moe_ws_down': nrm((L, FS, D), inv(FS)),
    }


def reference(x, c, ada_w, ada_b, norm_pre_mix, norm_post_mix, norm_pre_ffn, norm_post_ffn,
              rwkv_mu, rwkv_w_rkv, rwkv_w_w1, rwkv_w_w2, rwkv_w0, rwkv_w_a1, rwkv_w_a2, rwkv_a0,
              rwkv_w_g1, rwkv_w_g2, rwkv_k_k, rwkv_k_a, rwkv_r_k, rwkv_lnx_g, rwkv_lnx_b, rwkv_w_o,
              sb_w_qkv, sb_w_o,
              moe_w_router, moe_router_bias, moe_w_gate, moe_w_up, moe_w_down,
              moe_ws_gate, moe_ws_up, moe_ws_down):
    mod_all = jnp.einsum('bd,lde->lbe', jax.nn.silu(c), ada_w) + ada_b[:, None, :]
    for layer in range(DEPTH):
        shift_m, scale_m, gate_m, shift_f, scale_f, gate_f = jnp.split(
            mod_all[layer][:, None, :], N_ADA, axis=-1)
        h = rmsnorm(x, norm_pre_mix[layer]) * (1.0 + scale_m) + shift_m
        i = layer // N_MIXERS
        if layer % N_MIXERS == 0:
            y = rwkv7_time_mix(h, rwkv_mu[i], rwkv_w_rkv[i], rwkv_w_w1[i], rwkv_w_w2[i], rwkv_w0[i],
                               rwkv_w_a1[i], rwkv_w_a2[i], rwkv_a0[i], rwkv_w_g1[i], rwkv_w_g2[i],
                               rwkv_k_k[i], rwkv_k_a[i], rwkv_r_k[i], rwkv_lnx_g[i], rwkv_lnx_b[i],
                               rwkv_w_o[i])
        else:
            y = stick_breaking_attention(h, sb_w_qkv[i], sb_w_o[i])
        x = x + gate_m * rmsnorm(y, norm_post_mix[layer])
        h = rmsnorm(x, norm_pre_ffn[layer]) * (1.0 + scale_f) + shift_f
        y = moe_ffn(h, moe_w_router[layer], moe_router_bias[layer], moe_w_gate[layer],
                    moe_w_up[layer], moe_w_down[layer], moe_ws_gate[layer], moe_ws_up[layer],
                    moe_ws_down[layer])
        x = x + gate_f * rmsnorm(y, norm_post_ffn[layer])
    return x
```

```python
import functools

import jax
import jax.numpy as jnp
from jax import lax
from jax.experimental import pallas as pl
from jax.experimental.pallas import tpu as pltpu

D_MODEL = 1024
BATCH = 2
SEQ = 8192
TOKENS = BATCH * SEQ
DEPTH = 2
HEAD_DIM = 64
N_EXPERTS = 64
TOP_K = 8
D_EXPERT = 256
ROUTED_SCALE = 2.5
RMS_EPS = 1e-6
LNX_EPS = 64e-5
N_ADA = 6

LANES = 128
SUBLANES = 8
N_PAIRS = D_MODEL // LANES
WKV_CHUNK = 64
ROW_TILE = 256
MOE_TILE = 1024
ATT_BLOCK = 128
EXP_UNDERFLOW = -104.0
VMEM_LIMIT = 56 * 1024 * 1024

F32 = jnp.float32
BF16 = jnp.bfloat16


def _nn(a, b):
    return jnp.dot(a, b, preferred_element_type=F32)


def _nt(a, b):
    return lax.dot_general(a, b, (((1,), (1,)), ((), ())), preferred_element_type=F32)


def _tn(a, b):
    return lax.dot_general(a, b, (((0,), (0,)), ((), ())), preferred_element_type=F32)


def _sigmoid(x):
    return 1.0 / (1.0 + jnp.exp(-x))


def _rms(xv, g):
    ms = jnp.mean(xv * xv, axis=-1, keepdims=True)
    return xv * lax.rsqrt(ms + RMS_EPS) * g


def _split3(x):
    hi = x.astype(BF16)
    r1 = x - hi.astype(F32)
    mid = r1.astype(BF16)
    lo = (r1 - mid.astype(F32)).astype(BF16)
    return hi, mid, lo


def _params(*sem):
    return pltpu.CompilerParams(dimension_semantics=sem, vmem_limit_bytes=VMEM_LIMIT)


def _ada_kernel(c_ref, w_ref, b_ref, o_ref):
    cv = c_ref[...]
    o_ref[0] = _nn(cv * _sigmoid(cv), w_ref[0]) + b_ref[0]


def _ada_mod(c, ada_w, ada_b):
    tn = 1536
    c8 = jnp.pad(c, ((0, SUBLANES - BATCH), (0, 0)))
    out = pl.pallas_call(
        _ada_kernel,
        grid=(DEPTH, N_ADA * D_MODEL // tn),
        in_specs=[
            pl.BlockSpec((SUBLANES, D_MODEL), lambda l, j: (0, 0)),
            pl.BlockSpec((1, D_MODEL, tn), lambda l, j: (l, 0, j)),
            pl.BlockSpec((1, 1, tn), lambda l, j: (l, 0, j)),
        ],
        out_specs=pl.BlockSpec((1, SUBLANES, tn), lambda l, j: (l, 0, j)),
        out_shape=jax.ShapeDtypeStruct((DEPTH, SUBLANES, N_ADA * D_MODEL), F32),
        compiler_params=_params("parallel", "parallel"),
        name="ada_mod",
    )(c8, ada_w, ada_b.reshape(DEPTH, 1, N_ADA * D_MODEL))
    return out[:, :BATCH].reshape(DEPTH, BATCH, N_ADA, D_MODEL)


def _rwkv_proj_kernel(x_ref, xp_ref, mod_ref, gpre_ref, mu_ref, wrkv_ref, w1_ref, w2_ref, w0_ref,
                      a1_ref, a2_ref, a0_ref, g1_ref, g2_ref,
                      r_ref, k_ref, v_ref, lw_ref, a_ref, g_ref):
    i = pl.program_id(0)
    mod = mod_ref[0]
    shift, scale = mod[0:1], mod[1:2]
    gpre = gpre_ref[...]

    def modnorm(xv):
        return _rms(xv, gpre) * (1.0 + scale) + shift

    h = modnorm(x_ref[...])
    hp = modnorm(xp_ref[...])[SUBLANES - 1:SUBLANES]
    hp = jnp.where(i % (SEQ // ROW_TILE) == 0, 0.0, hp)
    row = lax.broadcasted_iota(jnp.int32, h.shape, 0)
    hs = jnp.where(row == 0, hp, pltpu.roll(h, 1, 0))
    xx = hs - h
    mu = mu_ref[...]

    def mix(n):
        return (h + xx * mu[n:n + 1]).astype(BF16)

    r_ref[...] = _nn(mix(0), wrkv_ref[0])
    k_ref[...] = _nn(mix(1), wrkv_ref[1])
    v_ref[...] = _nn(mix(2), wrkv_ref[2])
    wl = w0_ref[...] + _nn(jnp.tanh(_nn(mix(3), w1_ref[...])).astype(BF16), w2_ref[...])
    nwl = -wl
    sp = jnp.maximum(nwl, 0.0) + jnp.log(1.0 + jnp.exp(-jnp.abs(nwl)))
    lw_ref[...] = -jnp.exp(-sp - 0.5)
    a_ref[...] = _sigmoid(a0_ref[...] + _nn(_nn(mix(4), a1_ref[...]).astype(BF16), a2_ref[...]))
    g_ref[...] = _nn(_sigmoid(_nn(mix(5), g1_ref[...])).astype(BF16), g2_ref[...])


def _rwkv_proj(x, mod, gpre, mu, w_rkv, w1, w2, w0, a1, a2, a0, g1, g2):
    tm = ROW_TILE
    full = lambda shape: pl.BlockSpec(shape, lambda i: (0,) * len(shape))
    row_spec = pl.BlockSpec((tm, D_MODEL), lambda i: (i, 0))
    out_sds = jax.ShapeDtypeStruct((TOKENS, D_MODEL), F32)
    return pl.pallas_call(
        _rwkv_proj_kernel,
        grid=(TOKENS // tm,),
        in_specs=[
            row_spec,
            pl.BlockSpec((SUBLANES, D_MODEL), lambda i: (jnp.maximum(i * (tm // SUBLANES) - 1, 0), 0)),
            pl.BlockSpec((1, N_ADA, D_MODEL), lambda i: (i // (SEQ // tm), 0, 0)),
            full((1, D_MODEL)), full((6, D_MODEL)), full((3, D_MODEL, D_MODEL)),
            full(w1.shape), full(w2.shape), full((1, D_MODEL)),
            full(a1.shape), full(a2.shape), full((1, D_MODEL)),
            full(g1.shape), full(g2.shape),
        ],
        out_specs=[row_spec] * 6,
        out_shape=[out_sds] * 6,
        compiler_params=_params("parallel"),
        name="rwkv_proj",
    )(x, x, mod, gpre.reshape(1, D_MODEL), mu, w_rkv.astype(BF16), w1.astype(BF16), w2.astype(BF16),
      w0.reshape(1, D_MODEL), a1.astype(BF16), a2.astype(BF16), a0.reshape(1, D_MODEL),
      g1.astype(BF16), g2.astype(BF16))


def _wkv_kernel(r_ref, k_ref, v_ref, lw_ref, a_ref, kk_ref, ka_ref, rk_ref, lg_ref, lb_ref,
                o_ref, s_ref):
    c = WKV_CHUNK

    @pl.when(pl.program_id(2) == 0)
    def _():
        s_ref[...] = jnp.zeros_like(s_ref)

    lane = lax.broadcasted_iota(jnp.int32, (c, LANES), 1)
    head0 = lane < HEAD_DIM
    rowi = lax.broadcasted_iota(jnp.int32, (LANES, LANES), 0)
    coli = lax.broadcasted_iota(jnp.int32, (LANES, LANES), 1)
    rt, ct = rowi & (c - 1), coli & (c - 1)
    strict, incl, eye = rt > ct, rt >= ct, rowi == coli
    tril = (lax.broadcasted_iota(jnp.int32, (c, c), 0)
            >= lax.broadcasted_iota(jnp.int32, (c, c), 1)).astype(BF16)

    def headsum(x):
        s0 = jnp.sum(jnp.where(head0, x, 0.0), axis=-1, keepdims=True)
        s1 = jnp.sum(jnp.where(head0, 0.0, x), axis=-1, keepdims=True)
        return jnp.where(head0, s0, s1)

    def stack(x):
        return jnp.concatenate([jnp.where(head0, x, 0.0), jnp.where(head0, 0.0, x)], axis=0)

    r, k, v, lw, a = r_ref[...], k_ref[...], v_ref[...], lw_ref[...], a_ref[...]
    kk = k * kk_ref[...]
    kk = kk / jnp.maximum(jnp.sqrt(headsum(kk * kk)), 1e-12)
    kf = k * (1.0 + (a - 1.0) * ka_ref[...])
    bvec = kk * a

    hi, mid, lo = _split3(lw)
    cum = _nn(tril, hi) + _nn(tril, mid) + _nn(tril, lo)
    cum_end = cum[c - 1:c]
    w_inv = jnp.exp(-cum)
    w_rem = jnp.exp(cum_end - cum)
    a_st = stack(-kk * jnp.exp(cum - lw))
    r_st = stack(r * jnp.exp(cum))
    b_st = stack(bvec * w_inv)
    k_st = stack(kf * w_inv)
    bp_st = stack(bvec * w_rem)
    kp_st = stack(kf * w_rem)
    v_st = stack(v)

    sc = _nt(jnp.concatenate([a_st, r_st], axis=0), jnp.concatenate([b_st, k_st], axis=0))
    l_ab = jnp.where(strict, sc[:LANES, :LANES], 0.0)
    l_ak = jnp.where(strict, sc[:LANES, LANES:], 0.0)
    l_rb = jnp.where(incl, sc[LANES:, :LANES], 0.0)
    l_rk = jnp.where(incl, sc[LANES:, LANES:], 0.0)

    tinv = jnp.where(eye, 1.0, l_ab)
    pw = l_ab
    steps = c.bit_length() - 2
    for _ in range(steps):
        pw = _nn(pw, pw)
        tinv = tinv + _nn(tinv, pw)

    a_hat = _nn(tinv, a_st)
    u0 = _nn(tinv, _nn(l_ak, v_st))
    r_hat = r_st + _nn(l_rb, a_hat)
    o0 = _nn(l_rb, u0) + _nn(l_rk, v_st)
    g_mat = jnp.where(eye, jnp.exp(cum_end), 0.0) + _tn(bp_st, a_hat)
    h_mat = _tn(bp_st, u0) + _tn(kp_st, v_st)

    s0 = s_ref[...]
    o_st = _nn(r_hat, s0) + o0
    s_ref[...] = _nn(g_mat, s0) + h_mat
    o = o_st[:c] + o_st[c:]

    mean = headsum(o) * (1.0 / HEAD_DIM)
    oc = o - mean
    var = headsum(oc * oc) * (1.0 / HEAD_DIM)
    on = oc * lax.rsqrt(var + LNX_EPS) * lg_ref[...] + lb_ref[...]
    o_ref[...] = on + headsum(r * kf * rk_ref[...]) * v


def _wkv(r, k, v, lw, a, k_k, k_a, r_k, lnx_g, lnx_b):
    n_chunks = SEQ // WKV_CHUNK
    tok = pl.BlockSpec((WKV_CHUNK, LANES), lambda b, p, c: (b * n_chunks + c, p))
    par = pl.BlockSpec((1, LANES), lambda b, p, c: (0, p))
    vec = lambda t: t.reshape(1, D_MODEL)
    return pl.pallas_call(
        _wkv_kernel,
        grid=(BATCH, N_PAIRS, n_chunks),
        in_specs=[tok] * 5 + [par] * 5,
        out_specs=tok,
        out_shape=jax.ShapeDtypeStruct((TOKENS, D_MODEL), F32),
        scratch_shapes=[pltpu.VMEM((LANES, LANES), F32)],
        compiler_params=_params("parallel", "parallel", "arbitrary"),
        name="wkv_scan",
    )(r, k, v, lw, a, vec(k_k), vec(k_a), vec(r_k), vec(lnx_g), vec(lnx_b))


def _post_kernel(*refs, has_gate):
    if has_gate:
        o_ref, g_ref, *refs = refs
    else:
        o_ref, *refs = refs
    (x_ref, mod_ref, wo_ref, gpost_ref, gpre_ref, wr_hi_ref, wr_lo_ref, bias_ref,
     x1_ref, h2_ref, gates_ref) = refs
    mod = mod_ref[0]
    gate_m, shift_f, scale_f = mod[2:3], mod[3:4], mod[4:5]
    o = o_ref[...]
    if has_gate:
        o = o * g_ref[...]
    y = _nn(o.astype(BF16), wo_ref[...])
    x1 = x_ref[...] + gate_m * _rms(y, gpost_ref[...])
    x1_ref[...] = x1
    h2 = _rms(x1, gpre_ref[...]) * (1.0 + scale_f) + shift_f
    h2_ref[...] = h2.astype(BF16)

    hi = h2.astype(BF16)
    lo = (h2 - hi.astype(F32)).astype(BF16)
    logits = _nn(hi, wr_hi_ref[...]) + (_nn(lo, wr_hi_ref[...]) + _nn(hi, wr_lo_ref[...]))
    scores = _sigmoid(logits)
    lane = lax.broadcasted_iota(jnp.int32, scores.shape, 1)
    work = jnp.where(lane < N_EXPERTS, scores + bias_ref[...], -jnp.inf)
    picked = jnp.zeros_like(scores)
    for _ in range(TOP_K):
        best = jnp.max(work, axis=-1, keepdims=True)
        idx = jnp.min(jnp.where(work == best, lane, LANES), axis=-1, keepdims=True)
        hit = lane == idx
        picked = jnp.where(hit, scores, picked)
        work = jnp.where(hit, -jnp.inf, work)
    gates_ref[...] = ROUTED_SCALE * picked / jnp.sum(picked, axis=-1, keepdims=True)


def _post(o, g, x, mod, w_o, gpost, gpre, w_router, router_bias):
    tm = ROW_TILE
    full = lambda shape: pl.BlockSpec(shape, lambda i: (0,) * len(shape))
    row_spec = pl.BlockSpec((tm, D_MODEL), lambda i: (i, 0))
    wr = jnp.pad(w_router, ((0, 0), (0, LANES - N_EXPERTS)))
    wr_hi = wr.astype(BF16)
    wr_lo = (wr - wr_hi.astype(F32)).astype(BF16)
    bias = jnp.pad(router_bias, (0, LANES - N_EXPERTS)).reshape(1, LANES)
    has_gate = g is not None
    acts = [o, g] if has_gate else [o]
    return pl.pallas_call(
        functools.partial(_post_kernel, has_gate=has_gate),
        grid=(TOKENS // tm,),
        in_specs=[row_spec] * (len(acts) + 1) + [
            pl.BlockSpec((1, N_ADA, D_MODEL), lambda i: (i // (SEQ // tm), 0, 0)),
            full((D_MODEL, D_MODEL)), full((1, D_MODEL)), full((1, D_MODEL)),
            full((D_MODEL, LANES)), full((D_MODEL, LANES)), full((1, LANES)),
        ],
        out_specs=[row_spec, row_spec, pl.BlockSpec((tm, LANES), lambda i: (i, 0))],
        out_shape=[jax.ShapeDtypeStruct((TOKENS, D_MODEL), F32),
                   jax.ShapeDtypeStruct((TOKENS, D_MODEL), BF16),
                   jax.ShapeDtypeStruct((TOKENS, LANES), F32)],
        compiler_params=_params("parallel"),
        name="mixer_out_router",
    )(*acts, x, mod, w_o.astype(BF16), gpost.reshape(1, D_MODEL), gpre.reshape(1, D_MODEL),
      wr_hi, wr_lo, bias)


def _moe_kernel(h_ref, gates_ref, x_ref, mod_ref, gpost_ref, wg_ref, wu_ref, wd_ref,
                sg_ref, su_ref, sd_ref, o_ref, acc_ref):
    e = pl.program_id(1)
    h = h_ref[...]

    def ffn(wg, wu, wd):
        up = _nn(h, wu.astype(BF16))
        gt = _nn(h, wg.astype(BF16))
        return _nn((gt * _sigmoid(gt) * up).astype(BF16), wd.astype(BF16))

    @pl.when(e == 0)
    def _():
        acc_ref[...] = ffn(sg_ref[...], su_ref[...], sd_ref[...])

    gates = gates_ref[...]
    lane = lax.broadcasted_iota(jnp.int32, gates.shape, 1)
    gcol = jnp.sum(jnp.where(lane == e, gates, 0.0), axis=-1, keepdims=True)
    acc_ref[...] += ffn(wg_ref[0], wu_ref[0], wd_ref[0]) * gcol

    @pl.when(e == N_EXPERTS - 1)
    def _():
        gate_f = mod_ref[0][5:6]
        o_ref[...] = x_ref[...] + gate_f * _rms(acc_ref[...], gpost_ref[...])


def _moe(h2, gates, x1, mod, gpost, w_gate, w_up, w_down, ws_gate, ws_up, ws_down):
    tm = MOE_TILE
    full = lambda shape: pl.BlockSpec(shape, lambda i, e: (0,) * len(shape))
    row_spec = pl.BlockSpec((tm, D_MODEL), lambda i, e: (i, 0))
    return pl.pallas_call(
        _moe_kernel,
        grid=(TOKENS // tm, N_EXPERTS),
        in_specs=[
            row_spec,
            pl.BlockSpec((tm, LANES), lambda i, e: (i, 0)),
            row_spec,
            pl.BlockSpec((1, N_ADA, D_MODEL), lambda i, e: (i // (SEQ // tm), 0, 0)),
            full((1, D_MODEL)),
            pl.BlockSpec((1, D_MODEL, D_EXPERT), lambda i, e: (e, 0, 0)),
            pl.BlockSpec((1, D_MODEL, D_EXPERT), lambda i, e: (e, 0, 0)),
            pl.BlockSpec((1, D_EXPERT, D_MODEL), lambda i, e: (e, 0, 0)),
            full((D_MODEL, D_EXPERT)), full((D_MODEL, D_EXPERT)), full((D_EXPERT, D_MODEL)),
        ],
        out_specs=row_spec,
        out_shape=jax.ShapeDtypeStruct((TOKENS, D_MODEL), F32),
        scratch_shapes=[pltpu.VMEM((tm, D_MODEL), F32)],
        compiler_params=_params("parallel", "arbitrary"),
        name="moe_ffn",
    )(h2, gates, x1, mod, gpost.reshape(1, D_MODEL), w_gate, w_up, w_down, ws_gate, ws_up, ws_down)


def _sb_proj_kernel(x_ref, mod_ref, gpre_ref, w_ref, q_ref, k_ref, v_ref):
    mod = mod_ref[0]
    h = (_rms(x_ref[...], gpre_ref[...]) * (1.0 + mod[1:2]) + mod[0:1]).astype(BF16)
    q_ref[...] = _nn(h, w_ref[:, :D_MODEL]) * (1.0 / 8.0)
    k_ref[...] = _nn(h, w_ref[:, D_MODEL:2 * D_MODEL])
    v_ref[...] = _nn(h, w_ref[:, 2 * D_MODEL:])


def _sb_proj(x, mod, gpre, w_qkv):
    tm = ROW_TILE
    full = lambda shape: pl.BlockSpec(shape, lambda i: (0,) * len(shape))
    row_spec = pl.BlockSpec((tm, D_MODEL), lambda i: (i, 0))
    out_sds = jax.ShapeDtypeStruct((TOKENS, D_MODEL), F32)
    return pl.pallas_call(
        _sb_proj_kernel,
        grid=(TOKENS // tm,),
        in_specs=[row_spec,
                  pl.BlockSpec((1, N_ADA, D_MODEL), lambda i: (i // (SEQ // tm), 0, 0)),
                  full((1, D_MODEL)), full((D_MODEL, 3 * D_MODEL))],
        out_specs=[row_spec] * 3,
        out_shape=[out_sds] * 3,
        compiler_params=_params("parallel"),
        name="sb_qkv_proj",
    )(x, mod, gpre.reshape(1, D_MODEL), w_qkv.astype(BF16))


def _sb_attn_kernel(q_ref, k_ref, v_ref, o_ref):
    blk = ATT_BLOCK
    qb = pl.program_id(2)
    q = q_ref[...]
    lane = lax.broadcasted_iota(jnp.int32, (blk, LANES), 1)
    rowi = lax.broadcasted_iota(jnp.int32, (blk, blk), 0)
    coli = lax.broadcasted_iota(jnp.int32, (blk, blk), 1)
    below = coli < rowi
    later = (rowi > coli).astype(BF16)
    out = jnp.zeros((blk, LANES), F32)

    for hd in range(LANES // HEAD_DIM):
        in_head = (lane < HEAD_DIM) if hd == 0 else (lane >= HEAD_DIM)
        qh = jnp.where(in_head, q, 0.0)

        def cond(st):
            kb, _, _, cmax = st
            return jnp.logical_and(kb >= 0, cmax > EXP_UNDERFLOW)

        def body(st):
            kb, carry, acc, _ = st
            start = pl.multiple_of(kb * blk, blk)
            ks = k_ref[pl.ds(start, blk), :]
            vs = v_ref[pl.ds(start, blk), :]
            z = _nt(qh, ks)
            lb = jnp.minimum(z, 0.0) - jnp.log(1.0 + jnp.exp(-jnp.abs(z)))
            causal = jnp.logical_or(below, kb < qb)
            l1m = jnp.where(causal, lb - z, 0.0)
            hi = l1m.astype(BF16)
            lo = (l1m - hi.astype(F32)).astype(BF16)
            tail = _nn(hi, later) + _nn(lo, later) + carry
            w = jnp.where(causal, jnp.exp(lb + tail), 0.0)
            acc = acc + _nn(w, vs)
            carry = carry + jnp.sum(l1m, axis=-1, keepdims=True)
            return kb - 1, carry, acc, jnp.max(carry)

        init = (qb, jnp.zeros((blk, 1), F32), jnp.zeros((blk, LANES), F32), jnp.float32(0.0))
        _, _, acc, _ = lax.while_loop(cond, body, init)
        out = jnp.where(in_head, acc, out)

    o_ref[...] = out


def _sb_attn(q, k, v):
    blk = ATT_BLOCK
    n_blk = SEQ // blk
    q_spec = pl.BlockSpec((blk, LANES), lambda b, p, i: (b * n_blk + i, p))
    kv_spec = pl.BlockSpec((SEQ, LANES), lambda b, p, i: (b, p))
    return pl.pallas_call(
        _sb_attn_kernel,
        grid=(BATCH, N_PAIRS, n_blk),
        in_specs=[q_spec, kv_spec, kv_spec],
        out_specs=q_spec,
        out_shape=jax.ShapeDtypeStruct((TOKENS, D_MODEL), F32),
        compiler_params=_params("parallel", "parallel", "arbitrary"),
        name="sb_attention",
    )(q, k, v)


def kernel(x, c, ada_w, ada_b, norm_pre_mix, norm_post_mix, norm_pre_ffn, norm_post_ffn, rwkv_mu, rwkv_w_rkv, rwkv_w_w1, rwkv_w_w2, rwkv_w0, rwkv_w_a1, rwkv_w_a2, rwkv_a0, rwkv_w_g1, rwkv_w_g2, rwkv_k_k, rwkv_k_a, rwkv_r_k, rwkv_lnx_g, rwkv_lnx_b, rwkv_w_o, sb_w_qkv, sb_w_o, moe_w_router, moe_router_bias, moe_w_gate, moe_w_up, moe_w_down, moe_ws_gate, moe_ws_up, moe_ws_down):
    mod_all = _ada_mod(c, ada_w, ada_b)
    xt = x.reshape(TOKENS, D_MODEL)
    for layer in range(DEPTH):
        mod = mod_all[layer]
        i = layer // 2
        if layer % 2 == 0:
            r, k, v, lw, a, g = _rwkv_proj(
                xt, mod, norm_pre_mix[layer], rwkv_mu[i], rwkv_w_rkv[i], rwkv_w_w1[i], rwkv_w_w2[i],
                rwkv_w0[i], rwkv_w_a1[i], rwkv_w_a2[i], rwkv_a0[i], rwkv_w_g1[i], rwkv_w_g2[i])
            o = _wkv(r, k, v, lw, a, rwkv_k_k[i], rwkv_k_a[i], rwkv_r_k[i], rwkv_lnx_g[i], rwkv_lnx_b[i])
            w_o = rwkv_w_o[i]
        else:
            q, k, v = _sb_proj(xt, mod, norm_pre_mix[layer], sb_w_qkv[i])
            o, g = _sb_attn(q, k, v), None
            w_o = sb_w_o[i]
        x1, h2, gates = _post(o, g, xt, mod, w_o, norm_post_mix[layer], norm_pre_ffn[layer],
                              moe_w_router[layer], moe_router_bias[layer])
        xt = _moe(h2, gates, x1, mod, norm_post_ffn[layer], moe_w_gate[layer], moe_w_up[layer],
                  moe_w_down[layer], moe_ws_gate[layer], moe_ws_up[layer], moe_ws_down[layer])
    return xt.reshape(BATCH, SEQ, D_MODEL)
```

```python
import functools

import jax
import jax.numpy as jnp
from jax import lax
from jax.experimental import pallas as pl
from jax.experimental.pallas import tpu as pltpu

D_MODEL = 1024
BATCH = 2
SEQ = 8192
TOKENS = BATCH * SEQ
DEPTH = 2
HEAD_DIM = 64
N_EXPERTS = 64
TOP_K = 8
D_EXPERT = 256
ROUTED_SCALE = 2.5
RMS_EPS = 1e-6
LNX_EPS = 64e-5
N_ADA = 6

LANES = 128
SUBLANES = 8
N_PAIRS = D_MODEL // LANES
WKV_CHUNK = 64
WKV_PAIRS = 4
ROW_TILE = 256
MOE_TILE = 1024
ATT_BLOCK = 256
EXP_UNDERFLOW = -104.0
VMEM_LIMIT = 56 * 1024 * 1024

F32 = jnp.float32
BF16 = jnp.bfloat16


def _nn(a, b):
    return jnp.dot(a, b, preferred_element_type=F32)


def _nt(a, b):
    return lax.dot_general(a, b, (((1,), (1,)), ((), ())), preferred_element_type=F32)


def _tn(a, b):
    return lax.dot_general(a, b, (((0,), (0,)), ((), ())), preferred_element_type=F32)


def _bf(x):
    return x.astype(BF16)


def _sigmoid(x):
    return 1.0 / (1.0 + jnp.exp(-x))


def _rms(xv, g):
    ms = jnp.mean(xv * xv, axis=-1, keepdims=True)
    return xv * lax.rsqrt(ms + RMS_EPS) * g


def _split3(x):
    hi = x.astype(BF16)
    r1 = x - hi.astype(F32)
    mid = r1.astype(BF16)
    lo = (r1 - mid.astype(F32)).astype(BF16)
    return hi, mid, lo


def _params(*sem):
    return pltpu.CompilerParams(dimension_semantics=sem, vmem_limit_bytes=VMEM_LIMIT)


def _ada_kernel(c_ref, w_ref, b_ref, o_ref):
    cv = c_ref[...]
    o_ref[0] = _nn(cv * _sigmoid(cv), w_ref[0]) + b_ref[0]


def _ada_mod(c, ada_w, ada_b):
    tn = 1536
    c8 = jnp.pad(c, ((0, SUBLANES - BATCH), (0, 0)))
    out = pl.pallas_call(
        _ada_kernel,
        grid=(DEPTH, N_ADA * D_MODEL // tn),
        in_specs=[
            pl.BlockSpec((SUBLANES, D_MODEL), lambda l, j: (0, 0)),
            pl.BlockSpec((1, D_MODEL, tn), lambda l, j: (l, 0, j)),
            pl.BlockSpec((1, 1, tn), lambda l, j: (l, 0, j)),
        ],
        out_specs=pl.BlockSpec((1, SUBLANES, tn), lambda l, j: (l, 0, j)),
        out_shape=jax.ShapeDtypeStruct((DEPTH, SUBLANES, N_ADA * D_MODEL), F32),
        compiler_params=_params("parallel", "parallel"),
        name="ada_mod",
    )(c8, ada_w, ada_b.reshape(DEPTH, 1, N_ADA * D_MODEL))
    return out[:, :BATCH].reshape(DEPTH, BATCH, N_ADA, D_MODEL)


def _rwkv_proj_kernel(x_ref, xp_ref, mod_ref, gpre_ref, mu_ref, wrkv_ref, w1_ref, w2_ref, w0_ref,
                      a1_ref, a2_ref, a0_ref, g1_ref, g2_ref,
                      r_ref, k_ref, v_ref, lw_ref, a_ref, g_ref):
    i = pl.program_id(0)
    mod = mod_ref[0]
    shift, scale = mod[0:1], mod[1:2]
    gpre = gpre_ref[...]

    def modnorm(xv):
        return _rms(xv, gpre) * (1.0 + scale) + shift

    h = modnorm(x_ref[...])
    hp = modnorm(xp_ref[...])[SUBLANES - 1:SUBLANES]
    hp = jnp.where(i % (SEQ // ROW_TILE) == 0, 0.0, hp)
    row = lax.broadcasted_iota(jnp.int32, h.shape, 0)
    hs = jnp.where(row == 0, hp, pltpu.roll(h, 1, 0))
    xx = hs - h
    mu = mu_ref[...]

    def mix(n):
        return _bf(h + xx * mu[n:n + 1])

    r_ref[...] = _nn(mix(0), wrkv_ref[0])
    k_ref[...] = _nn(mix(1), wrkv_ref[1])
    v_ref[...] = _nn(mix(2), wrkv_ref[2])
    wl = w0_ref[...] + _nn(_bf(jnp.tanh(_nn(mix(3), w1_ref[...]))), w2_ref[...])
    nwl = -wl
    sp = jnp.maximum(nwl, 0.0) + jnp.log(1.0 + jnp.exp(-jnp.abs(nwl)))
    lw_ref[...] = -jnp.exp(-sp - 0.5)
    a_ref[...] = _sigmoid(a0_ref[...] + _nn(_bf(_nn(mix(4), a1_ref[...])), a2_ref[...]))
    g_ref[...] = _nn(_bf(_sigmoid(_nn(mix(5), g1_ref[...]))), g2_ref[...])


def _rwkv_proj(x, mod, gpre, mu, w_rkv, w1, w2, w0, a1, a2, a0, g1, g2):
    tm = ROW_TILE
    full = lambda shape: pl.BlockSpec(shape, lambda i: (0,) * len(shape))
    row_spec = pl.BlockSpec((tm, D_MODEL), lambda i: (i, 0))
    out_sds = jax.ShapeDtypeStruct((TOKENS, D_MODEL), F32)
    return pl.pallas_call(
        _rwkv_proj_kernel,
        grid=(TOKENS // tm,),
        in_specs=[
            row_spec,
            pl.BlockSpec((SUBLANES, D_MODEL), lambda i: (jnp.maximum(i * (tm // SUBLANES) - 1, 0), 0)),
            pl.BlockSpec((1, N_ADA, D_MODEL), lambda i: (i // (SEQ // tm), 0, 0)),
            full((1, D_MODEL)), full((6, D_MODEL)), full((3, D_MODEL, D_MODEL)),
            full(w1.shape), full(w2.shape), full((1, D_MODEL)),
            full(a1.shape), full(a2.shape), full((1, D_MODEL)),
            full(g1.shape), full(g2.shape),
        ],
        out_specs=[row_spec] * 6,
        out_shape=[out_sds] * 6,
        compiler_params=_params("parallel"),
        name="rwkv_proj",
    )(x, x, mod, gpre.reshape(1, D_MODEL), mu, _bf(w_rkv), _bf(w1), _bf(w2),
      w0.reshape(1, D_MODEL), _bf(a1), _bf(a2), a0.reshape(1, D_MODEL), _bf(g1), _bf(g2))


def _wkv_pair(r, k, v, lw, a, k_k, k_a, r_k, ln_g, ln_b, s_ref, o_ref):
    c = WKV_CHUNK
    lane = lax.broadcasted_iota(jnp.int32, (c, LANES), 1)
    head0 = lane < HEAD_DIM
    rowi = lax.broadcasted_iota(jnp.int32, (LANES, LANES), 0)
    coli = lax.broadcasted_iota(jnp.int32, (LANES, LANES), 1)
    rt, ct = rowi & (c - 1), coli & (c - 1)
    strict, incl, eye = rt > ct, rt >= ct, rowi == coli
    tril = (lax.broadcasted_iota(jnp.int32, (c, c), 0)
            >= lax.broadcasted_iota(jnp.int32, (c, c), 1)).astype(BF16)

    def headsum(x):
        s0 = jnp.sum(jnp.where(head0, x, 0.0), axis=-1, keepdims=True)
        s1 = jnp.sum(jnp.where(head0, 0.0, x), axis=-1, keepdims=True)
        return jnp.where(head0, s0, s1)

    def stack(x):
        return jnp.concatenate([jnp.where(head0, x, 0.0), jnp.where(head0, 0.0, x)], axis=0)

    kk = k * k_k
    kk = kk / jnp.maximum(jnp.sqrt(headsum(kk * kk)), 1e-12)
    kf = k * (1.0 + (a - 1.0) * k_a)
    bvec = kk * a
    yield

    cum = _nn(tril, jnp.concatenate(_split3(lw), axis=1))
    cum = cum[:, :LANES] + cum[:, LANES:2 * LANES] + cum[:, 2 * LANES:]
    cum_end = cum[c - 1:c]
    w_inv = jnp.exp(-cum)
    w_rem = jnp.exp(cum_end - cum)
    a_st = stack(-kk * jnp.exp(cum - lw))
    r_st = stack(r * jnp.exp(cum))
    b_st = stack(bvec * w_inv)
    k_st = stack(kf * w_inv)
    bp_st = _bf(stack(bvec * w_rem))
    kp_st = _bf(stack(kf * w_rem))
    v_st = _bf(stack(v))
    yield

    sc = _nt(_bf(jnp.concatenate([a_st, r_st], axis=0)), _bf(jnp.concatenate([b_st, k_st], axis=0)))
    l_ab = jnp.where(strict, sc[:LANES, :LANES], 0.0)
    l_ak = jnp.where(strict, sc[:LANES, LANES:], 0.0)
    l_rb = jnp.where(incl, sc[LANES:, :LANES], 0.0)
    l_rk = jnp.where(incl, sc[LANES:, LANES:], 0.0)
    yield

    tinv = jnp.where(eye, 1.0, l_ab)
    pw = l_ab
    for _ in range(c.bit_length() - 2):
        pwb = _bf(pw)
        pw = _nn(pwb, pwb)
        tinv = tinv + _nn(_bf(tinv), _bf(pw))
        yield

    au = _nn(_bf(tinv), _bf(jnp.concatenate([a_st, _nn(_bf(l_ak), v_st)], axis=1)))
    aub = _bf(au)
    yield
    rb = _nn(_bf(l_rb), aub)
    gh = _tn(bp_st, aub)
    r_hat = r_st + rb[:, :LANES]
    o0 = rb[:, LANES:] + _nn(_bf(l_rk), v_st)
    g_mat = jnp.where(eye, jnp.exp(cum_end), 0.0) + gh[:, :LANES]
    h_mat = gh[:, LANES:] + _tn(kp_st, v_st)
    yield

    os = _nn(_bf(jnp.concatenate([r_hat, g_mat], axis=0)), _bf(s_ref[...]))
    s_ref[...] = os[LANES:] + h_mat
    o_st = os[:LANES] + o0
    o = o_st[:c] + o_st[c:]
    yield

    mean = headsum(o) * (1.0 / HEAD_DIM)
    oc = o - mean
    var = headsum(oc * oc) * (1.0 / HEAD_DIM)
    on = oc * lax.rsqrt(var + LNX_EPS) * ln_g + ln_b
    o_ref[...] = on + headsum(r * kf * r_k) * v


def _wkv_kernel(r_ref, k_ref, v_ref, lw_ref, a_ref, kk_ref, ka_ref, rk_ref, lg_ref, lb_ref,
                o_ref, s_ref):
    @pl.when(pl.program_id(2) == 0)
    def _():
        s_ref[...] = jnp.zeros_like(s_ref)

    def pair(p):
        sl = pl.ds(p * LANES, LANES)
        ins = [ref[:, sl] for ref in (r_ref, k_ref, v_ref, lw_ref, a_ref, kk_ref, ka_ref, rk_ref, lg_ref, lb_ref)]
        return _wkv_pair(*ins, s_ref.at[p], o_ref.at[:, sl])

    stages = [pair(p) for p in range(WKV_PAIRS)]
    while stages:
        stages = [g for g in stages if next(g, True) is None]


def _wkv(r, k, v, lw, a, k_k, k_a, r_k, lnx_g, lnx_b):
    n_chunks = SEQ // WKV_CHUNK
    width = WKV_PAIRS * LANES
    tok = pl.BlockSpec((WKV_CHUNK, width), lambda b, p, c: (b * n_chunks + c, p))
    par = pl.BlockSpec((1, width), lambda b, p, c: (0, p))
    vec = lambda t: t.reshape(1, D_MODEL)
    return pl.pallas_call(
        _wkv_kernel,
        grid=(BATCH, N_PAIRS // WKV_PAIRS, n_chunks),
        in_specs=[tok] * 5 + [par] * 5,
        out_specs=tok,
        out_shape=jax.ShapeDtypeStruct((TOKENS, D_MODEL), F32),
        scratch_shapes=[pltpu.VMEM((WKV_PAIRS, LANES, LANES), F32)],
        compiler_params=_params("parallel", "parallel", "arbitrary"),
        name="wkv_scan",
    )(r, k, v, lw, a, vec(k_k), vec(k_a), vec(r_k), vec(lnx_g), vec(lnx_b))


def _post_kernel(*refs, has_gate):
    if has_gate:
        o_ref, g_ref, *refs = refs
    else:
        o_ref, *refs = refs
    (x_ref, mod_ref, wo_ref, gpost_ref, gpre_ref, wr_hi_ref, wr_lo_ref, bias_ref,
     x1_ref, h2_ref, gates_ref) = refs
    mod = mod_ref[0]
    gate_m, shift_f, scale_f = mod[2:3], mod[3:4], mod[4:5]
    o = o_ref[...]
    if has_gate:
        o = o * g_ref[...]
    y = _nn(_bf(o), wo_ref[...])
    x1 = x_ref[...] + gate_m * _rms(y, gpost_ref[...])
    x1_ref[...] = x1
    h2 = _rms(x1, gpre_ref[...]) * (1.0 + scale_f) + shift_f
    h2_ref[...] = _bf(h2)

    hi = _bf(h2)
    lo = _bf(h2 - hi.astype(F32))
    logits = _nn(hi, wr_hi_ref[...]) + (_nn(lo, wr_hi_ref[...]) + _nn(hi, wr_lo_ref[...]))
    scores = _sigmoid(logits)
    lane = lax.broadcasted_iota(jnp.int32, scores.shape, 1)
    work = jnp.where(lane < N_EXPERTS, scores + bias_ref[...], -jnp.inf)
    picked = jnp.zeros_like(scores)
    for _ in range(TOP_K):
        best = jnp.max(work, axis=-1, keepdims=True)
        idx = jnp.min(jnp.where(work == best, lane, LANES), axis=-1, keepdims=True)
        hit = lane == idx
        picked = jnp.where(hit, scores, picked)
        work = jnp.where(hit, -jnp.inf, work)
    gates_ref[...] = ROUTED_SCALE * picked / jnp.sum(picked, axis=-1, keepdims=True)


def _post(o, g, x, mod, w_o, gpost, gpre, w_router, router_bias):
    tm = ROW_TILE
    full = lambda shape: pl.BlockSpec(shape, lambda i: (0,) * len(shape))
    row_spec = pl.BlockSpec((tm, D_MODEL), lambda i: (i, 0))
    wr = jnp.pad(w_router, ((0, 0), (0, LANES - N_EXPERTS)))
    wr_hi = _bf(wr)
    wr_lo = _bf(wr - wr_hi.astype(F32))
    bias = jnp.pad(router_bias, (0, LANES - N_EXPERTS)).reshape(1, LANES)
    has_gate = g is not None
    acts = [o, g] if has_gate else [o]
    return pl.pallas_call(
        functools.partial(_post_kernel, has_gate=has_gate),
        grid=(TOKENS // tm,),
        in_specs=[row_spec] * (len(acts) + 1) + [
            pl.BlockSpec((1, N_ADA, D_MODEL), lambda i: (i // (SEQ // tm), 0, 0)),
            full((D_MODEL, D_MODEL)), full((1, D_MODEL)), full((1, D_MODEL)),
            full((D_MODEL, LANES)), full((D_MODEL, LANES)), full((1, LANES)),
        ],
        out_specs=[row_spec, row_spec, pl.BlockSpec((tm, LANES), lambda i: (i, 0))],
        out_shape=[jax.ShapeDtypeStruct((TOKENS, D_MODEL), F32),
                   jax.ShapeDtypeStruct((TOKENS, D_MODEL), BF16),
                   jax.ShapeDtypeStruct((TOKENS, LANES), F32)],
        compiler_params=_params("parallel"),
        name="mixer_out_router",
    )(*acts, x, mod, _bf(w_o), gpost.reshape(1, D_MODEL), gpre.reshape(1, D_MODEL),
      wr_hi, wr_lo, bias)


def _moe_kernel(h_ref, gates_ref, x_ref, mod_ref, gpost_ref, wg_ref, wu_ref, wd_ref,
                sg_ref, su_ref, sd_ref, o_ref, acc_ref):
    e = pl.program_id(1)
    h = h_ref[...]

    def ffn(wg, wu, wd):
        up = _nn(h, _bf(wu))
        gt = _nn(h, _bf(wg))
        return _nn(_bf(gt * _sigmoid(gt) * up), _bf(wd))

    @pl.when(e == 0)
    def _():
        acc_ref[...] = ffn(sg_ref[0], su_ref[0], sd_ref[0])

    gates = gates_ref[...]
    lane = lax.broadcasted_iota(jnp.int32, gates.shape, 1)
    gcol = jnp.sum(jnp.where(lane == e, gates, 0.0), axis=-1, keepdims=True)
    acc_ref[...] += ffn(wg_ref[0, 0], wu_ref[0, 0], wd_ref[0, 0]) * gcol

    @pl.when(e == N_EXPERTS - 1)
    def _():
        gate_f = mod_ref[0][5:6]
        o_ref[...] = x_ref[...] + gate_f * _rms(acc_ref[...], gpost_ref[...])


def _moe(layer, h2, gates, x1, mod, gpost, w_gate, w_up, w_down, ws_gate, ws_up, ws_down):
    tm = MOE_TILE
    row_spec = pl.BlockSpec((tm, D_MODEL), lambda i, e: (i, 0))
    return pl.pallas_call(
        _moe_kernel,
        grid=(TOKENS // tm, N_EXPERTS),
        in_specs=[
            row_spec,
            pl.BlockSpec((tm, LANES), lambda i, e: (i, 0)),
            row_spec,
            pl.BlockSpec((1, N_ADA, D_MODEL), lambda i, e: (i // (SEQ // tm), 0, 0)),
            pl.BlockSpec((1, D_MODEL), lambda i, e: (0, 0)),
            pl.BlockSpec((1, 1, D_MODEL, D_EXPERT), lambda i, e: (layer, e, 0, 0)),
            pl.BlockSpec((1, 1, D_MODEL, D_EXPERT), lambda i, e: (layer, e, 0, 0)),
            pl.BlockSpec((1, 1, D_EXPERT, D_MODEL), lambda i, e: (layer, e, 0, 0)),
            pl.BlockSpec((1, D_MODEL, D_EXPERT), lambda i, e: (layer, 0, 0)),
            pl.BlockSpec((1, D_MODEL, D_EXPERT), lambda i, e: (layer, 0, 0)),
            pl.BlockSpec((1, D_EXPERT, D_MODEL), lambda i, e: (layer, 0, 0)),
        ],
        out_specs=row_spec,
        out_shape=jax.ShapeDtypeStruct((TOKENS, D_MODEL), F32),
        scratch_shapes=[pltpu.VMEM((tm, D_MODEL), F32)],
        compiler_params=_params("parallel", "arbitrary"),
        name="moe_ffn",
    )(h2, gates, x1, mod, gpost.reshape(1, D_MODEL), w_gate, w_up, w_down, ws_gate, ws_up, ws_down)


def _sb_proj_kernel(x_ref, mod_ref, gpre_ref, w_ref, q_ref, k_ref, v_ref):
    mod = mod_ref[0]
    h = _bf(_rms(x_ref[...], gpre_ref[...]) * (1.0 + mod[1:2]) + mod[0:1])
    q_ref[...] = _bf(_nn(h, w_ref[:, :D_MODEL]) * (1.0 / 8.0))
    k_ref[...] = _bf(_nn(h, w_ref[:, D_MODEL:2 * D_MODEL]))
    v_ref[...] = _bf(_nn(h, w_ref[:, 2 * D_MODEL:]))


def _sb_proj(x, mod, gpre, w_qkv):
    tm = ROW_TILE
    full = lambda shape: pl.BlockSpec(shape, lambda i: (0,) * len(shape))
    row_spec = pl.BlockSpec((tm, D_MODEL), lambda i: (i, 0))
    out_sds = jax.ShapeDtypeStruct((TOKENS, D_MODEL), BF16)
    return pl.pallas_call(
        _sb_proj_kernel,
        grid=(TOKENS // tm,),
        in_specs=[row_spec,
                  pl.BlockSpec((1, N_ADA, D_MODEL), lambda i: (i // (SEQ // tm), 0, 0)),
                  full((1, D_MODEL)), full((D_MODEL, 3 * D_MODEL))],
        out_specs=[row_spec] * 3,
        out_shape=[out_sds] * 3,
        compiler_params=_params("parallel"),
        name="sb_qkv_proj",
    )(x, mod, gpre.reshape(1, D_MODEL), _bf(w_qkv))


def _sb_attn_kernel(q_ref, k_ref, v_ref, o_ref):
    blk = ATT_BLOCK
    n_heads = LANES // HEAD_DIM
    qb = pl.program_id(2)
    q = q_ref[...]
    lane = lax.broadcasted_iota(jnp.int32, (blk, LANES), 1)
    head0 = lane < HEAD_DIM
    qh = [jnp.where(head0, q, jnp.zeros_like(q)), jnp.where(head0, jnp.zeros_like(q), q)]
    rowi = lax.broadcasted_iota(jnp.int32, (blk, blk), 0)
    coli = lax.broadcasted_iota(jnp.int32, (blk, blk), 1)
    below = coli < rowi
    later = (rowi > coli).astype(BF16)
    later2 = jnp.concatenate([later, later], axis=0)
    ones2 = jnp.ones((2 * blk, LANES), BF16)

    def block(kb, carry, acc, diagonal):
        start = pl.multiple_of(kb * blk, blk)
        ks = k_ref[pl.ds(start, blk), :]
        vs = v_ref[pl.ds(start, blk), :]
        new_carry, new_acc = [], []
        for hd in range(n_heads):
            z = _nt(qh[hd], ks)
            lb = jnp.minimum(z, 0.0) - jnp.log(1.0 + jnp.exp(-jnp.abs(z)))
            l1m = lb - z
            if diagonal:
                l1m = jnp.where(below, l1m, 0.0)
            hi = _bf(l1m)
            hilo = jnp.concatenate([hi, _bf(l1m - hi.astype(F32))], axis=1)
            tail = _nn(hilo, later2) + jnp.concatenate([carry[hd]] * (blk // LANES), axis=1)
            w = jnp.exp(lb + tail)
            if diagonal:
                w = jnp.where(below, w, 0.0)
            new_acc.append(acc[hd] + _nn(_bf(w), vs))
            new_carry.append(carry[hd] + _nn(hilo, ones2))
        return new_carry, new_acc

    zeros = jnp.zeros((blk, LANES), F32)
    carry, acc = block(qb, [zeros] * n_heads, [zeros] * n_heads, diagonal=True)

    def cmax_of(cr):
        return jnp.max(functools.reduce(jnp.maximum, cr))

    def cond(st):
        kb, _, _, cmax = st
        return jnp.logical_and(kb >= 0, cmax > EXP_UNDERFLOW)

    def body(st):
        kb, cr, ac, _ = st
        cr, ac = block(kb, list(cr), list(ac), diagonal=False)
        return kb - 1, tuple(cr), tuple(ac), cmax_of(cr)

    _, _, acc, _ = lax.while_loop(cond, body, (qb - 1, tuple(carry), tuple(acc), cmax_of(carry)))
    o_ref[...] = jnp.where(head0, acc[0], acc[1])


def _sb_attn(q, k, v):
    blk = ATT_BLOCK
    n_blk = SEQ // blk
    q_spec = pl.BlockSpec((blk, LANES), lambda b, p, i: (b * n_blk + i, p))
    kv_spec = pl.BlockSpec((SEQ, LANES), lambda b, p, i: (b, p))
    return pl.pallas_call(
        _sb_attn_kernel,
        grid=(BATCH, N_PAIRS, n_blk),
        in_specs=[q_spec, kv_spec, kv_spec],
        out_specs=q_spec,
        out_shape=jax.ShapeDtypeStruct((TOKENS, D_MODEL), F32),
        compiler_params=_params("parallel", "parallel", "arbitrary"),
        name="sb_attention",
    )(q, k, v)


def kernel(x, c, ada_w, ada_b, norm_pre_mix, norm_post_mix, norm_pre_ffn, norm_post_ffn, rwkv_mu, rwkv_w_rkv, rwkv_w_w1, rwkv_w_w2, rwkv_w0, rwkv_w_a1, rwkv_w_a2, rwkv_a0, rwkv_w_g1, rwkv_w_g2, rwkv_k_k, rwkv_k_a, rwkv_r_k, rwkv_lnx_g, rwkv_lnx_b, rwkv_w_o, sb_w_qkv, sb_w_o, moe_w_router, moe_router_bias, moe_w_gate, moe_w_up, moe_w_down, moe_ws_gate, moe_ws_up, moe_ws_down):
    mod_all = _ada_mod(c, ada_w, ada_b)
    xt = x.reshape(TOKENS, D_MODEL)
    for layer in range(DEPTH):
        mod = mod_all[layer]
        i = layer // 2
        if layer % 2 == 0:
            r, k, v, lw, a, g = _rwkv_proj(
                xt, mod, norm_pre_mix[layer], rwkv_mu[i], rwkv_w_rkv[i], rwkv_w_w1[i], rwkv_w_w2[i],
                rwkv_w0[i], rwkv_w_a1[i], rwkv_w_a2[i], rwkv_a0[i], rwkv_w_g1[i], rwkv_w_g2[i])
            o = _wkv(r, k, v, lw, a, rwkv_k_k[i], rwkv_k_a[i], rwkv_r_k[i], rwkv_lnx_g[i], rwkv_lnx_b[i])
            w_o = rwkv_w_o[i]
        else:
            q, k, v = _sb_proj(xt, mod, norm_pre_mix[layer], sb_w_qkv[i])
            o, g = _sb_attn(q, k, v), None
            w_o = sb_w_o[i]
        x1, h2, gates = _post(o, g, xt, mod, w_o, norm_post_mix[layer], norm_pre_ffn[layer],
                              moe_w_router[layer], moe_router_bias[layer])
        xt = _moe(layer, h2, gates, x1, mod, norm_post_ffn[layer], moe_w_gate, moe_w_up,
                  moe_w_down, moe_ws_gate, moe_ws_up, moe_ws_down)
    return xt.reshape(BATCH, SEQ, D_MODEL)
```

```python
import functools

import jax
import jax.numpy as jnp
from jax import lax
from jax.experimental import pallas as pl
from jax.experimental.pallas import tpu as pltpu
from jax.experimental.pallas import tpu_sc as plsc

D_MODEL = 1024
BATCH = 2
SEQ = 8192
TOKENS = BATCH * SEQ
DEPTH = 2
HEAD_DIM = 64
N_EXPERTS = 64
TOP_K = 8
D_EXPERT = 256
ROUTED_SCALE = 2.5
RMS_EPS = 1e-6
LNX_EPS = 64e-5
N_ADA = 6

LANES = 128
SUBLANES = 8
N_PAIRS = D_MODEL // LANES
WKV_CHUNK = 64
WKV_PAIRS = 4
ROW_TILE = 256
EXPERT_BLOCK = 256
PACK_HALVES = 2
PACK_WORDS = D_MODEL // (2 * PACK_HALVES)
SC_WINDOW = 128
ATT_BLOCK = 256
EXP_UNDERFLOW = -104.0
VMEM_LIMIT = 56 * 1024 * 1024

F32 = jnp.float32
BF16 = jnp.bfloat16


def _nn(a, b):
    return jnp.dot(a, b, preferred_element_type=F32)


def _nt(a, b):
    return lax.dot_general(a, b, (((1,), (1,)), ((), ())), preferred_element_type=F32)


def _tn(a, b):
    return lax.dot_general(a, b, (((0,), (0,)), ((), ())), preferred_element_type=F32)


def _bf(x):
    return x.astype(BF16)


def _sigmoid(x):
    return 1.0 / (1.0 + jnp.exp(-x))


def _rms(xv, g):
    ms = jnp.mean(xv * xv, axis=-1, keepdims=True)
    return xv * lax.rsqrt(ms + RMS_EPS) * g


def _split3(x):
    hi = x.astype(BF16)
    r1 = x - hi.astype(F32)
    mid = r1.astype(BF16)
    lo = (r1 - mid.astype(F32)).astype(BF16)
    return hi, mid, lo


def _params(*sem):
    return pltpu.CompilerParams(dimension_semantics=sem, vmem_limit_bytes=VMEM_LIMIT)


def _ada_kernel(c_ref, w_ref, b_ref, o_ref):
    cv = c_ref[...]
    o_ref[0] = _nn(cv * _sigmoid(cv), w_ref[0]) + b_ref[0]


def _ada_mod(c, ada_w, ada_b):
    tn = 1536
    c8 = jnp.pad(c, ((0, SUBLANES - BATCH), (0, 0)))
    out = pl.pallas_call(
        _ada_kernel,
        grid=(DEPTH, N_ADA * D_MODEL // tn),
        in_specs=[
            pl.BlockSpec((SUBLANES, D_MODEL), lambda l, j: (0, 0)),
            pl.BlockSpec((1, D_MODEL, tn), lambda l, j: (l, 0, j)),
            pl.BlockSpec((1, 1, tn), lambda l, j: (l, 0, j)),
        ],
        out_specs=pl.BlockSpec((1, SUBLANES, tn), lambda l, j: (l, 0, j)),
        out_shape=jax.ShapeDtypeStruct((DEPTH, SUBLANES, N_ADA * D_MODEL), F32),
        compiler_params=_params("parallel", "parallel"),
        name="ada_mod",
    )(c8, ada_w, ada_b.reshape(DEPTH, 1, N_ADA * D_MODEL))
    return out[:, :BATCH].reshape(DEPTH, BATCH, N_ADA, D_MODEL)


def _rwkv_proj_kernel(x_ref, xp_ref, mod_ref, gpre_ref, mu_ref, wrkv_ref, w1_ref, w2_ref, w0_ref,
                      a1_ref, a2_ref, a0_ref, g1_ref, g2_ref,
                      r_ref, k_ref, v_ref, lw_ref, a_ref, g_ref):
    i = pl.program_id(0)
    mod = mod_ref[0]
    shift, scale = mod[0:1], mod[1:2]
    gpre = gpre_ref[...]

    def modnorm(xv):
        return _rms(xv, gpre) * (1.0 + scale) + shift

    h = modnorm(x_ref[...])
    hp = modnorm(xp_ref[...])[SUBLANES - 1:SUBLANES]
    hp = jnp.where(i % (SEQ // ROW_TILE) == 0, 0.0, hp)
    row = lax.broadcasted_iota(jnp.int32, h.shape, 0)
    hs = jnp.where(row == 0, hp, pltpu.roll(h, 1, 0))
    xx = hs - h
    mu = mu_ref[...]

    def mix(n):
        return _bf(h + xx * mu[n:n + 1])

    r_ref[...] = _nn(mix(0), wrkv_ref[0])
    k_ref[...] = _nn(mix(1), wrkv_ref[1])
    v_ref[...] = _nn(mix(2), wrkv_ref[2])
    wl = w0_ref[...] + _nn(_bf(jnp.tanh(_nn(mix(3), w1_ref[...]))), w2_ref[...])
    nwl = -wl
    sp = jnp.maximum(nwl, 0.0) + jnp.log(1.0 + jnp.exp(-jnp.abs(nwl)))
    lw_ref[...] = -jnp.exp(-sp - 0.5)
    a_ref[...] = _sigmoid(a0_ref[...] + _nn(_bf(_nn(mix(4), a1_ref[...])), a2_ref[...]))
    g_ref[...] = _nn(_bf(_sigmoid(_nn(mix(5), g1_ref[...]))), g2_ref[...])


def _rwkv_proj(x, mod, gpre, mu, w_rkv, w1, w2, w0, a1, a2, a0, g1, g2):
    tm = ROW_TILE
    full = lambda shape: pl.BlockSpec(shape, lambda i: (0,) * len(shape))
    row_spec = pl.BlockSpec((tm, D_MODEL), lambda i: (i, 0))
    out_sds = jax.ShapeDtypeStruct((TOKENS, D_MODEL), F32)
    return pl.pallas_call(
        _rwkv_proj_kernel,
        grid=(TOKENS // tm,),
        in_specs=[
            row_spec,
            pl.BlockSpec((SUBLANES, D_MODEL), lambda i: (jnp.maximum(i * (tm // SUBLANES) - 1, 0), 0)),
            pl.BlockSpec((1, N_ADA, D_MODEL), lambda i: (i // (SEQ // tm), 0, 0)),
            full((1, D_MODEL)), full((6, D_MODEL)), full((3, D_MODEL, D_MODEL)),
            full(w1.shape), full(w2.shape), full((1, D_MODEL)),
            full(a1.shape), full(a2.shape), full((1, D_MODEL)),
            full(g1.shape), full(g2.shape),
        ],
        out_specs=[row_spec] * 6,
        out_shape=[out_sds] * 6,
        compiler_params=_params("parallel"),
        name="rwkv_proj",
    )(x, x, mod, gpre.reshape(1, D_MODEL), mu, _bf(w_rkv), _bf(w1), _bf(w2),
      w0.reshape(1, D_MODEL), _bf(a1), _bf(a2), a0.reshape(1, D_MODEL), _bf(g1), _bf(g2))


def _wkv_pair(r, k, v, lw, a, k_k, k_a, r_k, ln_g, ln_b, s_ref, o_ref):
    c = WKV_CHUNK
    lane = lax.broadcasted_iota(jnp.int32, (c, LANES), 1)
    head0 = lane < HEAD_DIM
    rowi = lax.broadcasted_iota(jnp.int32, (LANES, LANES), 0)
    coli = lax.broadcasted_iota(jnp.int32, (LANES, LANES), 1)
    rt, ct = rowi & (c - 1), coli & (c - 1)
    strict, incl, eye = rt > ct, rt >= ct, rowi == coli
    tril = (lax.broadcasted_iota(jnp.int32, (c, c), 0)
            >= lax.broadcasted_iota(jnp.int32, (c, c), 1)).astype(BF16)

    def headsum(x):
        s0 = jnp.sum(jnp.where(head0, x, 0.0), axis=-1, keepdims=True)
        s1 = jnp.sum(jnp.where(head0, 0.0, x), axis=-1, keepdims=True)
        return jnp.where(head0, s0, s1)

    def stack(x):
        return jnp.concatenate([jnp.where(head0, x, 0.0), jnp.where(head0, 0.0, x)], axis=0)

    kk = k * k_k
    kk = kk / jnp.maximum(jnp.sqrt(headsum(kk * kk)), 1e-12)
    kf = k * (1.0 + (a - 1.0) * k_a)
    bvec = kk * a
    yield

    cum = _nn(tril, jnp.concatenate(_split3(lw), axis=1))
    cum = cum[:, :LANES] + cum[:, LANES:2 * LANES] + cum[:, 2 * LANES:]
    cum_end = cum[c - 1:c]
    w_inv = jnp.exp(-cum)
    w_rem = jnp.exp(cum_end - cum)
    a_st = stack(-kk * jnp.exp(cum - lw))
    r_st = stack(r * jnp.exp(cum))
    b_st = stack(bvec * w_inv)
    k_st = stack(kf * w_inv)
    bp_st = _bf(stack(bvec * w_rem))
    kp_st = _bf(stack(kf * w_rem))
    v_st = _bf(stack(v))
    yield

    sc = _nt(_bf(jnp.concatenate([a_st, r_st], axis=0)), _bf(jnp.concatenate([b_st, k_st], axis=0)))
    l_ab = jnp.where(strict, sc[:LANES, :LANES], 0.0)
    l_ak = jnp.where(strict, sc[:LANES, LANES:], 0.0)
    l_rb = jnp.where(incl, sc[LANES:, :LANES], 0.0)
    l_rk = jnp.where(incl, sc[LANES:, LANES:], 0.0)
    yield

    tinv = jnp.where(eye, 1.0, l_ab)
    pw = l_ab
    for _ in range(c.bit_length() - 2):
        pwb = _bf(pw)
        pw = _nn(pwb, pwb)
        tinv = tinv + _nn(_bf(tinv), _bf(pw))
        yield

    au = _nn(_bf(tinv), _bf(jnp.concatenate([a_st, _nn(_bf(l_ak), v_st)], axis=1)))
    aub = _bf(au)
    yield
    rb = _nn(_bf(l_rb), aub)
    gh = _tn(bp_st, aub)
    r_hat = r_st + rb[:, :LANES]
    o0 = rb[:, LANES:] + _nn(_bf(l_rk), v_st)
    g_mat = jnp.where(eye, jnp.exp(cum_end), 0.0) + gh[:, :LANES]
    h_mat = gh[:, LANES:] + _tn(kp_st, v_st)
    yield

    os = _nn(_bf(jnp.concatenate([r_hat, g_mat], axis=0)), _bf(s_ref[...]))
    s_ref[...] = os[LANES:] + h_mat
    o_st = os[:LANES] + o0
    o = o_st[:c] + o_st[c:]
    yield

    mean = headsum(o) * (1.0 / HEAD_DIM)
    oc = o - mean
    var = headsum(oc * oc) * (1.0 / HEAD_DIM)
    on = oc * lax.rsqrt(var + LNX_EPS) * ln_g + ln_b
    o_ref[...] = on + headsum(r * kf * r_k) * v


def _wkv_kernel(r_ref, k_ref, v_ref, lw_ref, a_ref, kk_ref, ka_ref, rk_ref, lg_ref, lb_ref,
                o_ref, s_ref):
    @pl.when(pl.program_id(2) == 0)
    def _():
        s_ref[...] = jnp.zeros_like(s_ref)

    def pair(p):
        sl = pl.ds(p * LANES, LANES)
        ins = [ref[:, sl] for ref in (r_ref, k_ref, v_ref, lw_ref, a_ref, kk_ref, ka_ref, rk_ref, lg_ref, lb_ref)]
        return _wkv_pair(*ins, s_ref.at[p], o_ref.at[:, sl])

    stages = [pair(p) for p in range(WKV_PAIRS)]
    while stages:
        stages = [g for g in stages if next(g, True) is None]


def _wkv(r, k, v, lw, a, k_k, k_a, r_k, lnx_g, lnx_b):
    n_chunks = SEQ // WKV_CHUNK
    width = WKV_PAIRS * LANES
    tok = pl.BlockSpec((WKV_CHUNK, width), lambda b, p, c: (b * n_chunks + c, p))
    par = pl.BlockSpec((1, width), lambda b, p, c: (0, p))
    vec = lambda t: t.reshape(1, D_MODEL)
    return pl.pallas_call(
        _wkv_kernel,
        grid=(BATCH, N_PAIRS // WKV_PAIRS, n_chunks),
        in_specs=[tok] * 5 + [par] * 5,
        out_specs=tok,
        out_shape=jax.ShapeDtypeStruct((TOKENS, D_MODEL), F32),
        scratch_shapes=[pltpu.VMEM((WKV_PAIRS, LANES, LANES), F32)],
        compiler_params=_params("parallel", "parallel", "arbitrary"),
        name="wkv_scan",
    )(r, k, v, lw, a, vec(k_k), vec(k_a), vec(r_k), vec(lnx_g), vec(lnx_b))


def _pack_rows(x):
    bits = lax.bitcast_convert_type(_bf(x).astype(F32), jnp.uint32)
    halves = []
    for h in range(PACK_HALVES):
        lo = bits[:, (2 * h) * PACK_WORDS:(2 * h + 1) * PACK_WORDS]
        hi = bits[:, (2 * h + 1) * PACK_WORDS:(2 * h + 2) * PACK_WORDS]
        halves.append(lax.bitcast_convert_type((lo >> 16) | (hi & jnp.uint32(0xFFFF0000)), jnp.int32))
    return halves


def _unpack_rows(halves):
    parts = []
    for w in halves:
        u = lax.bitcast_convert_type(w, jnp.uint32)
        parts.append(lax.bitcast_convert_type(u << 16, F32))
        parts.append(lax.bitcast_convert_type(u & jnp.uint32(0xFFFF0000), F32))
    return jnp.concatenate(parts, axis=1)


def _post_kernel(*refs, has_gate):
    if has_gate:
        o_ref, g_ref, *refs = refs
    else:
        o_ref, *refs = refs
    (x_ref, mod_ref, wo_ref, gpost_ref, gpre_ref, wr_hi_ref, wr_lo_ref, bias_ref,
     x1_ref, h2p_ref, gk_ref, ek_ref, rk_ref, cnt_ref, run_ref) = refs
    tm = ROW_TILE

    @pl.when(pl.program_id(0) == 0)
    def _():
        run_ref[...] = jnp.zeros_like(run_ref)

    mod = mod_ref[0]
    gate_m, shift_f, scale_f = mod[2:3], mod[3:4], mod[4:5]
    o = o_ref[...]
    if has_gate:
        o = o * g_ref[...]
    y = _nn(_bf(o), wo_ref[...])
    x1 = x_ref[...] + gate_m * _rms(y, gpost_ref[...])
    x1_ref[...] = x1
    h2 = _rms(x1, gpre_ref[...]) * (1.0 + scale_f) + shift_f
    for h, words in enumerate(_pack_rows(h2)):
        h2p_ref[h] = words

    hi = _bf(h2)
    lo = _bf(h2 - hi.astype(F32))
    logits = _nn(hi, wr_hi_ref[...]) + (_nn(lo, wr_hi_ref[...]) + _nn(hi, wr_lo_ref[...]))
    scores = _sigmoid(logits)
    lane = lax.broadcasted_iota(jnp.int32, scores.shape, 1)
    work = jnp.where(lane < N_EXPERTS, scores + bias_ref[...], -jnp.inf)
    picked = jnp.zeros_like(scores)
    chosen = jnp.zeros_like(scores)
    hits, ids = [], []
    for _ in range(TOP_K):
        best = jnp.max(work, axis=-1, keepdims=True)
        idx = jnp.min(jnp.where(work == best, lane, LANES), axis=-1, keepdims=True)
        hit = lane == idx
        picked = jnp.where(hit, scores, picked)
        chosen = jnp.where(hit, 1.0, chosen)
        work = jnp.where(hit, -jnp.inf, work)
        hits.append(hit)
        ids.append(idx)
    gates = ROUTED_SCALE * picked / jnp.sum(picked, axis=-1, keepdims=True)

    tril = (lax.broadcasted_iota(jnp.int32, (tm, tm), 0)
            >= lax.broadcasted_iota(jnp.int32, (tm, tm), 1)).astype(BF16)
    incl = _nn(tril, _bf(chosen))
    before = incl - chosen + run_ref[...]
    run_ref[...] += incl[tm - 1:tm]
    cnt_ref[...] = run_ref[...]

    gk = jnp.zeros_like(scores)
    ek = jnp.zeros(scores.shape, jnp.int32)
    rk = jnp.zeros(scores.shape, jnp.int32)
    for j in range(TOP_K):
        gcol = jnp.sum(jnp.where(hits[j], gates, 0.0), axis=-1, keepdims=True)
        rcol = jnp.sum(jnp.where(hits[j], before, 0.0), axis=-1, keepdims=True)
        gk = jnp.where(lane == j, gcol, gk)
        ek = jnp.where(lane == j, ids[j], ek)
        rk = jnp.where(lane == j, rcol.astype(jnp.int32), rk)
    gk_ref[...] = gk
    ek_ref[...] = ek
    rk_ref[...] = rk


def _post(o, g, x, mod, w_o, gpost, gpre, w_router, router_bias):
    tm = ROW_TILE
    full = lambda shape: pl.BlockSpec(shape, lambda i: (0,) * len(shape))
    row_spec = pl.BlockSpec((tm, D_MODEL), lambda i: (i, 0))
    lane_spec = pl.BlockSpec((tm, LANES), lambda i: (i, 0))
    wr = jnp.pad(w_router, ((0, 0), (0, LANES - N_EXPERTS)))
    wr_hi = _bf(wr)
    wr_lo = _bf(wr - wr_hi.astype(F32))
    bias = jnp.pad(router_bias, (0, LANES - N_EXPERTS)).reshape(1, LANES)
    has_gate = g is not None
    acts = [o, g] if has_gate else [o]
    return pl.pallas_call(
        functools.partial(_post_kernel, has_gate=has_gate),
        grid=(TOKENS // tm,),
        in_specs=[row_spec] * (len(acts) + 1) + [
            pl.BlockSpec((1, N_ADA, D_MODEL), lambda i: (i // (SEQ // tm), 0, 0)),
            full((D_MODEL, D_MODEL)), full((1, D_MODEL)), full((1, D_MODEL)),
            full((D_MODEL, LANES)), full((D_MODEL, LANES)), full((1, LANES)),
        ],
        out_specs=[row_spec, pl.BlockSpec((PACK_HALVES, tm, PACK_WORDS), lambda i: (0, i, 0)),
                   lane_spec, lane_spec, lane_spec, full((1, LANES))],
        out_shape=[jax.ShapeDtypeStruct((TOKENS, D_MODEL), F32),
                   jax.ShapeDtypeStruct((PACK_HALVES, TOKENS, PACK_WORDS), jnp.int32),
                   jax.ShapeDtypeStruct((TOKENS, LANES), F32),
                   jax.ShapeDtypeStruct((TOKENS, LANES), jnp.int32),
                   jax.ShapeDtypeStruct((TOKENS, LANES), jnp.int32),
                   jax.ShapeDtypeStruct((1, LANES), F32)],
        scratch_shapes=[pltpu.VMEM((1, LANES), F32)],
        compiler_params=_params("arbitrary"),
        name="mixer_out_router",
    )(*acts, x, mod, _bf(w_o), gpost.reshape(1, D_MODEL), gpre.reshape(1, D_MODEL),
      wr_hi, wr_lo, bias)


def _sc_mesh():
    return plsc.VectorSubcoreMesh(core_axis_name="core", subcore_axis_name="subcore")


def _sc_scatter_rows(rows, idx, n_out):
    n_src = rows.shape[1] // SC_WINDOW

    @pl.kernel(out_type=jax.ShapeDtypeStruct((PACK_HALVES, n_out, PACK_WORDS), rows.dtype),
               mesh=_sc_mesh(), scratch_types=[], name="sc_dispatch_rows")
    def scatter(x_hbm, i_hbm, o_hbm):
        for h in range(PACK_HALVES):
            def body(x_vmem, i_vmem, h=h):
                pltpu.sync_copy(x_vmem, o_hbm.at[h].at[i_vmem.at[0]])

            pltpu.emit_pipeline(
                body, grid=(idx.shape[0] // SC_WINDOW,),
                in_specs=[pl.BlockSpec((SC_WINDOW, PACK_WORDS), index_map=lambda i: (i % n_src, 0)),
                          pl.BlockSpec((1, SC_WINDOW), index_map=lambda i: (0, i))],
                out_specs=[],
                core_axis_name=("core", "subcore"),
                dimension_semantics=(pltpu.PARALLEL,),
            )(x_hbm.at[h], i_hbm)

    return scatter(rows, idx.reshape(1, -1))


def _sc_gather_rows(table, idx):
    n = idx.shape[0]

    @pl.kernel(out_type=jax.ShapeDtypeStruct((PACK_HALVES, n, PACK_WORDS), table.dtype),
               mesh=_sc_mesh(), scratch_types=[], name="sc_collect_rows")
    def gather(x_hbm, i_hbm, o_hbm):
        for h in range(PACK_HALVES):
            def body(i_vmem, o_vmem, h=h):
                pltpu.sync_copy(x_hbm.at[h].at[i_vmem.at[0]], o_vmem)

            pltpu.emit_pipeline(
                body, grid=(n // SC_WINDOW,),
                in_specs=[pl.BlockSpec((1, SC_WINDOW), index_map=lambda i: (0, i))],
                out_specs=[pl.BlockSpec((SC_WINDOW, PACK_WORDS), index_map=lambda i: (i, 0))],
                core_axis_name=("core", "subcore"),
                dimension_semantics=(pltpu.PARALLEL,),
            )(i_hbm, o_hbm.at[h])

    return gather(table, idx.reshape(1, n))


def _ffn(x, wg, wu, wd):
    up = _nn(x, _bf(wu))
    gt = _nn(x, _bf(wg))
    return _nn(_bf(gt * _sigmoid(gt) * up), _bf(wd))


def _expert_kernel(be_ref, nu_ref, xs_ref, wg_ref, wu_ref, wd_ref, ys_ref):
    @pl.when(pl.program_id(0) < nu_ref[0])
    def _():
        x = _bf(_unpack_rows([xs_ref[h] for h in range(PACK_HALVES)]))
        y = _ffn(x, wg_ref[0, 0], wu_ref[0, 0], wd_ref[0, 0])
        for h, words in enumerate(_pack_rows(y)):
            ys_ref[h] = words


def _experts(layer, xs, blk_expert, n_used, w_gate, w_up, w_down):
    n_blocks = xs.shape[1] // EXPERT_BLOCK
    row_spec = pl.BlockSpec((PACK_HALVES, EXPERT_BLOCK, PACK_WORDS), lambda b, be, nu: (0, b, 0))
    w_spec = lambda shape: pl.BlockSpec((1, 1) + shape, lambda b, be, nu: (layer, be[b], 0, 0))
    return pl.pallas_call(
        _expert_kernel,
        grid_spec=pltpu.PrefetchScalarGridSpec(
            num_scalar_prefetch=2,
            grid=(n_blocks,),
            in_specs=[row_spec, w_spec((D_MODEL, D_EXPERT)), w_spec((D_MODEL, D_EXPERT)),
                      w_spec((D_EXPERT, D_MODEL))],
            out_specs=row_spec,
        ),
        out_shape=jax.ShapeDtypeStruct(xs.shape, xs.dtype),
        compiler_params=_params("arbitrary"),
        name="moe_experts",
    )(blk_expert, n_used, xs, w_gate, w_up, w_down)


def _combine_kernel(*refs):
    g_refs = refs[:TOP_K]
    gk_ref, h2p_ref, x_ref, mod_ref, gpost_ref, sg_ref, su_ref, sd_ref, o_ref = refs[TOP_K:]
    h2 = _bf(_unpack_rows([h2p_ref[h] for h in range(PACK_HALVES)]))
    acc = _ffn(h2, sg_ref[0], su_ref[0], sd_ref[0])
    gk = gk_ref[...]
    lane = lax.broadcasted_iota(jnp.int32, gk.shape, 1)
    for j in range(TOP_K):
        gcol = jnp.sum(jnp.where(lane == j, gk, 0.0), axis=-1, keepdims=True)
        acc = acc + gcol * _unpack_rows([g_refs[j][h] for h in range(PACK_HALVES)])
    gate_f = mod_ref[0][5:6]
    o_ref[...] = x_ref[...] + gate_f * _rms(acc, gpost_ref[...])


def _combine(layer, picked, gk, h2p, x1, mod, gpost, ws_gate, ws_up, ws_down):
    tm = ROW_TILE
    n_tiles = TOKENS // tm
    row_spec = pl.BlockSpec((tm, D_MODEL), lambda i: (i, 0))
    pick_spec = lambda j: pl.BlockSpec((PACK_HALVES, tm, PACK_WORDS), lambda i: (0, j * n_tiles + i, 0))
    return pl.pallas_call(
        _combine_kernel,
        grid=(n_tiles,),
        in_specs=[pick_spec(j) for j in range(TOP_K)] + [
            pl.BlockSpec((tm, LANES), lambda i: (i, 0)),
            pl.BlockSpec((PACK_HALVES, tm, PACK_WORDS), lambda i: (0, i, 0)),
            row_spec,
            pl.BlockSpec((1, N_ADA, D_MODEL), lambda i: (i // (SEQ // tm), 0, 0)),
            pl.BlockSpec((1, D_MODEL), lambda i: (0, 0)),
            pl.BlockSpec((1, D_MODEL, D_EXPERT), lambda i: (layer, 0, 0)),
            pl.BlockSpec((1, D_MODEL, D_EXPERT), lambda i: (layer, 0, 0)),
            pl.BlockSpec((1, D_EXPERT, D_MODEL), lambda i: (layer, 0, 0)),
        ],
        out_specs=row_spec,
        out_shape=jax.ShapeDtypeStruct((TOKENS, D_MODEL), F32),
        compiler_params=_params("parallel"),
        name="moe_combine",
    )(*([picked] * TOP_K), gk, h2p, x1, mod, gpost.reshape(1, D_MODEL), ws_gate, ws_up, ws_down)


def _moe(layer, x1, h2p, gk, ek, rk, counts, mod, gpost, w_gate, w_up, w_down, ws_gate, ws_up, ws_down):
    cnt = counts[0, :N_EXPERTS].astype(jnp.int32)
    padded = (cnt + EXPERT_BLOCK - 1) // EXPERT_BLOCK * EXPERT_BLOCK
    pend = jnp.cumsum(padded)
    dest = jnp.take(pend - padded, ek[:, :TOP_K]) + rk[:, :TOP_K]
    dest = dest.T.reshape(-1)
    n_blocks = (TOKENS * TOP_K + N_EXPERTS * (EXPERT_BLOCK - 1)) // EXPERT_BLOCK + 1
    blk_start = jnp.arange(n_blocks, dtype=jnp.int32) * EXPERT_BLOCK
    blk_expert = jnp.minimum(jnp.searchsorted(pend, blk_start, side="right"), N_EXPERTS - 1).astype(jnp.int32)
    n_used = (pend[-1:] // EXPERT_BLOCK).astype(jnp.int32)

    xs = _sc_scatter_rows(h2p, dest, n_blocks * EXPERT_BLOCK)
    ys = _experts(layer, xs, blk_expert, n_used, w_gate, w_up, w_down)
    picked = _sc_gather_rows(ys, dest)
    return _combine(layer, picked, gk, h2p, x1, mod, gpost, ws_gate, ws_up, ws_down)


def _sb_proj_kernel(x_ref, mod_ref, gpre_ref, w_ref, q_ref, k_ref, v_ref):
    mod = mod_ref[0]
    h = _bf(_rms(x_ref[...], gpre_ref[...]) * (1.0 + mod[1:2]) + mod[0:1])
    q_ref[...] = _bf(_nn(h, w_ref[:, :D_MODEL]) * (1.0 / 8.0))
    k_ref[...] = _bf(_nn(h, w_ref[:, D_MODEL:2 * D_MODEL]))
    v_ref[...] = _bf(_nn(h, w_ref[:, 2 * D_MODEL:]))


def _sb_proj(x, mod, gpre, w_qkv):
    tm = ROW_TILE
    full = lambda shape: pl.BlockSpec(shape, lambda i: (0,) * len(shape))
    row_spec = pl.BlockSpec((tm, D_MODEL), lambda i: (i, 0))
    out_sds = jax.ShapeDtypeStruct((TOKENS, D_MODEL), BF16)
    return pl.pallas_call(
        _sb_proj_kernel,
        grid=(TOKENS // tm,),
        in_specs=[row_spec,
                  pl.BlockSpec((1, N_ADA, D_MODEL), lambda i: (i // (SEQ // tm), 0, 0)),
                  full((1, D_MODEL)), full((D_MODEL, 3 * D_MODEL))],
        out_specs=[row_spec] * 3,
        out_shape=[out_sds] * 3,
        compiler_params=_params("parallel"),
        name="sb_qkv_proj",
    )(x, mod, gpre.reshape(1, D_MODEL), _bf(w_qkv))


def _sb_attn_kernel(q_ref, k_ref, v_ref, o_ref):
    blk = ATT_BLOCK
    n_heads = LANES // HEAD_DIM
    qb = pl.program_id(2)
    q = q_ref[...]
    lane = lax.broadcasted_iota(jnp.int32, (blk, LANES), 1)
    head0 = lane < HEAD_DIM
    qh = [jnp.where(head0, q, jnp.zeros_like(q)), jnp.where(head0, jnp.zeros_like(q), q)]
    rowi = lax.broadcasted_iota(jnp.int32, (blk, blk), 0)
    coli = lax.broadcasted_iota(jnp.int32, (blk, blk), 1)
    below = coli < rowi
    later = (rowi > coli).astype(BF16)
    later2 = jnp.concatenate([later, later], axis=0)
    ones2 = jnp.ones((2 * blk, LANES), BF16)

    def block(kb, carry, acc, diagonal):
        start = pl.multiple_of(kb * blk, blk)
        ks = k_ref[pl.ds(start, blk), :]
        vs = v_ref[pl.ds(start, blk), :]
        new_carry, new_acc = [], []
        for hd in range(n_heads):
            z = _nt(qh[hd], ks)
            lb = jnp.minimum(z, 0.0) - jnp.log(1.0 + jnp.exp(-jnp.abs(z)))
            l1m = lb - z
            if diagonal:
                l1m = jnp.where(below, l1m, 0.0)
            hi = _bf(l1m)
            hilo = jnp.concatenate([hi, _bf(l1m - hi.astype(F32))], axis=1)
            tail = _nn(hilo, later2) + jnp.concatenate([carry[hd]] * (blk // LANES), axis=1)
            w = jnp.exp(lb + tail)
            if diagonal:
                w = jnp.where(below, w, 0.0)
            new_acc.append(acc[hd] + _nn(_bf(w), vs))
            new_carry.append(carry[hd] + _nn(hilo, ones2))
        return new_carry, new_acc

    zeros = jnp.zeros((blk, LANES), F32)
    carry, acc = block(qb, [zeros] * n_heads, [zeros] * n_heads, diagonal=True)

    def cmax_of(cr):
        return jnp.max(functools.reduce(jnp.maximum, cr))

    def cond(st):
        kb, _, _, cmax = st
        return jnp.logical_and(kb >= 0, cmax > EXP_UNDERFLOW)

    def body(st):
        kb, cr, ac, _ = st
        cr, ac = block(kb, list(cr), list(ac), diagonal=False)
        return kb - 1, tuple(cr), tuple(ac), cmax_of(cr)

    _, _, acc, _ = lax.while_loop(cond, body, (qb - 1, tuple(carry), tuple(acc), cmax_of(carry)))
    o_ref[...] = jnp.where(head0, acc[0], acc[1])


def _sb_attn(q, k, v):
    blk = ATT_BLOCK
    n_blk = SEQ // blk
    q_spec = pl.BlockSpec((blk, LANES), lambda b, p, i: (b * n_blk + i, p))
    kv_spec = pl.BlockSpec((SEQ, LANES), lambda b, p, i: (b, p))
    return pl.pallas_call(
        _sb_attn_kernel,
        grid=(BATCH, N_PAIRS, n_blk),
        in_specs=[q_spec, kv_spec, kv_spec],
        out_specs=q_spec,
        out_shape=jax.ShapeDtypeStruct((TOKENS, D_MODEL), F32),
        compiler_params=_params("parallel", "parallel", "arbitrary"),
        name="sb_attention",
    )(q, k, v)


def kernel(x, c, ada_w, ada_b, norm_pre_mix, norm_post_mix, norm_pre_ffn, norm_post_ffn, rwkv_mu, rwkv_w_rkv, rwkv_w_w1, rwkv_w_w2, rwkv_w0, rwkv_w_a1, rwkv_w_a2, rwkv_a0, rwkv_w_g1, rwkv_w_g2, rwkv_k_k, rwkv_k_a, rwkv_r_k, rwkv_lnx_g, rwkv_lnx_b, rwkv_w_o, sb_w_qkv, sb_w_o, moe_w_router, moe_router_bias, moe_w_gate, moe_w_up, moe_w_down, moe_ws_gate, moe_ws_up, moe_ws_down):
    mod_all = _ada_mod(c, ada_w, ada_b)
    xt = x.reshape(TOKENS, D_MODEL)
    for layer in range(DEPTH):
        mod = mod_all[layer]
        i = layer // 2
        if layer % 2 == 0:
            r, k, v, lw, a, g = _rwkv_proj(
                xt, mod, norm_pre_mix[layer], rwkv_mu[i], rwkv_w_rkv[i], rwkv_w_w1[i], rwkv_w_w2[i],
                rwkv_w0[i], rwkv_w_a1[i], rwkv_w_a2[i], rwkv_a0[i], rwkv_w_g1[i], rwkv_w_g2[i])
            o = _wkv(r, k, v, lw, a, rwkv_k_k[i], rwkv_k_a[i], rwkv_r_k[i], rwkv_lnx_g[i], rwkv_lnx_b[i])
            w_o = rwkv_w_o[i]
        else:
            q, k, v = _sb_proj(xt, mod, norm_pre_mix[layer], sb_w_qkv[i])
            o, g = _sb_attn(q, k, v), None
            w_o = sb_w_o[i]
        x1, h2p, gk, ek, rk, counts = _post(o, g, xt, mod, w_o, norm_post_mix[layer], norm_pre_ffn[layer],
                                            moe_w_router[layer], moe_router_bias[layer])
        xt = _moe(layer, x1, h2p, gk, ek, rk, counts, mod, norm_post_ffn[layer], moe_w_gate, moe_w_up,
                  moe_w_down, moe_ws_gate, moe_ws_up, moe_ws_down)
    return xt.reshape(BATCH, SEQ, D_MODEL)
```

```python
import functools

import jax
import jax.numpy as jnp
from jax import lax
from jax.experimental import pallas as pl
from jax.experimental.pallas import tpu as pltpu
from jax.experimental.pallas import tpu_sc as plsc

D_MODEL = 1024
BATCH = 2
SEQ = 8192
TOKENS = BATCH * SEQ
DEPTH = 2
HEAD_DIM = 64
N_EXPERTS = 64
TOP_K = 8
D_EXPERT = 256
ROUTED_SCALE = 2.5
RMS_EPS = 1e-6
LNX_EPS = 64e-5
N_ADA = 6

LANES = 128
SUBLANES = 8
N_PAIRS = D_MODEL // LANES
WKV_CHUNK = 64
WKV_PAIRS = 8
ROW_TILE = 256
EXPERT_BLOCK = 512
PACK_HALVES = 2
PACK_WORDS = D_MODEL // (2 * PACK_HALVES)
SC_WINDOW = 128
ATT_BLOCK = 256
EXP_UNDERFLOW = -106.0
VMEM_LIMIT = 56 * 1024 * 1024

F32 = jnp.float32
BF16 = jnp.bfloat16


def _nn(a, b):
    return jnp.dot(a, b, preferred_element_type=F32)


def _nt(a, b):
    return lax.dot_general(a, b, (((1,), (1,)), ((), ())), preferred_element_type=F32)


def _tn(a, b):
    return lax.dot_general(a, b, (((0,), (0,)), ((), ())), preferred_element_type=F32)


def _bf(x):
    return x.astype(BF16)


def _sigmoid(x):
    return 1.0 / (1.0 + jnp.exp(-x))


def _rms(xv, g):
    ms = jnp.mean(xv * xv, axis=-1, keepdims=True)
    return xv * lax.rsqrt(ms + RMS_EPS) * g


def _split3(x):
    hi = x.astype(BF16)
    r1 = x - hi.astype(F32)
    mid = r1.astype(BF16)
    lo = (r1 - mid.astype(F32)).astype(BF16)
    return hi, mid, lo


def _params(*sem):
    return pltpu.CompilerParams(dimension_semantics=sem, vmem_limit_bytes=VMEM_LIMIT)


def _ada_kernel(c_ref, w_ref, b_ref, o_ref):
    cv = c_ref[...]
    o_ref[0] = _nn(cv * _sigmoid(cv), w_ref[0]) + b_ref[0]


def _ada_mod(c, ada_w, ada_b):
    tn = 1536
    c8 = jnp.pad(c, ((0, SUBLANES - BATCH), (0, 0)))
    out = pl.pallas_call(
        _ada_kernel,
        grid=(DEPTH, N_ADA * D_MODEL // tn),
        in_specs=[
            pl.BlockSpec((SUBLANES, D_MODEL), lambda l, j: (0, 0)),
            pl.BlockSpec((1, D_MODEL, tn), lambda l, j: (l, 0, j)),
            pl.BlockSpec((1, 1, tn), lambda l, j: (l, 0, j)),
        ],
        out_specs=pl.BlockSpec((1, SUBLANES, tn), lambda l, j: (l, 0, j)),
        out_shape=jax.ShapeDtypeStruct((DEPTH, SUBLANES, N_ADA * D_MODEL), F32),
        compiler_params=_params("parallel", "parallel"),
        name="ada_mod",
    )(c8, ada_w, ada_b.reshape(DEPTH, 1, N_ADA * D_MODEL))
    return out[:, :BATCH].reshape(DEPTH, BATCH, N_ADA, D_MODEL)


def _rwkv_proj_kernel(x_ref, xp_ref, mod_ref, gpre_ref, mu_ref, wrkv_ref, w1_ref, w2_ref, w0_ref,
                      a1_ref, a2_ref, a0_ref, g1_ref, g2_ref,
                      r_ref, k_ref, v_ref, lw_ref, a_ref, g_ref):
    i = pl.program_id(0)
    mod = mod_ref[0]
    shift, scale = mod[0:1], mod[1:2]
    gpre = gpre_ref[...]

    def modnorm(xv):
        return _rms(xv, gpre) * (1.0 + scale) + shift

    h = modnorm(x_ref[...])
    hp = modnorm(xp_ref[...])[SUBLANES - 1:SUBLANES]
    hp = jnp.where(i % (SEQ // ROW_TILE) == 0, 0.0, hp)
    row = lax.broadcasted_iota(jnp.int32, h.shape, 0)
    hs = jnp.where(row == 0, hp, pltpu.roll(h, 1, 0))
    xx = hs - h
    mu = mu_ref[...]

    def mix(n):
        return _bf(h + xx * mu[n:n + 1])

    r_ref[...] = _nn(mix(0), wrkv_ref[0])
    k_ref[...] = _nn(mix(1), wrkv_ref[1])
    v_ref[...] = _nn(mix(2), wrkv_ref[2])
    wl = w0_ref[...] + _nn(_bf(jnp.tanh(_nn(mix(3), w1_ref[...]))), w2_ref[...])
    nwl = -wl
    sp = jnp.maximum(nwl, 0.0) + jnp.log(1.0 + jnp.exp(-jnp.abs(nwl)))
    lw_ref[...] = -jnp.exp(-sp - 0.5)
    a_ref[...] = _sigmoid(a0_ref[...] + _nn(_bf(_nn(mix(4), a1_ref[...])), a2_ref[...]))
    g_ref[...] = _nn(_bf(_sigmoid(_nn(mix(5), g1_ref[...]))), g2_ref[...])


def _rwkv_proj(x, mod, gpre, mu, w_rkv, w1, w2, w0, a1, a2, a0, g1, g2):
    tm = ROW_TILE
    full = lambda shape: pl.BlockSpec(shape, lambda i: (0,) * len(shape))
    row_spec = pl.BlockSpec((tm, D_MODEL), lambda i: (i, 0))
    out_sds = jax.ShapeDtypeStruct((TOKENS, D_MODEL), F32)
    return pl.pallas_call(
        _rwkv_proj_kernel,
        grid=(TOKENS // tm,),
        in_specs=[
            row_spec,
            pl.BlockSpec((SUBLANES, D_MODEL), lambda i: (jnp.maximum(i * (tm // SUBLANES) - 1, 0), 0)),
            pl.BlockSpec((1, N_ADA, D_MODEL), lambda i: (i // (SEQ // tm), 0, 0)),
            full((1, D_MODEL)), full((6, D_MODEL)), full((3, D_MODEL, D_MODEL)),
            full(w1.shape), full(w2.shape), full((1, D_MODEL)),
            full(a1.shape), full(a2.shape), full((1, D_MODEL)),
            full(g1.shape), full(g2.shape),
        ],
        out_specs=[row_spec] * 6,
        out_shape=[out_sds] * 6,
        compiler_params=_params("parallel"),
        name="rwkv_proj",
    )(x, x, mod, gpre.reshape(1, D_MODEL), mu, _bf(w_rkv), _bf(w1), _bf(w2),
      w0.reshape(1, D_MODEL), _bf(a1), _bf(a2), a0.reshape(1, D_MODEL), _bf(g1), _bf(g2))


def _wkv_pair(r, k, v, lw, a, k_k, k_a, r_k, ln_g, ln_b, s_ref, o_ref):
    c = WKV_CHUNK
    lane = lax.broadcasted_iota(jnp.int32, (c, LANES), 1)
    head0 = lane < HEAD_DIM
    rowi = lax.broadcasted_iota(jnp.int32, (LANES, LANES), 0)
    coli = lax.broadcasted_iota(jnp.int32, (LANES, LANES), 1)
    rt, ct = rowi & (c - 1), coli & (c - 1)
    strict, incl, eye = rt > ct, rt >= ct, rowi == coli
    tril = (lax.broadcasted_iota(jnp.int32, (c, c), 0)
            >= lax.broadcasted_iota(jnp.int32, (c, c), 1)).astype(BF16)

    def headsum(x):
        s0 = jnp.sum(jnp.where(head0, x, 0.0), axis=-1, keepdims=True)
        s1 = jnp.sum(jnp.where(head0, 0.0, x), axis=-1, keepdims=True)
        return jnp.where(head0, s0, s1)

    def stack(x):
        return jnp.concatenate([jnp.where(head0, x, 0.0), jnp.where(head0, 0.0, x)], axis=0)

    kk = k * k_k
    kk = kk / jnp.maximum(jnp.sqrt(headsum(kk * kk)), 1e-12)
    kf = k * (1.0 + (a - 1.0) * k_a)
    bvec = kk * a
    yield

    cum = _nn(tril, jnp.concatenate(_split3(lw), axis=1))
    cum = cum[:, :LANES] + cum[:, LANES:2 * LANES] + cum[:, 2 * LANES:]
    cum_end = cum[c - 1:c]
    w_inv = jnp.exp(-cum)
    w_rem = jnp.exp(cum_end - cum)
    a_st = stack(-kk * jnp.exp(cum - lw))
    r_st = stack(r * jnp.exp(cum))
    b_st = stack(bvec * w_inv)
    k_st = stack(kf * w_inv)
    bp_st = _bf(stack(bvec * w_rem))
    kp_st = _bf(stack(kf * w_rem))
    v_st = _bf(stack(v))
    yield

    sc = _nt(_bf(jnp.concatenate([a_st, r_st], axis=0)), _bf(jnp.concatenate([b_st, k_st], axis=0)))
    l_ab = jnp.where(strict, sc[:LANES, :LANES], 0.0)
    l_ak = jnp.where(strict, sc[:LANES, LANES:], 0.0)
    l_rb = jnp.where(incl, sc[LANES:, :LANES], 0.0)
    l_rk = jnp.where(incl, sc[LANES:, LANES:], 0.0)
    yield

    tinv = jnp.where(eye, 1.0, l_ab)
    pw = l_ab
    for _ in range(c.bit_length() - 2):
        pwb = _bf(pw)
        pw = _nn(pwb, pwb)
        tinv = tinv + _nn(_bf(tinv), _bf(pw))
        yield

    au = _nn(_bf(tinv), _bf(jnp.concatenate([a_st, _nn(_bf(l_ak), v_st)], axis=1)))
    aub = _bf(au)
    yield
    rb = _nn(_bf(l_rb), aub)
    gh = _tn(bp_st, aub)
    r_hat = r_st + rb[:, :LANES]
    o0 = rb[:, LANES:] + _nn(_bf(l_rk), v_st)
    g_mat = jnp.where(eye, jnp.exp(cum_end), 0.0) + gh[:, :LANES]
    h_mat = gh[:, LANES:] + _tn(kp_st, v_st)
    yield

    os = _nn(_bf(jnp.concatenate([r_hat, g_mat], axis=0)), _bf(s_ref[...]))
    s_ref[...] = os[LANES:] + h_mat
    o_st = os[:LANES] + o0
    o = o_st[:c] + o_st[c:]
    yield

    mean = headsum(o) * (1.0 / HEAD_DIM)
    oc = o - mean
    var = headsum(oc * oc) * (1.0 / HEAD_DIM)
    on = oc * lax.rsqrt(var + LNX_EPS) * ln_g + ln_b
    o_ref[...] = on + headsum(r * kf * r_k) * v


def _wkv_kernel(r_ref, k_ref, v_ref, lw_ref, a_ref, kk_ref, ka_ref, rk_ref, lg_ref, lb_ref,
                o_ref, s_ref):
    @pl.when(pl.program_id(2) == 0)
    def _():
        s_ref[...] = jnp.zeros_like(s_ref)

    def pair(p):
        sl = pl.ds(p * LANES, LANES)
        ins = [ref[:, sl] for ref in (r_ref, k_ref, v_ref, lw_ref, a_ref, kk_ref, ka_ref, rk_ref, lg_ref, lb_ref)]
        return _wkv_pair(*ins, s_ref.at[p], o_ref.at[:, sl])

    stages = [pair(p) for p in range(WKV_PAIRS)]
    while stages:
        stages = [g for g in stages if next(g, True) is None]


def _wkv(r, k, v, lw, a, k_k, k_a, r_k, lnx_g, lnx_b):
    n_chunks = SEQ // WKV_CHUNK
    width = WKV_PAIRS * LANES
    tok = pl.BlockSpec((WKV_CHUNK, width), lambda b, p, c: (b * n_chunks + c, p))
    par = pl.BlockSpec((1, width), lambda b, p, c: (0, p))
    vec = lambda t: t.reshape(1, D_MODEL)
    return pl.pallas_call(
        _wkv_kernel,
        grid=(BATCH, N_PAIRS // WKV_PAIRS, n_chunks),
        in_specs=[tok] * 5 + [par] * 5,
        out_specs=tok,
        out_shape=jax.ShapeDtypeStruct((TOKENS, D_MODEL), F32),
        scratch_shapes=[pltpu.VMEM((WKV_PAIRS, LANES, LANES), F32)],
        compiler_params=_params("parallel", "parallel", "arbitrary"),
        name="wkv_scan",
    )(r, k, v, lw, a, vec(k_k), vec(k_a), vec(r_k), vec(lnx_g), vec(lnx_b))


def _pack_rows(x):
    bits = lax.bitcast_convert_type(_bf(x).astype(F32), jnp.uint32)
    halves = []
    for h in range(PACK_HALVES):
        lo = bits[:, (2 * h) * PACK_WORDS:(2 * h + 1) * PACK_WORDS]
        hi = bits[:, (2 * h + 1) * PACK_WORDS:(2 * h + 2) * PACK_WORDS]
        halves.append(lax.bitcast_convert_type((lo >> 16) | (hi & jnp.uint32(0xFFFF0000)), jnp.int32))
    return halves


def _unpack_rows(halves):
    parts = []
    for w in halves:
        u = lax.bitcast_convert_type(w, jnp.uint32)
        parts.append(lax.bitcast_convert_type(u << 16, F32))
        parts.append(lax.bitcast_convert_type(u & jnp.uint32(0xFFFF0000), F32))
    return jnp.concatenate(parts, axis=1)


def _post_kernel(*refs, has_gate):
    if has_gate:
        o_ref, g_ref, *refs = refs
    else:
        o_ref, *refs = refs
    (x_ref, mod_ref, wo_ref, gpost_ref, gpre_ref, wr_hi_ref, wr_lo_ref, bias_ref,
     x1_ref, h2p_ref, gk_ref, ek_ref, rk_ref, cnt_ref, run_ref) = refs
    tm = ROW_TILE

    @pl.when(pl.program_id(0) == 0)
    def _():
        run_ref[...] = jnp.zeros_like(run_ref)

    mod = mod_ref[0]
    gate_m, shift_f, scale_f = mod[2:3], mod[3:4], mod[4:5]
    o = o_ref[...]
    if has_gate:
        o = o * g_ref[...]
    y = _nn(_bf(o), wo_ref[...])
    x1 = x_ref[...] + gate_m * _rms(y, gpost_ref[...])
    x1_ref[...] = x1
    h2 = _rms(x1, gpre_ref[...]) * (1.0 + scale_f) + shift_f
    for h, words in enumerate(_pack_rows(h2)):
        h2p_ref[h] = words

    hi = _bf(h2)
    lo = _bf(h2 - hi.astype(F32))
    logits = _nn(hi, wr_hi_ref[...]) + (_nn(lo, wr_hi_ref[...]) + _nn(hi, wr_lo_ref[...]))
    scores = _sigmoid(logits)
    lane = lax.broadcasted_iota(jnp.int32, scores.shape, 1)
    work = jnp.where(lane < N_EXPERTS, scores + bias_ref[...], -jnp.inf)
    picked = jnp.zeros_like(scores)
    chosen = jnp.zeros_like(scores)
    hits, ids = [], []
    for _ in range(TOP_K):
        best = jnp.max(work, axis=-1, keepdims=True)
        idx = jnp.min(jnp.where(work == best, lane, LANES), axis=-1, keepdims=True)
        hit = lane == idx
        picked = jnp.where(hit, scores, picked)
        chosen = jnp.where(hit, 1.0, chosen)
        work = jnp.where(hit, -jnp.inf, work)
        hits.append(hit)
        ids.append(idx)
    gates = ROUTED_SCALE * picked / jnp.sum(picked, axis=-1, keepdims=True)

    tril = (lax.broadcasted_iota(jnp.int32, (tm, tm), 0)
            >= lax.broadcasted_iota(jnp.int32, (tm, tm), 1)).astype(BF16)
    incl = _nn(tril, _bf(chosen))
    before = incl - chosen + run_ref[...]
    run_ref[...] += incl[tm - 1:tm]
    cnt_ref[...] = run_ref[...]

    gk = jnp.zeros_like(scores)
    ek = jnp.zeros(scores.shape, jnp.int32)
    rk = jnp.zeros(scores.shape, jnp.int32)
    for j in range(TOP_K):
        gcol = jnp.sum(jnp.where(hits[j], gates, 0.0), axis=-1, keepdims=True)
        rcol = jnp.sum(jnp.where(hits[j], before, 0.0), axis=-1, keepdims=True)
        gk = jnp.where(lane == j, gcol, gk)
        ek = jnp.where(lane == j, ids[j], ek)
        rk = jnp.where(lane == j, rcol.astype(jnp.int32), rk)
    gk_ref[...] = gk
    ek_ref[...] = ek
    rk_ref[...] = rk


def _post(o, g, x, mod, w_o, gpost, gpre, w_router, router_bias):
    tm = ROW_TILE
    full = lambda shape: pl.BlockSpec(shape, lambda i: (0,) * len(shape))
    row_spec = pl.BlockSpec((tm, D_MODEL), lambda i: (i, 0))
    lane_spec = pl.BlockSpec((tm, LANES), lambda i: (i, 0))
    wr = jnp.pad(w_router, ((0, 0), (0, LANES - N_EXPERTS)))
    wr_hi = _bf(wr)
    wr_lo = _bf(wr - wr_hi.astype(F32))
    bias = jnp.pad(router_bias, (0, LANES - N_EXPERTS)).reshape(1, LANES)
    has_gate = g is not None
    acts = [o, g] if has_gate else [o]
    return pl.pallas_call(
        functools.partial(_post_kernel, has_gate=has_gate),
        grid=(TOKENS // tm,),
        in_specs=[row_spec] * (len(acts) + 1) + [
            pl.BlockSpec((1, N_ADA, D_MODEL), lambda i: (i // (SEQ // tm), 0, 0)),
            full((D_MODEL, D_MODEL)), full((1, D_MODEL)), full((1, D_MODEL)),
            full((D_MODEL, LANES)), full((D_MODEL, LANES)), full((1, LANES)),
        ],
        out_specs=[row_spec, pl.BlockSpec((PACK_HALVES, tm, PACK_WORDS), lambda i: (0, i, 0)),
                   lane_spec, lane_spec, lane_spec, full((1, LANES))],
        out_shape=[jax.ShapeDtypeStruct((TOKENS, D_MODEL), F32),
                   jax.ShapeDtypeStruct((PACK_HALVES, TOKENS, PACK_WORDS), jnp.int32),
                   jax.ShapeDtypeStruct((TOKENS, LANES), F32),
                   jax.ShapeDtypeStruct((TOKENS, LANES), jnp.int32),
                   jax.ShapeDtypeStruct((TOKENS, LANES), jnp.int32),
                   jax.ShapeDtypeStruct((1, LANES), F32)],
        scratch_shapes=[pltpu.VMEM((1, LANES), F32)],
        compiler_params=_params("arbitrary"),
        name="mixer_out_router",
    )(*acts, x, mod, _bf(w_o), gpost.reshape(1, D_MODEL), gpre.reshape(1, D_MODEL),
      wr_hi, wr_lo, bias)


def _sc_mesh():
    return plsc.VectorSubcoreMesh(core_axis_name="core", subcore_axis_name="subcore")


def _sc_scatter_rows(rows, idx, n_out):
    n_src = rows.shape[1] // SC_WINDOW

    @pl.kernel(out_type=jax.ShapeDtypeStruct((PACK_HALVES, n_out, PACK_WORDS), rows.dtype),
               mesh=_sc_mesh(), scratch_types=[], name="sc_dispatch_rows")
    def scatter(x_hbm, i_hbm, o_hbm):
        for h in range(PACK_HALVES):
            def body(x_vmem, i_vmem, h=h):
                pltpu.sync_copy(x_vmem, o_hbm.at[h].at[i_vmem.at[0]])

            pltpu.emit_pipeline(
                body, grid=(idx.shape[0] // SC_WINDOW,),
                in_specs=[pl.BlockSpec((SC_WINDOW, PACK_WORDS), index_map=lambda i: (i % n_src, 0)),
                          pl.BlockSpec((1, SC_WINDOW), index_map=lambda i: (0, i))],
                out_specs=[],
                core_axis_name=("core", "subcore"),
                dimension_semantics=(pltpu.PARALLEL,),
            )(x_hbm.at[h], i_hbm)

    return scatter(rows, idx.reshape(1, -1))


def _sc_gather_rows(table, idx):
    n = idx.shape[0]

    @pl.kernel(out_type=jax.ShapeDtypeStruct((PACK_HALVES, n, PACK_WORDS), table.dtype),
               mesh=_sc_mesh(), scratch_types=[], name="sc_collect_rows")
    def gather(x_hbm, i_hbm, o_hbm):
        for h in range(PACK_HALVES):
            def body(i_vmem, o_vmem, h=h):
                pltpu.sync_copy(x_hbm.at[h].at[i_vmem.at[0]], o_vmem)

            pltpu.emit_pipeline(
                body, grid=(n // SC_WINDOW,),
                in_specs=[pl.BlockSpec((1, SC_WINDOW), index_map=lambda i: (0, i))],
                out_specs=[pl.BlockSpec((SC_WINDOW, PACK_WORDS), index_map=lambda i: (i, 0))],
                core_axis_name=("core", "subcore"),
                dimension_semantics=(pltpu.PARALLEL,),
            )(i_hbm, o_hbm.at[h])

    return gather(table, idx.reshape(1, n))


def _ffn(x, wg, wu, wd):
    up = _nn(x, _bf(wu))
    gt = _nn(x, _bf(wg))
    return _nn(_bf(gt * _sigmoid(gt) * up), _bf(wd))


def _expert_kernel(be_ref, nu_ref, xs_ref, wg_ref, wu_ref, wd_ref, ys_ref, wgb_ref, wub_ref, wdb_ref):
    b = pl.program_id(0)

    @pl.when(jnp.logical_or(b == 0, be_ref[b] != be_ref[jnp.maximum(b - 1, 0)]))
    def _():
        wgb_ref[...] = _bf(wg_ref[0, 0])
        wub_ref[...] = _bf(wu_ref[0, 0])
        wdb_ref[...] = _bf(wd_ref[0, 0])

    @pl.when(b < nu_ref[0])
    def _():
        x = _bf(_unpack_rows([xs_ref[h] for h in range(PACK_HALVES)]))
        y = _ffn(x, wgb_ref[...], wub_ref[...], wdb_ref[...])
        for h, words in enumerate(_pack_rows(y)):
            ys_ref[h] = words


def _experts(layer, xs, blk_expert, n_used, w_gate, w_up, w_down):
    n_blocks = xs.shape[1] // EXPERT_BLOCK
    row_spec = pl.BlockSpec((PACK_HALVES, EXPERT_BLOCK, PACK_WORDS), lambda b, be, nu: (0, b, 0))
    w_spec = lambda shape: pl.BlockSpec((1, 1) + shape, lambda b, be, nu: (layer, be[b], 0, 0))
    return pl.pallas_call(
        _expert_kernel,
        grid_spec=pltpu.PrefetchScalarGridSpec(
            num_scalar_prefetch=2,
            grid=(n_blocks,),
            in_specs=[row_spec, w_spec((D_MODEL, D_EXPERT)), w_spec((D_MODEL, D_EXPERT)),
                      w_spec((D_EXPERT, D_MODEL))],
            out_specs=row_spec,
            scratch_shapes=[pltpu.VMEM((D_MODEL, D_EXPERT), BF16), pltpu.VMEM((D_MODEL, D_EXPERT), BF16),
                            pltpu.VMEM((D_EXPERT, D_MODEL), BF16)],
        ),
        out_shape=jax.ShapeDtypeStruct(xs.shape, xs.dtype),
        compiler_params=_params("arbitrary"),
        name="moe_experts",
    )(blk_expert, n_used, xs, w_gate, w_up, w_down)


def _combine_kernel(*refs):
    g_refs = refs[:TOP_K]
    gk_ref, h2p_ref, x_ref, mod_ref, gpost_ref, sg_ref, su_ref, sd_ref, o_ref = refs[TOP_K:]
    h2 = _bf(_unpack_rows([h2p_ref[h] for h in range(PACK_HALVES)]))
    acc = _ffn(h2, sg_ref[0], su_ref[0], sd_ref[0])
    gk = gk_ref[...]
    lane = lax.broadcasted_iota(jnp.int32, gk.shape, 1)
    for j in range(TOP_K):
        gcol = jnp.sum(jnp.where(lane == j, gk, 0.0), axis=-1, keepdims=True)
        acc = acc + gcol * _unpack_rows([g_refs[j][h] for h in range(PACK_HALVES)])
    gate_f = mod_ref[0][5:6]
    o_ref[...] = x_ref[...] + gate_f * _rms(acc, gpost_ref[...])


def _combine(layer, picked, gk, h2p, x1, mod, gpost, ws_gate, ws_up, ws_down):
    tm = ROW_TILE
    n_tiles = TOKENS // tm
    row_spec = pl.BlockSpec((tm, D_MODEL), lambda i: (i, 0))
    pick_spec = lambda j: pl.BlockSpec((PACK_HALVES, tm, PACK_WORDS), lambda i: (0, j * n_tiles + i, 0))
    return pl.pallas_call(
        _combine_kernel,
        grid=(n_tiles,),
        in_specs=[pick_spec(j) for j in range(TOP_K)] + [
            pl.BlockSpec((tm, LANES), lambda i: (i, 0)),
            pl.BlockSpec((PACK_HALVES, tm, PACK_WORDS), lambda i: (0, i, 0)),
            row_spec,
            pl.BlockSpec((1, N_ADA, D_MODEL), lambda i: (i // (SEQ // tm), 0, 0)),
            pl.BlockSpec((1, D_MODEL), lambda i: (0, 0)),
            pl.BlockSpec((1, D_MODEL, D_EXPERT), lambda i: (layer, 0, 0)),
            pl.BlockSpec((1, D_MODEL, D_EXPERT), lambda i: (layer, 0, 0)),
            pl.BlockSpec((1, D_EXPERT, D_MODEL), lambda i: (layer, 0, 0)),
        ],
        out_specs=row_spec,
        out_shape=jax.ShapeDtypeStruct((TOKENS, D_MODEL), F32),
        compiler_params=_params("parallel"),
        name="moe_combine",
    )(*([picked] * TOP_K), gk, h2p, x1, mod, gpost.reshape(1, D_MODEL), ws_gate, ws_up, ws_down)


def _moe(layer, x1, h2p, gk, ek, rk, counts, mod, gpost, w_gate, w_up, w_down, ws_gate, ws_up, ws_down):
    cnt = counts[0, :N_EXPERTS].astype(jnp.int32)
    padded = (cnt + EXPERT_BLOCK - 1) // EXPERT_BLOCK * EXPERT_BLOCK
    eid = jnp.arange(N_EXPERTS, dtype=jnp.int32)
    pend = jnp.sum(jnp.where(eid[:, None] <= eid[None, :], padded[:, None], 0), axis=0)
    pstart = pend - padded
    ek_t, rk_t = ek[:, :TOP_K].T, rk[:, :TOP_K].T
    dest = rk_t + jnp.sum(jnp.where(ek_t[None] == eid[:, None, None], pstart[:, None, None], 0), axis=0)
    dest = dest.reshape(-1)
    n_blocks = (TOKENS * TOP_K + N_EXPERTS * (EXPERT_BLOCK - 1)) // EXPERT_BLOCK + 1
    blk_start = jnp.arange(n_blocks, dtype=jnp.int32) * EXPERT_BLOCK
    blk_expert = jnp.minimum(jnp.sum((blk_start[None, :] >= pend[:, None]).astype(jnp.int32), axis=0),
                             N_EXPERTS - 1)
    n_used = pend[-1:] // EXPERT_BLOCK

    xs = _sc_scatter_rows(h2p, dest, n_blocks * EXPERT_BLOCK)
    ys = _experts(layer, xs, blk_expert, n_used, w_gate, w_up, w_down)
    picked = _sc_gather_rows(ys, dest)
    return _combine(layer, picked, gk, h2p, x1, mod, gpost, ws_gate, ws_up, ws_down)


def _sb_proj_kernel(x_ref, mod_ref, gpre_ref, w_ref, q_ref, k_ref, v_ref):
    mod = mod_ref[0]
    h = _bf(_rms(x_ref[...], gpre_ref[...]) * (1.0 + mod[1:2]) + mod[0:1])
    q_ref[...] = _bf(_nn(h, w_ref[:, :D_MODEL]) * (1.0 / 8.0))
    k_ref[...] = _bf(_nn(h, w_ref[:, D_MODEL:2 * D_MODEL]))
    v_ref[...] = _bf(_nn(h, w_ref[:, 2 * D_MODEL:]))


def _sb_proj(x, mod, gpre, w_qkv):
    tm = ROW_TILE
    full = lambda shape: pl.BlockSpec(shape, lambda i: (0,) * len(shape))
    row_spec = pl.BlockSpec((tm, D_MODEL), lambda i: (i, 0))
    out_sds = jax.ShapeDtypeStruct((TOKENS, D_MODEL), BF16)
    return pl.pallas_call(
        _sb_proj_kernel,
        grid=(TOKENS // tm,),
        in_specs=[row_spec,
                  pl.BlockSpec((1, N_ADA, D_MODEL), lambda i: (i // (SEQ // tm), 0, 0)),
                  full((1, D_MODEL)), full((D_MODEL, 3 * D_MODEL))],
        out_specs=[row_spec] * 3,
        out_shape=[out_sds] * 3,
        compiler_params=_params("parallel"),
        name="sb_qkv_proj",
    )(x, mod, gpre.reshape(1, D_MODEL), _bf(w_qkv))


def _sb_attn_kernel(q_ref, k_ref, v_ref, o_ref):
    blk = ATT_BLOCK
    n_heads = LANES // HEAD_DIM
    qb = pl.program_id(2)
    q = q_ref[...]
    lane = lax.broadcasted_iota(jnp.int32, (blk, LANES), 1)
    head0 = lane < HEAD_DIM
    qh = [jnp.where(head0, q, jnp.zeros_like(q)), jnp.where(head0, jnp.zeros_like(q), q)]
    rowi = lax.broadcasted_iota(jnp.int32, (blk, blk), 0)
    coli = lax.broadcasted_iota(jnp.int32, (blk, blk), 1)
    below = coli < rowi
    later = (rowi > coli).astype(BF16)
    ones = jnp.ones((blk, LANES), BF16)

    def block(kb, carry, acc, diagonal):
        start = pl.multiple_of(kb * blk, blk)
        ks = k_ref[pl.ds(start, blk), :]
        vs = v_ref[pl.ds(start, blk), :]
        new_carry, new_acc = [], []
        for hd in range(n_heads):
            z = _nt(qh[hd], ks)
            lb = jnp.minimum(z, 0.0) - jnp.log(1.0 + jnp.exp(-jnp.abs(z)))
            l1m = lb - z
            if diagonal:
                l1m = jnp.where(below, l1m, 0.0)
            l1b = _bf(l1m)
            tail = _nn(l1b, later) + jnp.concatenate([carry[hd]] * (blk // LANES), axis=1)
            w = jnp.exp(lb + tail)
            if diagonal:
                w = jnp.where(below, w, 0.0)
            new_acc.append(acc[hd] + _nn(_bf(w), vs))
            new_carry.append(carry[hd] + _nn(l1b, ones))
        return new_carry, new_acc

    zeros = jnp.zeros((blk, LANES), F32)
    carry, acc = block(qb, [zeros] * n_heads, [zeros] * n_heads, diagonal=True)

    def cmax_of(cr):
        return jnp.max(functools.reduce(jnp.maximum, cr))

    def cond(st):
        kb, _, _, cmax = st
        return jnp.logical_and(kb >= 0, cmax > EXP_UNDERFLOW)

    def body(st):
        kb, cr, ac, _ = st
        cr, ac = block(kb, list(cr), list(ac), diagonal=False)
        return kb - 1, tuple(cr), tuple(ac), cmax_of(cr)

    _, _, acc, _ = lax.while_loop(cond, body, (qb - 1, tuple(carry), tuple(acc), cmax_of(carry)))
    o_ref[...] = jnp.where(head0, acc[0], acc[1])


def _sb_attn(q, k, v):
    blk = ATT_BLOCK
    n_blk = SEQ // blk
    q_spec = pl.BlockSpec((blk, LANES), lambda b, p, i: (b * n_blk + i, p))
    kv_spec = pl.BlockSpec((SEQ, LANES), lambda b, p, i: (b, p))
    return pl.pallas_call(
        _sb_attn_kernel,
        grid=(BATCH, N_PAIRS, n_blk),
        in_specs=[q_spec, kv_spec, kv_spec],
        out_specs=q_spec,
        out_shape=jax.ShapeDtypeStruct((TOKENS, D_MODEL), F32),
        compiler_params=_params("parallel", "parallel", "arbitrary"),
        name="sb_attention",
    )(q, k, v)


def kernel(x, c, ada_w, ada_b, norm_pre_mix, norm_post_mix, norm_pre_ffn, norm_post_ffn, rwkv_mu, rwkv_w_rkv, rwkv_w_w1, rwkv_w_w2, rwkv_w0, rwkv_w_a1, rwkv_w_a2, rwkv_a0, rwkv_w_g1, rwkv_w_g2, rwkv_k_k, rwkv_k_a, rwkv_r_k, rwkv_lnx_g, rwkv_lnx_b, rwkv_w_o, sb_w_qkv, sb_w_o, moe_w_router, moe_router_bias, moe_w_gate, moe_w_up, moe_w_down, moe_ws_gate, moe_ws_up, moe_ws_down):
    mod_all = _ada_mod(c, ada_w, ada_b)
    xt = x.reshape(TOKENS, D_MODEL)
    for layer in range(DEPTH):
        mod = mod_all[layer]
        i = layer // 2
        if layer % 2 == 0:
            r, k, v, lw, a, g = _rwkv_proj(
                xt, mod, norm_pre_mix[layer], rwkv_mu[i], rwkv_w_rkv[i], rwkv_w_w1[i], rwkv_w_w2[i],
                rwkv_w0[i], rwkv_w_a1[i], rwkv_w_a2[i], rwkv_a0[i], rwkv_w_g1[i], rwkv_w_g2[i])
            o = _wkv(r, k, v, lw, a, rwkv_k_k[i], rwkv_k_a[i], rwkv_r_k[i], rwkv_lnx_g[i], rwkv_lnx_b[i])
            w_o = rwkv_w_o[i]
        else:
            q, k, v = _sb_proj(xt, mod, norm_pre_mix[layer], sb_w_qkv[i])
            o, g = _sb_attn(q, k, v), None
            w_o = sb_w_o[i]
        x1, h2p, gk, ek, rk, counts = _post(o, g, xt, mod, w_o, norm_post_mix[layer], norm_pre_ffn[layer],
                                            moe_w_router[layer], moe_router_bias[layer])
        xt = _moe(layer, x1, h2p, gk, ek, rk, counts, mod, norm_post_ffn[layer], moe_w_gate, moe_w_up,
                  moe_w_down, moe_ws_gate, moe_ws_up, moe_ws_down)
    return xt.reshape(BATCH, SEQ, D_MODEL)
```

```python
import functools

import jax
import jax.numpy as jnp
from jax import lax
from jax.experimental import pallas as pl
from jax.experimental.pallas import tpu as pltpu
from jax.experimental.pallas import tpu_sc as plsc

D_MODEL = 1024
BATCH = 2
SEQ = 8192
TOKENS = BATCH * SEQ
DEPTH = 2
HEAD_DIM = 64
N_EXPERTS = 64
TOP_K = 8
D_EXPERT = 256
ROUTED_SCALE = 2.5
RMS_EPS = 1e-6
LNX_EPS = 64e-5
N_ADA = 6

LANES = 128
SUBLANES = 8
N_PAIRS = D_MODEL // LANES
WKV_CHUNK = 64
WKV_PAIRS = 8
ROW_TILE = 256
EXPERT_BLOCK = 512
PACK_HALVES = 2
PACK_WORDS = D_MODEL // (2 * PACK_HALVES)
SC_WINDOW = 128
ATT_BLOCK = 256
ATT_SUB = 128
ATT_SKEW = 0
EXP_UNDERFLOW = -106.0
VMEM_LIMIT = 56 * 1024 * 1024

F32 = jnp.float32
BF16 = jnp.bfloat16


def _nn(a, b):
    return jnp.dot(a, b, preferred_element_type=F32)


def _nt(a, b):
    return lax.dot_general(a, b, (((1,), (1,)), ((), ())), preferred_element_type=F32)


def _tn(a, b):
    return lax.dot_general(a, b, (((0,), (0,)), ((), ())), preferred_element_type=F32)


def _bf(x):
    return x.astype(BF16)


def _sigmoid(x):
    return 1.0 / (1.0 + jnp.exp(-x))


def _rms(xv, g):
    ms = jnp.mean(xv * xv, axis=-1, keepdims=True)
    return xv * lax.rsqrt(ms + RMS_EPS) * g


def _split3(x):
    hi = x.astype(BF16)
    r1 = x - hi.astype(F32)
    mid = r1.astype(BF16)
    lo = (r1 - mid.astype(F32)).astype(BF16)
    return hi, mid, lo


def _params(*sem):
    return pltpu.CompilerParams(dimension_semantics=sem, vmem_limit_bytes=VMEM_LIMIT)


def _ada_kernel(c_ref, w_ref, b_ref, o_ref):
    cv = c_ref[...]
    o_ref[0] = _nn(cv * _sigmoid(cv), w_ref[0]) + b_ref[0]


def _ada_mod(c, ada_w, ada_b):
    tn = 1536
    c8 = jnp.pad(c, ((0, SUBLANES - BATCH), (0, 0)))
    out = pl.pallas_call(
        _ada_kernel,
        grid=(DEPTH, N_ADA * D_MODEL // tn),
        in_specs=[
            pl.BlockSpec((SUBLANES, D_MODEL), lambda l, j: (0, 0)),
            pl.BlockSpec((1, D_MODEL, tn), lambda l, j: (l, 0, j)),
            pl.BlockSpec((1, 1, tn), lambda l, j: (l, 0, j)),
        ],
        out_specs=pl.BlockSpec((1, SUBLANES, tn), lambda l, j: (l, 0, j)),
        out_shape=jax.ShapeDtypeStruct((DEPTH, SUBLANES, N_ADA * D_MODEL), F32),
        compiler_params=_params("parallel", "parallel"),
        name="ada_mod",
    )(c8, ada_w, ada_b.reshape(DEPTH, 1, N_ADA * D_MODEL))
    return out[:, :BATCH].reshape(DEPTH, BATCH, N_ADA, D_MODEL)


def _rwkv_proj_kernel(x_ref, xp_ref, mod_ref, gpre_ref, mu_ref, wrkv_ref, w1_ref, w2_ref, w0_ref,
                      a1_ref, a2_ref, a0_ref, g1_ref, g2_ref,
                      r_ref, k_ref, v_ref, lw_ref, a_ref, g_ref):
    i = pl.program_id(0)
    mod = mod_ref[0]
    shift, scale = mod[0:1], mod[1:2]
    gpre = gpre_ref[...]

    def modnorm(xv):
        return _rms(xv, gpre) * (1.0 + scale) + shift

    h = modnorm(x_ref[...])
    hp = modnorm(xp_ref[...])[SUBLANES - 1:SUBLANES]
    hp = jnp.where(i % (SEQ // ROW_TILE) == 0, 0.0, hp)
    row = lax.broadcasted_iota(jnp.int32, h.shape, 0)
    hs = jnp.where(row == 0, hp, pltpu.roll(h, 1, 0))
    xx = hs - h
    mu = mu_ref[...]

    def mix(n):
        return _bf(h + xx * mu[n:n + 1])

    r_ref[...] = _nn(mix(0), wrkv_ref[0])
    k_ref[...] = _nn(mix(1), wrkv_ref[1])
    v_ref[...] = _nn(mix(2), wrkv_ref[2])
    wl = w0_ref[...] + _nn(_bf(jnp.tanh(_nn(mix(3), w1_ref[...]))), w2_ref[...])
    nwl = -wl
    sp = jnp.maximum(nwl, 0.0) + jnp.log(1.0 + jnp.exp(-jnp.abs(nwl)))
    lw_ref[...] = -jnp.exp(-sp - 0.5)
    a_ref[...] = _sigmoid(a0_ref[...] + _nn(_bf(_nn(mix(4), a1_ref[...])), a2_ref[...]))
    g_ref[...] = _nn(_bf(_sigmoid(_nn(mix(5), g1_ref[...]))), g2_ref[...])


def _rwkv_proj(x, mod, gpre, mu, w_rkv, w1, w2, w0, a1, a2, a0, g1, g2):
    tm = ROW_TILE
    full = lambda shape: pl.BlockSpec(shape, lambda i: (0,) * len(shape))
    row_spec = pl.BlockSpec((tm, D_MODEL), lambda i: (i, 0))
    out_sds = jax.ShapeDtypeStruct((TOKENS, D_MODEL), F32)
    return pl.pallas_call(
        _rwkv_proj_kernel,
        grid=(TOKENS // tm,),
        in_specs=[
            row_spec,
            pl.BlockSpec((SUBLANES, D_MODEL), lambda i: (jnp.maximum(i * (tm // SUBLANES) - 1, 0), 0)),
            pl.BlockSpec((1, N_ADA, D_MODEL), lambda i: (i // (SEQ // tm), 0, 0)),
            full((1, D_MODEL)), full((6, D_MODEL)), full((3, D_MODEL, D_MODEL)),
            full(w1.shape), full(w2.shape), full((1, D_MODEL)),
            full(a1.shape), full(a2.shape), full((1, D_MODEL)),
            full(g1.shape), full(g2.shape),
        ],
        out_specs=[row_spec] * 6,
        out_shape=[out_sds] * 6,
        compiler_params=_params("parallel"),
        name="rwkv_proj",
    )(x, x, mod, gpre.reshape(1, D_MODEL), mu, _bf(w_rkv), _bf(w1), _bf(w2),
      w0.reshape(1, D_MODEL), _bf(a1), _bf(a2), a0.reshape(1, D_MODEL), _bf(g1), _bf(g2))


def _wkv_pair(r, k, v, lw, a, k_k, k_a, r_k, ln_g, ln_b, s_ref, o_ref):
    c = WKV_CHUNK
    lane = lax.broadcasted_iota(jnp.int32, (c, LANES), 1)
    head0 = lane < HEAD_DIM
    rowi = lax.broadcasted_iota(jnp.int32, (LANES, LANES), 0)
    coli = lax.broadcasted_iota(jnp.int32, (LANES, LANES), 1)
    rt, ct = rowi & (c - 1), coli & (c - 1)
    strict, incl, eye = rt > ct, rt >= ct, rowi == coli
    tril = (lax.broadcasted_iota(jnp.int32, (c, c), 0)
            >= lax.broadcasted_iota(jnp.int32, (c, c), 1)).astype(BF16)

    def headsum(x):
        s0 = jnp.sum(jnp.where(head0, x, 0.0), axis=-1, keepdims=True)
        s1 = jnp.sum(jnp.where(head0, 0.0, x), axis=-1, keepdims=True)
        return jnp.where(head0, s0, s1)

    def stack(x):
        return jnp.concatenate([jnp.where(head0, x, 0.0), jnp.where(head0, 0.0, x)], axis=0)

    kk = k * k_k
    kk = kk / jnp.maximum(jnp.sqrt(headsum(kk * kk)), 1e-12)
    kf = k * (1.0 + (a - 1.0) * k_a)
    bvec = kk * a
    yield

    cum = _nn(tril, jnp.concatenate(_split3(lw), axis=1))
    cum = cum[:, :LANES] + cum[:, LANES:2 * LANES] + cum[:, 2 * LANES:]
    cum_end = cum[c - 1:c]
    w_inv = jnp.exp(-cum)
    w_rem = jnp.exp(cum_end - cum)
    a_st = stack(-kk * jnp.exp(cum - lw))
    r_st = stack(r * jnp.exp(cum))
    b_st = stack(bvec * w_inv)
    k_st = stack(kf * w_inv)
    bp_st = _bf(stack(bvec * w_rem))
    kp_st = _bf(stack(kf * w_rem))
    v_st = _bf(stack(v))
    yield

    sc = _nt(_bf(jnp.concatenate([a_st, r_st], axis=0)), _bf(jnp.concatenate([b_st, k_st], axis=0)))
    l_ab = jnp.where(strict, sc[:LANES, :LANES], 0.0)
    l_ak = jnp.where(strict, sc[:LANES, LANES:], 0.0)
    l_rb = jnp.where(incl, sc[LANES:, :LANES], 0.0)
    l_rk = jnp.where(incl, sc[LANES:, LANES:], 0.0)
    yield

    tinv = jnp.where(eye, 1.0, l_ab)
    pw = l_ab
    for _ in range(c.bit_length() - 2):
        pwb = _bf(pw)
        pw = _nn(pwb, pwb)
        tinv = tinv + _nn(_bf(tinv), _bf(pw))
        yield

    au = _nn(_bf(tinv), _bf(jnp.concatenate([a_st, _nn(_bf(l_ak), v_st)], axis=1)))
    aub = _bf(au)
    yield
    rb = _nn(_bf(l_rb), aub)
    gh = _tn(bp_st, aub)
    r_hat = r_st + rb[:, :LANES]
    o0 = rb[:, LANES:] + _nn(_bf(l_rk), v_st)
    g_mat = jnp.where(eye, jnp.exp(cum_end), 0.0) + gh[:, :LANES]
    h_mat = gh[:, LANES:] + _tn(kp_st, v_st)
    yield

    os = _nn(_bf(jnp.concatenate([r_hat, g_mat], axis=0)), _bf(s_ref[...]))
    s_ref[...] = os[LANES:] + h_mat
    o_st = os[:LANES] + o0
    o = o_st[:c] + o_st[c:]
    yield

    mean = headsum(o) * (1.0 / HEAD_DIM)
    oc = o - mean
    var = headsum(oc * oc) * (1.0 / HEAD_DIM)
    on = oc * lax.rsqrt(var + LNX_EPS) * ln_g + ln_b
    o_ref[...] = on + headsum(r * kf * r_k) * v


def _wkv_kernel(r_ref, k_ref, v_ref, lw_ref, a_ref, kk_ref, ka_ref, rk_ref, lg_ref, lb_ref,
                o_ref, s_ref):
    @pl.when(pl.program_id(2) == 0)
    def _():
        s_ref[...] = jnp.zeros_like(s_ref)

    def pair(p):
        sl = pl.ds(p * LANES, LANES)
        ins = [ref[:, sl] for ref in (r_ref, k_ref, v_ref, lw_ref, a_ref, kk_ref, ka_ref, rk_ref, lg_ref, lb_ref)]
        return _wkv_pair(*ins, s_ref.at[p], o_ref.at[:, sl])

    stages = [pair(p) for p in range(WKV_PAIRS)]
    while stages:
        stages = [g for g in stages if next(g, True) is None]


def _wkv(r, k, v, lw, a, k_k, k_a, r_k, lnx_g, lnx_b):
    n_chunks = SEQ // WKV_CHUNK
    width = WKV_PAIRS * LANES
    tok = pl.BlockSpec((WKV_CHUNK, width), lambda b, p, c: (b * n_chunks + c, p))
    par = pl.BlockSpec((1, width), lambda b, p, c: (0, p))
    vec = lambda t: t.reshape(1, D_MODEL)
    return pl.pallas_call(
        _wkv_kernel,
        grid=(BATCH, N_PAIRS // WKV_PAIRS, n_chunks),
        in_specs=[tok] * 5 + [par] * 5,
        out_specs=tok,
        out_shape=jax.ShapeDtypeStruct((TOKENS, D_MODEL), F32),
        scratch_shapes=[pltpu.VMEM((WKV_PAIRS, LANES, LANES), F32)],
        compiler_params=_params("parallel", "parallel", "arbitrary"),
        name="wkv_scan",
    )(r, k, v, lw, a, vec(k_k), vec(k_a), vec(r_k), vec(lnx_g), vec(lnx_b))


def _pack_rows(x):
    bits = lax.bitcast_convert_type(_bf(x).astype(F32), jnp.uint32)
    halves = []
    for h in range(PACK_HALVES):
        lo = bits[:, (2 * h) * PACK_WORDS:(2 * h + 1) * PACK_WORDS]
        hi = bits[:, (2 * h + 1) * PACK_WORDS:(2 * h + 2) * PACK_WORDS]
        halves.append(lax.bitcast_convert_type((lo >> 16) | (hi & jnp.uint32(0xFFFF0000)), jnp.int32))
    return halves


def _unpack_rows(halves):
    parts = []
    for w in halves:
        u = lax.bitcast_convert_type(w, jnp.uint32)
        parts.append(lax.bitcast_convert_type(u << 16, F32))
        parts.append(lax.bitcast_convert_type(u & jnp.uint32(0xFFFF0000), F32))
    return jnp.concatenate(parts, axis=1)


def _post_kernel(*refs, has_gate):
    if has_gate:
        o_ref, g_ref, *refs = refs
    else:
        o_ref, *refs = refs
    (x_ref, mod_ref, wo_ref, gpost_ref, gpre_ref, wr_hi_ref, wr_lo_ref, bias_ref,
     x1_ref, h2p_ref, gk_ref, ek_ref, rk_ref, cnt_ref, run_ref) = refs
    tm = ROW_TILE

    @pl.when(pl.program_id(0) == 0)
    def _():
        run_ref[...] = jnp.zeros_like(run_ref)

    mod = mod_ref[0]
    gate_m, shift_f, scale_f = mod[2:3], mod[3:4], mod[4:5]
    o = o_ref[...]
    if has_gate:
        o = o * g_ref[...]
    y = _nn(_bf(o), wo_ref[...])
    x1 = x_ref[...] + gate_m * _rms(y, gpost_ref[...])
    x1_ref[...] = x1
    h2 = _rms(x1, gpre_ref[...]) * (1.0 + scale_f) + shift_f
    for h, words in enumerate(_pack_rows(h2)):
        h2p_ref[h] = words

    hi = _bf(h2)
    lo = _bf(h2 - hi.astype(F32))
    wr_hi, wr_lo = wr_hi_ref[...], wr_lo_ref[...]
    scores = _sigmoid(_nt(wr_hi, hi) + (_nt(wr_hi, lo) + _nt(wr_lo, hi)))
    erow = lax.broadcasted_iota(jnp.int32, scores.shape, 0)
    work = scores + jnp.concatenate([bias_ref[...]] * (tm // LANES), axis=1)
    picked = jnp.zeros_like(scores)
    chosen = jnp.zeros_like(scores)
    hits, ids = [], []
    for _ in range(TOP_K):
        best = jnp.max(work, axis=0, keepdims=True)
        idx = jnp.min(jnp.where(work == best, erow, N_EXPERTS), axis=0, keepdims=True)
        hit = erow == idx
        picked = jnp.where(hit, scores, picked)
        chosen = jnp.where(hit, 1.0, chosen)
        work = jnp.where(hit, -jnp.inf, work)
        hits.append(hit)
        ids.append(idx)
    gates = ROUTED_SCALE * picked / jnp.sum(picked, axis=0, keepdims=True)

    upto = (lax.broadcasted_iota(jnp.int32, (tm, tm), 0)
            <= lax.broadcasted_iota(jnp.int32, (tm, tm), 1)).astype(BF16)
    chosen_b = _bf(chosen)
    run = run_ref[...]
    before = _nn(chosen_b, upto) - chosen + jnp.concatenate([run] * (tm // LANES), axis=1)
    run = run + _nn(chosen_b, jnp.ones((tm, LANES), BF16))
    run_ref[...] = run
    cnt_ref[...] = run

    krow = lax.broadcasted_iota(jnp.int32, (TOP_K, tm), 0)
    gk = jnp.zeros((TOP_K, tm), F32)
    ek = jnp.zeros((TOP_K, tm), jnp.int32)
    rk = jnp.zeros((TOP_K, tm), jnp.int32)
    for j in range(TOP_K):
        grow = jnp.sum(jnp.where(hits[j], gates, 0.0), axis=0, keepdims=True)
        rrow = jnp.sum(jnp.where(hits[j], before, 0.0), axis=0, keepdims=True)
        gk = jnp.where(krow == j, grow, gk)
        ek = jnp.where(krow == j, ids[j], ek)
        rk = jnp.where(krow == j, rrow.astype(jnp.int32), rk)
    gk_ref[...] = gk
    ek_ref[...] = ek
    rk_ref[...] = rk


def _post(o, g, x, mod, w_o, gpost, gpre, w_router, router_bias):
    tm = ROW_TILE
    full = lambda shape: pl.BlockSpec(shape, lambda i: (0,) * len(shape))
    row_spec = pl.BlockSpec((tm, D_MODEL), lambda i: (i, 0))
    k_spec = pl.BlockSpec((TOP_K, tm), lambda i: (0, i))
    wr = w_router.T
    wr_hi = _bf(wr)
    wr_lo = _bf(wr - wr_hi.astype(F32))
    bias = jnp.broadcast_to(router_bias[:, None], (N_EXPERTS, LANES))
    has_gate = g is not None
    acts = [o, g] if has_gate else [o]
    return pl.pallas_call(
        functools.partial(_post_kernel, has_gate=has_gate),
        grid=(TOKENS // tm,),
        in_specs=[row_spec] * (len(acts) + 1) + [
            pl.BlockSpec((1, N_ADA, D_MODEL), lambda i: (i // (SEQ // tm), 0, 0)),
            full((D_MODEL, D_MODEL)), full((1, D_MODEL)), full((1, D_MODEL)),
            full((N_EXPERTS, D_MODEL)), full((N_EXPERTS, D_MODEL)), full((N_EXPERTS, LANES)),
        ],
        out_specs=[row_spec, pl.BlockSpec((PACK_HALVES, tm, PACK_WORDS), lambda i: (0, i, 0)),
                   k_spec, k_spec, k_spec, full((N_EXPERTS, LANES))],
        out_shape=[jax.ShapeDtypeStruct((TOKENS, D_MODEL), F32),
                   jax.ShapeDtypeStruct((PACK_HALVES, TOKENS, PACK_WORDS), jnp.int32),
                   jax.ShapeDtypeStruct((TOP_K, TOKENS), F32),
                   jax.ShapeDtypeStruct((TOP_K, TOKENS), jnp.int32),
                   jax.ShapeDtypeStruct((TOP_K, TOKENS), jnp.int32),
                   jax.ShapeDtypeStruct((N_EXPERTS, LANES), F32)],
        scratch_shapes=[pltpu.VMEM((N_EXPERTS, LANES), F32)],
        compiler_params=_params("arbitrary"),
        name="mixer_out_router",
    )(*acts, x, mod, _bf(w_o), gpost.reshape(1, D_MODEL), gpre.reshape(1, D_MODEL),
      wr_hi, wr_lo, bias)


def _sc_mesh():
    return plsc.VectorSubcoreMesh(core_axis_name="core", subcore_axis_name="subcore")


def _sc_scatter_rows(rows, idx, n_out):
    n_copies = idx.shape[0]

    @pl.kernel(out_type=jax.ShapeDtypeStruct((PACK_HALVES, n_out, PACK_WORDS), rows.dtype),
               mesh=_sc_mesh(), scratch_types=[], name="sc_dispatch_rows")
    def scatter(x_hbm, i_hbm, o_hbm):
        for h in range(PACK_HALVES):
            def body(x_vmem, i_vmem, h=h):
                for k in range(n_copies):
                    pltpu.sync_copy(x_vmem, o_hbm.at[h].at[i_vmem.at[k]])

            pltpu.emit_pipeline(
                body, grid=(rows.shape[1] // SC_WINDOW,),
                in_specs=[pl.BlockSpec((SC_WINDOW, PACK_WORDS), index_map=lambda i: (i, 0)),
                          pl.BlockSpec((n_copies, SC_WINDOW), index_map=lambda i: (0, i))],
                out_specs=[],
                core_axis_name=("core", "subcore"),
                dimension_semantics=(pltpu.PARALLEL,),
            )(x_hbm.at[h], i_hbm)

    return scatter(rows, idx)


def _sc_gather_rows(table, idx):
    n = idx.shape[0]

    @pl.kernel(out_type=jax.ShapeDtypeStruct((PACK_HALVES, n, PACK_WORDS), table.dtype),
               mesh=_sc_mesh(), scratch_types=[], name="sc_collect_rows")
    def gather(x_hbm, i_hbm, o_hbm):
        for h in range(PACK_HALVES):
            def body(i_vmem, o_vmem, h=h):
                pltpu.sync_copy(x_hbm.at[h].at[i_vmem.at[0]], o_vmem)

            pltpu.emit_pipeline(
                body, grid=(n // SC_WINDOW,),
                in_specs=[pl.BlockSpec((1, SC_WINDOW), index_map=lambda i: (0, i))],
                out_specs=[pl.BlockSpec((SC_WINDOW, PACK_WORDS), index_map=lambda i: (i, 0))],
                core_axis_name=("core", "subcore"),
                dimension_semantics=(pltpu.PARALLEL,),
            )(i_hbm, o_hbm.at[h])

    return gather(table, idx.reshape(1, n))


def _ffn(x, wg, wu, wd):
    up = _nn(x, _bf(wu))
    gt = _nn(x, _bf(wg))
    return _nn(_bf(gt * _sigmoid(gt) * up), _bf(wd))


def _expert_kernel(be_ref, nu_ref, xs_ref, wg_ref, wu_ref, wd_ref, ys_ref, wgb_ref, wub_ref, wdb_ref):
    b = pl.program_id(0)

    @pl.when(jnp.logical_or(b == 0, be_ref[b] != be_ref[jnp.maximum(b - 1, 0)]))
    def _():
        wgb_ref[...] = _bf(wg_ref[0, 0])
        wub_ref[...] = _bf(wu_ref[0, 0])
        wdb_ref[...] = _bf(wd_ref[0, 0])

    @pl.when(b < nu_ref[0])
    def _():
        x = _bf(_unpack_rows([xs_ref[h] for h in range(PACK_HALVES)]))
        y = _ffn(x, wgb_ref[...], wub_ref[...], wdb_ref[...])
        for h, words in enumerate(_pack_rows(y)):
            ys_ref[h] = words


def _experts(layer, xs, blk_expert, n_used, w_gate, w_up, w_down):
    n_blocks = xs.shape[1] // EXPERT_BLOCK
    row_spec = pl.BlockSpec((PACK_HALVES, EXPERT_BLOCK, PACK_WORDS), lambda b, be, nu: (0, b, 0))
    w_spec = lambda shape: pl.BlockSpec((1, 1) + shape, lambda b, be, nu: (layer, be[b], 0, 0))
    return pl.pallas_call(
        _expert_kernel,
        grid_spec=pltpu.PrefetchScalarGridSpec(
            num_scalar_prefetch=2,
            grid=(n_blocks,),
            in_specs=[row_spec, w_spec((D_MODEL, D_EXPERT)), w_spec((D_MODEL, D_EXPERT)),
                      w_spec((D_EXPERT, D_MODEL))],
            out_specs=row_spec,
            scratch_shapes=[pltpu.VMEM((D_MODEL, D_EXPERT), BF16), pltpu.VMEM((D_MODEL, D_EXPERT), BF16),
                            pltpu.VMEM((D_EXPERT, D_MODEL), BF16)],
        ),
        out_shape=jax.ShapeDtypeStruct(xs.shape, xs.dtype),
        compiler_params=_params("arbitrary"),
        name="moe_experts",
    )(blk_expert, n_used, xs, w_gate, w_up, w_down)


def _combine_kernel(*refs):
    g_refs = refs[:TOP_K]
    gk_ref, h2p_ref, x_ref, mod_ref, gpost_ref, sg_ref, su_ref, sd_ref, o_ref = refs[TOP_K:]
    h2 = _bf(_unpack_rows([h2p_ref[h] for h in range(PACK_HALVES)]))
    acc = _ffn(h2, sg_ref[0], su_ref[0], sd_ref[0])
    gk = gk_ref[...]
    lane = lax.broadcasted_iota(jnp.int32, gk.shape, 1)
    for j in range(TOP_K):
        gcol = jnp.sum(jnp.where(lane == j, gk, 0.0), axis=-1, keepdims=True)
        acc = acc + gcol * _unpack_rows([g_refs[j][h] for h in range(PACK_HALVES)])
    gate_f = mod_ref[0][5:6]
    o_ref[...] = x_ref[...] + gate_f * _rms(acc, gpost_ref[...])


def _combine(layer, picked, gk, h2p, x1, mod, gpost, ws_gate, ws_up, ws_down):
    tm = ROW_TILE
    n_tiles = TOKENS // tm
    row_spec = pl.BlockSpec((tm, D_MODEL), lambda i: (i, 0))
    pick_spec = lambda j: pl.BlockSpec((PACK_HALVES, tm, PACK_WORDS), lambda i: (0, j * n_tiles + i, 0))
    return pl.pallas_call(
        _combine_kernel,
        grid=(n_tiles,),
        in_specs=[pick_spec(j) for j in range(TOP_K)] + [
            pl.BlockSpec((tm, LANES), lambda i: (i, 0)),
            pl.BlockSpec((PACK_HALVES, tm, PACK_WORDS), lambda i: (0, i, 0)),
            row_spec,
            pl.BlockSpec((1, N_ADA, D_MODEL), lambda i: (i // (SEQ // tm), 0, 0)),
            pl.BlockSpec((1, D_MODEL), lambda i: (0, 0)),
            pl.BlockSpec((1, D_MODEL, D_EXPERT), lambda i: (layer, 0, 0)),
            pl.BlockSpec((1, D_MODEL, D_EXPERT), lambda i: (layer, 0, 0)),
            pl.BlockSpec((1, D_EXPERT, D_MODEL), lambda i: (layer, 0, 0)),
        ],
        out_specs=row_spec,
        out_shape=jax.ShapeDtypeStruct((TOKENS, D_MODEL), F32),
        compiler_params=_params("parallel"),
        name="moe_combine",
    )(*([picked] * TOP_K), gk, h2p, x1, mod, gpost.reshape(1, D_MODEL), ws_gate, ws_up, ws_down)


def _moe(layer, x1, h2p, gk, ek, rk, counts, mod, gpost, w_gate, w_up, w_down, ws_gate, ws_up, ws_down):
    cnt = counts[:, 0].astype(jnp.int32)
    padded = (cnt + EXPERT_BLOCK - 1) // EXPERT_BLOCK * EXPERT_BLOCK
    eid = jnp.arange(N_EXPERTS, dtype=jnp.int32)
    pend = jnp.sum(jnp.where(eid[:, None] <= eid[None, :], padded[:, None], 0), axis=0)
    pstart = pend - padded
    dest = rk + jnp.sum(jnp.where(ek[None] == eid[:, None, None], pstart[:, None, None], 0), axis=0)
    gk = jnp.pad(gk.T, ((0, 0), (0, LANES - TOP_K)))
    n_blocks = (TOKENS * TOP_K + N_EXPERTS * (EXPERT_BLOCK - 1)) // EXPERT_BLOCK + 1
    blk_start = jnp.arange(n_blocks, dtype=jnp.int32) * EXPERT_BLOCK
    blk_expert = jnp.minimum(jnp.sum((blk_start[None, :] >= pend[:, None]).astype(jnp.int32), axis=0),
                             N_EXPERTS - 1)
    n_used = pend[-1:] // EXPERT_BLOCK

    xs = _sc_scatter_rows(h2p, dest, n_blocks * EXPERT_BLOCK)
    ys = _experts(layer, xs, blk_expert, n_used, w_gate, w_up, w_down)
    picked = _sc_gather_rows(ys, dest.reshape(-1))
    return _combine(layer, picked, gk, h2p, x1, mod, gpost, ws_gate, ws_up, ws_down)


def _sb_proj_kernel(x_ref, mod_ref, gpre_ref, w_ref, q_ref, k_ref, v_ref):
    mod = mod_ref[0]
    h = _bf(_rms(x_ref[...], gpre_ref[...]) * (1.0 + mod[1:2]) + mod[0:1])
    q_ref[...] = _bf(_nn(h, w_ref[:, :D_MODEL]) * (1.0 / 8.0))
    k_ref[...] = _bf(_nn(h, w_ref[:, D_MODEL:2 * D_MODEL]))
    v_ref[...] = _bf(_nn(h, w_ref[:, 2 * D_MODEL:]))


def _sb_proj(x, mod, gpre, w_qkv):
    tm = ROW_TILE
    full = lambda shape: pl.BlockSpec(shape, lambda i: (0,) * len(shape))
    row_spec = pl.BlockSpec((tm, D_MODEL), lambda i: (i, 0))
    out_sds = jax.ShapeDtypeStruct((TOKENS, D_MODEL), BF16)
    return pl.pallas_call(
        _sb_proj_kernel,
        grid=(TOKENS // tm,),
        in_specs=[row_spec,
                  pl.BlockSpec((1, N_ADA, D_MODEL), lambda i: (i // (SEQ // tm), 0, 0)),
                  full((1, D_MODEL)), full((D_MODEL, 3 * D_MODEL))],
        out_specs=[row_spec] * 3,
        out_shape=[out_sds] * 3,
        compiler_params=_params("parallel"),
        name="sb_qkv_proj",
    )(x, mod, gpre.reshape(1, D_MODEL), _bf(w_qkv))


def _sb_attn_kernel(q_ref, k_ref, v_ref, o_ref):
    blk = ATT_BLOCK
    n_heads = LANES // HEAD_DIM
    qb = pl.program_id(2)
    q = q_ref[...]
    lane = lax.broadcasted_iota(jnp.int32, (blk, LANES), 1)
    head0 = lane < HEAD_DIM
    qh = [jnp.where(head0, q, jnp.zeros_like(q)), jnp.where(head0, jnp.zeros_like(q), q)]
    rowi = lax.broadcasted_iota(jnp.int32, (blk, blk), 0)
    coli = lax.broadcasted_iota(jnp.int32, (blk, blk), 1)
    later = (rowi > coli).astype(BF16)
    ones = jnp.ones((blk, LANES), BF16)

    sub = ATT_SUB
    n_sub = blk // sub
    sub_row = lax.broadcasted_iota(jnp.int32, (sub, blk), 0)
    sub_col = lax.broadcasted_iota(jnp.int32, (sub, blk), 1)
    below = [sub_col < sub_row + r * sub for r in range(n_sub)]
    sub_head0 = lax.broadcasted_iota(jnp.int32, (sub, LANES), 1) < HEAD_DIM
    chains = [(hd, r) for hd in range(n_heads) for r in range(n_sub)]

    def chain(hd, r, ks, vs, carry, acc, diagonal, out):
        rows = slice(r * sub, (r + 1) * sub)
        z = _nt(qh[hd][rows], ks)
        yield
        lb = jnp.minimum(z, 0.0) - jnp.log(1.0 + jnp.exp(-jnp.abs(z)))
        l1m = lb - z
        if diagonal:
            l1m = jnp.where(below[r], l1m, 0.0)
        l1b = _bf(l1m)
        yield
        tail = _nn(l1b, later) + jnp.concatenate([carry] * (blk // LANES), axis=1)
        new_carry = carry + _nn(l1b, ones)
        yield
        w = jnp.exp(lb + tail)
        if diagonal:
            w = jnp.where(below[r], w, 0.0)
        wb = _bf(w)
        yield
        out.append((new_carry, acc + _nn(wb, vs)))

    def block(kb, carry, acc, diagonal):
        start = pl.multiple_of(kb * blk, blk)
        ks = k_ref[pl.ds(start, blk), :]
        vs = v_ref[pl.ds(start, blk), :]
        outs = [[] for _ in chains]
        gens = [chain(hd, r, ks, vs, carry[c], acc[c], diagonal, outs[c])
                for c, (hd, r) in enumerate(chains)]
        live, step = set(range(len(gens))), 0
        while live:
            for c in sorted(live):
                if step >= c * ATT_SKEW and next(gens[c], True) is True:
                    live.discard(c)
            step += 1
        return [o[0][0] for o in outs], [o[0][1] for o in outs]

    zeros = jnp.zeros((sub, LANES), F32)
    carry, acc = block(qb, [zeros] * len(chains), [zeros] * len(chains), diagonal=True)

    def cmax_of(cr):
        return jnp.max(functools.reduce(jnp.maximum, cr))

    def cond(st):
        kb, _, _, cmax = st
        return jnp.logical_and(kb >= 0, cmax > EXP_UNDERFLOW)

    def body(st):
        kb, cr, ac, _ = st
        cr, ac = block(kb, list(cr), list(ac), diagonal=False)
        return kb - 1, tuple(cr), tuple(ac), cmax_of(cr)

    _, _, acc, _ = lax.while_loop(cond, body, (qb - 1, tuple(carry), tuple(acc), cmax_of(carry)))
    for r in range(n_sub):
        rows = slice(r * sub, (r + 1) * sub)
        o_ref[rows, :] = jnp.where(sub_head0, acc[r], acc[n_sub + r])


def _sb_attn(q, k, v):
    blk = ATT_BLOCK
    n_blk = SEQ // blk
    q_spec = pl.BlockSpec((blk, LANES), lambda b, p, i: (b * n_blk + i, p))
    kv_spec = pl.BlockSpec((SEQ, LANES), lambda b, p, i: (b, p))
    return pl.pallas_call(
        _sb_attn_kernel,
        grid=(BATCH, N_PAIRS, n_blk),
        in_specs=[q_spec, kv_spec, kv_spec],
        out_specs=q_spec,
        out_shape=jax.ShapeDtypeStruct((TOKENS, D_MODEL), F32),
        compiler_params=_params("parallel", "parallel", "arbitrary"),
        name="sb_attention",
    )(q, k, v)


def kernel(x, c, ada_w, ada_b, norm_pre_mix, norm_post_mix, norm_pre_ffn, norm_post_ffn, rwkv_mu, rwkv_w_rkv, rwkv_w_w1, rwkv_w_w2, rwkv_w0, rwkv_w_a1, rwkv_w_a2, rwkv_a0, rwkv_w_g1, rwkv_w_g2, rwkv_k_k, rwkv_k_a, rwkv_r_k, rwkv_lnx_g, rwkv_lnx_b, rwkv_w_o, sb_w_qkv, sb_w_o, moe_w_router, moe_router_bias, moe_w_gate, moe_w_up, moe_w_down, moe_ws_gate, moe_ws_up, moe_ws_down):
    mod_all = _ada_mod(c, ada_w, ada_b)
    xt = x.reshape(TOKENS, D_MODEL)
    for layer in range(DEPTH):
        mod = mod_all[layer]
        i = layer // 2
        if layer % 2 == 0:
            r, k, v, lw, a, g = _rwkv_proj(
                xt, mod, norm_pre_mix[layer], rwkv_mu[i], rwkv_w_rkv[i], rwkv_w_w1[i], rwkv_w_w2[i],
                rwkv_w0[i], rwkv_w_a1[i], rwkv_w_a2[i], rwkv_a0[i], rwkv_w_g1[i], rwkv_w_g2[i])
            o = _wkv(r, k, v, lw, a, rwkv_k_k[i], rwkv_k_a[i], rwkv_r_k[i], rwkv_lnx_g[i], rwkv_lnx_b[i])
            w_o = rwkv_w_o[i]
        else:
            q, k, v = _sb_proj(xt, mod, norm_pre_mix[layer], sb_w_qkv[i])
            o, g = _sb_attn(q, k, v), None
            w_o = sb_w_o[i]
        x1, h2p, gk, ek, rk, counts = _post(o, g, xt, mod, w_o, norm_post_mix[layer], norm_pre_ffn[layer],
                                            moe_w_router[layer], moe_router_bias[layer])
        xt = _moe(layer, x1, h2p, gk, ek, rk, counts, mod, norm_post_ffn[layer], moe_w_gate, moe_w_up,
                  moe_w_down, moe_ws_gate, moe_ws_up, moe_ws_down)
    return xt.reshape(BATCH, SEQ, D_MODEL)
```

```python
import functools

import jax
import jax.numpy as jnp
from jax import lax
from jax.experimental import pallas as pl
from jax.experimental.pallas import tpu as pltpu
from jax.experimental.pallas import tpu_sc as plsc

D_MODEL = 1024
BATCH = 2
SEQ = 8192
TOKENS = BATCH * SEQ
DEPTH = 2
HEAD_DIM = 64
N_EXPERTS = 64
TOP_K = 8
D_EXPERT = 256
ROUTED_SCALE = 2.5
RMS_EPS = 1e-6
LNX_EPS = 64e-5
N_ADA = 6

LANES = 128
SUBLANES = 8
N_PAIRS = D_MODEL // LANES
WKV_CHUNK = 64
WKV_PAIRS = 8
ROW_TILE = 256
SHARED_TILE = 512
EXPERT_BLOCK = 512
PACK_HALVES = 2
PACK_WORDS = D_MODEL // (2 * PACK_HALVES)
SC_WINDOW = 128
ATT_BLOCK = 256
ATT_SUB = 128
LOG2_E = 1.4426950408889634
EXP2_UNDERFLOW = -153.0
VMEM_LIMIT = 56 * 1024 * 1024

F32 = jnp.float32
BF16 = jnp.bfloat16


def _nn(a, b):
    return jnp.dot(a, b, preferred_element_type=F32)


def _nt(a, b):
    return lax.dot_general(a, b, (((1,), (1,)), ((), ())), preferred_element_type=F32)


def _tn(a, b):
    return lax.dot_general(a, b, (((0,), (0,)), ((), ())), preferred_element_type=F32)


def _bf(x):
    return x.astype(BF16)


def _sigmoid(x):
    return 1.0 / (1.0 + jnp.exp(-x))


def _rms(xv, g):
    ms = jnp.mean(xv * xv, axis=-1, keepdims=True)
    return xv * lax.rsqrt(ms + RMS_EPS) * g


def _split3(x):
    hi = x.astype(BF16)
    r1 = x - hi.astype(F32)
    mid = r1.astype(BF16)
    lo = (r1 - mid.astype(F32)).astype(BF16)
    return hi, mid, lo


def _params(*sem):
    return pltpu.CompilerParams(dimension_semantics=sem, vmem_limit_bytes=VMEM_LIMIT)


def _ada_kernel(c_ref, w_ref, b_ref, o_ref):
    cv = c_ref[...]
    o_ref[0] = _nn(cv * _sigmoid(cv), w_ref[0]) + b_ref[0]


def _ada_mod(c, ada_w, ada_b):
    tn = 1536
    c8 = jnp.pad(c, ((0, SUBLANES - BATCH), (0, 0)))
    out = pl.pallas_call(
        _ada_kernel,
        grid=(DEPTH, N_ADA * D_MODEL // tn),
        in_specs=[
            pl.BlockSpec((SUBLANES, D_MODEL), lambda l, j: (0, 0)),
            pl.BlockSpec((1, D_MODEL, tn), lambda l, j: (l, 0, j)),
            pl.BlockSpec((1, 1, tn), lambda l, j: (l, 0, j)),
        ],
        out_specs=pl.BlockSpec((1, SUBLANES, tn), lambda l, j: (l, 0, j)),
        out_shape=jax.ShapeDtypeStruct((DEPTH, SUBLANES, N_ADA * D_MODEL), F32),
        compiler_params=_params("parallel", "parallel"),
        name="ada_mod",
    )(c8, ada_w, ada_b.reshape(DEPTH, 1, N_ADA * D_MODEL))
    return out[:, :BATCH].reshape(DEPTH, BATCH, N_ADA, D_MODEL)


def _rwkv_proj_kernel(x_ref, xp_ref, mod_ref, gpre_ref, mu_ref, wrkv_ref, w1_ref, w2_ref, w0_ref,
                      a1_ref, a2_ref, a0_ref, g1_ref, g2_ref,
                      r_ref, k_ref, v_ref, lw_ref, a_ref, g_ref):
    i = pl.program_id(0)
    mod = mod_ref[0]
    shift, scale = mod[0:1], mod[1:2]
    gpre = gpre_ref[...]

    def modnorm(xv):
        return _rms(xv, gpre) * (1.0 + scale) + shift

    h = modnorm(x_ref[...])
    hp = modnorm(xp_ref[...])[SUBLANES - 1:SUBLANES]
    hp = jnp.where(i % (SEQ // ROW_TILE) == 0, 0.0, hp)
    row = lax.broadcasted_iota(jnp.int32, h.shape, 0)
    hs = jnp.where(row == 0, hp, pltpu.roll(h, 1, 0))
    xx = hs - h
    mu = mu_ref[...]

    def mix(n):
        return _bf(h + xx * mu[n:n + 1])

    r_ref[...] = _nn(mix(0), wrkv_ref[0])
    k_ref[...] = _nn(mix(1), wrkv_ref[1])
    v_ref[...] = _nn(mix(2), wrkv_ref[2])
    wl = w0_ref[...] + _nn(_bf(jnp.tanh(_nn(mix(3), w1_ref[...]))), w2_ref[...])
    nwl = -wl
    sp = jnp.maximum(nwl, 0.0) + jnp.log(1.0 + jnp.exp(-jnp.abs(nwl)))
    lw_ref[...] = -jnp.exp(-sp - 0.5)
    a_ref[...] = _sigmoid(a0_ref[...] + _nn(_bf(_nn(mix(4), a1_ref[...])), a2_ref[...]))
    g_ref[...] = _nn(_bf(_sigmoid(_nn(mix(5), g1_ref[...]))), g2_ref[...])


def _rwkv_proj(x, mod, gpre, mu, w_rkv, w1, w2, w0, a1, a2, a0, g1, g2):
    tm = ROW_TILE
    full = lambda shape: pl.BlockSpec(shape, lambda i: (0,) * len(shape))
    row_spec = pl.BlockSpec((tm, D_MODEL), lambda i: (i, 0))
    out_sds = jax.ShapeDtypeStruct((TOKENS, D_MODEL), F32)
    return pl.pallas_call(
        _rwkv_proj_kernel,
        grid=(TOKENS // tm,),
        in_specs=[
            row_spec,
            pl.BlockSpec((SUBLANES, D_MODEL), lambda i: (jnp.maximum(i * (tm // SUBLANES) - 1, 0), 0)),
            pl.BlockSpec((1, N_ADA, D_MODEL), lambda i: (i // (SEQ // tm), 0, 0)),
            full((1, D_MODEL)), full((6, D_MODEL)), full((3, D_MODEL, D_MODEL)),
            full(w1.shape), full(w2.shape), full((1, D_MODEL)),
            full(a1.shape), full(a2.shape), full((1, D_MODEL)),
            full(g1.shape), full(g2.shape),
        ],
        out_specs=[row_spec] * 6,
        out_shape=[out_sds] * 6,
        compiler_params=_params("parallel"),
        name="rwkv_proj",
    )(x, x, mod, gpre.reshape(1, D_MODEL), mu, _bf(w_rkv), _bf(w1), _bf(w2),
      w0.reshape(1, D_MODEL), _bf(a1), _bf(a2), a0.reshape(1, D_MODEL), _bf(g1), _bf(g2))


def _wkv_pair(r, k, v, lw, a, k_k, k_a, r_k, ln_g, ln_b, s_ref, o_ref):
    c = WKV_CHUNK
    lane = lax.broadcasted_iota(jnp.int32, (c, LANES), 1)
    head0 = lane < HEAD_DIM
    rowi = lax.broadcasted_iota(jnp.int32, (LANES, LANES), 0)
    coli = lax.broadcasted_iota(jnp.int32, (LANES, LANES), 1)
    rt, ct = rowi & (c - 1), coli & (c - 1)
    strict, incl, eye = rt > ct, rt >= ct, rowi == coli
    tril = (lax.broadcasted_iota(jnp.int32, (c, c), 0)
            >= lax.broadcasted_iota(jnp.int32, (c, c), 1)).astype(BF16)

    def headsum(x):
        s0 = jnp.sum(jnp.where(head0, x, 0.0), axis=-1, keepdims=True)
        s1 = jnp.sum(jnp.where(head0, 0.0, x), axis=-1, keepdims=True)
        return jnp.where(head0, s0, s1)

    def stack(x):
        return jnp.concatenate([jnp.where(head0, x, 0.0), jnp.where(head0, 0.0, x)], axis=0)

    kk = k * k_k
    kk = kk / jnp.maximum(jnp.sqrt(headsum(kk * kk)), 1e-12)
    kf = k * (1.0 + (a - 1.0) * k_a)
    bvec = kk * a
    yield

    cum = _nn(tril, jnp.concatenate(_split3(lw), axis=1))
    cum = cum[:, :LANES] + cum[:, LANES:2 * LANES] + cum[:, 2 * LANES:]
    cum_end = cum[c - 1:c]
    w_inv = jnp.exp(-cum)
    w_rem = jnp.exp(cum_end - cum)
    a_st = stack(-kk * jnp.exp(cum - lw))
    r_st = stack(r * jnp.exp(cum))
    b_st = stack(bvec * w_inv)
    k_st = stack(kf * w_inv)
    bp_st = _bf(stack(bvec * w_rem))
    kp_st = _bf(stack(kf * w_rem))
    v_st = _bf(stack(v))
    yield

    sc = _nt(_bf(jnp.concatenate([a_st, r_st], axis=0)), _bf(jnp.concatenate([b_st, k_st], axis=0)))
    l_ab = jnp.where(strict, sc[:LANES, :LANES], 0.0)
    l_ak = jnp.where(strict, sc[:LANES, LANES:], 0.0)
    l_rb = jnp.where(incl, sc[LANES:, :LANES], 0.0)
    l_rk = jnp.where(incl, sc[LANES:, LANES:], 0.0)
    yield

    tinv = jnp.where(eye, 1.0, l_ab)
    pw = l_ab
    for _ in range(c.bit_length() - 2):
        pwb = _bf(pw)
        pw = _nn(pwb, pwb)
        tinv = tinv + _nn(_bf(tinv), _bf(pw))
        yield

    au = _nn(_bf(tinv), _bf(jnp.concatenate([a_st, _nn(_bf(l_ak), v_st)], axis=1)))
    aub = _bf(au)
    yield
    rb = _nn(_bf(l_rb), aub)
    gh = _tn(bp_st, aub)
    r_hat = r_st + rb[:, :LANES]
    o0 = rb[:, LANES:] + _nn(_bf(l_rk), v_st)
    g_mat = jnp.where(eye, jnp.exp(cum_end), 0.0) + gh[:, :LANES]
    h_mat = gh[:, LANES:] + _tn(kp_st, v_st)
    yield

    os = _nn(_bf(jnp.concatenate([r_hat, g_mat], axis=0)), _bf(s_ref[...]))
    s_ref[...] = os[LANES:] + h_mat
    o_st = os[:LANES] + o0
    o = o_st[:c] + o_st[c:]
    yield

    mean = headsum(o) * (1.0 / HEAD_DIM)
    oc = o - mean
    var = headsum(oc * oc) * (1.0 / HEAD_DIM)
    on = oc * lax.rsqrt(var + LNX_EPS) * ln_g + ln_b
    o_ref[...] = on + headsum(r * kf * r_k) * v


def _wkv_kernel(r_ref, k_ref, v_ref, lw_ref, a_ref, kk_ref, ka_ref, rk_ref, lg_ref, lb_ref,
                o_ref, s_ref):
    @pl.when(pl.program_id(2) == 0)
    def _():
        s_ref[...] = jnp.zeros_like(s_ref)

    def pair(p):
        sl = pl.ds(p * LANES, LANES)
        ins = [ref[:, sl] for ref in (r_ref, k_ref, v_ref, lw_ref, a_ref, kk_ref, ka_ref, rk_ref, lg_ref, lb_ref)]
        return _wkv_pair(*ins, s_ref.at[p], o_ref.at[:, sl])

    stages = [pair(p) for p in range(WKV_PAIRS)]
    while stages:
        stages = [g for g in stages if next(g, True) is None]


def _wkv(r, k, v, lw, a, k_k, k_a, r_k, lnx_g, lnx_b):
    n_chunks = SEQ // WKV_CHUNK
    width = WKV_PAIRS * LANES
    tok = pl.BlockSpec((WKV_CHUNK, width), lambda b, p, c: (b * n_chunks + c, p))
    par = pl.BlockSpec((1, width), lambda b, p, c: (0, p))
    vec = lambda t: t.reshape(1, D_MODEL)
    return pl.pallas_call(
        _wkv_kernel,
        grid=(BATCH, N_PAIRS // WKV_PAIRS, n_chunks),
        in_specs=[tok] * 5 + [par] * 5,
        out_specs=tok,
        out_shape=jax.ShapeDtypeStruct((TOKENS, D_MODEL), F32),
        scratch_shapes=[pltpu.VMEM((WKV_PAIRS, LANES, LANES), F32)],
        compiler_params=_params("parallel", "parallel", "arbitrary"),
        name="wkv_scan",
    )(r, k, v, lw, a, vec(k_k), vec(k_a), vec(r_k), vec(lnx_g), vec(lnx_b))


def _pack_rows(x):
    bits = lax.bitcast_convert_type(_bf(x).astype(F32), jnp.uint32)
    halves = []
    for h in range(PACK_HALVES):
        lo = bits[:, (2 * h) * PACK_WORDS:(2 * h + 1) * PACK_WORDS]
        hi = bits[:, (2 * h + 1) * PACK_WORDS:(2 * h + 2) * PACK_WORDS]
        halves.append(lax.bitcast_convert_type((lo >> 16) | (hi & jnp.uint32(0xFFFF0000)), jnp.int32))
    return halves


def _unpack_rows(halves):
    parts = []
    for w in halves:
        u = lax.bitcast_convert_type(w, jnp.uint32)
        parts.append(lax.bitcast_convert_type(u << 16, F32))
        parts.append(lax.bitcast_convert_type(u & jnp.uint32(0xFFFF0000), F32))
    return jnp.concatenate(parts, axis=1)


def _post_kernel(*refs, has_gate):
    if has_gate:
        o_ref, g_ref, *refs = refs
    else:
        o_ref, *refs = refs
    (x_ref, mod_ref, wo_ref, gpost_ref, gpre_ref, wr_hi_ref, wr_lo_ref, bias_ref,
     x1_ref, h2p_ref, gk_ref, ek_ref, rk_ref, cnt_ref, run_ref) = refs
    tm = ROW_TILE

    @pl.when(pl.program_id(0) == 0)
    def _():
        run_ref[...] = jnp.zeros_like(run_ref)

    mod = mod_ref[0]
    gate_m, shift_f, scale_f = mod[2:3], mod[3:4], mod[4:5]
    o = o_ref[...]
    if has_gate:
        o = o * g_ref[...]
    y = _nn(_bf(o), wo_ref[...])
    x1 = x_ref[...] + gate_m * _rms(y, gpost_ref[...])
    x1_ref[...] = x1
    h2 = _rms(x1, gpre_ref[...]) * (1.0 + scale_f) + shift_f
    for h, words in enumerate(_pack_rows(h2)):
        h2p_ref[h] = words

    hi = _bf(h2)
    lo = _bf(h2 - hi.astype(F32))
    wr_hi, wr_lo = wr_hi_ref[...], wr_lo_ref[...]
    scores = _sigmoid(_nt(wr_hi, hi) + (_nt(wr_hi, lo) + _nt(wr_lo, hi)))
    erow = lax.broadcasted_iota(jnp.int32, scores.shape, 0)
    work = scores + jnp.concatenate([bias_ref[...]] * (tm // LANES), axis=1)
    picked = jnp.zeros_like(scores)
    chosen = jnp.zeros_like(scores)
    hits, ids = [], []
    for _ in range(TOP_K):
        best = jnp.max(work, axis=0, keepdims=True)
        idx = jnp.min(jnp.where(work == best, erow, N_EXPERTS), axis=0, keepdims=True)
        hit = erow == idx
        picked = jnp.where(hit, scores, picked)
        chosen = jnp.where(hit, 1.0, chosen)
        work = jnp.where(hit, -jnp.inf, work)
        hits.append(hit)
        ids.append(idx)
    gates = ROUTED_SCALE * picked / jnp.sum(picked, axis=0, keepdims=True)

    upto = (lax.broadcasted_iota(jnp.int32, (tm, tm), 0)
            <= lax.broadcasted_iota(jnp.int32, (tm, tm), 1)).astype(BF16)
    chosen_b = _bf(chosen)
    run = run_ref[...]
    before = _nn(chosen_b, upto) - chosen + jnp.concatenate([run] * (tm // LANES), axis=1)
    run = run + _nn(chosen_b, jnp.ones((tm, LANES), BF16))
    run_ref[...] = run
    cnt_ref[...] = run

    krow = lax.broadcasted_iota(jnp.int32, (TOP_K, tm), 0)
    gk = jnp.zeros((TOP_K, tm), F32)
    ek = jnp.zeros((TOP_K, tm), jnp.int32)
    rk = jnp.zeros((TOP_K, tm), jnp.int32)
    for j in range(TOP_K):
        grow = jnp.sum(jnp.where(hits[j], gates, 0.0), axis=0, keepdims=True)
        rrow = jnp.sum(jnp.where(hits[j], before, 0.0), axis=0, keepdims=True)
        gk = jnp.where(krow == j, grow, gk)
        ek = jnp.where(krow == j, ids[j], ek)
        rk = jnp.where(krow == j, rrow.astype(jnp.int32), rk)
    gk_ref[...] = gk
    ek_ref[...] = ek
    rk_ref[...] = rk


def _post(o, g, x, mod, w_o, gpost, gpre, w_router, router_bias):
    tm = ROW_TILE
    full = lambda shape: pl.BlockSpec(shape, lambda i: (0,) * len(shape))
    row_spec = pl.BlockSpec((tm, D_MODEL), lambda i: (i, 0))
    k_spec = pl.BlockSpec((TOP_K, tm), lambda i: (0, i))
    wr = w_router.T
    wr_hi = _bf(wr)
    wr_lo = _bf(wr - wr_hi.astype(F32))
    bias = jnp.broadcast_to(router_bias[:, None], (N_EXPERTS, LANES))
    has_gate = g is not None
    acts = [o, g] if has_gate else [o]
    return pl.pallas_call(
        functools.partial(_post_kernel, has_gate=has_gate),
        grid=(TOKENS // tm,),
        in_specs=[row_spec] * (len(acts) + 1) + [
            pl.BlockSpec((1, N_ADA, D_MODEL), lambda i: (i // (SEQ // tm), 0, 0)),
            full((D_MODEL, D_MODEL)), full((1, D_MODEL)), full((1, D_MODEL)),
            full((N_EXPERTS, D_MODEL)), full((N_EXPERTS, D_MODEL)), full((N_EXPERTS, LANES)),
        ],
        out_specs=[row_spec, pl.BlockSpec((PACK_HALVES, tm, PACK_WORDS), lambda i: (0, i, 0)),
                   k_spec, k_spec, k_spec, full((N_EXPERTS, LANES))],
        out_shape=[jax.ShapeDtypeStruct((TOKENS, D_MODEL), F32),
                   jax.ShapeDtypeStruct((PACK_HALVES, TOKENS, PACK_WORDS), jnp.int32),
                   jax.ShapeDtypeStruct((TOP_K, TOKENS), F32),
                   jax.ShapeDtypeStruct((TOP_K, TOKENS), jnp.int32),
                   jax.ShapeDtypeStruct((TOP_K, TOKENS), jnp.int32),
                   jax.ShapeDtypeStruct((N_EXPERTS, LANES), F32)],
        scratch_shapes=[pltpu.VMEM((N_EXPERTS, LANES), F32)],
        compiler_params=_params("arbitrary"),
        name="mixer_out_router",
    )(*acts, x, mod, _bf(w_o), gpost.reshape(1, D_MODEL), gpre.reshape(1, D_MODEL),
      wr_hi, wr_lo, bias)


def _sc_mesh():
    return plsc.VectorSubcoreMesh(core_axis_name="core", subcore_axis_name="subcore")


def _sc_scatter_rows(rows, idx, n_out):
    n_copies = idx.shape[0]

    @pl.kernel(out_type=jax.ShapeDtypeStruct((PACK_HALVES, n_out, PACK_WORDS), rows.dtype),
               mesh=_sc_mesh(), scratch_types=[], name="sc_dispatch_rows")
    def scatter(x_hbm, i_hbm, o_hbm):
        for h in range(PACK_HALVES):
            def body(x_vmem, i_vmem, h=h):
                for k in range(n_copies):
                    pltpu.sync_copy(x_vmem, o_hbm.at[h].at[i_vmem.at[k]])

            pltpu.emit_pipeline(
                body, grid=(rows.shape[1] // SC_WINDOW,),
                in_specs=[pl.BlockSpec((SC_WINDOW, PACK_WORDS), index_map=lambda i: (i, 0)),
                          pl.BlockSpec((n_copies, SC_WINDOW), index_map=lambda i: (0, i))],
                out_specs=[],
                core_axis_name=("core", "subcore"),
                dimension_semantics=(pltpu.PARALLEL,),
            )(x_hbm.at[h], i_hbm)

    return scatter(rows, idx)


def _sc_gather_rows(table, idx):
    n = idx.shape[0]

    @pl.kernel(out_type=jax.ShapeDtypeStruct((PACK_HALVES, n, PACK_WORDS), table.dtype),
               mesh=_sc_mesh(), scratch_types=[], name="sc_collect_rows")
    def gather(x_hbm, i_hbm, o_hbm):
        for h in range(PACK_HALVES):
            def body(i_vmem, o_vmem, h=h):
                pltpu.sync_copy(x_hbm.at[h].at[i_vmem.at[0]], o_vmem)

            pltpu.emit_pipeline(
                body, grid=(n // SC_WINDOW,),
                in_specs=[pl.BlockSpec((1, SC_WINDOW), index_map=lambda i: (0, i))],
                out_specs=[pl.BlockSpec((SC_WINDOW, PACK_WORDS), index_map=lambda i: (i, 0))],
                core_axis_name=("core", "subcore"),
                dimension_semantics=(pltpu.PARALLEL,),
            )(i_hbm, o_hbm.at[h])

    return gather(table, idx.reshape(1, n))


def _ffn(x, wg, wu, wd):
    up = _nn(x, _bf(wu))
    gt = _nn(x, _bf(wg))
    return _nn(_bf(gt * _sigmoid(gt) * up), _bf(wd))


def _expert_kernel(be_ref, nu_ref, xs_ref, wg_ref, wu_ref, wd_ref, ys_ref, wgb_ref, wub_ref, wdb_ref):
    b = pl.program_id(0)

    @pl.when(jnp.logical_or(b == 0, be_ref[b] != be_ref[jnp.maximum(b - 1, 0)]))
    def _():
        wgb_ref[...] = _bf(wg_ref[0, 0])
        wub_ref[...] = _bf(wu_ref[0, 0])
        wdb_ref[...] = _bf(wd_ref[0, 0])

    @pl.when(b < nu_ref[0])
    def _():
        x = _bf(_unpack_rows([xs_ref[h] for h in range(PACK_HALVES)]))
        y = _ffn(x, wgb_ref[...], wub_ref[...], wdb_ref[...])
        for h, words in enumerate(_pack_rows(y)):
            ys_ref[h] = words


def _experts(layer, xs, blk_expert, n_used, w_gate, w_up, w_down):
    n_blocks = xs.shape[1] // EXPERT_BLOCK
    row_spec = pl.BlockSpec((PACK_HALVES, EXPERT_BLOCK, PACK_WORDS), lambda b, be, nu: (0, b, 0))
    w_spec = lambda shape: pl.BlockSpec((1, 1) + shape, lambda b, be, nu: (layer, be[b], 0, 0))
    return pl.pallas_call(
        _expert_kernel,
        grid_spec=pltpu.PrefetchScalarGridSpec(
            num_scalar_prefetch=2,
            grid=(n_blocks,),
            in_specs=[row_spec, w_spec((D_MODEL, D_EXPERT)), w_spec((D_MODEL, D_EXPERT)),
                      w_spec((D_EXPERT, D_MODEL))],
            out_specs=row_spec,
            scratch_shapes=[pltpu.VMEM((D_MODEL, D_EXPERT), BF16), pltpu.VMEM((D_MODEL, D_EXPERT), BF16),
                            pltpu.VMEM((D_EXPERT, D_MODEL), BF16)],
        ),
        out_shape=jax.ShapeDtypeStruct(xs.shape, xs.dtype),
        compiler_params=_params("arbitrary"),
        name="moe_experts",
    )(blk_expert, n_used, xs, w_gate, w_up, w_down)


def _shared_kernel(h2p_ref, sg_ref, su_ref, sd_ref, o_ref):
    h2 = _bf(_unpack_rows([h2p_ref[h] for h in range(PACK_HALVES)]))
    for h, words in enumerate(_pack_rows(_ffn(h2, sg_ref[0], su_ref[0], sd_ref[0]))):
        o_ref[h] = words


def _shared_expert(layer, h2p, ws_gate, ws_up, ws_down):
    tm = SHARED_TILE
    row_spec = pl.BlockSpec((PACK_HALVES, tm, PACK_WORDS), lambda i: (0, i, 0))
    return pl.pallas_call(
        _shared_kernel,
        grid=(TOKENS // tm,),
        in_specs=[row_spec,
                  pl.BlockSpec((1, D_MODEL, D_EXPERT), lambda i: (layer, 0, 0)),
                  pl.BlockSpec((1, D_MODEL, D_EXPERT), lambda i: (layer, 0, 0)),
                  pl.BlockSpec((1, D_EXPERT, D_MODEL), lambda i: (layer, 0, 0))],
        out_specs=row_spec,
        out_shape=jax.ShapeDtypeStruct(h2p.shape, h2p.dtype),
        compiler_params=_params("parallel"),
        name="moe_shared_expert",
    )(h2p, ws_gate, ws_up, ws_down)


def _combine_kernel(*refs):
    g_refs = refs[:TOP_K]
    gk_ref, sh_ref, x_ref, mod_ref, gpost_ref, o_ref = refs[TOP_K:]
    acc = _unpack_rows([sh_ref[h] for h in range(PACK_HALVES)])
    gk = gk_ref[...]
    lane = lax.broadcasted_iota(jnp.int32, gk.shape, 1)
    for j in range(TOP_K):
        gcol = jnp.sum(jnp.where(lane == j, gk, 0.0), axis=-1, keepdims=True)
        acc = acc + gcol * _unpack_rows([g_refs[j][h] for h in range(PACK_HALVES)])
    gate_f = mod_ref[0][5:6]
    o_ref[...] = x_ref[...] + gate_f * _rms(acc, gpost_ref[...])


def _combine(picked, gk, shared, x1, mod, gpost):
    tm = ROW_TILE
    n_tiles = TOKENS // tm
    row_spec = pl.BlockSpec((tm, D_MODEL), lambda i: (i, 0))
    pick_spec = lambda j: pl.BlockSpec((PACK_HALVES, tm, PACK_WORDS), lambda i: (0, j * n_tiles + i, 0))
    return pl.pallas_call(
        _combine_kernel,
        grid=(n_tiles,),
        in_specs=[pick_spec(j) for j in range(TOP_K)] + [
            pl.BlockSpec((tm, LANES), lambda i: (i, 0)),
            pl.BlockSpec((PACK_HALVES, tm, PACK_WORDS), lambda i: (0, i, 0)),
            row_spec,
            pl.BlockSpec((1, N_ADA, D_MODEL), lambda i: (i // (SEQ // tm), 0, 0)),
            pl.BlockSpec((1, D_MODEL), lambda i: (0, 0)),
        ],
        out_specs=row_spec,
        out_shape=jax.ShapeDtypeStruct((TOKENS, D_MODEL), F32),
        compiler_params=_params("parallel"),
        name="moe_combine",
    )(*([picked] * TOP_K), gk, shared, x1, mod, gpost.reshape(1, D_MODEL))


def _moe(layer, x1, h2p, gk, ek, rk, counts, mod, gpost, w_gate, w_up, w_down, ws_gate, ws_up, ws_down):
    cnt = counts[:, 0].astype(jnp.int32)
    padded = (cnt + EXPERT_BLOCK - 1) // EXPERT_BLOCK * EXPERT_BLOCK
    eid = jnp.arange(N_EXPERTS, dtype=jnp.int32)
    pend = jnp.sum(jnp.where(eid[:, None] <= eid[None, :], padded[:, None], 0), axis=0)
    pstart = pend - padded
    dest = rk + jnp.sum(jnp.where(ek[None] == eid[:, None, None], pstart[:, None, None], 0), axis=0)
    gk = jnp.pad(gk.T, ((0, 0), (0, LANES - TOP_K)))
    n_blocks = (TOKENS * TOP_K + N_EXPERTS * (EXPERT_BLOCK - 1)) // EXPERT_BLOCK + 1
    blk_start = jnp.arange(n_blocks, dtype=jnp.int32) * EXPERT_BLOCK
    blk_expert = jnp.minimum(jnp.sum((blk_start[None, :] >= pend[:, None]).astype(jnp.int32), axis=0),
                             N_EXPERTS - 1)
    n_used = pend[-1:] // EXPERT_BLOCK

    xs = _sc_scatter_rows(h2p, dest, n_blocks * EXPERT_BLOCK)
    shared = _shared_expert(layer, h2p, ws_gate, ws_up, ws_down)
    ys = _experts(layer, xs, blk_expert, n_used, w_gate, w_up, w_down)
    picked = _sc_gather_rows(ys, dest.reshape(-1))
    return _combine(picked, gk, shared, x1, mod, gpost)


def _sb_proj_kernel(x_ref, mod_ref, gpre_ref, w_ref, q_ref, k_ref, v_ref):
    mod = mod_ref[0]
    h = _bf(_rms(x_ref[...], gpre_ref[...]) * (1.0 + mod[1:2]) + mod[0:1])
    q_ref[...] = _bf(_nn(h, w_ref[:, :D_MODEL]) * (LOG2_E / 8.0))
    k_ref[...] = _bf(_nn(h, w_ref[:, D_MODEL:2 * D_MODEL]))
    v_ref[...] = _bf(_nn(h, w_ref[:, 2 * D_MODEL:]))


def _sb_proj(x, mod, gpre, w_qkv):
    tm = ROW_TILE
    full = lambda shape: pl.BlockSpec(shape, lambda i: (0,) * len(shape))
    row_spec = pl.BlockSpec((tm, D_MODEL), lambda i: (i, 0))
    out_sds = jax.ShapeDtypeStruct((TOKENS, D_MODEL), BF16)
    return pl.pallas_call(
        _sb_proj_kernel,
        grid=(TOKENS // tm,),
        in_specs=[row_spec,
                  pl.BlockSpec((1, N_ADA, D_MODEL), lambda i: (i // (SEQ // tm), 0, 0)),
                  full((1, D_MODEL)), full((D_MODEL, 3 * D_MODEL))],
        out_specs=[row_spec] * 3,
        out_shape=[out_sds] * 3,
        compiler_params=_params("parallel"),
        name="sb_qkv_proj",
    )(x, mod, gpre.reshape(1, D_MODEL), _bf(w_qkv))


def _sb_attn_kernel(q_ref, k_ref, v_ref, o_ref):
    blk = ATT_BLOCK
    n_heads = LANES // HEAD_DIM
    qb = pl.program_id(2)
    q = q_ref[...]
    lane = lax.broadcasted_iota(jnp.int32, (blk, LANES), 1)
    head0 = lane < HEAD_DIM
    qh = [jnp.where(head0, q, jnp.zeros_like(q)), jnp.where(head0, jnp.zeros_like(q), q)]
    rowi = lax.broadcasted_iota(jnp.int32, (blk, blk), 0)
    coli = lax.broadcasted_iota(jnp.int32, (blk, blk), 1)
    later = (rowi > coli).astype(BF16)
    ones = jnp.ones((blk, LANES), BF16)

    sub = ATT_SUB
    n_sub = blk // sub
    sub_row = lax.broadcasted_iota(jnp.int32, (sub, blk), 0)
    sub_col = lax.broadcasted_iota(jnp.int32, (sub, blk), 1)
    below = [sub_col < sub_row + r * sub for r in range(n_sub)]
    sub_head0 = lax.broadcasted_iota(jnp.int32, (sub, LANES), 1) < HEAD_DIM
    chains = [(hd, r) for hd in range(n_heads) for r in range(n_sub)]

    def chain(hd, r, ks, vs, get_carry, get_acc, mode, out):
        rows = slice(r * sub, (r + 1) * sub)
        u = _nt(qh[hd][rows], ks)
        yield
        neg_abs = lax.bitcast_convert_type(
            lax.bitcast_convert_type(u, jnp.uint32) | jnp.uint32(0x80000000), F32)
        lb = jnp.minimum(u, 0.0) - jnp.log2(1.0 + jnp.exp2(neg_abs))
        l1m = lb - u
        if mode == "diag":
            l1m = jnp.where(below[r], l1m, 0.0)
        l1b = _bf(l1m)
        yield
        carry = get_carry()
        tail = _nn(l1b, later) + jnp.concatenate([carry] * (blk // LANES), axis=1)
        out["carry"] = carry + _nn(l1b, ones)
        yield
        w = jnp.exp2(lb + tail)
        if mode == "diag":
            w = jnp.where(below[r], w, 0.0)
        elif mode == "prev":
            w = jnp.where(qb > 0, w, 0.0)
        wb = _bf(w)
        yield
        out["acc"] = get_acc() + _nn(wb, vs)

    def load_kv(kb):
        start = pl.multiple_of(kb * blk, blk)
        return k_ref[pl.ds(start, blk), :], v_ref[pl.ds(start, blk), :]

    def run(gens):
        while gens:
            gens = [g for g in gens if next(g, True) is None]

    def block(kb, carry, acc):
        ks, vs = load_kv(kb)
        outs = [{} for _ in chains]
        run([chain(hd, r, ks, vs, lambda c=c: carry[c], lambda c=c: acc[c], "plain", outs[c])
             for c, (hd, r) in enumerate(chains)])
        return [o["carry"] for o in outs], [o["acc"] for o in outs]

    zeros = jnp.zeros((sub, LANES), F32)
    ks_d, vs_d = load_kv(qb)
    ks_p, vs_p = load_kv(jnp.maximum(qb - 1, 0))
    outs_d = [{} for _ in chains]
    outs_p = [{} for _ in chains]
    run([chain(hd, r, ks_d, vs_d, lambda: zeros, lambda: zeros, "diag", outs_d[c])
         for c, (hd, r) in enumerate(chains)]
        + [chain(hd, r, ks_p, vs_p, lambda c=c: outs_d[c]["carry"], lambda c=c: outs_d[c]["acc"],
                 "prev", outs_p[c]) for c, (hd, r) in enumerate(chains)])
    carry = [o["carry"] for o in outs_p]
    acc = [o["acc"] for o in outs_p]

    def cmax_of(cr):
        return jnp.max(functools.reduce(jnp.maximum, cr))

    def cond(st):
        kb, _, _, cmax = st
        return jnp.logical_and(kb >= 0, cmax > EXP2_UNDERFLOW)

    def body(st):
        kb, cr, ac, _ = st
        cr, ac = block(kb, list(cr), list(ac))
        return kb - 1, tuple(cr), tuple(ac), cmax_of(cr)

    _, _, acc, _ = lax.while_loop(cond, body, (qb - 2, tuple(carry), tuple(acc), cmax_of(carry)))
    for r in range(n_sub):
        rows = slice(r * sub, (r + 1) * sub)
        o_ref[rows, :] = jnp.where(sub_head0, acc[r], acc[n_sub + r])


def _sb_attn(q, k, v):
    blk = ATT_BLOCK
    n_blk = SEQ // blk
    q_spec = pl.BlockSpec((blk, LANES), lambda b, p, i: (b * n_blk + i, p))
    kv_spec = pl.BlockSpec((SEQ, LANES), lambda b, p, i: (b, p))
    return pl.pallas_call(
        _sb_attn_kernel,
        grid=(BATCH, N_PAIRS, n_blk),
        in_specs=[q_spec, kv_spec, kv_spec],
        out_specs=q_spec,
        out_shape=jax.ShapeDtypeStruct((TOKENS, D_MODEL), F32),
        compiler_params=_params("parallel", "parallel", "arbitrary"),
        name="sb_attention",
    )(q, k, v)


def kernel(x, c, ada_w, ada_b, norm_pre_mix, norm_post_mix, norm_pre_ffn, norm_post_ffn, rwkv_mu, rwkv_w_rkv, rwkv_w_w1, rwkv_w_w2, rwkv_w0, rwkv_w_a1, rwkv_w_a2, rwkv_a0, rwkv_w_g1, rwkv_w_g2, rwkv_k_k, rwkv_k_a, rwkv_r_k, rwkv_lnx_g, rwkv_lnx_b, rwkv_w_o, sb_w_qkv, sb_w_o, moe_w_router, moe_router_bias, moe_w_gate, moe_w_up, moe_w_down, moe_ws_gate, moe_ws_up, moe_ws_down):
    mod_all = _ada_mod(c, ada_w, ada_b)
    xt = x.reshape(TOKENS, D_MODEL)
    for layer in range(DEPTH):
        mod = mod_all[layer]
        i = layer // 2
        if layer % 2 == 0:
            r, k, v, lw, a, g = _rwkv_proj(
                xt, mod, norm_pre_mix[layer], rwkv_mu[i], rwkv_w_rkv[i], rwkv_w_w1[i], rwkv_w_w2[i],
                rwkv_w0[i], rwkv_w_a1[i], rwkv_w_a2[i], rwkv_a0[i], rwkv_w_g1[i], rwkv_w_g2[i])
            o = _wkv(r, k, v, lw, a, rwkv_k_k[i], rwkv_k_a[i], rwkv_r_k[i], rwkv_lnx_g[i], rwkv_lnx_b[i])
            w_o = rwkv_w_o[i]
        else:
            q, k, v = _sb_proj(xt, mod, norm_pre_mix[layer], sb_w_qkv[i])
            o, g = _sb_attn(q, k, v), None
            w_o = sb_w_o[i]
        x1, h2p, gk, ek, rk, counts = _post(o, g, xt, mod, w_o, norm_post_mix[layer], norm_pre_ffn[layer],
                                            moe_w_router[layer], moe_router_bias[layer])
        xt = _moe(layer, x1, h2p, gk, ek, rk, counts, mod, norm_post_ffn[layer], moe_w_gate, moe_w_up,
                  moe_w_down, moe_ws_gate, moe_ws_up, moe_ws_down)
    return xt.reshape(BATCH, SEQ, D_MODEL)
```

```python
import functools

import jax
import jax.numpy as jnp
from jax import lax
from jax.experimental import pallas as pl
from jax.experimental.pallas import tpu as pltpu
from jax.experimental.pallas import tpu_sc as plsc

D_MODEL = 1024
BATCH = 2
SEQ = 8192
TOKENS = BATCH * SEQ
DEPTH = 2
HEAD_DIM = 64
N_EXPERTS = 64
TOP_K = 8
D_EXPERT = 256
ROUTED_SCALE = 2.5
RMS_EPS = 1e-6
LNX_EPS = 64e-5
N_ADA = 6

LANES = 128
SUBLANES = 8
N_PAIRS = D_MODEL // LANES
WKV_CHUNK = 64
WKV_PAIRS = 8
ROW_TILE = 512
SHARED_TILE = 512
EXPERT_BLOCK = 512
PACK_HALVES = 2
PACK_WORDS = D_MODEL // (2 * PACK_HALVES)
SC_WINDOW = 128
ATT_BLOCK = 256
ATT_SUB = 128
LOG2_E = 1.4426950408889634
EXP2_UNDERFLOW = -153.0
VMEM_LIMIT = 56 * 1024 * 1024

F32 = jnp.float32
BF16 = jnp.bfloat16


def _nn(a, b):
    return jnp.dot(a, b, preferred_element_type=F32)


def _nt(a, b):
    return lax.dot_general(a, b, (((1,), (1,)), ((), ())), preferred_element_type=F32)


def _tn(a, b):
    return lax.dot_general(a, b, (((0,), (0,)), ((), ())), preferred_element_type=F32)


def _bf(x):
    return x.astype(BF16)


def _sigmoid(x):
    return 1.0 / (1.0 + jnp.exp(-x))


def _rms(xv, g):
    ms = jnp.mean(xv * xv, axis=-1, keepdims=True)
    return xv * lax.rsqrt(ms + RMS_EPS) * g


def _split3(x):
    hi = x.astype(BF16)
    r1 = x - hi.astype(F32)
    mid = r1.astype(BF16)
    lo = (r1 - mid.astype(F32)).astype(BF16)
    return hi, mid, lo


def _params(*sem):
    return pltpu.CompilerParams(dimension_semantics=sem, vmem_limit_bytes=VMEM_LIMIT)


def _ada_kernel(c_ref, w_ref, b_ref, o_ref):
    cv = c_ref[...]
    o_ref[0] = _nn(cv * _sigmoid(cv), w_ref[0]) + b_ref[0]


def _ada_mod(c, ada_w, ada_b):
    tn = 1536
    c8 = jnp.pad(c, ((0, SUBLANES - BATCH), (0, 0)))
    out = pl.pallas_call(
        _ada_kernel,
        grid=(DEPTH, N_ADA * D_MODEL // tn),
        in_specs=[
            pl.BlockSpec((SUBLANES, D_MODEL), lambda l, j: (0, 0)),
            pl.BlockSpec((1, D_MODEL, tn), lambda l, j: (l, 0, j)),
            pl.BlockSpec((1, 1, tn), lambda l, j: (l, 0, j)),
        ],
        out_specs=pl.BlockSpec((1, SUBLANES, tn), lambda l, j: (l, 0, j)),
        out_shape=jax.ShapeDtypeStruct((DEPTH, SUBLANES, N_ADA * D_MODEL), F32),
        compiler_params=_params("parallel", "parallel"),
        name="ada_mod",
    )(c8, ada_w, ada_b.reshape(DEPTH, 1, N_ADA * D_MODEL))
    return out[:, :BATCH].reshape(DEPTH, BATCH, N_ADA, D_MODEL)


def _rwkv_proj_kernel(x_ref, xp_ref, mod_ref, gpre_ref, mu_ref, wrkv_ref, w1_ref, w2_ref, w0_ref,
                      a1_ref, a2_ref, a0_ref, g1_ref, g2_ref,
                      r_ref, k_ref, v_ref, lw_ref, a_ref, g_ref):
    i = pl.program_id(0)
    mod = mod_ref[0]
    shift, scale = mod[0:1], mod[1:2]
    gpre = gpre_ref[...]

    def modnorm(xv):
        return _rms(xv, gpre) * (1.0 + scale) + shift

    h = modnorm(x_ref[...])
    hp = modnorm(xp_ref[...])[SUBLANES - 1:SUBLANES]
    hp = jnp.where(i % (SEQ // ROW_TILE) == 0, 0.0, hp)
    row = lax.broadcasted_iota(jnp.int32, h.shape, 0)
    hs = jnp.where(row == 0, hp, pltpu.roll(h, 1, 0))
    xx = hs - h
    mu = mu_ref[...]

    def mix(n):
        return _bf(h + xx * mu[n:n + 1])

    r_ref[...] = _nn(mix(0), wrkv_ref[0])
    k_ref[...] = _nn(mix(1), wrkv_ref[1])
    v_ref[...] = _nn(mix(2), wrkv_ref[2])
    wl = w0_ref[...] + _nn(_bf(jnp.tanh(_nn(mix(3), w1_ref[...]))), w2_ref[...])
    nwl = -wl
    sp = jnp.maximum(nwl, 0.0) + jnp.log(1.0 + jnp.exp(-jnp.abs(nwl)))
    lw_ref[...] = -jnp.exp(-sp - 0.5)
    a_ref[...] = _sigmoid(a0_ref[...] + _nn(_bf(_nn(mix(4), a1_ref[...])), a2_ref[...]))
    g_ref[...] = _nn(_bf(_sigmoid(_nn(mix(5), g1_ref[...]))), g2_ref[...])


def _rwkv_proj(x, mod, gpre, mu, w_rkv, w1, w2, w0, a1, a2, a0, g1, g2):
    tm = ROW_TILE
    full = lambda shape: pl.BlockSpec(shape, lambda i: (0,) * len(shape))
    row_spec = pl.BlockSpec((tm, D_MODEL), lambda i: (i, 0))
    out_sds = jax.ShapeDtypeStruct((TOKENS, D_MODEL), F32)
    return pl.pallas_call(
        _rwkv_proj_kernel,
        grid=(TOKENS // tm,),
        in_specs=[
            row_spec,
            pl.BlockSpec((SUBLANES, D_MODEL), lambda i: (jnp.maximum(i * (tm // SUBLANES) - 1, 0), 0)),
            pl.BlockSpec((1, N_ADA, D_MODEL), lambda i: (i // (SEQ // tm), 0, 0)),
            full((1, D_MODEL)), full((6, D_MODEL)), full((3, D_MODEL, D_MODEL)),
            full(w1.shape), full(w2.shape), full((1, D_MODEL)),
            full(a1.shape), full(a2.shape), full((1, D_MODEL)),
            full(g1.shape), full(g2.shape),
        ],
        out_specs=[row_spec] * 6,
        out_shape=[out_sds] * 6,
        compiler_params=_params("parallel"),
        name="rwkv_proj",
    )(x, x, mod, gpre.reshape(1, D_MODEL), mu, _bf(w_rkv), _bf(w1), _bf(w2),
      w0.reshape(1, D_MODEL), _bf(a1), _bf(a2), a0.reshape(1, D_MODEL), _bf(g1), _bf(g2))


def _wkv_pair(r, k, v, lw, a, k_k, k_a, r_k, ln_g, ln_b, s_ref, o_ref):
    c = WKV_CHUNK
    lane = lax.broadcasted_iota(jnp.int32, (c, LANES), 1)
    head0 = lane < HEAD_DIM
    rowi = lax.broadcasted_iota(jnp.int32, (LANES, LANES), 0)
    coli = lax.broadcasted_iota(jnp.int32, (LANES, LANES), 1)
    rt, ct = rowi & (c - 1), coli & (c - 1)
    strict, incl, eye = rt > ct, rt >= ct, rowi == coli
    tril = (lax.broadcasted_iota(jnp.int32, (c, c), 0)
            >= lax.broadcasted_iota(jnp.int32, (c, c), 1)).astype(BF16)

    def headsum(x):
        s0 = jnp.sum(jnp.where(head0, x, 0.0), axis=-1, keepdims=True)
        s1 = jnp.sum(jnp.where(head0, 0.0, x), axis=-1, keepdims=True)
        return jnp.where(head0, s0, s1)

    def stack(x):
        return jnp.concatenate([jnp.where(head0, x, 0.0), jnp.where(head0, 0.0, x)], axis=0)

    kk = k * k_k
    kk = kk / jnp.maximum(jnp.sqrt(headsum(kk * kk)), 1e-12)
    kf = k * (1.0 + (a - 1.0) * k_a)
    bvec = kk * a
    yield

    cum = _nn(tril, jnp.concatenate(_split3(lw), axis=1))
    cum = cum[:, :LANES] + cum[:, LANES:2 * LANES] + cum[:, 2 * LANES:]
    cum_end = cum[c - 1:c]
    w_inv = jnp.exp(-cum)
    w_rem = jnp.exp(cum_end - cum)
    a_st = stack(-kk * jnp.exp(cum - lw))
    r_st = stack(r * jnp.exp(cum))
    b_st = stack(bvec * w_inv)
    k_st = stack(kf * w_inv)
    bp_st = _bf(stack(bvec * w_rem))
    kp_st = _bf(stack(kf * w_rem))
    v_st = _bf(stack(v))
    yield

    sc = _nt(_bf(jnp.concatenate([a_st, r_st], axis=0)), _bf(jnp.concatenate([b_st, k_st], axis=0)))
    l_ab = jnp.where(strict, sc[:LANES, :LANES], 0.0)
    l_ak = jnp.where(strict, sc[:LANES, LANES:], 0.0)
    l_rb = jnp.where(incl, sc[LANES:, :LANES], 0.0)
    l_rk = jnp.where(incl, sc[LANES:, LANES:], 0.0)
    yield

    tinv = jnp.where(eye, 1.0, l_ab)
    pw = l_ab
    for _ in range(c.bit_length() - 2):
        pwb = _bf(pw)
        pw = _nn(pwb, pwb)
        tinv = tinv + _nn(_bf(tinv), _bf(pw))
        yield

    au = _nn(_bf(tinv), _bf(jnp.concatenate([a_st, _nn(_bf(l_ak), v_st)], axis=1)))
    aub = _bf(au)
    yield
    rhs = jnp.concatenate([aub, jnp.concatenate([jnp.zeros_like(v_st), v_st], axis=1)], axis=0)
    rb = _nn(_bf(jnp.concatenate([l_rb, l_rk], axis=1)), rhs)
    gh = _tn(jnp.concatenate([bp_st, kp_st], axis=0), rhs)
    r_hat = r_st + rb[:, :LANES]
    o0 = rb[:, LANES:]
    g_mat = jnp.where(eye, jnp.exp(cum_end), 0.0) + gh[:, :LANES]
    h_mat = gh[:, LANES:]
    yield

    os = _nn(_bf(jnp.concatenate([r_hat, g_mat], axis=0)), _bf(s_ref[...]))
    s_ref[...] = os[LANES:] + h_mat
    o_st = os[:LANES] + o0
    o = o_st[:c] + o_st[c:]
    yield

    mean = headsum(o) * (1.0 / HEAD_DIM)
    oc = o - mean
    var = headsum(oc * oc) * (1.0 / HEAD_DIM)
    on = oc * lax.rsqrt(var + LNX_EPS) * ln_g + ln_b
    o_ref[...] = on + headsum(r * kf * r_k) * v


def _wkv_kernel(r_ref, k_ref, v_ref, lw_ref, a_ref, kk_ref, ka_ref, rk_ref, lg_ref, lb_ref,
                o_ref, s_ref):
    @pl.when(pl.program_id(2) == 0)
    def _():
        s_ref[...] = jnp.zeros_like(s_ref)

    def pair(p):
        sl = pl.ds(p * LANES, LANES)
        ins = [ref[:, sl] for ref in (r_ref, k_ref, v_ref, lw_ref, a_ref, kk_ref, ka_ref, rk_ref, lg_ref, lb_ref)]
        return _wkv_pair(*ins, s_ref.at[p], o_ref.at[:, sl])

    stages = [pair(p) for p in range(WKV_PAIRS)]
    while stages:
        stages = [g for g in stages if next(g, True) is None]


def _wkv(r, k, v, lw, a, k_k, k_a, r_k, lnx_g, lnx_b):
    n_chunks = SEQ // WKV_CHUNK
    width = WKV_PAIRS * LANES
    tok = pl.BlockSpec((WKV_CHUNK, width), lambda b, p, c: (b * n_chunks + c, p))
    par = pl.BlockSpec((1, width), lambda b, p, c: (0, p))
    vec = lambda t: t.reshape(1, D_MODEL)
    return pl.pallas_call(
        _wkv_kernel,
        grid=(BATCH, N_PAIRS // WKV_PAIRS, n_chunks),
        in_specs=[tok] * 5 + [par] * 5,
        out_specs=tok,
        out_shape=jax.ShapeDtypeStruct((TOKENS, D_MODEL), F32),
        scratch_shapes=[pltpu.VMEM((WKV_PAIRS, LANES, LANES), F32)],
        compiler_params=_params("parallel", "parallel", "arbitrary"),
        name="wkv_scan",
    )(r, k, v, lw, a, vec(k_k), vec(k_a), vec(r_k), vec(lnx_g), vec(lnx_b))


def _pack_rows(x):
    bits = lax.bitcast_convert_type(_bf(x).astype(F32), jnp.uint32)
    halves = []
    for h in range(PACK_HALVES):
        lo = bits[:, (2 * h) * PACK_WORDS:(2 * h + 1) * PACK_WORDS]
        hi = bits[:, (2 * h + 1) * PACK_WORDS:(2 * h + 2) * PACK_WORDS]
        halves.append(lax.bitcast_convert_type((lo >> 16) | (hi & jnp.uint32(0xFFFF0000)), jnp.int32))
    return halves


def _unpack_rows(halves):
    parts = []
    for w in halves:
        u = lax.bitcast_convert_type(w, jnp.uint32)
        parts.append(lax.bitcast_convert_type(u << 16, F32))
        parts.append(lax.bitcast_convert_type(u & jnp.uint32(0xFFFF0000), F32))
    return jnp.concatenate(parts, axis=1)


def _post_kernel(*refs, has_gate):
    if has_gate:
        o_ref, g_ref, *refs = refs
    else:
        o_ref, *refs = refs
    (x_ref, mod_ref, wo_ref, gpost_ref, gpre_ref, wr_hi_ref, wr_lo_ref, bias_ref,
     x1_ref, h2p_ref, gk_ref, ek_ref, rk_ref, cnt_ref, run_ref) = refs
    tm = ROW_TILE

    @pl.when(pl.program_id(0) == 0)
    def _():
        run_ref[...] = jnp.zeros_like(run_ref)

    mod = mod_ref[0]
    gate_m, shift_f, scale_f = mod[2:3], mod[3:4], mod[4:5]
    o = o_ref[...]
    if has_gate:
        o = o * g_ref[...]
    y = _nn(_bf(o), wo_ref[...])
    x1 = x_ref[...] + gate_m * _rms(y, gpost_ref[...])
    x1_ref[...] = x1
    h2 = _rms(x1, gpre_ref[...]) * (1.0 + scale_f) + shift_f
    for h, words in enumerate(_pack_rows(h2)):
        h2p_ref[h] = words

    hi = _bf(h2)
    lo = _bf(h2 - hi.astype(F32))
    wr_hi, wr_lo = wr_hi_ref[...], wr_lo_ref[...]
    scores = _sigmoid(_nt(wr_hi, hi) + (_nt(wr_hi, lo) + _nt(wr_lo, hi)))
    erow = lax.broadcasted_iota(jnp.int32, scores.shape, 0)
    work = scores + jnp.concatenate([bias_ref[...]] * (tm // LANES), axis=1)
    picked = jnp.zeros_like(scores)
    chosen = jnp.zeros_like(scores)
    hits, ids = [], []
    for _ in range(TOP_K):
        best = jnp.max(work, axis=0, keepdims=True)
        idx = jnp.min(jnp.where(work == best, erow, N_EXPERTS), axis=0, keepdims=True)
        hit = erow == idx
        picked = jnp.where(hit, scores, picked)
        chosen = jnp.where(hit, 1.0, chosen)
        work = jnp.where(hit, -jnp.inf, work)
        hits.append(hit)
        ids.append(idx)
    gates = ROUTED_SCALE * picked / jnp.sum(picked, axis=0, keepdims=True)

    upto = (lax.broadcasted_iota(jnp.int32, (tm, tm), 0)
            <= lax.broadcasted_iota(jnp.int32, (tm, tm), 1)).astype(BF16)
    chosen_b = _bf(chosen)
    run = run_ref[...]
    before = _nn(chosen_b, upto) - chosen + jnp.concatenate([run] * (tm // LANES), axis=1)
    run = run + _nn(chosen_b, jnp.ones((tm, LANES), BF16))
    run_ref[...] = run
    cnt_ref[...] = run

    krow = lax.broadcasted_iota(jnp.int32, (TOP_K, tm), 0)
    gk = jnp.zeros((TOP_K, tm), F32)
    ek = jnp.zeros((TOP_K, tm), jnp.int32)
    rk = jnp.zeros((TOP_K, tm), jnp.int32)
    for j in range(TOP_K):
        grow = jnp.sum(jnp.where(hits[j], gates, 0.0), axis=0, keepdims=True)
        rrow = jnp.sum(jnp.where(hits[j], before, 0.0), axis=0, keepdims=True)
        gk = jnp.where(krow == j, grow, gk)
        ek = jnp.where(krow == j, ids[j], ek)
        rk = jnp.where(krow == j, rrow.astype(jnp.int32), rk)
    gk_ref[...] = gk
    ek_ref[...] = ek
    rk_ref[...] = rk


def _post(o, g, x, mod, w_o, gpost, gpre, w_router, router_bias):
    tm = ROW_TILE
    full = lambda shape: pl.BlockSpec(shape, lambda i: (0,) * len(shape))
    row_spec = pl.BlockSpec((tm, D_MODEL), lambda i: (i, 0))
    k_spec = pl.BlockSpec((TOP_K, tm), lambda i: (0, i))
    wr = w_router.T
    wr_hi = _bf(wr)
    wr_lo = _bf(wr - wr_hi.astype(F32))
    bias = jnp.broadcast_to(router_bias[:, None], (N_EXPERTS, LANES))
    has_gate = g is not None
    acts = [o, g] if has_gate else [o]
    return pl.pallas_call(
        functools.partial(_post_kernel, has_gate=has_gate),
        grid=(TOKENS // tm,),
        in_specs=[row_spec] * (len(acts) + 1) + [
            pl.BlockSpec((1, N_ADA, D_MODEL), lambda i: (i // (SEQ // tm), 0, 0)),
            full((D_MODEL, D_MODEL)), full((1, D_MODEL)), full((1, D_MODEL)),
            full((N_EXPERTS, D_MODEL)), full((N_EXPERTS, D_MODEL)), full((N_EXPERTS, LANES)),
        ],
        out_specs=[row_spec, pl.BlockSpec((PACK_HALVES, tm, PACK_WORDS), lambda i: (0, i, 0)),
                   k_spec, k_spec, k_spec, full((N_EXPERTS, LANES))],
        out_shape=[jax.ShapeDtypeStruct((TOKENS, D_MODEL), F32),
                   jax.ShapeDtypeStruct((PACK_HALVES, TOKENS, PACK_WORDS), jnp.int32),
                   jax.ShapeDtypeStruct((TOP_K, TOKENS), F32),
                   jax.ShapeDtypeStruct((TOP_K, TOKENS), jnp.int32),
                   jax.ShapeDtypeStruct((TOP_K, TOKENS), jnp.int32),
                   jax.ShapeDtypeStruct((N_EXPERTS, LANES), F32)],
        scratch_shapes=[pltpu.VMEM((N_EXPERTS, LANES), F32)],
        compiler_params=_params("arbitrary"),
        name="mixer_out_router",
    )(*acts, x, mod, _bf(w_o), gpost.reshape(1, D_MODEL), gpre.reshape(1, D_MODEL),
      wr_hi, wr_lo, bias)


def _sc_mesh():
    return plsc.VectorSubcoreMesh(core_axis_name="core", subcore_axis_name="subcore")


def _sc_scatter_rows(rows, idx, n_out):
    n_copies = idx.shape[0]

    @pl.kernel(out_type=jax.ShapeDtypeStruct((PACK_HALVES, n_out, PACK_WORDS), rows.dtype),
               mesh=_sc_mesh(), scratch_types=[], name="sc_dispatch_rows")
    def scatter(x_hbm, i_hbm, o_hbm):
        for h in range(PACK_HALVES):
            def body(x_vmem, i_vmem, h=h):
                for k in range(n_copies):
                    pltpu.sync_copy(x_vmem, o_hbm.at[h].at[i_vmem.at[k]])

            pltpu.emit_pipeline(
                body, grid=(rows.shape[1] // SC_WINDOW,),
                in_specs=[pl.BlockSpec((SC_WINDOW, PACK_WORDS), index_map=lambda i: (i, 0)),
                          pl.BlockSpec((n_copies, SC_WINDOW), index_map=lambda i: (0, i))],
                out_specs=[],
                core_axis_name=("core", "subcore"),
                dimension_semantics=(pltpu.PARALLEL,),
            )(x_hbm.at[h], i_hbm)

    return scatter(rows, idx)


def _sc_gather_rows(table, idx):
    n = idx.shape[0]

    @pl.kernel(out_type=jax.ShapeDtypeStruct((PACK_HALVES, n, PACK_WORDS), table.dtype),
               mesh=_sc_mesh(), scratch_types=[], name="sc_collect_rows")
    def gather(x_hbm, i_hbm, o_hbm):
        for h in range(PACK_HALVES):
            def body(i_vmem, o_vmem, h=h):
                pltpu.sync_copy(x_hbm.at[h].at[i_vmem.at[0]], o_vmem)

            pltpu.emit_pipeline(
                body, grid=(n // SC_WINDOW,),
                in_specs=[pl.BlockSpec((1, SC_WINDOW), index_map=lambda i: (0, i))],
                out_specs=[pl.BlockSpec((SC_WINDOW, PACK_WORDS), index_map=lambda i: (i, 0))],
                core_axis_name=("core", "subcore"),
                dimension_semantics=(pltpu.PARALLEL,),
            )(i_hbm, o_hbm.at[h])

    return gather(table, idx.reshape(1, n))


def _ffn(x, wg, wu, wd):
    up = _nn(x, _bf(wu))
    gt = _nn(x, _bf(wg))
    return _nn(_bf(gt * _sigmoid(gt) * up), _bf(wd))


def _expert_kernel(be_ref, nu_ref, xs_ref, wg_ref, wu_ref, wd_ref, ys_ref, wgb_ref, wub_ref, wdb_ref):
    b = pl.program_id(0)

    @pl.when(jnp.logical_or(b == 0, be_ref[b] != be_ref[jnp.maximum(b - 1, 0)]))
    def _():
        wgb_ref[...] = _bf(wg_ref[0, 0])
        wub_ref[...] = _bf(wu_ref[0, 0])
        wdb_ref[...] = _bf(wd_ref[0, 0])

    @pl.when(b < nu_ref[0])
    def _():
        x = _bf(_unpack_rows([xs_ref[h] for h in range(PACK_HALVES)]))
        y = _ffn(x, wgb_ref[...], wub_ref[...], wdb_ref[...])
        for h, words in enumerate(_pack_rows(y)):
            ys_ref[h] = words


def _experts(layer, xs, blk_expert, n_used, w_gate, w_up, w_down):
    n_blocks = xs.shape[1] // EXPERT_BLOCK
    row_spec = pl.BlockSpec((PACK_HALVES, EXPERT_BLOCK, PACK_WORDS), lambda b, be, nu: (0, b, 0))
    w_spec = lambda shape: pl.BlockSpec((1, 1) + shape, lambda b, be, nu: (layer, be[b], 0, 0))
    return pl.pallas_call(
        _expert_kernel,
        grid_spec=pltpu.PrefetchScalarGridSpec(
            num_scalar_prefetch=2,
            grid=(n_blocks,),
            in_specs=[row_spec, w_spec((D_MODEL, D_EXPERT)), w_spec((D_MODEL, D_EXPERT)),
                      w_spec((D_EXPERT, D_MODEL))],
            out_specs=row_spec,
            scratch_shapes=[pltpu.VMEM((D_MODEL, D_EXPERT), BF16), pltpu.VMEM((D_MODEL, D_EXPERT), BF16),
                            pltpu.VMEM((D_EXPERT, D_MODEL), BF16)],
        ),
        out_shape=jax.ShapeDtypeStruct(xs.shape, xs.dtype),
        compiler_params=_params("arbitrary"),
        name="moe_experts",
    )(blk_expert, n_used, xs, w_gate, w_up, w_down)


def _shared_kernel(h2p_ref, sg_ref, su_ref, sd_ref, o_ref):
    h2 = _bf(_unpack_rows([h2p_ref[h] for h in range(PACK_HALVES)]))
    for h, words in enumerate(_pack_rows(_ffn(h2, sg_ref[0], su_ref[0], sd_ref[0]))):
        o_ref[h] = words


def _shared_expert(layer, h2p, ws_gate, ws_up, ws_down):
    tm = SHARED_TILE
    row_spec = pl.BlockSpec((PACK_HALVES, tm, PACK_WORDS), lambda i: (0, i, 0))
    return pl.pallas_call(
        _shared_kernel,
        grid=(TOKENS // tm,),
        in_specs=[row_spec,
                  pl.BlockSpec((1, D_MODEL, D_EXPERT), lambda i: (layer, 0, 0)),
                  pl.BlockSpec((1, D_MODEL, D_EXPERT), lambda i: (layer, 0, 0)),
                  pl.BlockSpec((1, D_EXPERT, D_MODEL), lambda i: (layer, 0, 0))],
        out_specs=row_spec,
        out_shape=jax.ShapeDtypeStruct(h2p.shape, h2p.dtype),
        compiler_params=_params("parallel"),
        name="moe_shared_expert",
    )(h2p, ws_gate, ws_up, ws_down)


def _combine_kernel(*refs):
    g_refs = refs[:TOP_K]
    gk_ref, sh_ref, x_ref, mod_ref, gpost_ref, o_ref = refs[TOP_K:]
    acc = _unpack_rows([sh_ref[h] for h in range(PACK_HALVES)])
    gk = gk_ref[...]
    lane = lax.broadcasted_iota(jnp.int32, gk.shape, 1)
    for j in range(TOP_K):
        gcol = jnp.sum(jnp.where(lane == j, gk, 0.0), axis=-1, keepdims=True)
        acc = acc + gcol * _unpack_rows([g_refs[j][h] for h in range(PACK_HALVES)])
    gate_f = mod_ref[0][5:6]
    o_ref[...] = x_ref[...] + gate_f * _rms(acc, gpost_ref[...])


def _combine(picked, gk, shared, x1, mod, gpost):
    tm = ROW_TILE
    n_tiles = TOKENS // tm
    row_spec = pl.BlockSpec((tm, D_MODEL), lambda i: (i, 0))
    pick_spec = lambda j: pl.BlockSpec((PACK_HALVES, tm, PACK_WORDS), lambda i: (0, j * n_tiles + i, 0))
    return pl.pallas_call(
        _combine_kernel,
        grid=(n_tiles,),
        in_specs=[pick_spec(j) for j in range(TOP_K)] + [
            pl.BlockSpec((tm, LANES), lambda i: (i, 0)),
            pl.BlockSpec((PACK_HALVES, tm, PACK_WORDS), lambda i: (0, i, 0)),
            row_spec,
            pl.BlockSpec((1, N_ADA, D_MODEL), lambda i: (i // (SEQ // tm), 0, 0)),
            pl.BlockSpec((1, D_MODEL), lambda i: (0, 0)),
        ],
        out_specs=row_spec,
        out_shape=jax.ShapeDtypeStruct((TOKENS, D_MODEL), F32),
        compiler_params=_params("parallel"),
        name="moe_combine",
    )(*([picked] * TOP_K), gk, shared, x1, mod, gpost.reshape(1, D_MODEL))


def _moe(layer, x1, h2p, gk, ek, rk, counts, mod, gpost, w_gate, w_up, w_down, ws_gate, ws_up, ws_down):
    cnt = counts[:, 0].astype(jnp.int32)
    padded = (cnt + EXPERT_BLOCK - 1) // EXPERT_BLOCK * EXPERT_BLOCK
    eid = jnp.arange(N_EXPERTS, dtype=jnp.int32)
    pend = jnp.sum(jnp.where(eid[:, None] <= eid[None, :], padded[:, None], 0), axis=0)
    pstart = pend - padded
    dest = rk + jnp.sum(jnp.where(ek[None] == eid[:, None, None], pstart[:, None, None], 0), axis=0)
    gk = jnp.pad(gk.T, ((0, 0), (0, LANES - TOP_K)))
    n_blocks = (TOKENS * TOP_K + N_EXPERTS * (EXPERT_BLOCK - 1)) // EXPERT_BLOCK + 1
    blk_start = jnp.arange(n_blocks, dtype=jnp.int32) * EXPERT_BLOCK
    blk_expert = jnp.minimum(jnp.sum((blk_start[None, :] >= pend[:, None]).astype(jnp.int32), axis=0),
                             N_EXPERTS - 1)
    n_used = pend[-1:] // EXPERT_BLOCK

    xs = _sc_scatter_rows(h2p, dest, n_blocks * EXPERT_BLOCK)
    shared = _shared_expert(layer, h2p, ws_gate, ws_up, ws_down)
    ys = _experts(layer, xs, blk_expert, n_used, w_gate, w_up, w_down)
    picked = _sc_gather_rows(ys, dest.reshape(-1))
    return _combine(picked, gk, shared, x1, mod, gpost)


def _sb_proj_kernel(x_ref, mod_ref, gpre_ref, w_ref, q_ref, k_ref, v_ref):
    mod = mod_ref[0]
    h = _bf(_rms(x_ref[...], gpre_ref[...]) * (1.0 + mod[1:2]) + mod[0:1])
    q_ref[...] = _bf(_nn(h, w_ref[:, :D_MODEL]) * (LOG2_E / 8.0))
    k_ref[...] = _bf(_nn(h, w_ref[:, D_MODEL:2 * D_MODEL]))
    v_ref[...] = _bf(_nn(h, w_ref[:, 2 * D_MODEL:]))


def _sb_proj(x, mod, gpre, w_qkv):
    tm = ROW_TILE
    full = lambda shape: pl.BlockSpec(shape, lambda i: (0,) * len(shape))
    row_spec = pl.BlockSpec((tm, D_MODEL), lambda i: (i, 0))
    out_sds = jax.ShapeDtypeStruct((TOKENS, D_MODEL), BF16)
    return pl.pallas_call(
        _sb_proj_kernel,
        grid=(TOKENS // tm,),
        in_specs=[row_spec,
                  pl.BlockSpec((1, N_ADA, D_MODEL), lambda i: (i // (SEQ // tm), 0, 0)),
                  full((1, D_MODEL)), full((D_MODEL, 3 * D_MODEL))],
        out_specs=[row_spec] * 3,
        out_shape=[out_sds] * 3,
        compiler_params=_params("parallel"),
        name="sb_qkv_proj",
    )(x, mod, gpre.reshape(1, D_MODEL), _bf(w_qkv))


def _sb_attn_kernel(q_ref, k_ref, v_ref, o_ref):
    blk = ATT_BLOCK
    n_heads = LANES // HEAD_DIM
    qb = pl.program_id(2)
    q = q_ref[...]
    lane = lax.broadcasted_iota(jnp.int32, (blk, LANES), 1)
    head0 = lane < HEAD_DIM
    qh = [jnp.where(head0, q, jnp.zeros_like(q)), jnp.where(head0, jnp.zeros_like(q), q)]
    rowi = lax.broadcasted_iota(jnp.int32, (blk, blk), 0)
    coli = lax.broadcasted_iota(jnp.int32, (blk, blk), 1)
    later = (rowi > coli).astype(BF16)
    ones = jnp.ones((blk, LANES), BF16)

    sub = ATT_SUB
    n_sub = blk // sub
    sub_row = lax.broadcasted_iota(jnp.int32, (sub, blk), 0)
    sub_col = lax.broadcasted_iota(jnp.int32, (sub, blk), 1)
    below = [sub_col < sub_row + r * sub for r in range(n_sub)]
    sub_head0 = lax.broadcasted_iota(jnp.int32, (sub, LANES), 1) < HEAD_DIM
    chains = [(hd, r) for hd in range(n_heads) for r in range(n_sub)]

    def chain(hd, r, ks, vs, get_carry, get_acc, mode, out):
        rows = slice(r * sub, (r + 1) * sub)
        u = _nt(qh[hd][rows], ks)
        yield
        neg_abs = lax.bitcast_convert_type(
            lax.bitcast_convert_type(u, jnp.uint32) | jnp.uint32(0x80000000), F32)
        lb = jnp.minimum(u, 0.0) - jnp.log2(1.0 + jnp.exp2(neg_abs))
        l1m = lb - u
        if mode == "diag":
            l1m = jnp.where(below[r], l1m, 0.0)
        l1b = _bf(l1m)
        yield
        carry = get_carry()
        tail = _nn(l1b, later) + jnp.concatenate([carry] * (blk // LANES), axis=1)
        out["carry"] = carry + _nn(l1b, ones)
        yield
        w = jnp.exp2(lb + tail)
        if mode == "diag":
            w = jnp.where(below[r], w, 0.0)
        elif mode == "prev":
            w = jnp.where(qb > 0, w, 0.0)
        wb = _bf(w)
        yield
        out["acc"] = get_acc() + _nn(wb, vs)

    def load_kv(kb):
        start = pl.multiple_of(kb * blk, blk)
        return k_ref[pl.ds(start, blk), :], v_ref[pl.ds(start, blk), :]

    def run(gens):
        while gens:
            gens = [g for g in gens if next(g, True) is None]

    def block(kb, carry, acc):
        ks, vs = load_kv(kb)
        outs = [{} for _ in chains]
        run([chain(hd, r, ks, vs, lambda c=c: carry[c], lambda c=c: acc[c], "plain", outs[c])
             for c, (hd, r) in enumerate(chains)])
        return [o["carry"] for o in outs], [o["acc"] for o in outs]

    zeros = jnp.zeros((sub, LANES), F32)
    ks_d, vs_d = load_kv(qb)
    ks_p, vs_p = load_kv(jnp.maximum(qb - 1, 0))
    outs_d = [{} for _ in chains]
    outs_p = [{} for _ in chains]
    run([chain(hd, r, ks_d, vs_d, lambda: zeros, lambda: zeros, "diag", outs_d[c])
         for c, (hd, r) in enumerate(chains)]
        + [chain(hd, r, ks_p, vs_p, lambda c=c: outs_d[c]["carry"], lambda c=c: outs_d[c]["acc"],
                 "prev", outs_p[c]) for c, (hd, r) in enumerate(chains)])
    carry = [o["carry"] for o in outs_p]
    acc = [o["acc"] for o in outs_p]

    def cmax_of(cr):
        return jnp.max(functools.reduce(jnp.maximum, cr))

    def cond(st):
        kb, _, _, cmax = st
        return jnp.logical_and(kb >= 0, cmax > EXP2_UNDERFLOW)

    def body(st):
        kb, cr, ac, _ = st
        cr, ac = block(kb, list(cr), list(ac))
        return kb - 1, tuple(cr), tuple(ac), cmax_of(cr)

    _, _, acc, _ = lax.while_loop(cond, body, (qb - 2, tuple(carry), tuple(acc), cmax_of(carry)))
    for r in range(n_sub):
        rows = slice(r * sub, (r + 1) * sub)
        o_ref[rows, :] = jnp.where(sub_head0, acc[r], acc[n_sub + r])


def _sb_attn(q, k, v):
    blk = ATT_BLOCK
    n_blk = SEQ // blk
    q_spec = pl.BlockSpec((blk, LANES), lambda b, p, i: (b * n_blk + i, p))
    kv_spec = pl.BlockSpec((SEQ, LANES), lambda b, p, i: (b, p))
    return pl.pallas_call(
        _sb_attn_kernel,
        grid=(BATCH, N_PAIRS, n_blk),
        in_specs=[q_spec, kv_spec, kv_spec],
        out_specs=q_spec,
        out_shape=jax.ShapeDtypeStruct((TOKENS, D_MODEL), F32),
        compiler_params=_params("parallel", "parallel", "arbitrary"),
        name="sb_attention",
    )(q, k, v)


def kernel(x, c, ada_w, ada_b, norm_pre_mix, norm_post_mix, norm_pre_ffn, norm_post_ffn, rwkv_mu, rwkv_w_rkv, rwkv_w_w1, rwkv_w_w2, rwkv_w0, rwkv_w_a1, rwkv_w_a2, rwkv_a0, rwkv_w_g1, rwkv_w_g2, rwkv_k_k, rwkv_k_a, rwkv_r_k, rwkv_lnx_g, rwkv_lnx_b, rwkv_w_o, sb_w_qkv, sb_w_o, moe_w_router, moe_router_bias, moe_w_gate, moe_w_up, moe_w_down, moe_ws_gate, moe_ws_up, moe_ws_down):
    mod_all = _ada_mod(c, ada_w, ada_b)
    xt = x.reshape(TOKENS, D_MODEL)
    for layer in range(DEPTH):
        mod = mod_all[layer]
        i = layer // 2
        if layer % 2 == 0:
            r, k, v, lw, a, g = _rwkv_proj(
                xt, mod, norm_pre_mix[layer], rwkv_mu[i], rwkv_w_rkv[i], rwkv_w_w1[i], rwkv_w_w2[i],
                rwkv_w0[i], rwkv_w_a1[i], rwkv_w_a2[i], rwkv_a0[i], rwkv_w_g1[i], rwkv_w_g2[i])
            o = _wkv(r, k, v, lw, a, rwkv_k_k[i], rwkv_k_a[i], rwkv_r_k[i], rwkv_lnx_g[i], rwkv_lnx_b[i])
            w_o = rwkv_w_o[i]
        else:
            q, k, v = _sb_proj(xt, mod, norm_pre_mix[layer], sb_w_qkv[i])
            o, g = _sb_attn(q, k, v), None
            w_o = sb_w_o[i]
        x1, h2p, gk, ek, rk, counts = _post(o, g, xt, mod, w_o, norm_post_mix[layer], norm_pre_ffn[layer],
                                            moe_w_router[layer], moe_router_bias[layer])
        xt = _moe(layer, x1, h2p, gk, ek, rk, counts, mod, norm_post_ffn[layer], moe_w_gate, moe_w_up,
                  moe_w_down, moe_ws_gate, moe_ws_up, moe_ws_down)
    return xt.reshape(BATCH, SEQ, D_MODEL)
```

```python
import functools

import jax
import jax.numpy as jnp
from jax import lax
from jax.experimental import pallas as pl
from jax.experimental.pallas import tpu as pltpu
from jax.experimental.pallas import tpu_sc as plsc

D_MODEL = 1024
BATCH = 2
SEQ = 8192
TOKENS = BATCH * SEQ
DEPTH = 2
HEAD_DIM = 64
N_EXPERTS = 64
TOP_K = 8
D_EXPERT = 256
ROUTED_SCALE = 2.5
RMS_EPS = 1e-6
LNX_EPS = 64e-5
N_ADA = 6

LANES = 128
SUBLANES = 8
N_PAIRS = D_MODEL // LANES
WKV_CHUNK = 64
WKV_PAIRS = 8
ROW_TILE = 512
SHARED_TILE = 512
EXPERT_BLOCK = 512
PACK_HALVES = 2
PACK_WORDS = D_MODEL // (2 * PACK_HALVES)
SC_WINDOW = 128
ATT_BLOCK = 256
ATT_SUB = 128
LOG2_E = 1.4426950408889634
EXP2_UNDERFLOW = -153.0
VMEM_LIMIT = 56 * 1024 * 1024

F32 = jnp.float32
BF16 = jnp.bfloat16


def _nn(a, b):
    return jnp.dot(a, b, preferred_element_type=F32)


def _nt(a, b):
    return lax.dot_general(a, b, (((1,), (1,)), ((), ())), preferred_element_type=F32)


def _tn(a, b):
    return lax.dot_general(a, b, (((0,), (0,)), ((), ())), preferred_element_type=F32)


def _bf(x):
    return x.astype(BF16)


def _sigmoid(x):
    return 1.0 / (1.0 + jnp.exp(-x))


def _rms(xv, g):
    ms = jnp.mean(xv * xv, axis=-1, keepdims=True)
    return xv * lax.rsqrt(ms + RMS_EPS) * g


def _split3(x):
    hi = x.astype(BF16)
    r1 = x - hi.astype(F32)
    mid = r1.astype(BF16)
    lo = (r1 - mid.astype(F32)).astype(BF16)
    return hi, mid, lo


def _params(*sem):
    return pltpu.CompilerParams(dimension_semantics=sem, vmem_limit_bytes=VMEM_LIMIT)


def _ada_kernel(c_ref, w_ref, b_ref, o_ref):
    cv = c_ref[...]
    o_ref[0] = _nn(cv * _sigmoid(cv), w_ref[0]) + b_ref[0]


def _ada_mod(c, ada_w, ada_b):
    tn = 1536
    c8 = jnp.pad(c, ((0, SUBLANES - BATCH), (0, 0)))
    out = pl.pallas_call(
        _ada_kernel,
        grid=(DEPTH, N_ADA * D_MODEL // tn),
        in_specs=[
            pl.BlockSpec((SUBLANES, D_MODEL), lambda l, j: (0, 0)),
            pl.BlockSpec((1, D_MODEL, tn), lambda l, j: (l, 0, j)),
            pl.BlockSpec((1, 1, tn), lambda l, j: (l, 0, j)),
        ],
        out_specs=pl.BlockSpec((1, SUBLANES, tn), lambda l, j: (l, 0, j)),
        out_shape=jax.ShapeDtypeStruct((DEPTH, SUBLANES, N_ADA * D_MODEL), F32),
        compiler_params=_params("parallel", "parallel"),
        name="ada_mod",
    )(c8, ada_w, ada_b.reshape(DEPTH, 1, N_ADA * D_MODEL))
    return out[:, :BATCH].reshape(DEPTH, BATCH, N_ADA, D_MODEL)


def _rwkv_proj_kernel(x_ref, xp_ref, mod_ref, gpre_ref, mu_ref, wrkv_ref, w1_ref, w2_ref, w0_ref,
                      a1_ref, a2_ref, a0_ref, g1_ref, g2_ref,
                      r_ref, k_ref, v_ref, lw_ref, a_ref, g_ref):
    i = pl.program_id(0)
    mod = mod_ref[0]
    shift, scale = mod[0:1], mod[1:2]
    gpre = gpre_ref[...]

    def modnorm(xv):
        return _rms(xv, gpre) * (1.0 + scale) + shift

    h = modnorm(x_ref[...])
    hp = modnorm(xp_ref[...])[SUBLANES - 1:SUBLANES]
    hp = jnp.where(i % (SEQ // ROW_TILE) == 0, 0.0, hp)
    row = lax.broadcasted_iota(jnp.int32, h.shape, 0)
    hs = jnp.where(row == 0, hp, pltpu.roll(h, 1, 0))
    xx = hs - h
    mu = mu_ref[...]

    def mix(n):
        return _bf(h + xx * mu[n:n + 1])

    r_ref[...] = _nn(mix(0), wrkv_ref[0])
    k_ref[...] = _nn(mix(1), wrkv_ref[1])
    v_ref[...] = _nn(mix(2), wrkv_ref[2])
    wl = w0_ref[...] + _nn(_bf(jnp.tanh(_nn(mix(3), w1_ref[...]))), w2_ref[...])
    nwl = -wl
    sp = jnp.maximum(nwl, 0.0) + jnp.log(1.0 + jnp.exp(-jnp.abs(nwl)))
    lw_ref[...] = -jnp.exp(-sp - 0.5)
    a_ref[...] = _sigmoid(a0_ref[...] + _nn(_bf(_nn(mix(4), a1_ref[...])), a2_ref[...]))
    g_ref[...] = _nn(_bf(_sigmoid(_nn(mix(5), g1_ref[...]))), g2_ref[...])


def _rwkv_proj(x, mod, gpre, mu, w_rkv, w1, w2, w0, a1, a2, a0, g1, g2):
    tm = ROW_TILE
    full = lambda shape: pl.BlockSpec(shape, lambda i: (0,) * len(shape))
    row_spec = pl.BlockSpec((tm, D_MODEL), lambda i: (i, 0))
    out_sds = jax.ShapeDtypeStruct((TOKENS, D_MODEL), F32)
    return pl.pallas_call(
        _rwkv_proj_kernel,
        grid=(TOKENS // tm,),
        in_specs=[
            row_spec,
            pl.BlockSpec((SUBLANES, D_MODEL), lambda i: (jnp.maximum(i * (tm // SUBLANES) - 1, 0), 0)),
            pl.BlockSpec((1, N_ADA, D_MODEL), lambda i: (i // (SEQ // tm), 0, 0)),
            full((1, D_MODEL)), full((6, D_MODEL)), full((3, D_MODEL, D_MODEL)),
            full(w1.shape), full(w2.shape), full((1, D_MODEL)),
            full(a1.shape), full(a2.shape), full((1, D_MODEL)),
            full(g1.shape), full(g2.shape),
        ],
        out_specs=[row_spec] * 6,
        out_shape=[out_sds] * 6,
        compiler_params=_params("parallel"),
        name="rwkv_proj",
    )(x, x, mod, gpre.reshape(1, D_MODEL), mu, _bf(w_rkv), _bf(w1), _bf(w2),
      w0.reshape(1, D_MODEL), _bf(a1), _bf(a2), a0.reshape(1, D_MODEL), _bf(g1), _bf(g2))


def _wkv_pair(r, k, v, lw, a, k_k, k_a, r_k, ln_g, ln_b, s_ref, o_ref):
    c = WKV_CHUNK
    lane = lax.broadcasted_iota(jnp.int32, (c, LANES), 1)
    head0 = lane < HEAD_DIM
    rowi = lax.broadcasted_iota(jnp.int32, (LANES, LANES), 0)
    coli = lax.broadcasted_iota(jnp.int32, (LANES, LANES), 1)
    rt, ct = rowi & (c - 1), coli & (c - 1)
    strict, incl, eye = rt > ct, rt >= ct, rowi == coli
    tril = (lax.broadcasted_iota(jnp.int32, (c, c), 0)
            >= lax.broadcasted_iota(jnp.int32, (c, c), 1)).astype(BF16)

    def headsum(x):
        s0 = jnp.sum(jnp.where(head0, x, 0.0), axis=-1, keepdims=True)
        s1 = jnp.sum(jnp.where(head0, 0.0, x), axis=-1, keepdims=True)
        return jnp.where(head0, s0, s1)

    def stack(x):
        return jnp.concatenate([jnp.where(head0, x, 0.0), jnp.where(head0, 0.0, x)], axis=0)

    kk = k * k_k
    kk = kk / jnp.maximum(jnp.sqrt(headsum(kk * kk)), 1e-12)
    kf = k * (1.0 + (a - 1.0) * k_a)
    bvec = kk * a
    yield

    cum = _nn(tril, jnp.concatenate(_split3(lw), axis=1))
    cum = cum[:, :LANES] + cum[:, LANES:2 * LANES] + cum[:, 2 * LANES:]
    cum_end = cum[c - 1:c]
    w_inv = jnp.exp(-cum)
    w_rem = jnp.exp(cum_end - cum)
    a_st = stack(-kk * jnp.exp(cum - lw))
    r_st = stack(r * jnp.exp(cum))
    b_st = stack(bvec * w_inv)
    k_st = stack(kf * w_inv)
    bp_st = _bf(stack(bvec * w_rem))
    kp_st = _bf(stack(kf * w_rem))
    v_st = _bf(stack(v))
    yield

    sc = _nt(_bf(jnp.concatenate([a_st, r_st], axis=0)), _bf(jnp.concatenate([b_st, k_st], axis=0)))
    l_ab = jnp.where(strict, sc[:LANES, :LANES], 0.0)
    l_ak = jnp.where(strict, sc[:LANES, LANES:], 0.0)
    l_rb = jnp.where(incl, sc[LANES:, :LANES], 0.0)
    l_rk = jnp.where(incl, sc[LANES:, LANES:], 0.0)
    yield

    tinv = jnp.where(eye, 1.0, l_ab)
    pw = l_ab
    for _ in range(c.bit_length() - 2):
        pwb = _bf(pw)
        pw = _nn(pwb, pwb)
        tinv = tinv + _nn(_bf(tinv), _bf(pw))
        yield

    au = _nn(_bf(tinv), _bf(jnp.concatenate([a_st, _nn(_bf(l_ak), v_st)], axis=1)))
    aub = _bf(au)
    yield
    rhs = jnp.concatenate([aub, jnp.concatenate([jnp.zeros_like(v_st), v_st], axis=1)], axis=0)
    rb = _nn(_bf(jnp.concatenate([l_rb, l_rk], axis=1)), rhs)
    gh = _tn(jnp.concatenate([bp_st, kp_st], axis=0), rhs)
    r_hat = r_st + rb[:, :LANES]
    o0 = rb[:, LANES:]
    g_mat = jnp.where(eye, jnp.exp(cum_end), 0.0) + gh[:, :LANES]
    h_mat = gh[:, LANES:]
    yield

    os = _nn(_bf(jnp.concatenate([r_hat, g_mat], axis=0)), _bf(s_ref[...]))
    s_ref[...] = os[LANES:] + h_mat
    o_st = os[:LANES] + o0
    o = o_st[:c] + o_st[c:]
    yield

    mean = headsum(o) * (1.0 / HEAD_DIM)
    oc = o - mean
    var = headsum(oc * oc) * (1.0 / HEAD_DIM)
    on = oc * lax.rsqrt(var + LNX_EPS) * ln_g + ln_b
    o_ref[...] = on + headsum(r * kf * r_k) * v


def _wkv_kernel(r_ref, k_ref, v_ref, lw_ref, a_ref, kk_ref, ka_ref, rk_ref, lg_ref, lb_ref,
                o_ref, s_ref):
    @pl.when(pl.program_id(2) == 0)
    def _():
        s_ref[...] = jnp.zeros_like(s_ref)

    def pair(p):
        sl = pl.ds(p * LANES, LANES)
        ins = [ref[:, sl] for ref in (r_ref, k_ref, v_ref, lw_ref, a_ref, kk_ref, ka_ref, rk_ref, lg_ref, lb_ref)]
        return _wkv_pair(*ins, s_ref.at[p], o_ref.at[:, sl])

    stages = [pair(p) for p in range(WKV_PAIRS)]
    while stages:
        stages = [g for g in stages if next(g, True) is None]


def _wkv(r, k, v, lw, a, k_k, k_a, r_k, lnx_g, lnx_b):
    n_chunks = SEQ // WKV_CHUNK
    width = WKV_PAIRS * LANES
    tok = pl.BlockSpec((WKV_CHUNK, width), lambda b, p, c: (b * n_chunks + c, p))
    par = pl.BlockSpec((1, width), lambda b, p, c: (0, p))
    vec = lambda t: t.reshape(1, D_MODEL)
    return pl.pallas_call(
        _wkv_kernel,
        grid=(BATCH, N_PAIRS // WKV_PAIRS, n_chunks),
        in_specs=[tok] * 5 + [par] * 5,
        out_specs=tok,
        out_shape=jax.ShapeDtypeStruct((TOKENS, D_MODEL), F32),
        scratch_shapes=[pltpu.VMEM((WKV_PAIRS, LANES, LANES), F32)],
        compiler_params=_params("parallel", "parallel", "arbitrary"),
        name="wkv_scan",
    )(r, k, v, lw, a, vec(k_k), vec(k_a), vec(r_k), vec(lnx_g), vec(lnx_b))


def _pack_rows(x):
    bits = lax.bitcast_convert_type(_bf(x).astype(F32), jnp.uint32)
    halves = []
    for h in range(PACK_HALVES):
        lo = bits[:, (2 * h) * PACK_WORDS:(2 * h + 1) * PACK_WORDS]
        hi = bits[:, (2 * h + 1) * PACK_WORDS:(2 * h + 2) * PACK_WORDS]
        halves.append(lax.bitcast_convert_type((lo >> 16) | (hi & jnp.uint32(0xFFFF0000)), jnp.int32))
    return halves


def _unpack_rows(halves):
    parts = []
    for w in halves:
        u = lax.bitcast_convert_type(w, jnp.uint32)
        parts.append(lax.bitcast_convert_type(u << 16, F32))
        parts.append(lax.bitcast_convert_type(u & jnp.uint32(0xFFFF0000), F32))
    return jnp.concatenate(parts, axis=1)


def _post_kernel(*refs, has_gate):
    if has_gate:
        o_ref, g_ref, *refs = refs
    else:
        o_ref, *refs = refs
    (x_ref, mod_ref, wo_ref, gpost_ref, gpre_ref, wr_hi_ref, wr_lo_ref, bias_ref,
     x1_ref, h2p_ref, gk_ref, ek_ref, rk_ref, cnt_ref, run_ref) = refs
    tm = ROW_TILE

    @pl.when(pl.program_id(0) == 0)
    def _():
        run_ref[...] = jnp.zeros_like(run_ref)

    mod = mod_ref[0]
    gate_m, shift_f, scale_f = mod[2:3], mod[3:4], mod[4:5]
    o = o_ref[...]
    if has_gate:
        o = o * g_ref[...]
    y = _nn(_bf(o), wo_ref[...])
    x1 = x_ref[...] + gate_m * _rms(y, gpost_ref[...])
    x1_ref[...] = x1
    h2 = _rms(x1, gpre_ref[...]) * (1.0 + scale_f) + shift_f
    for h, words in enumerate(_pack_rows(h2)):
        h2p_ref[h] = words

    hi = _bf(h2)
    lo = _bf(h2 - hi.astype(F32))
    wr_hi, wr_lo = wr_hi_ref[...], wr_lo_ref[...]
    scores = _sigmoid(_nt(wr_hi, hi) + (_nt(wr_hi, lo) + _nt(wr_lo, hi)))
    erow = lax.broadcasted_iota(jnp.int32, scores.shape, 0)
    work = scores + jnp.concatenate([bias_ref[...]] * (tm // LANES), axis=1)
    picked = jnp.zeros_like(scores)
    chosen = jnp.zeros_like(scores)
    hits, ids = [], []
    for _ in range(TOP_K):
        best = jnp.max(work, axis=0, keepdims=True)
        idx = jnp.min(jnp.where(work == best, erow, N_EXPERTS), axis=0, keepdims=True)
        hit = erow == idx
        picked = jnp.where(hit, scores, picked)
        chosen = jnp.where(hit, 1.0, chosen)
        work = jnp.where(hit, -jnp.inf, work)
        hits.append(hit)
        ids.append(idx)
    gates = ROUTED_SCALE * picked / jnp.sum(picked, axis=0, keepdims=True)

    upto = (lax.broadcasted_iota(jnp.int32, (tm, tm), 0)
            <= lax.broadcasted_iota(jnp.int32, (tm, tm), 1)).astype(BF16)
    chosen_b = _bf(chosen)
    run = run_ref[...]
    before = _nn(chosen_b, upto) - chosen + jnp.concatenate([run] * (tm // LANES), axis=1)
    run = run + _nn(chosen_b, jnp.ones((tm, LANES), BF16))
    run_ref[...] = run
    cnt_ref[...] = run

    krow = lax.broadcasted_iota(jnp.int32, (TOP_K, tm), 0)
    gk = jnp.zeros((TOP_K, tm), F32)
    ek = jnp.zeros((TOP_K, tm), jnp.int32)
    rk = jnp.zeros((TOP_K, tm), jnp.int32)
    for j in range(TOP_K):
        grow = jnp.sum(jnp.where(hits[j], gates, 0.0), axis=0, keepdims=True)
        rrow = jnp.sum(jnp.where(hits[j], before, 0.0), axis=0, keepdims=True)
        gk = jnp.where(krow == j, grow, gk)
        ek = jnp.where(krow == j, ids[j], ek)
        rk = jnp.where(krow == j, rrow.astype(jnp.int32), rk)
    gk_ref[...] = gk
    ek_ref[...] = ek
    rk_ref[...] = rk


def _post(o, g, x, mod, w_o, gpost, gpre, w_router, router_bias):
    tm = ROW_TILE
    full = lambda shape: pl.BlockSpec(shape, lambda i: (0,) * len(shape))
    row_spec = pl.BlockSpec((tm, D_MODEL), lambda i: (i, 0))
    k_spec = pl.BlockSpec((TOP_K, tm), lambda i: (0, i))
    wr = w_router.T
    wr_hi = _bf(wr)
    wr_lo = _bf(wr - wr_hi.astype(F32))
    bias = jnp.broadcast_to(router_bias[:, None], (N_EXPERTS, LANES))
    has_gate = g is not None
    acts = [o, g] if has_gate else [o]
    return pl.pallas_call(
        functools.partial(_post_kernel, has_gate=has_gate),
        grid=(TOKENS // tm,),
        in_specs=[row_spec] * (len(acts) + 1) + [
            pl.BlockSpec((1, N_ADA, D_MODEL), lambda i: (i // (SEQ // tm), 0, 0)),
            full((D_MODEL, D_MODEL)), full((1, D_MODEL)), full((1, D_MODEL)),
            full((N_EXPERTS, D_MODEL)), full((N_EXPERTS, D_MODEL)), full((N_EXPERTS, LANES)),
        ],
        out_specs=[row_spec, pl.BlockSpec((PACK_HALVES, tm, PACK_WORDS), lambda i: (0, i, 0)),
                   k_spec, k_spec, k_spec, full((N_EXPERTS, LANES))],
        out_shape=[jax.ShapeDtypeStruct((TOKENS, D_MODEL), F32),
                   jax.ShapeDtypeStruct((PACK_HALVES, TOKENS, PACK_WORDS), jnp.int32),
                   jax.ShapeDtypeStruct((TOP_K, TOKENS), F32),
                   jax.ShapeDtypeStruct((TOP_K, TOKENS), jnp.int32),
                   jax.ShapeDtypeStruct((TOP_K, TOKENS), jnp.int32),
                   jax.ShapeDtypeStruct((N_EXPERTS, LANES), F32)],
        scratch_shapes=[pltpu.VMEM((N_EXPERTS, LANES), F32)],
        compiler_params=_params("arbitrary"),
        name="mixer_out_router",
    )(*acts, x, mod, _bf(w_o), gpost.reshape(1, D_MODEL), gpre.reshape(1, D_MODEL),
      wr_hi, wr_lo, bias)


def _sc_mesh():
    return plsc.VectorSubcoreMesh(core_axis_name="core", subcore_axis_name="subcore")


def _sc_scatter_rows(rows, idx, n_out):
    n_copies = idx.shape[0]

    @pl.kernel(out_type=jax.ShapeDtypeStruct((PACK_HALVES, n_out, PACK_WORDS), rows.dtype),
               mesh=_sc_mesh(), scratch_types=[], name="sc_dispatch_rows")
    def scatter(x_hbm, i_hbm, o_hbm):
        for h in range(PACK_HALVES):
            def body(x_vmem, i_vmem, h=h):
                for k in range(n_copies):
                    pltpu.sync_copy(x_vmem, o_hbm.at[h].at[i_vmem.at[k]])

            pltpu.emit_pipeline(
                body, grid=(rows.shape[1] // SC_WINDOW,),
                in_specs=[pl.BlockSpec((SC_WINDOW, PACK_WORDS), index_map=lambda i: (i, 0)),
                          pl.BlockSpec((n_copies, SC_WINDOW), index_map=lambda i: (0, i))],
                out_specs=[],
                core_axis_name=("core", "subcore"),
                dimension_semantics=(pltpu.PARALLEL,),
            )(x_hbm.at[h], i_hbm)

    return scatter(rows, idx)


def _sc_gather_rows(table, idx):
    n = idx.shape[0]

    @pl.kernel(out_type=jax.ShapeDtypeStruct((PACK_HALVES, n, PACK_WORDS), table.dtype),
               mesh=_sc_mesh(), scratch_types=[], name="sc_collect_rows")
    def gather(x_hbm, i_hbm, o_hbm):
        for h in range(PACK_HALVES):
            def body(i_vmem, o_vmem, h=h):
                pltpu.sync_copy(x_hbm.at[h].at[i_vmem.at[0]], o_vmem)

            pltpu.emit_pipeline(
                body, grid=(n // SC_WINDOW,),
                in_specs=[pl.BlockSpec((1, SC_WINDOW), index_map=lambda i: (0, i))],
                out_specs=[pl.BlockSpec((SC_WINDOW, PACK_WORDS), index_map=lambda i: (i, 0))],
                core_axis_name=("core", "subcore"),
                dimension_semantics=(pltpu.PARALLEL,),
            )(i_hbm, o_hbm.at[h])

    return gather(table, idx.reshape(1, n))


def _ffn(x, wg, wu, wd):
    up = _nn(x, _bf(wu))
    gt = _nn(x, _bf(wg))
    return _nn(_bf(gt * _sigmoid(gt) * up), _bf(wd))


def _expert_kernel(first_ref, nblk_ref, xs_hbm, wg_ref, wu_ref, wd_ref, ys_hbm, wgb_ref, wub_ref, wdb_ref):
    e = pl.program_id(0)
    wgb_ref[...] = _bf(wg_ref[0, 0])
    wub_ref[...] = _bf(wu_ref[0, 0])
    wdb_ref[...] = _bf(wd_ref[0, 0])
    first = first_ref[e]

    def body(x_ref, y_ref):
        x = _bf(_unpack_rows([x_ref[h] for h in range(PACK_HALVES)]))
        y = _ffn(x, wgb_ref[...], wub_ref[...], wdb_ref[...])
        for h, words in enumerate(_pack_rows(y)):
            y_ref[h] = words

    rows = pl.BlockSpec((PACK_HALVES, EXPERT_BLOCK, PACK_WORDS), lambda j: (0, first + j, 0))
    pltpu.emit_pipeline(body, grid=(nblk_ref[e],), in_specs=[rows], out_specs=[rows])(xs_hbm, ys_hbm)


def _experts(layer, xs, first_blk, n_blk, w_gate, w_up, w_down):
    w_spec = lambda shape: pl.BlockSpec((1, 1) + shape, lambda e, fb, nb: (layer, e, 0, 0))
    return pl.pallas_call(
        _expert_kernel,
        grid_spec=pltpu.PrefetchScalarGridSpec(
            num_scalar_prefetch=2,
            grid=(N_EXPERTS,),
            in_specs=[pl.BlockSpec(memory_space=pl.ANY), w_spec((D_MODEL, D_EXPERT)),
                      w_spec((D_MODEL, D_EXPERT)), w_spec((D_EXPERT, D_MODEL))],
            out_specs=pl.BlockSpec(memory_space=pl.ANY),
            scratch_shapes=[pltpu.VMEM((D_MODEL, D_EXPERT), BF16), pltpu.VMEM((D_MODEL, D_EXPERT), BF16),
                            pltpu.VMEM((D_EXPERT, D_MODEL), BF16)],
        ),
        out_shape=jax.ShapeDtypeStruct(xs.shape, xs.dtype),
        compiler_params=_params("arbitrary"),
        name="moe_experts",
    )(first_blk, n_blk, xs, w_gate, w_up, w_down)


def _shared_kernel(h2p_ref, sg_ref, su_ref, sd_ref, o_ref):
    h2 = _bf(_unpack_rows([h2p_ref[h] for h in range(PACK_HALVES)]))
    for h, words in enumerate(_pack_rows(_ffn(h2, sg_ref[0], su_ref[0], sd_ref[0]))):
        o_ref[h] = words


def _shared_expert(layer, h2p, ws_gate, ws_up, ws_down):
    tm = SHARED_TILE
    row_spec = pl.BlockSpec((PACK_HALVES, tm, PACK_WORDS), lambda i: (0, i, 0))
    return pl.pallas_call(
        _shared_kernel,
        grid=(TOKENS // tm,),
        in_specs=[row_spec,
                  pl.BlockSpec((1, D_MODEL, D_EXPERT), lambda i: (layer, 0, 0)),
                  pl.BlockSpec((1, D_MODEL, D_EXPERT), lambda i: (layer, 0, 0)),
                  pl.BlockSpec((1, D_EXPERT, D_MODEL), lambda i: (layer, 0, 0))],
        out_specs=row_spec,
        out_shape=jax.ShapeDtypeStruct(h2p.shape, h2p.dtype),
        compiler_params=_params("parallel"),
        name="moe_shared_expert",
    )(h2p, ws_gate, ws_up, ws_down)


def _combine_kernel(*refs):
    g_refs = refs[:TOP_K]
    gk_ref, sh_ref, x_ref, mod_ref, gpost_ref, o_ref = refs[TOP_K:]
    acc = _unpack_rows([sh_ref[h] for h in range(PACK_HALVES)])
    gk = gk_ref[...]
    lane = lax.broadcasted_iota(jnp.int32, gk.shape, 1)
    for j in range(TOP_K):
        gcol = jnp.sum(jnp.where(lane == j, gk, 0.0), axis=-1, keepdims=True)
        acc = acc + gcol * _unpack_rows([g_refs[j][h] for h in range(PACK_HALVES)])
    gate_f = mod_ref[0][5:6]
    o_ref[...] = x_ref[...] + gate_f * _rms(acc, gpost_ref[...])


def _combine(picked, gk, shared, x1, mod, gpost):
    tm = ROW_TILE
    n_tiles = TOKENS // tm
    row_spec = pl.BlockSpec((tm, D_MODEL), lambda i: (i, 0))
    pick_spec = lambda j: pl.BlockSpec((PACK_HALVES, tm, PACK_WORDS), lambda i: (0, j * n_tiles + i, 0))
    return pl.pallas_call(
        _combine_kernel,
        grid=(n_tiles,),
        in_specs=[pick_spec(j) for j in range(TOP_K)] + [
            pl.BlockSpec((tm, LANES), lambda i: (i, 0)),
            pl.BlockSpec((PACK_HALVES, tm, PACK_WORDS), lambda i: (0, i, 0)),
            row_spec,
            pl.BlockSpec((1, N_ADA, D_MODEL), lambda i: (i // (SEQ // tm), 0, 0)),
            pl.BlockSpec((1, D_MODEL), lambda i: (0, 0)),
        ],
        out_specs=row_spec,
        out_shape=jax.ShapeDtypeStruct((TOKENS, D_MODEL), F32),
        compiler_params=_params("parallel"),
        name="moe_combine",
    )(*([picked] * TOP_K), gk, shared, x1, mod, gpost.reshape(1, D_MODEL))


def _moe(layer, x1, h2p, gk, ek, rk, counts, mod, gpost, w_gate, w_up, w_down, ws_gate, ws_up, ws_down):
    cnt = counts[:, 0].astype(jnp.int32)
    padded = (cnt + EXPERT_BLOCK - 1) // EXPERT_BLOCK * EXPERT_BLOCK
    eid = jnp.arange(N_EXPERTS, dtype=jnp.int32)
    pend = jnp.sum(jnp.where(eid[:, None] <= eid[None, :], padded[:, None], 0), axis=0)
    pstart = pend - padded
    dest = rk + jnp.sum(jnp.where(ek[None] == eid[:, None, None], pstart[:, None, None], 0), axis=0)
    gk = jnp.pad(gk.T, ((0, 0), (0, LANES - TOP_K)))
    n_blocks = (TOKENS * TOP_K + N_EXPERTS * (EXPERT_BLOCK - 1)) // EXPERT_BLOCK + 1

    xs = _sc_scatter_rows(h2p, dest, n_blocks * EXPERT_BLOCK)
    shared = _shared_expert(layer, h2p, ws_gate, ws_up, ws_down)
    ys = _experts(layer, xs, pstart // EXPERT_BLOCK, padded // EXPERT_BLOCK, w_gate, w_up, w_down)
    picked = _sc_gather_rows(ys, dest.reshape(-1))
    return _combine(picked, gk, shared, x1, mod, gpost)


def _sb_proj_kernel(x_ref, mod_ref, gpre_ref, w_ref, q_ref, k_ref, v_ref):
    mod = mod_ref[0]
    h = _bf(_rms(x_ref[...], gpre_ref[...]) * (1.0 + mod[1:2]) + mod[0:1])
    q_ref[...] = _bf(_nn(h, w_ref[:, :D_MODEL]) * (LOG2_E / 8.0))
    k_ref[...] = _bf(_nn(h, w_ref[:, D_MODEL:2 * D_MODEL]))
    v_ref[...] = _bf(_nn(h, w_ref[:, 2 * D_MODEL:]))


def _sb_proj(x, mod, gpre, w_qkv):
    tm = ROW_TILE
    full = lambda shape: pl.BlockSpec(shape, lambda i: (0,) * len(shape))
    row_spec = pl.BlockSpec((tm, D_MODEL), lambda i: (i, 0))
    out_sds = jax.ShapeDtypeStruct((TOKENS, D_MODEL), BF16)
    return pl.pallas_call(
        _sb_proj_kernel,
        grid=(TOKENS // tm,),
        in_specs=[row_spec,
                  pl.BlockSpec((1, N_ADA, D_MODEL), lambda i: (i // (SEQ // tm), 0, 0)),
                  full((1, D_MODEL)), full((D_MODEL, 3 * D_MODEL))],
        out_specs=[row_spec] * 3,
        out_shape=[out_sds] * 3,
        compiler_params=_params("parallel"),
        name="sb_qkv_proj",
    )(x, mod, gpre.reshape(1, D_MODEL), _bf(w_qkv))


def _sb_attn_kernel(q_ref, k_ref, v_ref, o_ref):
    blk = ATT_BLOCK
    n_heads = LANES // HEAD_DIM
    qb = pl.program_id(2)
    q = q_ref[...]
    lane = lax.broadcasted_iota(jnp.int32, (blk, LANES), 1)
    head0 = lane < HEAD_DIM
    qh = [jnp.where(head0, q, jnp.zeros_like(q)), jnp.where(head0, jnp.zeros_like(q), q)]
    rowi = lax.broadcasted_iota(jnp.int32, (blk, blk), 0)
    coli = lax.broadcasted_iota(jnp.int32, (blk, blk), 1)
    later = (rowi > coli).astype(BF16)
    ones = jnp.ones((blk, LANES), BF16)

    sub = ATT_SUB
    n_sub = blk // sub
    sub_row = lax.broadcasted_iota(jnp.int32, (sub, blk), 0)
    sub_col = lax.broadcasted_iota(jnp.int32, (sub, blk), 1)
    below = [sub_col < sub_row + r * sub for r in range(n_sub)]
    sub_head0 = lax.broadcasted_iota(jnp.int32, (sub, LANES), 1) < HEAD_DIM
    chains = [(hd, r) for hd in range(n_heads) for r in range(n_sub)]

    def chain(hd, r, ks, vs, get_carry, get_acc, mode, out):
        rows = slice(r * sub, (r + 1) * sub)
        u = _nt(qh[hd][rows], ks)
        yield
        neg_abs = lax.bitcast_convert_type(
            lax.bitcast_convert_type(u, jnp.uint32) | jnp.uint32(0x80000000), F32)
        lb = jnp.minimum(u, 0.0) - jnp.log2(1.0 + jnp.exp2(neg_abs))
        l1m = lb - u
        if mode == "diag":
            l1m = jnp.where(below[r], l1m, 0.0)
        l1b = _bf(l1m)
        yield
        carry = get_carry()
        tail = _nn(l1b, later) + jnp.concatenate([carry] * (blk // LANES), axis=1)
        out["carry"] = carry + _nn(l1b, ones)
        yield
        w = jnp.exp2(lb + tail)
        if mode == "diag":
            w = jnp.where(below[r], w, 0.0)
        elif mode == "prev":
            w = jnp.where(qb > 0, w, 0.0)
        wb = _bf(w)
        yield
        out["acc"] = get_acc() + _nn(wb, vs)

    def load_kv(kb):
        start = pl.multiple_of(kb * blk, blk)
        return k_ref[pl.ds(start, blk), :], v_ref[pl.ds(start, blk), :]

    def run(gens):
        while gens:
            gens = [g for g in gens if next(g, True) is None]

    def block(kb, carry, acc):
        ks, vs = load_kv(kb)
        outs = [{} for _ in chains]
        run([chain(hd, r, ks, vs, lambda c=c: carry[c], lambda c=c: acc[c], "plain", outs[c])
             for c, (hd, r) in enumerate(chains)])
        return [o["carry"] for o in outs], [o["acc"] for o in outs]

    zeros = jnp.zeros((sub, LANES), F32)
    ks_d, vs_d = load_kv(qb)
    ks_p, vs_p = load_kv(jnp.maximum(qb - 1, 0))
    outs_d = [{} for _ in chains]
    outs_p = [{} for _ in chains]
    run([chain(hd, r, ks_d, vs_d, lambda: zeros, lambda: zeros, "diag", outs_d[c])
         for c, (hd, r) in enumerate(chains)]
        + [chain(hd, r, ks_p, vs_p, lambda c=c: outs_d[c]["carry"], lambda c=c: outs_d[c]["acc"],
                 "prev", outs_p[c]) for c, (hd, r) in enumerate(chains)])
    carry = [o["carry"] for o in outs_p]
    acc = [o["acc"] for o in outs_p]

    def cmax_of(cr):
        return jnp.max(functools.reduce(jnp.maximum, cr))

    def cond(st):
        kb, _, _, cmax = st
        return jnp.logical_and(kb >= 0, cmax > EXP2_UNDERFLOW)

    def body(st):
        kb, cr, ac, _ = st
        cr, ac = block(kb, list(cr), list(ac))
        return kb - 1, tuple(cr), tuple(ac), cmax_of(cr)

    _, _, acc, _ = lax.while_loop(cond, body, (qb - 2, tuple(carry), tuple(acc), cmax_of(carry)))
    for r in range(n_sub):
        rows = slice(r * sub, (r + 1) * sub)
        o_ref[rows, :] = jnp.where(sub_head0, acc[r], acc[n_sub + r])


def _sb_attn(q, k, v):
    blk = ATT_BLOCK
    n_blk = SEQ // blk
    q_spec = pl.BlockSpec((blk, LANES), lambda b, p, i: (b * n_blk + i, p))
    kv_spec = pl.BlockSpec((SEQ, LANES), lambda b, p, i: (b, p))
    return pl.pallas_call(
        _sb_attn_kernel,
        grid=(BATCH, N_PAIRS, n_blk),
        in_specs=[q_spec, kv_spec, kv_spec],
        out_specs=q_spec,
        out_shape=jax.ShapeDtypeStruct((TOKENS, D_MODEL), F32),
        compiler_params=_params("parallel", "parallel", "arbitrary"),
        name="sb_attention",
    )(q, k, v)


def kernel(x, c, ada_w, ada_b, norm_pre_mix, norm_post_mix, norm_pre_ffn, norm_post_ffn, rwkv_mu, rwkv_w_rkv, rwkv_w_w1, rwkv_w_w2, rwkv_w0, rwkv_w_a1, rwkv_w_a2, rwkv_a0, rwkv_w_g1, rwkv_w_g2, rwkv_k_k, rwkv_k_a, rwkv_r_k, rwkv_lnx_g, rwkv_lnx_b, rwkv_w_o, sb_w_qkv, sb_w_o, moe_w_router, moe_router_bias, moe_w_gate, moe_w_up, moe_w_down, moe_ws_gate, moe_ws_up, moe_ws_down):
    mod_all = _ada_mod(c, ada_w, ada_b)
    xt = x.reshape(TOKENS, D_MODEL)
    for layer in range(DEPTH):
        mod = mod_all[layer]
        i = layer // 2
        if layer % 2 == 0:
            r, k, v, lw, a, g = _rwkv_proj(
                xt, mod, norm_pre_mix[layer], rwkv_mu[i], rwkv_w_rkv[i], rwkv_w_w1[i], rwkv_w_w2[i],
                rwkv_w0[i], rwkv_w_a1[i], rwkv_w_a2[i], rwkv_a0[i], rwkv_w_g1[i], rwkv_w_g2[i])
            o = _wkv(r, k, v, lw, a, rwkv_k_k[i], rwkv_k_a[i], rwkv_r_k[i], rwkv_lnx_g[i], rwkv_lnx_b[i])
            w_o = rwkv_w_o[i]
        else:
            q, k, v = _sb_proj(xt, mod, norm_pre_mix[layer], sb_w_qkv[i])
            o, g = _sb_attn(q, k, v), None
            w_o = sb_w_o[i]
        x1, h2p, gk, ek, rk, counts = _post(o, g, xt, mod, w_o, norm_post_mix[layer], norm_pre_ffn[layer],
                                            moe_w_router[layer], moe_router_bias[layer])
        xt = _moe(layer, x1, h2p, gk, ek, rk, counts, mod, norm_post_ffn[layer], moe_w_gate, moe_w_up,
                  moe_w_down, moe_ws_gate, moe_ws_up, moe_ws_down)
    return xt.reshape(BATCH, SEQ, D_MODEL)
```

```python
import functools

import jax
import jax.numpy as jnp
from jax import lax
from jax.experimental import pallas as pl
from jax.experimental.pallas import tpu as pltpu
from jax.experimental.pallas import tpu_sc as plsc

D_MODEL = 1024
BATCH = 2
SEQ = 8192
TOKENS = BATCH * SEQ
DEPTH = 2
HEAD_DIM = 64
N_EXPERTS = 64
TOP_K = 8
D_EXPERT = 256
ROUTED_SCALE = 2.5
RMS_EPS = 1e-6
LNX_EPS = 64e-5
N_ADA = 6

LANES = 128
SUBLANES = 8
N_PAIRS = D_MODEL // LANES
WKV_CHUNK = 64
WKV_PAIRS = 8
ROW_TILE = 512
SHARED_TILE = 512
EXPERT_BLOCK = 512
EXPERT_SPLIT = 2
PACK_HALVES = 2
PACK_WORDS = D_MODEL // (2 * PACK_HALVES)
SC_WINDOW = 128
ATT_BLOCK = 256
ATT_SUB = 128
LOG2_E = 1.4426950408889634
EXP2_UNDERFLOW = -153.0
VMEM_LIMIT = 56 * 1024 * 1024

F32 = jnp.float32
BF16 = jnp.bfloat16


def _nn(a, b):
    return jnp.dot(a, b, preferred_element_type=F32)


def _nt(a, b):
    return lax.dot_general(a, b, (((1,), (1,)), ((), ())), preferred_element_type=F32)


def _tn(a, b):
    return lax.dot_general(a, b, (((0,), (0,)), ((), ())), preferred_element_type=F32)


def _bf(x):
    return x.astype(BF16)


def _sigmoid(x):
    return 1.0 / (1.0 + jnp.exp(-x))


def _rms(xv, g):
    ms = jnp.mean(xv * xv, axis=-1, keepdims=True)
    return xv * lax.rsqrt(ms + RMS_EPS) * g


def _split3(x):
    hi = x.astype(BF16)
    r1 = x - hi.astype(F32)
    mid = r1.astype(BF16)
    lo = (r1 - mid.astype(F32)).astype(BF16)
    return hi, mid, lo


def _params(*sem):
    return pltpu.CompilerParams(dimension_semantics=sem, vmem_limit_bytes=VMEM_LIMIT)


def _ada_kernel(c_ref, w_ref, b_ref, o_ref):
    cv = c_ref[...]
    o_ref[0] = _nn(cv * _sigmoid(cv), w_ref[0]) + b_ref[0]


def _ada_mod(c, ada_w, ada_b):
    tn = 1536
    c8 = jnp.pad(c, ((0, SUBLANES - BATCH), (0, 0)))
    out = pl.pallas_call(
        _ada_kernel,
        grid=(DEPTH, N_ADA * D_MODEL // tn),
        in_specs=[
            pl.BlockSpec((SUBLANES, D_MODEL), lambda l, j: (0, 0)),
            pl.BlockSpec((1, D_MODEL, tn), lambda l, j: (l, 0, j)),
            pl.BlockSpec((1, 1, tn), lambda l, j: (l, 0, j)),
        ],
        out_specs=pl.BlockSpec((1, SUBLANES, tn), lambda l, j: (l, 0, j)),
        out_shape=jax.ShapeDtypeStruct((DEPTH, SUBLANES, N_ADA * D_MODEL), F32),
        compiler_params=_params("parallel", "parallel"),
        name="ada_mod",
    )(c8, ada_w, ada_b.reshape(DEPTH, 1, N_ADA * D_MODEL))
    return out[:, :BATCH].reshape(DEPTH, BATCH, N_ADA, D_MODEL)


def _rwkv_proj_kernel(x_ref, xp_ref, mod_ref, gpre_ref, mu_ref, wrkv_ref, w1_ref, w2_ref, w0_ref,
                      a1_ref, a2_ref, a0_ref, g1_ref, g2_ref,
                      r_ref, k_ref, v_ref, lw_ref, a_ref, g_ref):
    i = pl.program_id(0)
    mod = mod_ref[0]
    shift, scale = mod[0:1], mod[1:2]
    gpre = gpre_ref[...]

    def modnorm(xv):
        return _rms(xv, gpre) * (1.0 + scale) + shift

    h = modnorm(x_ref[...])
    hp = modnorm(xp_ref[...])[SUBLANES - 1:SUBLANES]
    hp = jnp.where(i % (SEQ // ROW_TILE) == 0, 0.0, hp)
    row = lax.broadcasted_iota(jnp.int32, h.shape, 0)
    hs = jnp.where(row == 0, hp, pltpu.roll(h, 1, 0))
    xx = hs - h
    mu = mu_ref[...]

    def mix(n):
        return _bf(h + xx * mu[n:n + 1])

    r_ref[...] = _nn(mix(0), wrkv_ref[0])
    k_ref[...] = _nn(mix(1), wrkv_ref[1])
    v_ref[...] = _nn(mix(2), wrkv_ref[2])
    wl = w0_ref[...] + _nn(_bf(jnp.tanh(_nn(mix(3), w1_ref[...]))), w2_ref[...])
    nwl = -wl
    sp = jnp.maximum(nwl, 0.0) + jnp.log(1.0 + jnp.exp(-jnp.abs(nwl)))
    lw_ref[...] = -jnp.exp(-sp - 0.5)
    a_ref[...] = _sigmoid(a0_ref[...] + _nn(_bf(_nn(mix(4), a1_ref[...])), a2_ref[...]))
    g_ref[...] = _nn(_bf(_sigmoid(_nn(mix(5), g1_ref[...]))), g2_ref[...])


def _rwkv_proj(x, mod, gpre, mu, w_rkv, w1, w2, w0, a1, a2, a0, g1, g2):
    tm = ROW_TILE
    full = lambda shape: pl.BlockSpec(shape, lambda i: (0,) * len(shape))
    row_spec = pl.BlockSpec((tm, D_MODEL), lambda i: (i, 0))
    out_sds = jax.ShapeDtypeStruct((TOKENS, D_MODEL), F32)
    return pl.pallas_call(
        _rwkv_proj_kernel,
        grid=(TOKENS // tm,),
        in_specs=[
            row_spec,
            pl.BlockSpec((SUBLANES, D_MODEL), lambda i: (jnp.maximum(i * (tm // SUBLANES) - 1, 0), 0)),
            pl.BlockSpec((1, N_ADA, D_MODEL), lambda i: (i // (SEQ // tm), 0, 0)),
            full((1, D_MODEL)), full((6, D_MODEL)), full((3, D_MODEL, D_MODEL)),
            full(w1.shape), full(w2.shape), full((1, D_MODEL)),
            full(a1.shape), full(a2.shape), full((1, D_MODEL)),
            full(g1.shape), full(g2.shape),
        ],
        out_specs=[row_spec] * 6,
        out_shape=[out_sds] * 6,
        compiler_params=_params("parallel"),
        name="rwkv_proj",
    )(x, x, mod, gpre.reshape(1, D_MODEL), mu, _bf(w_rkv), _bf(w1), _bf(w2),
      w0.reshape(1, D_MODEL), _bf(a1), _bf(a2), a0.reshape(1, D_MODEL), _bf(g1), _bf(g2))


def _wkv_pair(r, k, v, lw, a, k_k, k_a, r_k, ln_g, ln_b, s_ref, o_ref):
    c = WKV_CHUNK
    lane = lax.broadcasted_iota(jnp.int32, (c, LANES), 1)
    head0 = lane < HEAD_DIM
    rowi = lax.broadcasted_iota(jnp.int32, (LANES, LANES), 0)
    coli = lax.broadcasted_iota(jnp.int32, (LANES, LANES), 1)
    rt, ct = rowi & (c - 1), coli & (c - 1)
    strict, incl, eye = rt > ct, rt >= ct, rowi == coli
    tril = (lax.broadcasted_iota(jnp.int32, (c, c), 0)
            >= lax.broadcasted_iota(jnp.int32, (c, c), 1)).astype(BF16)

    def headsum(x):
        s0 = jnp.sum(jnp.where(head0, x, 0.0), axis=-1, keepdims=True)
        s1 = jnp.sum(jnp.where(head0, 0.0, x), axis=-1, keepdims=True)
        return jnp.where(head0, s0, s1)

    def stack(x):
        return jnp.concatenate([jnp.where(head0, x, 0.0), jnp.where(head0, 0.0, x)], axis=0)

    kk = k * k_k
    kk = kk / jnp.maximum(jnp.sqrt(headsum(kk * kk)), 1e-12)
    kf = k * (1.0 + (a - 1.0) * k_a)
    bvec = kk * a
    yield

    cum = _nn(tril, jnp.concatenate(_split3(lw), axis=1))
    cum = cum[:, :LANES] + cum[:, LANES:2 * LANES] + cum[:, 2 * LANES:]
    cum_end = cum[c - 1:c]
    w_inv = jnp.exp(-cum)
    w_rem = jnp.exp(cum_end - cum)
    a_st = stack(-kk * jnp.exp(cum - lw))
    r_st = stack(r * jnp.exp(cum))
    b_st = stack(bvec * w_inv)
    k_st = stack(kf * w_inv)
    bp_st = _bf(stack(bvec * w_rem))
    kp_st = _bf(stack(kf * w_rem))
    v_st = _bf(stack(v))
    yield

    sc = _nt(_bf(jnp.concatenate([a_st, r_st], axis=0)), _bf(jnp.concatenate([b_st, k_st], axis=0)))
    l_ab = jnp.where(strict, sc[:LANES, :LANES], 0.0)
    l_ak = jnp.where(strict, sc[:LANES, LANES:], 0.0)
    l_rb = jnp.where(incl, sc[LANES:, :LANES], 0.0)
    l_rk = jnp.where(incl, sc[LANES:, LANES:], 0.0)
    yield

    tinv = jnp.where(eye, 1.0, l_ab)
    pw = l_ab
    for _ in range(c.bit_length() - 2):
        pwb = _bf(pw)
        pw = _nn(pwb, pwb)
        tinv = tinv + _nn(_bf(tinv), _bf(pw))
        yield

    au = _nn(_bf(tinv), _bf(jnp.concatenate([a_st, _nn(_bf(l_ak), v_st)], axis=1)))
    aub = _bf(au)
    yield
    rhs = jnp.concatenate([aub, jnp.concatenate([jnp.zeros_like(v_st), v_st], axis=1)], axis=0)
    rb = _nn(_bf(jnp.concatenate([l_rb, l_rk], axis=1)), rhs)
    gh = _tn(jnp.concatenate([bp_st, kp_st], axis=0), rhs)
    r_hat = r_st + rb[:, :LANES]
    o0 = rb[:, LANES:]
    g_mat = jnp.where(eye, jnp.exp(cum_end), 0.0) + gh[:, :LANES]
    h_mat = gh[:, LANES:]
    yield

    os = _nn(_bf(jnp.concatenate([r_hat, g_mat], axis=0)), _bf(s_ref[...]))
    s_ref[...] = os[LANES:] + h_mat
    o_st = os[:LANES] + o0
    o = o_st[:c] + o_st[c:]
    yield

    mean = headsum(o) * (1.0 / HEAD_DIM)
    oc = o - mean
    var = headsum(oc * oc) * (1.0 / HEAD_DIM)
    on = oc * lax.rsqrt(var + LNX_EPS) * ln_g + ln_b
    o_ref[...] = on + headsum(r * kf * r_k) * v


def _wkv_kernel(r_ref, k_ref, v_ref, lw_ref, a_ref, kk_ref, ka_ref, rk_ref, lg_ref, lb_ref,
                o_ref, s_ref):
    @pl.when(pl.program_id(2) == 0)
    def _():
        s_ref[...] = jnp.zeros_like(s_ref)

    def pair(p):
        sl = pl.ds(p * LANES, LANES)
        ins = [ref[:, sl] for ref in (r_ref, k_ref, v_ref, lw_ref, a_ref, kk_ref, ka_ref, rk_ref, lg_ref, lb_ref)]
        return _wkv_pair(*ins, s_ref.at[p], o_ref.at[:, sl])

    stages = [pair(p) for p in range(WKV_PAIRS)]
    while stages:
        stages = [g for g in stages if next(g, True) is None]


def _wkv(r, k, v, lw, a, k_k, k_a, r_k, lnx_g, lnx_b):
    n_chunks = SEQ // WKV_CHUNK
    width = WKV_PAIRS * LANES
    tok = pl.BlockSpec((WKV_CHUNK, width), lambda b, p, c: (b * n_chunks + c, p))
    par = pl.BlockSpec((1, width), lambda b, p, c: (0, p))
    vec = lambda t: t.reshape(1, D_MODEL)
    return pl.pallas_call(
        _wkv_kernel,
        grid=(BATCH, N_PAIRS // WKV_PAIRS, n_chunks),
        in_specs=[tok] * 5 + [par] * 5,
        out_specs=tok,
        out_shape=jax.ShapeDtypeStruct((TOKENS, D_MODEL), F32),
        scratch_shapes=[pltpu.VMEM((WKV_PAIRS, LANES, LANES), F32)],
        compiler_params=_params("parallel", "parallel", "arbitrary"),
        name="wkv_scan",
    )(r, k, v, lw, a, vec(k_k), vec(k_a), vec(r_k), vec(lnx_g), vec(lnx_b))


def _pack_rows(x):
    bits = lax.bitcast_convert_type(_bf(x).astype(F32), jnp.uint32)
    halves = []
    for h in range(PACK_HALVES):
        lo = bits[:, (2 * h) * PACK_WORDS:(2 * h + 1) * PACK_WORDS]
        hi = bits[:, (2 * h + 1) * PACK_WORDS:(2 * h + 2) * PACK_WORDS]
        halves.append(lax.bitcast_convert_type((lo >> 16) | (hi & jnp.uint32(0xFFFF0000)), jnp.int32))
    return halves


def _unpack_rows(halves):
    parts = []
    for w in halves:
        u = lax.bitcast_convert_type(w, jnp.uint32)
        parts.append(lax.bitcast_convert_type(u << 16, F32))
        parts.append(lax.bitcast_convert_type(u & jnp.uint32(0xFFFF0000), F32))
    return jnp.concatenate(parts, axis=1)


def _post_kernel(*refs, has_gate):
    if has_gate:
        o_ref, g_ref, *refs = refs
    else:
        o_ref, *refs = refs
    (x_ref, mod_ref, wo_ref, gpost_ref, gpre_ref, wr_hi_ref, wr_lo_ref, bias_ref,
     x1_ref, h2p_ref, gk_ref, ek_ref, rk_ref, cnt_ref, run_ref) = refs
    tm = ROW_TILE

    @pl.when(pl.program_id(0) == 0)
    def _():
        run_ref[...] = jnp.zeros_like(run_ref)

    mod = mod_ref[0]
    gate_m, shift_f, scale_f = mod[2:3], mod[3:4], mod[4:5]
    o = o_ref[...]
    if has_gate:
        o = o * g_ref[...]
    y = _nn(_bf(o), wo_ref[...])
    x1 = x_ref[...] + gate_m * _rms(y, gpost_ref[...])
    x1_ref[...] = x1
    h2 = _rms(x1, gpre_ref[...]) * (1.0 + scale_f) + shift_f
    for h, words in enumerate(_pack_rows(h2)):
        h2p_ref[h] = words

    hi = _bf(h2)
    lo = _bf(h2 - hi.astype(F32))
    wr_hi, wr_lo = wr_hi_ref[...], wr_lo_ref[...]
    scores = _sigmoid(_nt(wr_hi, hi) + (_nt(wr_hi, lo) + _nt(wr_lo, hi)))
    erow = lax.broadcasted_iota(jnp.int32, scores.shape, 0)
    work = scores + jnp.concatenate([bias_ref[...]] * (tm // LANES), axis=1)
    picked = jnp.zeros_like(scores)
    chosen = jnp.zeros_like(scores)
    hits, ids = [], []
    for _ in range(TOP_K):
        best = jnp.max(work, axis=0, keepdims=True)
        idx = jnp.min(jnp.where(work == best, erow, N_EXPERTS), axis=0, keepdims=True)
        hit = erow == idx
        picked = jnp.where(hit, scores, picked)
        chosen = jnp.where(hit, 1.0, chosen)
        work = jnp.where(hit, -jnp.inf, work)
        hits.append(hit)
        ids.append(idx)
    gates = ROUTED_SCALE * picked / jnp.sum(picked, axis=0, keepdims=True)

    upto = (lax.broadcasted_iota(jnp.int32, (tm, tm), 0)
            <= lax.broadcasted_iota(jnp.int32, (tm, tm), 1)).astype(BF16)
    chosen_b = _bf(chosen)
    run = run_ref[...]
    before = _nn(chosen_b, upto) - chosen + jnp.concatenate([run] * (tm // LANES), axis=1)
    run = run + _nn(chosen_b, jnp.ones((tm, LANES), BF16))
    run_ref[...] = run
    cnt_ref[...] = run

    krow = lax.broadcasted_iota(jnp.int32, (TOP_K, tm), 0)
    gk = jnp.zeros((TOP_K, tm), F32)
    ek = jnp.zeros((TOP_K, tm), jnp.int32)
    rk = jnp.zeros((TOP_K, tm), jnp.int32)
    for j in range(TOP_K):
        grow = jnp.sum(jnp.where(hits[j], gates, 0.0), axis=0, keepdims=True)
        rrow = jnp.sum(jnp.where(hits[j], before, 0.0), axis=0, keepdims=True)
        gk = jnp.where(krow == j, grow, gk)
        ek = jnp.where(krow == j, ids[j], ek)
        rk = jnp.where(krow == j, rrow.astype(jnp.int32), rk)
    gk_ref[...] = gk
    ek_ref[...] = ek
    rk_ref[...] = rk


def _post(o, g, x, mod, w_o, gpost, gpre, w_router, router_bias):
    tm = ROW_TILE
    full = lambda shape: pl.BlockSpec(shape, lambda i: (0,) * len(shape))
    row_spec = pl.BlockSpec((tm, D_MODEL), lambda i: (i, 0))
    k_spec = pl.BlockSpec((TOP_K, tm), lambda i: (0, i))
    wr = w_router.T
    wr_hi = _bf(wr)
    wr_lo = _bf(wr - wr_hi.astype(F32))
    bias = jnp.broadcast_to(router_bias[:, None], (N_EXPERTS, LANES))
    has_gate = g is not None
    acts = [o, g] if has_gate else [o]
    return pl.pallas_call(
        functools.partial(_post_kernel, has_gate=has_gate),
        grid=(TOKENS // tm,),
        in_specs=[row_spec] * (len(acts) + 1) + [
            pl.BlockSpec((1, N_ADA, D_MODEL), lambda i: (i // (SEQ // tm), 0, 0)),
            full((D_MODEL, D_MODEL)), full((1, D_MODEL)), full((1, D_MODEL)),
            full((N_EXPERTS, D_MODEL)), full((N_EXPERTS, D_MODEL)), full((N_EXPERTS, LANES)),
        ],
        out_specs=[row_spec, pl.BlockSpec((PACK_HALVES, tm, PACK_WORDS), lambda i: (0, i, 0)),
                   k_spec, k_spec, k_spec, full((N_EXPERTS, LANES))],
        out_shape=[jax.ShapeDtypeStruct((TOKENS, D_MODEL), F32),
                   jax.ShapeDtypeStruct((PACK_HALVES, TOKENS, PACK_WORDS), jnp.int32),
                   jax.ShapeDtypeStruct((TOP_K, TOKENS), F32),
                   jax.ShapeDtypeStruct((TOP_K, TOKENS), jnp.int32),
                   jax.ShapeDtypeStruct((TOP_K, TOKENS), jnp.int32),
                   jax.ShapeDtypeStruct((N_EXPERTS, LANES), F32)],
        scratch_shapes=[pltpu.VMEM((N_EXPERTS, LANES), F32)],
        compiler_params=_params("arbitrary"),
        name="mixer_out_router",
    )(*acts, x, mod, _bf(w_o), gpost.reshape(1, D_MODEL), gpre.reshape(1, D_MODEL),
      wr_hi, wr_lo, bias)


def _sc_mesh():
    return plsc.VectorSubcoreMesh(core_axis_name="core", subcore_axis_name="subcore")


def _sc_scatter_rows(rows, idx, n_out):
    n_copies = idx.shape[0]

    @pl.kernel(out_type=jax.ShapeDtypeStruct((PACK_HALVES, n_out, PACK_WORDS), rows.dtype),
               mesh=_sc_mesh(), scratch_types=[], name="sc_dispatch_rows")
    def scatter(x_hbm, i_hbm, o_hbm):
        for h in range(PACK_HALVES):
            def body(x_vmem, i_vmem, h=h):
                for k in range(n_copies):
                    pltpu.sync_copy(x_vmem, o_hbm.at[h].at[i_vmem.at[k]])

            pltpu.emit_pipeline(
                body, grid=(rows.shape[1] // SC_WINDOW,),
                in_specs=[pl.BlockSpec((SC_WINDOW, PACK_WORDS), index_map=lambda i: (i, 0)),
                          pl.BlockSpec((n_copies, SC_WINDOW), index_map=lambda i: (0, i))],
                out_specs=[],
                core_axis_name=("core", "subcore"),
                dimension_semantics=(pltpu.PARALLEL,),
            )(x_hbm.at[h], i_hbm)

    return scatter(rows, idx)


def _sc_gather_rows(tables, idx):
    n = idx.shape[0]

    @pl.kernel(out_type=jax.ShapeDtypeStruct((PACK_HALVES, n, PACK_WORDS), tables[0].dtype),
               mesh=_sc_mesh(), scratch_types=[], name="sc_collect_rows")
    def gather(*refs):
        x_hbm, (i_hbm, o_hbm) = refs[:PACK_HALVES], refs[PACK_HALVES:]
        for h in range(PACK_HALVES):
            def body(i_vmem, o_vmem, h=h):
                pltpu.sync_copy(x_hbm[h].at[i_vmem.at[0]], o_vmem)

            pltpu.emit_pipeline(
                body, grid=(n // SC_WINDOW,),
                in_specs=[pl.BlockSpec((1, SC_WINDOW), index_map=lambda i: (0, i))],
                out_specs=[pl.BlockSpec((SC_WINDOW, PACK_WORDS), index_map=lambda i: (i, 0))],
                core_axis_name=("core", "subcore"),
                dimension_semantics=(pltpu.PARALLEL,),
            )(i_hbm, o_hbm.at[h])

    return gather(*tables, idx.reshape(1, n))


def _ffn(x, wg, wu, wd):
    up = _nn(x, _bf(wu))
    gt = _nn(x, _bf(wg))
    return _nn(_bf(gt * _sigmoid(gt) * up), _bf(wd))


def _expert_kernel(be_ref, nu_ref, *refs):
    n_in = PACK_HALVES * EXPERT_SPLIT
    x_refs = refs[:n_in]
    wg_ref, wu_ref, wd_ref = refs[n_in:n_in + 3]
    y_refs = refs[n_in + 3:n_in + 3 + PACK_HALVES]
    wgb_ref, wub_ref, wdb_ref = refs[n_in + 3 + PACK_HALVES:]
    b = pl.program_id(0)

    @pl.when(jnp.logical_or(b == 0, be_ref[b] != be_ref[jnp.maximum(b - 1, 0)]))
    def _():
        wgb_ref[...] = _bf(wg_ref[0, 0])
        wub_ref[...] = _bf(wu_ref[0, 0])
        wdb_ref[...] = _bf(wd_ref[0, 0])

    @pl.when(b < nu_ref[0])
    def _():
        halves = [jnp.concatenate([x_refs[h * EXPERT_SPLIT + j][0] for j in range(EXPERT_SPLIT)], axis=0)
                  for h in range(PACK_HALVES)]
        y = _ffn(_bf(_unpack_rows(halves)), wgb_ref[...], wub_ref[...], wdb_ref[...])
        for h, words in enumerate(_pack_rows(y)):
            y_refs[h][...] = words


def _experts(layer, xs, blk_expert, n_used, w_gate, w_up, w_down):
    n_rows = xs.shape[1]
    n_blocks = n_rows // EXPERT_BLOCK
    part = EXPERT_BLOCK // EXPERT_SPLIT
    x_spec = lambda h, j: pl.BlockSpec((1, part, PACK_WORDS), lambda b, be, nu: (h, b * EXPERT_SPLIT + j, 0))
    w_spec = lambda shape: pl.BlockSpec((1, 1) + shape, lambda b, be, nu: (layer, be[b], 0, 0))
    y_spec = pl.BlockSpec((EXPERT_BLOCK, PACK_WORDS), lambda b, be, nu: (b, 0))
    n_in = PACK_HALVES * EXPERT_SPLIT
    return pl.pallas_call(
        _expert_kernel,
        grid_spec=pltpu.PrefetchScalarGridSpec(
            num_scalar_prefetch=2,
            grid=(n_blocks,),
            in_specs=[x_spec(h, j) for h in range(PACK_HALVES) for j in range(EXPERT_SPLIT)]
                     + [w_spec((D_MODEL, D_EXPERT)), w_spec((D_MODEL, D_EXPERT)), w_spec((D_EXPERT, D_MODEL))],
            out_specs=[y_spec] * PACK_HALVES,
            scratch_shapes=[pltpu.VMEM((D_MODEL, D_EXPERT), BF16), pltpu.VMEM((D_MODEL, D_EXPERT), BF16),
                            pltpu.VMEM((D_EXPERT, D_MODEL), BF16)],
        ),
        out_shape=[jax.ShapeDtypeStruct((n_rows, PACK_WORDS), xs.dtype)] * PACK_HALVES,
        compiler_params=_params("arbitrary"),
        name="moe_experts",
    )(blk_expert, n_used, *([xs] * n_in), w_gate, w_up, w_down)


def _shared_kernel(h2p_ref, sg_ref, su_ref, sd_ref, o_ref):
    h2 = _bf(_unpack_rows([h2p_ref[h] for h in range(PACK_HALVES)]))
    for h, words in enumerate(_pack_rows(_ffn(h2, sg_ref[0], su_ref[0], sd_ref[0]))):
        o_ref[h] = words


def _shared_expert(layer, h2p, ws_gate, ws_up, ws_down):
    tm = SHARED_TILE
    row_spec = pl.BlockSpec((PACK_HALVES, tm, PACK_WORDS), lambda i: (0, i, 0))
    return pl.pallas_call(
        _shared_kernel,
        grid=(TOKENS // tm,),
        in_specs=[row_spec,
                  pl.BlockSpec((1, D_MODEL, D_EXPERT), lambda i: (layer, 0, 0)),
                  pl.BlockSpec((1, D_MODEL, D_EXPERT), lambda i: (layer, 0, 0)),
                  pl.BlockSpec((1, D_EXPERT, D_MODEL), lambda i: (layer, 0, 0))],
        out_specs=row_spec,
        out_shape=jax.ShapeDtypeStruct(h2p.shape, h2p.dtype),
        compiler_params=_params("parallel"),
        name="moe_shared_expert",
    )(h2p, ws_gate, ws_up, ws_down)


def _combine_kernel(*refs):
    g_refs = refs[:TOP_K]
    gk_ref, sh_ref, x_ref, mod_ref, gpost_ref, o_ref = refs[TOP_K:]
    acc = _unpack_rows([sh_ref[h] for h in range(PACK_HALVES)])
    gk = gk_ref[...]
    lane = lax.broadcasted_iota(jnp.int32, gk.shape, 1)
    for j in range(TOP_K):
        gcol = jnp.sum(jnp.where(lane == j, gk, 0.0), axis=-1, keepdims=True)
        acc = acc + gcol * _unpack_rows([g_refs[j][h] for h in range(PACK_HALVES)])
    gate_f = mod_ref[0][5:6]
    o_ref[...] = x_ref[...] + gate_f * _rms(acc, gpost_ref[...])


def _combine(picked, gk, shared, x1, mod, gpost):
    tm = ROW_TILE
    n_tiles = TOKENS // tm
    row_spec = pl.BlockSpec((tm, D_MODEL), lambda i: (i, 0))
    pick_spec = lambda j: pl.BlockSpec((PACK_HALVES, tm, PACK_WORDS), lambda i: (0, j * n_tiles + i, 0))
    return pl.pallas_call(
        _combine_kernel,
        grid=(n_tiles,),
        in_specs=[pick_spec(j) for j in range(TOP_K)] + [
            pl.BlockSpec((tm, LANES), lambda i: (i, 0)),
            pl.BlockSpec((PACK_HALVES, tm, PACK_WORDS), lambda i: (0, i, 0)),
            row_spec,
            pl.BlockSpec((1, N_ADA, D_MODEL), lambda i: (i // (SEQ // tm), 0, 0)),
            pl.BlockSpec((1, D_MODEL), lambda i: (0, 0)),
        ],
        out_specs=row_spec,
        out_shape=jax.ShapeDtypeStruct((TOKENS, D_MODEL), F32),
        compiler_params=_params("parallel"),
        name="moe_combine",
    )(*([picked] * TOP_K), gk, shared, x1, mod, gpost.reshape(1, D_MODEL))


def _moe(layer, x1, h2p, gk, ek, rk, counts, mod, gpost, w_gate, w_up, w_down, ws_gate, ws_up, ws_down):
    cnt = counts[:, 0].astype(jnp.int32)
    padded = (cnt + EXPERT_BLOCK - 1) // EXPERT_BLOCK * EXPERT_BLOCK
    eid = jnp.arange(N_EXPERTS, dtype=jnp.int32)
    pend = jnp.sum(jnp.where(eid[:, None] <= eid[None, :], padded[:, None], 0), axis=0)
    pstart = pend - padded
    dest = rk + jnp.sum(jnp.where(ek[None] == eid[:, None, None], pstart[:, None, None], 0), axis=0)
    gk = jnp.pad(gk.T, ((0, 0), (0, LANES - TOP_K)))
    n_blocks = (TOKENS * TOP_K + N_EXPERTS * (EXPERT_BLOCK - 1)) // EXPERT_BLOCK + 1
    blk_start = jnp.arange(n_blocks, dtype=jnp.int32) * EXPERT_BLOCK
    blk_expert = jnp.minimum(jnp.sum((blk_start[None, :] >= pend[:, None]).astype(jnp.int32), axis=0),
                             N_EXPERTS - 1)
    n_used = pend[-1:] // EXPERT_BLOCK

    xs = _sc_scatter_rows(h2p, dest, n_blocks * EXPERT_BLOCK)
    shared = _shared_expert(layer, h2p, ws_gate, ws_up, ws_down)
    ys = _experts(layer, xs, blk_expert, n_used, w_gate, w_up, w_down)
    picked = _sc_gather_rows(ys, dest.reshape(-1))
    return _combine(picked, gk, shared, x1, mod, gpost)


def _sb_proj_kernel(x_ref, mod_ref, gpre_ref, w_ref, q_ref, k_ref, v_ref):
    mod = mod_ref[0]
    h = _bf(_rms(x_ref[...], gpre_ref[...]) * (1.0 + mod[1:2]) + mod[0:1])
    q_ref[...] = _bf(_nn(h, w_ref[:, :D_MODEL]) * (LOG2_E / 8.0))
    k_ref[...] = _bf(_nn(h, w_ref[:, D_MODEL:2 * D_MODEL]))
    v_ref[...] = _bf(_nn(h, w_ref[:, 2 * D_MODEL:]))


def _sb_proj(x, mod, gpre, w_qkv):
    tm = ROW_TILE
    full = lambda shape: pl.BlockSpec(shape, lambda i: (0,) * len(shape))
    row_spec = pl.BlockSpec((tm, D_MODEL), lambda i: (i, 0))
    out_sds = jax.ShapeDtypeStruct((TOKENS, D_MODEL), BF16)
    return pl.pallas_call(
        _sb_proj_kernel,
        grid=(TOKENS // tm,),
        in_specs=[row_spec,
                  pl.BlockSpec((1, N_ADA, D_MODEL), lambda i: (i // (SEQ // tm), 0, 0)),
                  full((1, D_MODEL)), full((D_MODEL, 3 * D_MODEL))],
        out_specs=[row_spec] * 3,
        out_shape=[out_sds] * 3,
        compiler_params=_params("parallel"),
        name="sb_qkv_proj",
    )(x, mod, gpre.reshape(1, D_MODEL), _bf(w_qkv))


def _sb_attn_kernel(q_ref, k_ref, v_ref, o_ref):
    blk = ATT_BLOCK
    n_heads = LANES // HEAD_DIM
    qb = pl.program_id(2)
    q = q_ref[...]
    lane = lax.broadcasted_iota(jnp.int32, (blk, LANES), 1)
    head0 = lane < HEAD_DIM
    qh = [jnp.where(head0, q, jnp.zeros_like(q)), jnp.where(head0, jnp.zeros_like(q), q)]
    rowi = lax.broadcasted_iota(jnp.int32, (blk, blk), 0)
    coli = lax.broadcasted_iota(jnp.int32, (blk, blk), 1)
    later = (rowi > coli).astype(BF16)
    ones = jnp.ones((blk, LANES), BF16)

    sub = ATT_SUB
    n_sub = blk // sub
    sub_row = lax.broadcasted_iota(jnp.int32, (sub, blk), 0)
    sub_col = lax.broadcasted_iota(jnp.int32, (sub, blk), 1)
    below = [sub_col < sub_row + r * sub for r in range(n_sub)]
    sub_head0 = lax.broadcasted_iota(jnp.int32, (sub, LANES), 1) < HEAD_DIM
    chains = [(hd, r) for hd in range(n_heads) for r in range(n_sub)]

    def chain(hd, r, ks, vs, get_carry, get_acc, mode, out):
        rows = slice(r * sub, (r + 1) * sub)
        u = _nt(qh[hd][rows], ks)
        yield
        neg_abs = lax.bitcast_convert_type(
            lax.bitcast_convert_type(u, jnp.uint32) | jnp.uint32(0x80000000), F32)
        lb = jnp.minimum(u, 0.0) - jnp.log2(1.0 + jnp.exp2(neg_abs))
        l1m = lb - u
        if mode == "diag":
            l1m = jnp.where(below[r], l1m, 0.0)
        l1b = _bf(l1m)
        yield
        carry = get_carry()
        tail = _nn(l1b, later) + jnp.concatenate([carry] * (blk // LANES), axis=1)
        out["carry"] = carry + _nn(l1b, ones)
        yield
        w = jnp.exp2(lb + tail)
        if mode == "diag":
            w = jnp.where(below[r], w, 0.0)
        elif mode == "prev":
            w = jnp.where(qb > 0, w, 0.0)
        wb = _bf(w)
        yield
        out["acc"] = get_acc() + _nn(wb, vs)

    def load_kv(kb):
        start = pl.multiple_of(kb * blk, blk)
        return k_ref[pl.ds(start, blk), :], v_ref[pl.ds(start, blk), :]

    def run(gens):
        while gens:
            gens = [g for g in gens if next(g, True) is None]

    def block(kb, carry, acc):
        ks, vs = load_kv(kb)
        outs = [{} for _ in chains]
        run([chain(hd, r, ks, vs, lambda c=c: carry[c], lambda c=c: acc[c], "plain", outs[c])
             for c, (hd, r) in enumerate(chains)])
        return [o["carry"] for o in outs], [o["acc"] for o in outs]

    zeros = jnp.zeros((sub, LANES), F32)
    ks_d, vs_d = load_kv(qb)
    ks_p, vs_p = load_kv(jnp.maximum(qb - 1, 0))
    outs_d = [{} for _ in chains]
    outs_p = [{} for _ in chains]
    run([chain(hd, r, ks_d, vs_d, lambda: zeros, lambda: zeros, "diag", outs_d[c])
         for c, (hd, r) in enumerate(chains)]
        + [chain(hd, r, ks_p, vs_p, lambda c=c: outs_d[c]["carry"], lambda c=c: outs_d[c]["acc"],
                 "prev", outs_p[c]) for c, (hd, r) in enumerate(chains)])
    carry = [o["carry"] for o in outs_p]
    acc = [o["acc"] for o in outs_p]

    def cmax_of(cr):
        return jnp.max(functools.reduce(jnp.maximum, cr))

    def cond(st):
        kb, _, _, cmax = st
        return jnp.logical_and(kb >= 0, cmax > EXP2_UNDERFLOW)

    def body(st):
        kb, cr, ac, _ = st
        cr, ac = block(kb, list(cr), list(ac))
        return kb - 1, tuple(cr), tuple(ac), cmax_of(cr)

    _, _, acc, _ = lax.while_loop(cond, body, (qb - 2, tuple(carry), tuple(acc), cmax_of(carry)))
    for r in range(n_sub):
        rows = slice(r * sub, (r + 1) * sub)
        o_ref[rows, :] = jnp.where(sub_head0, acc[r], acc[n_sub + r])


def _sb_attn(q, k, v):
    blk = ATT_BLOCK
    n_blk = SEQ // blk
    q_spec = pl.BlockSpec((blk, LANES), lambda b, p, i: (b * n_blk + i, p))
    kv_spec = pl.BlockSpec((SEQ, LANES), lambda b, p, i: (b, p))
    return pl.pallas_call(
        _sb_attn_kernel,
        grid=(BATCH, N_PAIRS, n_blk),
        in_specs=[q_spec, kv_spec, kv_spec],
        out_specs=q_spec,
        out_shape=jax.ShapeDtypeStruct((TOKENS, D_MODEL), F32),
        compiler_params=_params("parallel", "parallel", "arbitrary"),
        name="sb_attention",
    )(q, k, v)


def kernel(x, c, ada_w, ada_b, norm_pre_mix, norm_post_mix, norm_pre_ffn, norm_post_ffn, rwkv_mu, rwkv_w_rkv, rwkv_w_w1, rwkv_w_w2, rwkv_w0, rwkv_w_a1, rwkv_w_a2, rwkv_a0, rwkv_w_g1, rwkv_w_g2, rwkv_k_k, rwkv_k_a, rwkv_r_k, rwkv_lnx_g, rwkv_lnx_b, rwkv_w_o, sb_w_qkv, sb_w_o, moe_w_router, moe_router_bias, moe_w_gate, moe_w_up, moe_w_down, moe_ws_gate, moe_ws_up, moe_ws_down):
    mod_all = _ada_mod(c, ada_w, ada_b)
    xt = x.reshape(TOKENS, D_MODEL)
    for layer in range(DEPTH):
        mod = mod_all[layer]
        i = layer // 2
        if layer % 2 == 0:
            r, k, v, lw, a, g = _rwkv_proj(
                xt, mod, norm_pre_mix[layer], rwkv_mu[i], rwkv_w_rkv[i], rwkv_w_w1[i], rwkv_w_w2[i],
                rwkv_w0[i], rwkv_w_a1[i], rwkv_w_a2[i], rwkv_a0[i], rwkv_w_g1[i], rwkv_w_g2[i])
            o = _wkv(r, k, v, lw, a, rwkv_k_k[i], rwkv_k_a[i], rwkv_r_k[i], rwkv_lnx_g[i], rwkv_lnx_b[i])
            w_o = rwkv_w_o[i]
        else:
            q, k, v = _sb_proj(xt, mod, norm_pre_mix[layer], sb_w_qkv[i])
            o, g = _sb_attn(q, k, v), None
            w_o = sb_w_o[i]
        x1, h2p, gk, ek, rk, counts = _post(o, g, xt, mod, w_o, norm_post_mix[layer], norm_pre_ffn[layer],
                                            moe_w_router[layer], moe_router_bias[layer])
        xt = _moe(layer, x1, h2p, gk, ek, rk, counts, mod, norm_post_ffn[layer], moe_w_gate, moe_w_up,
                  moe_w_down, moe_ws_gate, moe_ws_up, moe_ws_down)
    return xt.reshape(BATCH, SEQ, D_MODEL)
```

```python
import functools

import jax
import jax.numpy as jnp
from jax import lax
from jax.experimental import pallas as pl
from jax.experimental.pallas import tpu as pltpu
from jax.experimental.pallas import tpu_sc as plsc

D_MODEL = 1024
BATCH = 2
SEQ = 8192
TOKENS = BATCH * SEQ
DEPTH = 2
HEAD_DIM = 64
N_EXPERTS = 64
TOP_K = 8
D_EXPERT = 256
ROUTED_SCALE = 2.5
RMS_EPS = 1e-6
LNX_EPS = 64e-5
N_ADA = 6

LANES = 128
SUBLANES = 8
N_PAIRS = D_MODEL // LANES
WKV_CHUNK = 64
WKV_PAIRS = 8
ROW_TILE = 512
SHARED_TILE = 512
EXPERT_BLOCK = 1024
EXPERT_SPLIT = 2
PACK_HALVES = 2
PACK_WORDS = D_MODEL // (2 * PACK_HALVES)
SC_WINDOW = 128
ATT_BLOCK = 256
ATT_SUB = 128
LOG2_E = 1.4426950408889634
EXP2_UNDERFLOW = -153.0
VMEM_LIMIT = 56 * 1024 * 1024

F32 = jnp.float32
BF16 = jnp.bfloat16


def _nn(a, b):
    return jnp.dot(a, b, preferred_element_type=F32)


def _nt(a, b):
    return lax.dot_general(a, b, (((1,), (1,)), ((), ())), preferred_element_type=F32)


def _tn(a, b):
    return lax.dot_general(a, b, (((0,), (0,)), ((), ())), preferred_element_type=F32)


def _bf(x):
    return x.astype(BF16)


def _sigmoid(x):
    return 1.0 / (1.0 + jnp.exp(-x))


def _rms(xv, g):
    ms = jnp.mean(xv * xv, axis=-1, keepdims=True)
    return xv * lax.rsqrt(ms + RMS_EPS) * g


def _split3(x):
    hi = x.astype(BF16)
    r1 = x - hi.astype(F32)
    mid = r1.astype(BF16)
    lo = (r1 - mid.astype(F32)).astype(BF16)
    return hi, mid, lo


def _params(*sem):
    return pltpu.CompilerParams(dimension_semantics=sem, vmem_limit_bytes=VMEM_LIMIT)


def _ada_kernel(c_ref, w_ref, b_ref, o_ref):
    cv = c_ref[...]
    o_ref[0] = _nn(cv * _sigmoid(cv), w_ref[0]) + b_ref[0]


def _ada_mod(c, ada_w, ada_b):
    tn = 1536
    c8 = jnp.pad(c, ((0, SUBLANES - BATCH), (0, 0)))
    out = pl.pallas_call(
        _ada_kernel,
        grid=(DEPTH, N_ADA * D_MODEL // tn),
        in_specs=[
            pl.BlockSpec((SUBLANES, D_MODEL), lambda l, j: (0, 0)),
            pl.BlockSpec((1, D_MODEL, tn), lambda l, j: (l, 0, j)),
            pl.BlockSpec((1, 1, tn), lambda l, j: (l, 0, j)),
        ],
        out_specs=pl.BlockSpec((1, SUBLANES, tn), lambda l, j: (l, 0, j)),
        out_shape=jax.ShapeDtypeStruct((DEPTH, SUBLANES, N_ADA * D_MODEL), F32),
        compiler_params=_params("parallel", "parallel"),
        name="ada_mod",
    )(c8, ada_w, ada_b.reshape(DEPTH, 1, N_ADA * D_MODEL))
    return out[:, :BATCH].reshape(DEPTH, BATCH, N_ADA, D_MODEL)


def _rwkv_proj_kernel(x_ref, xp_ref, mod_ref, gpre_ref, mu_ref, wrkv_ref, w1_ref, w2_ref, w0_ref,
                      a1_ref, a2_ref, a0_ref, g1_ref, g2_ref,
                      r_ref, k_ref, v_ref, lw_ref, a_ref, g_ref):
    i = pl.program_id(0)
    mod = mod_ref[0]
    shift, scale = mod[0:1], mod[1:2]
    gpre = gpre_ref[...]

    def modnorm(xv):
        return _rms(xv, gpre) * (1.0 + scale) + shift

    h = modnorm(x_ref[...])
    hp = modnorm(xp_ref[...])[SUBLANES - 1:SUBLANES]
    hp = jnp.where(i % (SEQ // ROW_TILE) == 0, 0.0, hp)
    row = lax.broadcasted_iota(jnp.int32, h.shape, 0)
    hs = jnp.where(row == 0, hp, pltpu.roll(h, 1, 0))
    xx = hs - h
    mu = mu_ref[...]

    def mix(n):
        return _bf(h + xx * mu[n:n + 1])

    r_ref[...] = _nn(mix(0), wrkv_ref[0])
    k_ref[...] = _nn(mix(1), wrkv_ref[1])
    v_ref[...] = _nn(mix(2), wrkv_ref[2])
    wl = w0_ref[...] + _nn(_bf(jnp.tanh(_nn(mix(3), w1_ref[...]))), w2_ref[...])
    nwl = -wl
    sp = jnp.maximum(nwl, 0.0) + jnp.log(1.0 + jnp.exp(-jnp.abs(nwl)))
    lw_ref[...] = -jnp.exp(-sp - 0.5)
    a_ref[...] = _sigmoid(a0_ref[...] + _nn(_bf(_nn(mix(4), a1_ref[...])), a2_ref[...]))
    g_ref[...] = _nn(_bf(_sigmoid(_nn(mix(5), g1_ref[...]))), g2_ref[...])


def _rwkv_proj(x, mod, gpre, mu, w_rkv, w1, w2, w0, a1, a2, a0, g1, g2):
    tm = ROW_TILE
    full = lambda shape: pl.BlockSpec(shape, lambda i: (0,) * len(shape))
    row_spec = pl.BlockSpec((tm, D_MODEL), lambda i: (i, 0))
    out_sds = jax.ShapeDtypeStruct((TOKENS, D_MODEL), F32)
    return pl.pallas_call(
        _rwkv_proj_kernel,
        grid=(TOKENS // tm,),
        in_specs=[
            row_spec,
            pl.BlockSpec((SUBLANES, D_MODEL), lambda i: (jnp.maximum(i * (tm // SUBLANES) - 1, 0), 0)),
            pl.BlockSpec((1, N_ADA, D_MODEL), lambda i: (i // (SEQ // tm), 0, 0)),
            full((1, D_MODEL)), full((6, D_MODEL)), full((3, D_MODEL, D_MODEL)),
            full(w1.shape), full(w2.shape), full((1, D_MODEL)),
            full(a1.shape), full(a2.shape), full((1, D_MODEL)),
            full(g1.shape), full(g2.shape),
        ],
        out_specs=[row_spec] * 6,
        out_shape=[out_sds] * 6,
        compiler_params=_params("parallel"),
        name="rwkv_proj",
    )(x, x, mod, gpre.reshape(1, D_MODEL), mu, _bf(w_rkv), _bf(w1), _bf(w2),
      w0.reshape(1, D_MODEL), _bf(a1), _bf(a2), a0.reshape(1, D_MODEL), _bf(g1), _bf(g2))


def _wkv_pair(r, k, v, lw, a, k_k, k_a, r_k, ln_g, ln_b, s_ref, o_ref):
    c = WKV_CHUNK
    lane = lax.broadcasted_iota(jnp.int32, (c, LANES), 1)
    head0 = lane < HEAD_DIM
    rowi = lax.broadcasted_iota(jnp.int32, (LANES, LANES), 0)
    coli = lax.broadcasted_iota(jnp.int32, (LANES, LANES), 1)
    rt, ct = rowi & (c - 1), coli & (c - 1)
    strict, incl, eye = rt > ct, rt >= ct, rowi == coli
    tril = (lax.broadcasted_iota(jnp.int32, (c, c), 0)
            >= lax.broadcasted_iota(jnp.int32, (c, c), 1)).astype(BF16)

    def headsum(x):
        s0 = jnp.sum(jnp.where(head0, x, 0.0), axis=-1, keepdims=True)
        s1 = jnp.sum(jnp.where(head0, 0.0, x), axis=-1, keepdims=True)
        return jnp.where(head0, s0, s1)

    def stack(x):
        return jnp.concatenate([jnp.where(head0, x, 0.0), jnp.where(head0, 0.0, x)], axis=0)

    kk = k * k_k
    kk = kk / jnp.maximum(jnp.sqrt(headsum(kk * kk)), 1e-12)
    kf = k * (1.0 + (a - 1.0) * k_a)
    bvec = kk * a
    yield

    cum = _nn(tril, jnp.concatenate(_split3(lw), axis=1))
    cum = cum[:, :LANES] + cum[:, LANES:2 * LANES] + cum[:, 2 * LANES:]
    cum_end = cum[c - 1:c]
    w_inv = jnp.exp(-cum)
    w_rem = jnp.exp(cum_end - cum)
    a_st = stack(-kk * jnp.exp(cum - lw))
    r_st = stack(r * jnp.exp(cum))
    b_st = stack(bvec * w_inv)
    k_st = stack(kf * w_inv)
    bp_st = _bf(stack(bvec * w_rem))
    kp_st = _bf(stack(kf * w_rem))
    v_st = _bf(stack(v))
    yield

    sc = _nt(_bf(jnp.concatenate([a_st, r_st], axis=0)), _bf(jnp.concatenate([b_st, k_st], axis=0)))
    l_ab = jnp.where(strict, sc[:LANES, :LANES], 0.0)
    l_ak = jnp.where(strict, sc[:LANES, LANES:], 0.0)
    l_rb = jnp.where(incl, sc[LANES:, :LANES], 0.0)
    l_rk = jnp.where(incl, sc[LANES:, LANES:], 0.0)
    yield

    tinv = jnp.where(eye, 1.0, l_ab)
    pw = l_ab
    for _ in range(c.bit_length() - 2):
        pwb = _bf(pw)
        pw = _nn(pwb, pwb)
        tinv = tinv + _nn(_bf(tinv), _bf(pw))
        yield

    au = _nn(_bf(tinv), _bf(jnp.concatenate([a_st, _nn(_bf(l_ak), v_st)], axis=1)))
    aub = _bf(au)
    yield
    rhs = jnp.concatenate([aub, jnp.concatenate([jnp.zeros_like(v_st), v_st], axis=1)], axis=0)
    rb = _nn(_bf(jnp.concatenate([l_rb, l_rk], axis=1)), rhs)
    gh = _tn(jnp.concatenate([bp_st, kp_st], axis=0), rhs)
    r_hat = r_st + rb[:, :LANES]
    o0 = rb[:, LANES:]
    g_mat = jnp.where(eye, jnp.exp(cum_end), 0.0) + gh[:, :LANES]
    h_mat = gh[:, LANES:]
    yield

    os = _nn(_bf(jnp.concatenate([r_hat, g_mat], axis=0)), _bf(s_ref[...]))
    s_ref[...] = os[LANES:] + h_mat
    o_st = os[:LANES] + o0
    o = o_st[:c] + o_st[c:]
    yield

    mean = headsum(o) * (1.0 / HEAD_DIM)
    oc = o - mean
    var = headsum(oc * oc) * (1.0 / HEAD_DIM)
    on = oc * lax.rsqrt(var + LNX_EPS) * ln_g + ln_b
    o_ref[...] = on + headsum(r * kf * r_k) * v


def _wkv_kernel(r_ref, k_ref, v_ref, lw_ref, a_ref, kk_ref, ka_ref, rk_ref, lg_ref, lb_ref,
                o_ref, s_ref):
    @pl.when(pl.program_id(2) == 0)
    def _():
        s_ref[...] = jnp.zeros_like(s_ref)

    def pair(p):
        sl = pl.ds(p * LANES, LANES)
        ins = [ref[:, sl] for ref in (r_ref, k_ref, v_ref, lw_ref, a_ref, kk_ref, ka_ref, rk_ref, lg_ref, lb_ref)]
        return _wkv_pair(*ins, s_ref.at[p], o_ref.at[:, sl])

    stages = [pair(p) for p in range(WKV_PAIRS)]
    while stages:
        stages = [g for g in stages if next(g, True) is None]


def _wkv(r, k, v, lw, a, k_k, k_a, r_k, lnx_g, lnx_b):
    n_chunks = SEQ // WKV_CHUNK
    width = WKV_PAIRS * LANES
    tok = pl.BlockSpec((WKV_CHUNK, width), lambda b, p, c: (b * n_chunks + c, p))
    par = pl.BlockSpec((1, width), lambda b, p, c: (0, p))
    vec = lambda t: t.reshape(1, D_MODEL)
    return pl.pallas_call(
        _wkv_kernel,
        grid=(BATCH, N_PAIRS // WKV_PAIRS, n_chunks),
        in_specs=[tok] * 5 + [par] * 5,
        out_specs=tok,
        out_shape=jax.ShapeDtypeStruct((TOKENS, D_MODEL), F32),
        scratch_shapes=[pltpu.VMEM((WKV_PAIRS, LANES, LANES), F32)],
        compiler_params=_params("parallel", "parallel", "arbitrary"),
        name="wkv_scan",
    )(r, k, v, lw, a, vec(k_k), vec(k_a), vec(r_k), vec(lnx_g), vec(lnx_b))


def _pack_rows(x):
    bits = lax.bitcast_convert_type(_bf(x).astype(F32), jnp.uint32)
    halves = []
    for h in range(PACK_HALVES):
        lo = bits[:, (2 * h) * PACK_WORDS:(2 * h + 1) * PACK_WORDS]
        hi = bits[:, (2 * h + 1) * PACK_WORDS:(2 * h + 2) * PACK_WORDS]
        halves.append(lax.bitcast_convert_type((lo >> 16) | (hi & jnp.uint32(0xFFFF0000)), jnp.int32))
    return halves


def _unpack_rows(halves):
    parts = []
    for w in halves:
        u = lax.bitcast_convert_type(w, jnp.uint32)
        parts.append(lax.bitcast_convert_type(u << 16, F32))
        parts.append(lax.bitcast_convert_type(u & jnp.uint32(0xFFFF0000), F32))
    return jnp.concatenate(parts, axis=1)


def _post_kernel(*refs, has_gate):
    if has_gate:
        o_ref, g_ref, *refs = refs
    else:
        o_ref, *refs = refs
    (x_ref, mod_ref, wo_ref, gpost_ref, gpre_ref, wr_hi_ref, wr_lo_ref, bias_ref,
     x1_ref, h2p_ref, gk_ref, ek_ref, rk_ref, cnt_ref, run_ref) = refs
    tm = ROW_TILE

    @pl.when(pl.program_id(0) == 0)
    def _():
        run_ref[...] = jnp.zeros_like(run_ref)

    mod = mod_ref[0]
    gate_m, shift_f, scale_f = mod[2:3], mod[3:4], mod[4:5]
    o = o_ref[...]
    if has_gate:
        o = o * g_ref[...]
    y = _nn(_bf(o), wo_ref[...])
    x1 = x_ref[...] + gate_m * _rms(y, gpost_ref[...])
    x1_ref[...] = x1
    h2 = _rms(x1, gpre_ref[...]) * (1.0 + scale_f) + shift_f
    for h, words in enumerate(_pack_rows(h2)):
        h2p_ref[h] = words

    hi = _bf(h2)
    lo = _bf(h2 - hi.astype(F32))
    wr_hi, wr_lo = wr_hi_ref[...], wr_lo_ref[...]
    scores = _sigmoid(_nt(wr_hi, hi) + (_nt(wr_hi, lo) + _nt(wr_lo, hi)))
    erow = lax.broadcasted_iota(jnp.int32, scores.shape, 0)
    work = scores + jnp.concatenate([bias_ref[...]] * (tm // LANES), axis=1)
    picked = jnp.zeros_like(scores)
    chosen = jnp.zeros_like(scores)
    hits, ids = [], []
    for _ in range(TOP_K):
        best = jnp.max(work, axis=0, keepdims=True)
        idx = jnp.min(jnp.where(work == best, erow, N_EXPERTS), axis=0, keepdims=True)
        hit = erow == idx
        picked = jnp.where(hit, scores, picked)
        chosen = jnp.where(hit, 1.0, chosen)
        work = jnp.where(hit, -jnp.inf, work)
        hits.append(hit)
        ids.append(idx)
    gates = ROUTED_SCALE * picked / jnp.sum(picked, axis=0, keepdims=True)

    upto = (lax.broadcasted_iota(jnp.int32, (tm, tm), 0)
            <= lax.broadcasted_iota(jnp.int32, (tm, tm), 1)).astype(BF16)
    chosen_b = _bf(chosen)
    run = run_ref[...]
    before = _nn(chosen_b, upto) - chosen + jnp.concatenate([run] * (tm // LANES), axis=1)
    run = run + _nn(chosen_b, jnp.ones((tm, LANES), BF16))
    run_ref[...] = run
    cnt_ref[...] = run

    krow = lax.broadcasted_iota(jnp.int32, (TOP_K, tm), 0)
    gk = jnp.zeros((TOP_K, tm), F32)
    ek = jnp.zeros((TOP_K, tm), jnp.int32)
    rk = jnp.zeros((TOP_K, tm), jnp.int32)
    for j in range(TOP_K):
        grow = jnp.sum(jnp.where(hits[j], gates, 0.0), axis=0, keepdims=True)
        rrow = jnp.sum(jnp.where(hits[j], before, 0.0), axis=0, keepdims=True)
        gk = jnp.where(krow == j, grow, gk)
        ek = jnp.where(krow == j, ids[j], ek)
        rk = jnp.where(krow == j, rrow.astype(jnp.int32), rk)
    gk_ref[...] = gk
    ek_ref[...] = ek
    rk_ref[...] = rk


def _post(o, g, x, mod, w_o, gpost, gpre, w_router, router_bias):
    tm = ROW_TILE
    full = lambda shape: pl.BlockSpec(shape, lambda i: (0,) * len(shape))
    row_spec = pl.BlockSpec((tm, D_MODEL), lambda i: (i, 0))
    k_spec = pl.BlockSpec((TOP_K, tm), lambda i: (0, i))
    wr = w_router.T
    wr_hi = _bf(wr)
    wr_lo = _bf(wr - wr_hi.astype(F32))
    bias = jnp.broadcast_to(router_bias[:, None], (N_EXPERTS, LANES))
    has_gate = g is not None
    acts = [o, g] if has_gate else [o]
    return pl.pallas_call(
        functools.partial(_post_kernel, has_gate=has_gate),
        grid=(TOKENS // tm,),
        in_specs=[row_spec] * (len(acts) + 1) + [
            pl.BlockSpec((1, N_ADA, D_MODEL), lambda i: (i // (SEQ // tm), 0, 0)),
            full((D_MODEL, D_MODEL)), full((1, D_MODEL)), full((1, D_MODEL)),
            full((N_EXPERTS, D_MODEL)), full((N_EXPERTS, D_MODEL)), full((N_EXPERTS, LANES)),
        ],
        out_specs=[row_spec, pl.BlockSpec((PACK_HALVES, tm, PACK_WORDS), lambda i: (0, i, 0)),
                   k_spec, k_spec, k_spec, full((N_EXPERTS, LANES))],
        out_shape=[jax.ShapeDtypeStruct((TOKENS, D_MODEL), F32),
                   jax.ShapeDtypeStruct((PACK_HALVES, TOKENS, PACK_WORDS), jnp.int32),
                   jax.ShapeDtypeStruct((TOP_K, TOKENS), F32),
                   jax.ShapeDtypeStruct((TOP_K, TOKENS), jnp.int32),
                   jax.ShapeDtypeStruct((TOP_K, TOKENS), jnp.int32),
                   jax.ShapeDtypeStruct((N_EXPERTS, LANES), F32)],
        scratch_shapes=[pltpu.VMEM((N_EXPERTS, LANES), F32)],
        compiler_params=_params("arbitrary"),
        name="mixer_out_router",
    )(*acts, x, mod, _bf(w_o), gpost.reshape(1, D_MODEL), gpre.reshape(1, D_MODEL),
      wr_hi, wr_lo, bias)


def _sc_mesh():
    return plsc.VectorSubcoreMesh(core_axis_name="core", subcore_axis_name="subcore")


def _sc_scatter_rows(rows, idx, n_out):
    n_copies = idx.shape[0]

    @pl.kernel(out_type=jax.ShapeDtypeStruct((PACK_HALVES, n_out, PACK_WORDS), rows.dtype),
               mesh=_sc_mesh(), scratch_types=[], name="sc_dispatch_rows")
    def scatter(x_hbm, i_hbm, o_hbm):
        for h in range(PACK_HALVES):
            def body(x_vmem, i_vmem, h=h):
                for k in range(n_copies):
                    pltpu.sync_copy(x_vmem, o_hbm.at[h].at[i_vmem.at[k]])

            pltpu.emit_pipeline(
                body, grid=(rows.shape[1] // SC_WINDOW,),
                in_specs=[pl.BlockSpec((SC_WINDOW, PACK_WORDS), index_map=lambda i: (i, 0)),
                          pl.BlockSpec((n_copies, SC_WINDOW), index_map=lambda i: (0, i))],
                out_specs=[],
                core_axis_name=("core", "subcore"),
                dimension_semantics=(pltpu.PARALLEL,),
            )(x_hbm.at[h], i_hbm)

    return scatter(rows, idx)


def _sc_gather_rows(tables, idx):
    n = idx.shape[0]

    @pl.kernel(out_type=jax.ShapeDtypeStruct((PACK_HALVES, n, PACK_WORDS), tables[0].dtype),
               mesh=_sc_mesh(), scratch_types=[], name="sc_collect_rows")
    def gather(*refs):
        x_hbm, (i_hbm, o_hbm) = refs[:PACK_HALVES], refs[PACK_HALVES:]
        for h in range(PACK_HALVES):
            def body(i_vmem, o_vmem, h=h):
                pltpu.sync_copy(x_hbm[h].at[i_vmem.at[0]], o_vmem)

            pltpu.emit_pipeline(
                body, grid=(n // SC_WINDOW,),
                in_specs=[pl.BlockSpec((1, SC_WINDOW), index_map=lambda i: (0, i))],
                out_specs=[pl.BlockSpec((SC_WINDOW, PACK_WORDS), index_map=lambda i: (i, 0))],
                core_axis_name=("core", "subcore"),
                dimension_semantics=(pltpu.PARALLEL,),
            )(i_hbm, o_hbm.at[h])

    return gather(*tables, idx.reshape(1, n))


def _ffn(x, wg, wu, wd):
    up = _nn(x, _bf(wu))
    gt = _nn(x, _bf(wg))
    return _nn(_bf(gt * _sigmoid(gt) * up), _bf(wd))


def _expert_kernel(be_ref, nu_ref, *refs):
    n_in = PACK_HALVES * EXPERT_SPLIT
    x_refs = refs[:n_in]
    wg_ref, wu_ref, wd_ref = refs[n_in:n_in + 3]
    y_refs = refs[n_in + 3:n_in + 3 + PACK_HALVES]
    wgb_ref, wub_ref, wdb_ref = refs[n_in + 3 + PACK_HALVES:]
    b = pl.program_id(0)

    @pl.when(jnp.logical_or(b == 0, be_ref[b] != be_ref[jnp.maximum(b - 1, 0)]))
    def _():
        wgb_ref[...] = _bf(wg_ref[0, 0])
        wub_ref[...] = _bf(wu_ref[0, 0])
        wdb_ref[...] = _bf(wd_ref[0, 0])

    @pl.when(b < nu_ref[0])
    def _():
        halves = [jnp.concatenate([x_refs[h * EXPERT_SPLIT + j][0] for j in range(EXPERT_SPLIT)], axis=0)
                  for h in range(PACK_HALVES)]
        y = _ffn(_bf(_unpack_rows(halves)), wgb_ref[...], wub_ref[...], wdb_ref[...])
        for h, words in enumerate(_pack_rows(y)):
            y_refs[h][...] = words


def _experts(layer, xs, blk_expert, n_used, w_gate, w_up, w_down):
    n_rows = xs.shape[1]
    n_blocks = n_rows // EXPERT_BLOCK
    part = EXPERT_BLOCK // EXPERT_SPLIT
    x_spec = lambda h, j: pl.BlockSpec((1, part, PACK_WORDS), lambda b, be, nu: (h, b * EXPERT_SPLIT + j, 0))
    w_spec = lambda shape: pl.BlockSpec((1, 1) + shape, lambda b, be, nu: (layer, be[b], 0, 0))
    y_spec = pl.BlockSpec((EXPERT_BLOCK, PACK_WORDS), lambda b, be, nu: (b, 0))
    n_in = PACK_HALVES * EXPERT_SPLIT
    return pl.pallas_call(
        _expert_kernel,
        grid_spec=pltpu.PrefetchScalarGridSpec(
            num_scalar_prefetch=2,
            grid=(n_blocks,),
            in_specs=[x_spec(h, j) for h in range(PACK_HALVES) for j in range(EXPERT_SPLIT)]
                     + [w_spec((D_MODEL, D_EXPERT)), w_spec((D_MODEL, D_EXPERT)), w_spec((D_EXPERT, D_MODEL))],
            out_specs=[y_spec] * PACK_HALVES,
            scratch_shapes=[pltpu.VMEM((D_MODEL, D_EXPERT), BF16), pltpu.VMEM((D_MODEL, D_EXPERT), BF16),
                            pltpu.VMEM((D_EXPERT, D_MODEL), BF16)],
        ),
        out_shape=[jax.ShapeDtypeStruct((n_rows, PACK_WORDS), xs.dtype)] * PACK_HALVES,
        compiler_params=_params("arbitrary"),
        name="moe_experts",
    )(blk_expert, n_used, *([xs] * n_in), w_gate, w_up, w_down)


def _shared_kernel(h2p_ref, sg_ref, su_ref, sd_ref, o_ref):
    h2 = _bf(_unpack_rows([h2p_ref[h] for h in range(PACK_HALVES)]))
    for h, words in enumerate(_pack_rows(_ffn(h2, sg_ref[0], su_ref[0], sd_ref[0]))):
        o_ref[h] = words


def _shared_expert(layer, h2p, ws_gate, ws_up, ws_down):
    tm = SHARED_TILE
    row_spec = pl.BlockSpec((PACK_HALVES, tm, PACK_WORDS), lambda i: (0, i, 0))
    return pl.pallas_call(
        _shared_kernel,
        grid=(TOKENS // tm,),
        in_specs=[row_spec,
                  pl.BlockSpec((1, D_MODEL, D_EXPERT), lambda i: (layer, 0, 0)),
                  pl.BlockSpec((1, D_MODEL, D_EXPERT), lambda i: (layer, 0, 0)),
                  pl.BlockSpec((1, D_EXPERT, D_MODEL), lambda i: (layer, 0, 0))],
        out_specs=row_spec,
        out_shape=jax.ShapeDtypeStruct(h2p.shape, h2p.dtype),
        compiler_params=_params("parallel"),
        name="moe_shared_expert",
    )(h2p, ws_gate, ws_up, ws_down)


def _combine_kernel(*refs):
    g_refs = refs[:TOP_K]
    gk_ref, sh_ref, x_ref, mod_ref, gpost_ref, o_ref = refs[TOP_K:]
    acc = _unpack_rows([sh_ref[h] for h in range(PACK_HALVES)])
    gk = gk_ref[...]
    lane = lax.broadcasted_iota(jnp.int32, gk.shape, 1)
    for j in range(TOP_K):
        gcol = jnp.sum(jnp.where(lane == j, gk, 0.0), axis=-1, keepdims=True)
        acc = acc + gcol * _unpack_rows([g_refs[j][h] for h in range(PACK_HALVES)])
    gate_f = mod_ref[0][5:6]
    o_ref[...] = x_ref[...] + gate_f * _rms(acc, gpost_ref[...])


def _combine(picked, gk, shared, x1, mod, gpost):
    tm = ROW_TILE
    n_tiles = TOKENS // tm
    row_spec = pl.BlockSpec((tm, D_MODEL), lambda i: (i, 0))
    pick_spec = lambda j: pl.BlockSpec((PACK_HALVES, tm, PACK_WORDS), lambda i: (0, j * n_tiles + i, 0))
    return pl.pallas_call(
        _combine_kernel,
        grid=(n_tiles,),
        in_specs=[pick_spec(j) for j in range(TOP_K)] + [
            pl.BlockSpec((tm, LANES), lambda i: (i, 0)),
            pl.BlockSpec((PACK_HALVES, tm, PACK_WORDS), lambda i: (0, i, 0)),
            row_spec,
            pl.BlockSpec((1, N_ADA, D_MODEL), lambda i: (i // (SEQ // tm), 0, 0)),
            pl.BlockSpec((1, D_MODEL), lambda i: (0, 0)),
        ],
        out_specs=row_spec,
        out_shape=jax.ShapeDtypeStruct((TOKENS, D_MODEL), F32),
        compiler_params=_params("parallel"),
        name="moe_combine",
    )(*([picked] * TOP_K), gk, shared, x1, mod, gpost.reshape(1, D_MODEL))


def _moe(layer, x1, h2p, gk, ek, rk, counts, mod, gpost, w_gate, w_up, w_down, ws_gate, ws_up, ws_down):
    cnt = counts[:, 0].astype(jnp.int32)
    padded = (cnt + EXPERT_BLOCK - 1) // EXPERT_BLOCK * EXPERT_BLOCK
    eid = jnp.arange(N_EXPERTS, dtype=jnp.int32)
    pend = jnp.sum(jnp.where(eid[:, None] <= eid[None, :], padded[:, None], 0), axis=0)
    pstart = pend - padded
    dest = rk + jnp.sum(jnp.where(ek[None] == eid[:, None, None], pstart[:, None, None], 0), axis=0)
    gk = jnp.pad(gk.T, ((0, 0), (0, LANES - TOP_K)))
    n_blocks = (TOKENS * TOP_K + N_EXPERTS * (EXPERT_BLOCK - 1)) // EXPERT_BLOCK + 1
    blk_start = jnp.arange(n_blocks, dtype=jnp.int32) * EXPERT_BLOCK
    blk_expert = jnp.minimum(jnp.sum((blk_start[None, :] >= pend[:, None]).astype(jnp.int32), axis=0),
                             N_EXPERTS - 1)
    n_used = pend[-1:] // EXPERT_BLOCK

    xs = _sc_scatter_rows(h2p, dest, n_blocks * EXPERT_BLOCK)
    shared = _shared_expert(layer, h2p, ws_gate, ws_up, ws_down)
    ys = _experts(layer, xs, blk_expert, n_used, w_gate, w_up, w_down)
    picked = _sc_gather_rows(ys, dest.reshape(-1))
    return _combine(picked, gk, shared, x1, mod, gpost)


def _sb_proj_kernel(x_ref, mod_ref, gpre_ref, w_ref, q_ref, k_ref, v_ref):
    mod = mod_ref[0]
    h = _bf(_rms(x_ref[...], gpre_ref[...]) * (1.0 + mod[1:2]) + mod[0:1])
    q_ref[...] = _bf(_nn(h, w_ref[:, :D_MODEL]) * (LOG2_E / 8.0))
    k_ref[...] = _bf(_nn(h, w_ref[:, D_MODEL:2 * D_MODEL]))
    v_ref[...] = _bf(_nn(h, w_ref[:, 2 * D_MODEL:]))


def _sb_proj(x, mod, gpre, w_qkv):
    tm = ROW_TILE
    full = lambda shape: pl.BlockSpec(shape, lambda i: (0,) * len(shape))
    row_spec = pl.BlockSpec((tm, D_MODEL), lambda i: (i, 0))
    out_sds = jax.ShapeDtypeStruct((TOKENS, D_MODEL), BF16)
    return pl.pallas_call(
        _sb_proj_kernel,
        grid=(TOKENS // tm,),
        in_specs=[row_spec,
                  pl.BlockSpec((1, N_ADA, D_MODEL), lambda i: (i // (SEQ // tm), 0, 0)),
                  full((1, D_MODEL)), full((D_MODEL, 3 * D_MODEL))],
        out_specs=[row_spec] * 3,
        out_shape=[out_sds] * 3,
        compiler_params=_params("parallel"),
        name="sb_qkv_proj",
    )(x, mod, gpre.reshape(1, D_MODEL), _bf(w_qkv))


def _sb_attn_kernel(q_ref, k_ref, v_ref, o_ref):
    blk = ATT_BLOCK
    n_heads = LANES // HEAD_DIM
    qb = pl.program_id(2)
    q = q_ref[...]
    lane = lax.broadcasted_iota(jnp.int32, (blk, LANES), 1)
    head0 = lane < HEAD_DIM
    qh = [jnp.where(head0, q, jnp.zeros_like(q)), jnp.where(head0, jnp.zeros_like(q), q)]
    rowi = lax.broadcasted_iota(jnp.int32, (blk, blk), 0)
    coli = lax.broadcasted_iota(jnp.int32, (blk, blk), 1)
    later = (rowi > coli).astype(BF16)
    ones = jnp.ones((blk, LANES), BF16)

    sub = ATT_SUB
    n_sub = blk // sub
    sub_row = lax.broadcasted_iota(jnp.int32, (sub, blk), 0)
    sub_col = lax.broadcasted_iota(jnp.int32, (sub, blk), 1)
    below = [sub_col < sub_row + r * sub for r in range(n_sub)]
    sub_head0 = lax.broadcasted_iota(jnp.int32, (sub, LANES), 1) < HEAD_DIM
    chains = [(hd, r) for hd in range(n_heads) for r in range(n_sub)]

    def chain(hd, r, ks, vs, get_carry, get_acc, mode, out):
        rows = slice(r * sub, (r + 1) * sub)
        u = _nt(qh[hd][rows], ks)
        yield
        neg_abs = lax.bitcast_convert_type(
            lax.bitcast_convert_type(u, jnp.uint32) | jnp.uint32(0x80000000), F32)
        lb = jnp.minimum(u, 0.0) - jnp.log2(1.0 + jnp.exp2(neg_abs))
        l1m = lb - u
        if mode == "diag":
            l1m = jnp.where(below[r], l1m, 0.0)
        l1b = _bf(l1m)
        yield
        carry = get_carry()
        tail = _nn(l1b, later) + jnp.concatenate([carry] * (blk // LANES), axis=1)
        out["carry"] = carry + _nn(l1b, ones)
        yield
        w = jnp.exp2(lb + tail)
        if mode == "diag":
            w = jnp.where(below[r], w, 0.0)
        elif mode == "prev":
            w = jnp.where(qb > 0, w, 0.0)
        wb = _bf(w)
        yield
        out["acc"] = get_acc() + _nn(wb, vs)

    def load_kv(kb):
        start = pl.multiple_of(kb * blk, blk)
        return k_ref[pl.ds(start, blk), :], v_ref[pl.ds(start, blk), :]

    def run(gens):
        while gens:
            gens = [g for g in gens if next(g, True) is None]

    def block(kb, carry, acc):
        ks, vs = load_kv(kb)
        outs = [{} for _ in chains]
        run([chain(hd, r, ks, vs, lambda c=c: carry[c], lambda c=c: acc[c], "plain", outs[c])
             for c, (hd, r) in enumerate(chains)])
        return [o["carry"] for o in outs], [o["acc"] for o in outs]

    zeros = jnp.zeros((sub, LANES), F32)
    ks_d, vs_d = load_kv(qb)
    ks_p, vs_p = load_kv(jnp.maximum(qb - 1, 0))
    outs_d = [{} for _ in chains]
    outs_p = [{} for _ in chains]
    run([chain(hd, r, ks_d, vs_d, lambda: zeros, lambda: zeros, "diag", outs_d[c])
         for c, (hd, r) in enumerate(chains)]
        + [chain(hd, r, ks_p, vs_p, lambda c=c: outs_d[c]["carry"], lambda c=c: outs_d[c]["acc"],
                 "prev", outs_p[c]) for c, (hd, r) in enumerate(chains)])
    carry = [o["carry"] for o in outs_p]
    acc = [o["acc"] for o in outs_p]

    def cmax_of(cr):
        return jnp.max(functools.reduce(jnp.maximum, cr))

    def cond(st):
        kb, _, _, cmax = st
        return jnp.logical_and(kb >= 0, cmax > EXP2_UNDERFLOW)

    def body(st):
        kb, cr, ac, _ = st
        cr, ac = block(kb, list(cr), list(ac))
        return kb - 1, tuple(cr), tuple(ac), cmax_of(cr)

    _, _, acc, _ = lax.while_loop(cond, body, (qb - 2, tuple(carry), tuple(acc), cmax_of(carry)))
    for r in range(n_sub):
        rows = slice(r * sub, (r + 1) * sub)
        o_ref[rows, :] = jnp.where(sub_head0, acc[r], acc[n_sub + r])


def _sb_attn(q, k, v):
    blk = ATT_BLOCK
    n_blk = SEQ // blk
    q_spec = pl.BlockSpec((blk, LANES), lambda b, p, i: (b * n_blk + i, p))
    kv_spec = pl.BlockSpec((SEQ, LANES), lambda b, p, i: (b, p))
    return pl.pallas_call(
        _sb_attn_kernel,
        grid=(BATCH, N_PAIRS, n_blk),
        in_specs=[q_spec, kv_spec, kv_spec],
        out_specs=q_spec,
        out_shape=jax.ShapeDtypeStruct((TOKENS, D_MODEL), F32),
        compiler_params=_params("parallel", "parallel", "arbitrary"),
        name="sb_attention",
    )(q, k, v)


def kernel(x, c, ada_w, ada_b, norm_pre_mix, norm_post_mix, norm_pre_ffn, norm_post_ffn, rwkv_mu, rwkv_w_rkv, rwkv_w_w1, rwkv_w_w2, rwkv_w0, rwkv_w_a1, rwkv_w_a2, rwkv_a0, rwkv_w_g1, rwkv_w_g2, rwkv_k_k, rwkv_k_a, rwkv_r_k, rwkv_lnx_g, rwkv_lnx_b, rwkv_w_o, sb_w_qkv, sb_w_o, moe_w_router, moe_router_bias, moe_w_gate, moe_w_up, moe_w_down, moe_ws_gate, moe_ws_up, moe_ws_down):
    mod_all = _ada_mod(c, ada_w, ada_b)
    xt = x.reshape(TOKENS, D_MODEL)
    for layer in range(DEPTH):
        mod = mod_all[layer]
        i = layer // 2
        if layer % 2 == 0:
            r, k, v, lw, a, g = _rwkv_proj(
                xt, mod, norm_pre_mix[layer], rwkv_mu[i], rwkv_w_rkv[i], rwkv_w_w1[i], rwkv_w_w2[i],
                rwkv_w0[i], rwkv_w_a1[i], rwkv_w_a2[i], rwkv_a0[i], rwkv_w_g1[i], rwkv_w_g2[i])
            o = _wkv(r, k, v, lw, a, rwkv_k_k[i], rwkv_k_a[i], rwkv_r_k[i], rwkv_lnx_g[i], rwkv_lnx_b[i])
            w_o = rwkv_w_o[i]
        else:
            q, k, v = _sb_proj(xt, mod, norm_pre_mix[layer], sb_w_qkv[i])
            o, g = _sb_attn(q, k, v), None
            w_o = sb_w_o[i]
        x1, h2p, gk, ek, rk, counts = _post(o, g, xt, mod, w_o, norm_post_mix[layer], norm_pre_ffn[layer],
                                            moe_w_router[layer], moe_router_bias[layer])
        xt = _moe(layer, x1, h2p, gk, ek, rk, counts, mod, norm_post_ffn[layer], moe_w_gate, moe_w_up,
                  moe_w_down, moe_ws_gate, moe_ws_up, moe_ws_down)
    return xt.reshape(BATCH, SEQ, D_MODEL)
```

```python
import functools

import jax
import jax.numpy as jnp
from jax import lax
from jax.experimental import pallas as pl
from jax.experimental.pallas import tpu as pltpu
from jax.experimental.pallas import tpu_sc as plsc

D_MODEL = 1024
BATCH = 2
SEQ = 8192
TOKENS = BATCH * SEQ
DEPTH = 2
HEAD_DIM = 64
N_EXPERTS = 64
TOP_K = 8
D_EXPERT = 256
ROUTED_SCALE = 2.5
RMS_EPS = 1e-6
LNX_EPS = 64e-5
N_ADA = 6

LANES = 128
SUBLANES = 8
N_PAIRS = D_MODEL // LANES
WKV_CHUNK = 64
WKV_PAIRS = 8
ROW_TILE = 512
SHARED_TILE = 512
EXPERT_BLOCK = 1024
EXPERT_SPLIT = 2
PACK_HALVES = 2
PACK_WORDS = D_MODEL // (2 * PACK_HALVES)
SC_WINDOW = 128
ATT_BLOCK = 256
ATT_SUB = 128
LOG2_E = 1.4426950408889634
EXP2_UNDERFLOW = -153.0
VMEM_LIMIT = 56 * 1024 * 1024

F32 = jnp.float32
BF16 = jnp.bfloat16


def _nn(a, b):
    return jnp.dot(a, b, preferred_element_type=F32)


def _nt(a, b):
    return lax.dot_general(a, b, (((1,), (1,)), ((), ())), preferred_element_type=F32)


def _tn(a, b):
    return lax.dot_general(a, b, (((0,), (0,)), ((), ())), preferred_element_type=F32)


def _bf(x):
    return x.astype(BF16)


def _sigmoid(x):
    return 1.0 / (1.0 + jnp.exp(-x))


def _rms(xv, g):
    ms = jnp.mean(xv * xv, axis=-1, keepdims=True)
    return xv * lax.rsqrt(ms + RMS_EPS) * g


def _split3(x):
    hi = x.astype(BF16)
    r1 = x - hi.astype(F32)
    mid = r1.astype(BF16)
    lo = (r1 - mid.astype(F32)).astype(BF16)
    return hi, mid, lo


def _params(*sem):
    return pltpu.CompilerParams(dimension_semantics=sem, vmem_limit_bytes=VMEM_LIMIT)


def _ada_kernel(c_ref, w_ref, b_ref, o_ref):
    cv = c_ref[...]
    o_ref[0] = _nn(cv * _sigmoid(cv), w_ref[0]) + b_ref[0]


def _ada_mod(c, ada_w, ada_b):
    tn = 1536
    c8 = jnp.pad(c, ((0, SUBLANES - BATCH), (0, 0)))
    out = pl.pallas_call(
        _ada_kernel,
        grid=(DEPTH, N_ADA * D_MODEL // tn),
        in_specs=[
            pl.BlockSpec((SUBLANES, D_MODEL), lambda l, j: (0, 0)),
            pl.BlockSpec((1, D_MODEL, tn), lambda l, j: (l, 0, j)),
            pl.BlockSpec((1, 1, tn), lambda l, j: (l, 0, j)),
        ],
        out_specs=pl.BlockSpec((1, SUBLANES, tn), lambda l, j: (l, 0, j)),
        out_shape=jax.ShapeDtypeStruct((DEPTH, SUBLANES, N_ADA * D_MODEL), F32),
        compiler_params=_params("parallel", "parallel"),
        name="ada_mod",
    )(c8, ada_w, ada_b.reshape(DEPTH, 1, N_ADA * D_MODEL))
    return out[:, :BATCH].reshape(DEPTH, BATCH, N_ADA, D_MODEL)


def _rwkv_proj_kernel(x_ref, xp_ref, mod_ref, gpre_ref, mu_ref, wrkv_ref, w1_ref, w2_ref, w0_ref,
                      a1_ref, a2_ref, a0_ref, g1_ref, g2_ref,
                      r_ref, k_ref, v_ref, lw_ref, a_ref, g_ref):
    i = pl.program_id(0)
    mod = mod_ref[0]
    shift, scale = mod[0:1], mod[1:2]
    gpre = gpre_ref[...]

    def modnorm(xv):
        return _rms(xv, gpre) * (1.0 + scale) + shift

    h = modnorm(x_ref[...])
    hp = modnorm(xp_ref[...])[SUBLANES - 1:SUBLANES]
    hp = jnp.where(i % (SEQ // ROW_TILE) == 0, 0.0, hp)
    row = lax.broadcasted_iota(jnp.int32, h.shape, 0)
    hs = jnp.where(row == 0, hp, pltpu.roll(h, 1, 0))
    xx = hs - h
    mu = mu_ref[...]

    def mix(n):
        return _bf(h + xx * mu[n:n + 1])

    r_ref[...] = _bf(_nn(mix(0), wrkv_ref[0]))
    k_ref[...] = _bf(_nn(mix(1), wrkv_ref[1]))
    v_ref[...] = _bf(_nn(mix(2), wrkv_ref[2]))
    wl = w0_ref[...] + _nn(_bf(jnp.tanh(_nn(mix(3), w1_ref[...]))), w2_ref[...])
    nwl = -wl
    sp = jnp.maximum(nwl, 0.0) + jnp.log(1.0 + jnp.exp(-jnp.abs(nwl)))
    lw_ref[...] = -jnp.exp(-sp - 0.5)
    a_ref[...] = _sigmoid(a0_ref[...] + _nn(_bf(_nn(mix(4), a1_ref[...])), a2_ref[...]))
    g_ref[...] = _bf(_nn(_bf(_sigmoid(_nn(mix(5), g1_ref[...]))), g2_ref[...]))


def _rwkv_proj(x, mod, gpre, mu, w_rkv, w1, w2, w0, a1, a2, a0, g1, g2):
    tm = ROW_TILE
    full = lambda shape: pl.BlockSpec(shape, lambda i: (0,) * len(shape))
    row_spec = pl.BlockSpec((tm, D_MODEL), lambda i: (i, 0))
    out_sds = [jax.ShapeDtypeStruct((TOKENS, D_MODEL), dt) for dt in (BF16, BF16, BF16, F32, F32, BF16)]
    return pl.pallas_call(
        _rwkv_proj_kernel,
        grid=(TOKENS // tm,),
        in_specs=[
            row_spec,
            pl.BlockSpec((SUBLANES, D_MODEL), lambda i: (jnp.maximum(i * (tm // SUBLANES) - 1, 0), 0)),
            pl.BlockSpec((1, N_ADA, D_MODEL), lambda i: (i // (SEQ // tm), 0, 0)),
            full((1, D_MODEL)), full((6, D_MODEL)), full((3, D_MODEL, D_MODEL)),
            full(w1.shape), full(w2.shape), full((1, D_MODEL)),
            full(a1.shape), full(a2.shape), full((1, D_MODEL)),
            full(g1.shape), full(g2.shape),
        ],
        out_specs=[row_spec] * 6,
        out_shape=out_sds,
        compiler_params=_params("parallel"),
        name="rwkv_proj",
    )(x, x, mod, gpre.reshape(1, D_MODEL), mu, _bf(w_rkv), _bf(w1), _bf(w2),
      w0.reshape(1, D_MODEL), _bf(a1), _bf(a2), a0.reshape(1, D_MODEL), _bf(g1), _bf(g2))


def _wkv_pair(r, k, v, lw, a, k_k, k_a, r_k, ln_g, ln_b, s_ref, o_ref):
    c = WKV_CHUNK
    lane = lax.broadcasted_iota(jnp.int32, (c, LANES), 1)
    head0 = lane < HEAD_DIM
    rowi = lax.broadcasted_iota(jnp.int32, (LANES, LANES), 0)
    coli = lax.broadcasted_iota(jnp.int32, (LANES, LANES), 1)
    rt, ct = rowi & (c - 1), coli & (c - 1)
    strict, incl, eye = rt > ct, rt >= ct, rowi == coli
    tril = (lax.broadcasted_iota(jnp.int32, (c, c), 0)
            >= lax.broadcasted_iota(jnp.int32, (c, c), 1)).astype(BF16)

    def headsum(x):
        s0 = jnp.sum(jnp.where(head0, x, 0.0), axis=-1, keepdims=True)
        s1 = jnp.sum(jnp.where(head0, 0.0, x), axis=-1, keepdims=True)
        return jnp.where(head0, s0, s1)

    def stack(x):
        return jnp.concatenate([jnp.where(head0, x, 0.0), jnp.where(head0, 0.0, x)], axis=0)

    kk = k * k_k
    kk = kk / jnp.maximum(jnp.sqrt(headsum(kk * kk)), 1e-12)
    kf = k * (1.0 + (a - 1.0) * k_a)
    bvec = kk * a
    yield

    cum = _nn(tril, jnp.concatenate(_split3(lw), axis=1))
    cum = cum[:, :LANES] + cum[:, LANES:2 * LANES] + cum[:, 2 * LANES:]
    cum_end = cum[c - 1:c]
    w_inv = jnp.exp(-cum)
    w_rem = jnp.exp(cum_end - cum)
    a_st = stack(-kk * jnp.exp(cum - lw))
    r_st = stack(r * jnp.exp(cum))
    b_st = stack(bvec * w_inv)
    k_st = stack(kf * w_inv)
    bp_st = _bf(stack(bvec * w_rem))
    kp_st = _bf(stack(kf * w_rem))
    v_st = _bf(stack(v))
    yield

    sc = _nt(_bf(jnp.concatenate([a_st, r_st], axis=0)), _bf(jnp.concatenate([b_st, k_st], axis=0)))
    l_ab = jnp.where(strict, sc[:LANES, :LANES], 0.0)
    l_ak = jnp.where(strict, sc[:LANES, LANES:], 0.0)
    l_rb = jnp.where(incl, sc[LANES:, :LANES], 0.0)
    l_rk = jnp.where(incl, sc[LANES:, LANES:], 0.0)
    yield

    tinv = jnp.where(eye, 1.0, l_ab)
    pw = l_ab
    for _ in range(c.bit_length() - 2):
        pwb = _bf(pw)
        pw = _nn(pwb, pwb)
        tinv = tinv + _nn(_bf(tinv), _bf(pw))
        yield

    au = _nn(_bf(tinv), _bf(jnp.concatenate([a_st, _nn(_bf(l_ak), v_st)], axis=1)))
    aub = _bf(au)
    yield
    rhs = jnp.concatenate([aub, jnp.concatenate([jnp.zeros_like(v_st), v_st], axis=1)], axis=0)
    rb = _nn(_bf(jnp.concatenate([l_rb, l_rk], axis=1)), rhs)
    gh = _tn(jnp.concatenate([bp_st, kp_st], axis=0), rhs)
    r_hat = r_st + rb[:, :LANES]
    o0 = rb[:, LANES:]
    g_mat = jnp.where(eye, jnp.exp(cum_end), 0.0) + gh[:, :LANES]
    h_mat = gh[:, LANES:]
    yield

    os = _nn(_bf(jnp.concatenate([r_hat, g_mat], axis=0)), _bf(s_ref[...]))
    s_ref[...] = os[LANES:] + h_mat
    o_st = os[:LANES] + o0
    o = o_st[:c] + o_st[c:]
    yield

    mean = headsum(o) * (1.0 / HEAD_DIM)
    oc = o - mean
    var = headsum(oc * oc) * (1.0 / HEAD_DIM)
    on = oc * lax.rsqrt(var + LNX_EPS) * ln_g + ln_b
    o_ref[...] = on + headsum(r * kf * r_k) * v


def _wkv_kernel(r_ref, k_ref, v_ref, lw_ref, a_ref, kk_ref, ka_ref, rk_ref, lg_ref, lb_ref,
                o_ref, s_ref):
    @pl.when(pl.program_id(2) == 0)
    def _():
        s_ref[...] = jnp.zeros_like(s_ref)

    def pair(p):
        sl = pl.ds(p * LANES, LANES)
        ins = [ref[:, sl].astype(F32)
               for ref in (r_ref, k_ref, v_ref, lw_ref, a_ref, kk_ref, ka_ref, rk_ref, lg_ref, lb_ref)]
        return _wkv_pair(*ins, s_ref.at[p], o_ref.at[:, sl])

    stages = [pair(p) for p in range(WKV_PAIRS)]
    while stages:
        stages = [g for g in stages if next(g, True) is None]


def _wkv(r, k, v, lw, a, k_k, k_a, r_k, lnx_g, lnx_b):
    n_chunks = SEQ // WKV_CHUNK
    width = WKV_PAIRS * LANES
    tok = pl.BlockSpec((WKV_CHUNK, width), lambda b, p, c: (b * n_chunks + c, p))
    par = pl.BlockSpec((1, width), lambda b, p, c: (0, p))
    vec = lambda t: t.reshape(1, D_MODEL)
    return pl.pallas_call(
        _wkv_kernel,
        grid=(BATCH, N_PAIRS // WKV_PAIRS, n_chunks),
        in_specs=[tok] * 5 + [par] * 5,
        out_specs=tok,
        out_shape=jax.ShapeDtypeStruct((TOKENS, D_MODEL), F32),
        scratch_shapes=[pltpu.VMEM((WKV_PAIRS, LANES, LANES), F32)],
        compiler_params=_params("parallel", "parallel", "arbitrary"),
        name="wkv_scan",
    )(r, k, v, lw, a, vec(k_k), vec(k_a), vec(r_k), vec(lnx_g), vec(lnx_b))


def _pack_rows(x):
    bits = lax.bitcast_convert_type(_bf(x).astype(F32), jnp.uint32)
    halves = []
    for h in range(PACK_HALVES):
        lo = bits[:, (2 * h) * PACK_WORDS:(2 * h + 1) * PACK_WORDS]
        hi = bits[:, (2 * h + 1) * PACK_WORDS:(2 * h + 2) * PACK_WORDS]
        halves.append(lax.bitcast_convert_type((lo >> 16) | (hi & jnp.uint32(0xFFFF0000)), jnp.int32))
    return halves


def _unpack_rows(halves):
    parts = []
    for w in halves:
        u = lax.bitcast_convert_type(w, jnp.uint32)
        parts.append(lax.bitcast_convert_type(u << 16, F32))
        parts.append(lax.bitcast_convert_type(u & jnp.uint32(0xFFFF0000), F32))
    return jnp.concatenate(parts, axis=1)


def _post_kernel(*refs, has_gate):
    if has_gate:
        o_ref, g_ref, *refs = refs
    else:
        o_ref, *refs = refs
    (x_ref, mod_ref, wo_ref, gpost_ref, gpre_ref, wr_hi_ref, wr_lo_ref, bias_ref,
     x1_ref, h2p_ref, gk_ref, ek_ref, rk_ref, cnt_ref, run_ref) = refs
    tm = ROW_TILE

    @pl.when(pl.program_id(0) == 0)
    def _():
        run_ref[...] = jnp.zeros_like(run_ref)

    mod = mod_ref[0]
    gate_m, shift_f, scale_f = mod[2:3], mod[3:4], mod[4:5]
    o = o_ref[...]
    if has_gate:
        o = o * g_ref[...].astype(F32)
    y = _nn(_bf(o), wo_ref[...])
    x1 = x_ref[...] + gate_m * _rms(y, gpost_ref[...])
    x1_ref[...] = x1
    h2 = _rms(x1, gpre_ref[...]) * (1.0 + scale_f) + shift_f
    for h, words in enumerate(_pack_rows(h2)):
        h2p_ref[h] = words

    hi = _bf(h2)
    lo = _bf(h2 - hi.astype(F32))
    wr_hi, wr_lo = wr_hi_ref[...], wr_lo_ref[...]
    scores = _sigmoid(_nt(wr_hi, hi) + (_nt(wr_hi, lo) + _nt(wr_lo, hi)))
    erow = lax.broadcasted_iota(jnp.int32, scores.shape, 0)
    work = scores + jnp.concatenate([bias_ref[...]] * (tm // LANES), axis=1)
    picked = jnp.zeros_like(scores)
    chosen = jnp.zeros_like(scores)
    hits, ids = [], []
    for _ in range(TOP_K):
        best = jnp.max(work, axis=0, keepdims=True)
        idx = jnp.min(jnp.where(work == best, erow, N_EXPERTS), axis=0, keepdims=True)
        hit = erow == idx
        picked = jnp.where(hit, scores, picked)
        chosen = jnp.where(hit, 1.0, chosen)
        work = jnp.where(hit, -jnp.inf, work)
        hits.append(hit)
        ids.append(idx)
    gates = ROUTED_SCALE * picked / jnp.sum(picked, axis=0, keepdims=True)

    upto = (lax.broadcasted_iota(jnp.int32, (tm, tm), 0)
            <= lax.broadcasted_iota(jnp.int32, (tm, tm), 1)).astype(BF16)
    chosen_b = _bf(chosen)
    run = run_ref[...]
    before = _nn(chosen_b, upto) - chosen + jnp.concatenate([run] * (tm // LANES), axis=1)
    run = run + _nn(chosen_b, jnp.ones((tm, LANES), BF16))
    run_ref[...] = run
    cnt_ref[...] = run

    krow = lax.broadcasted_iota(jnp.int32, (TOP_K, tm), 0)
    gk = jnp.zeros((TOP_K, tm), F32)
    ek = jnp.zeros((TOP_K, tm), jnp.int32)
    rk = jnp.zeros((TOP_K, tm), jnp.int32)
    for j in range(TOP_K):
        grow = jnp.sum(jnp.where(hits[j], gates, 0.0), axis=0, keepdims=True)
        rrow = jnp.sum(jnp.where(hits[j], before, 0.0), axis=0, keepdims=True)
        gk = jnp.where(krow == j, grow, gk)
        ek = jnp.where(krow == j, ids[j], ek)
        rk = jnp.where(krow == j, rrow.astype(jnp.int32), rk)
    gk_ref[...] = gk
    ek_ref[...] = ek
    rk_ref[...] = rk


def _post(o, g, x, mod, w_o, gpost, gpre, w_router, router_bias):
    tm = ROW_TILE
    full = lambda shape: pl.BlockSpec(shape, lambda i: (0,) * len(shape))
    row_spec = pl.BlockSpec((tm, D_MODEL), lambda i: (i, 0))
    k_spec = pl.BlockSpec((TOP_K, tm), lambda i: (0, i))
    wr = w_router.T
    wr_hi = _bf(wr)
    wr_lo = _bf(wr - wr_hi.astype(F32))
    bias = jnp.broadcast_to(router_bias[:, None], (N_EXPERTS, LANES))
    has_gate = g is not None
    acts = [o, g] if has_gate else [o]
    return pl.pallas_call(
        functools.partial(_post_kernel, has_gate=has_gate),
        grid=(TOKENS // tm,),
        in_specs=[row_spec] * (len(acts) + 1) + [
            pl.BlockSpec((1, N_ADA, D_MODEL), lambda i: (i // (SEQ // tm), 0, 0)),
            full((D_MODEL, D_MODEL)), full((1, D_MODEL)), full((1, D_MODEL)),
            full((N_EXPERTS, D_MODEL)), full((N_EXPERTS, D_MODEL)), full((N_EXPERTS, LANES)),
        ],
        out_specs=[row_spec, pl.BlockSpec((PACK_HALVES, tm, PACK_WORDS), lambda i: (0, i, 0)),
                   k_spec, k_spec, k_spec, full((N_EXPERTS, LANES))],
        out_shape=[jax.ShapeDtypeStruct((TOKENS, D_MODEL), F32),
                   jax.ShapeDtypeStruct((PACK_HALVES, TOKENS, PACK_WORDS), jnp.int32),
                   jax.ShapeDtypeStruct((TOP_K, TOKENS), F32),
                   jax.ShapeDtypeStruct((TOP_K, TOKENS), jnp.int32),
                   jax.ShapeDtypeStruct((TOP_K, TOKENS), jnp.int32),
                   jax.ShapeDtypeStruct((N_EXPERTS, LANES), F32)],
        scratch_shapes=[pltpu.VMEM((N_EXPERTS, LANES), F32)],
        compiler_params=_params("arbitrary"),
        name="mixer_out_router",
    )(*acts, x, mod, _bf(w_o), gpost.reshape(1, D_MODEL), gpre.reshape(1, D_MODEL),
      wr_hi, wr_lo, bias)


def _sc_mesh():
    return plsc.VectorSubcoreMesh(core_axis_name="core", subcore_axis_name="subcore")


def _sc_scatter_rows(rows, idx, n_out):
    n_copies = idx.shape[0]

    @pl.kernel(out_type=jax.ShapeDtypeStruct((PACK_HALVES, n_out, PACK_WORDS), rows.dtype),
               mesh=_sc_mesh(), scratch_types=[], name="sc_dispatch_rows")
    def scatter(x_hbm, i_hbm, o_hbm):
        for h in range(PACK_HALVES):
            def body(x_vmem, i_vmem, h=h):
                for k in range(n_copies):
                    pltpu.sync_copy(x_vmem, o_hbm.at[h].at[i_vmem.at[k]])

            pltpu.emit_pipeline(
                body, grid=(rows.shape[1] // SC_WINDOW,),
                in_specs=[pl.BlockSpec((SC_WINDOW, PACK_WORDS), index_map=lambda i: (i, 0)),
                          pl.BlockSpec((n_copies, SC_WINDOW), index_map=lambda i: (0, i))],
                out_specs=[],
                core_axis_name=("core", "subcore"),
                dimension_semantics=(pltpu.PARALLEL,),
            )(x_hbm.at[h], i_hbm)

    return scatter(rows, idx)


def _sc_gather_rows(tables, idx):
    n = idx.shape[0]

    @pl.kernel(out_type=jax.ShapeDtypeStruct((PACK_HALVES, n, PACK_WORDS), tables[0].dtype),
               mesh=_sc_mesh(), scratch_types=[], name="sc_collect_rows")
    def gather(*refs):
        x_hbm, (i_hbm, o_hbm) = refs[:PACK_HALVES], refs[PACK_HALVES:]
        for h in range(PACK_HALVES):
            def body(i_vmem, o_vmem, h=h):
                pltpu.sync_copy(x_hbm[h].at[i_vmem.at[0]], o_vmem)

            pltpu.emit_pipeline(
                body, grid=(n // SC_WINDOW,),
                in_specs=[pl.BlockSpec((1, SC_WINDOW), index_map=lambda i: (0, i))],
                out_specs=[pl.BlockSpec((SC_WINDOW, PACK_WORDS), index_map=lambda i: (i, 0))],
                core_axis_name=("core", "subcore"),
                dimension_semantics=(pltpu.PARALLEL,),
            )(i_hbm, o_hbm.at[h])

    return gather(*tables, idx.reshape(1, n))


def _ffn(x, wg, wu, wd):
    up = _nn(x, _bf(wu))
    gt = _nn(x, _bf(wg))
    return _nn(_bf(gt * _sigmoid(gt) * up), _bf(wd))


def _expert_kernel(be_ref, nu_ref, *refs):
    n_in = PACK_HALVES * EXPERT_SPLIT
    x_refs = refs[:n_in]
    wg_ref, wu_ref, wd_ref = refs[n_in:n_in + 3]
    y_refs = refs[n_in + 3:n_in + 3 + PACK_HALVES]
    wgb_ref, wub_ref, wdb_ref = refs[n_in + 3 + PACK_HALVES:]
    b = pl.program_id(0)

    @pl.when(jnp.logical_or(b == 0, be_ref[b] != be_ref[jnp.maximum(b - 1, 0)]))
    def _():
        wgb_ref[...] = _bf(wg_ref[0, 0])
        wub_ref[...] = _bf(wu_ref[0, 0])
        wdb_ref[...] = _bf(wd_ref[0, 0])

    @pl.when(b < nu_ref[0])
    def _():
        halves = [jnp.concatenate([x_refs[h * EXPERT_SPLIT + j][0] for j in range(EXPERT_SPLIT)], axis=0)
                  for h in range(PACK_HALVES)]
        y = _ffn(_bf(_unpack_rows(halves)), wgb_ref[...], wub_ref[...], wdb_ref[...])
        for h, words in enumerate(_pack_rows(y)):
            y_refs[h][...] = words


def _experts(layer, xs, blk_expert, n_used, w_gate, w_up, w_down):
    n_rows = xs.shape[1]
    n_blocks = n_rows // EXPERT_BLOCK
    part = EXPERT_BLOCK // EXPERT_SPLIT
    x_spec = lambda h, j: pl.BlockSpec((1, part, PACK_WORDS), lambda b, be, nu: (h, b * EXPERT_SPLIT + j, 0))
    w_spec = lambda shape: pl.BlockSpec((1, 1) + shape, lambda b, be, nu: (layer, be[b], 0, 0))
    y_spec = pl.BlockSpec((EXPERT_BLOCK, PACK_WORDS), lambda b, be, nu: (b, 0))
    n_in = PACK_HALVES * EXPERT_SPLIT
    return pl.pallas_call(
        _expert_kernel,
        grid_spec=pltpu.PrefetchScalarGridSpec(
            num_scalar_prefetch=2,
            grid=(n_blocks,),
            in_specs=[x_spec(h, j) for h in range(PACK_HALVES) for j in range(EXPERT_SPLIT)]
                     + [w_spec((D_MODEL, D_EXPERT)), w_spec((D_MODEL, D_EXPERT)), w_spec((D_EXPERT, D_MODEL))],
            out_specs=[y_spec] * PACK_HALVES,
            scratch_shapes=[pltpu.VMEM((D_MODEL, D_EXPERT), BF16), pltpu.VMEM((D_MODEL, D_EXPERT), BF16),
                            pltpu.VMEM((D_EXPERT, D_MODEL), BF16)],
        ),
        out_shape=[jax.ShapeDtypeStruct((n_rows, PACK_WORDS), xs.dtype)] * PACK_HALVES,
        compiler_params=_params("arbitrary"),
        name="moe_experts",
    )(blk_expert, n_used, *([xs] * n_in), w_gate, w_up, w_down)


def _shared_kernel(h2p_ref, sg_ref, su_ref, sd_ref, o_ref):
    h2 = _bf(_unpack_rows([h2p_ref[h] for h in range(PACK_HALVES)]))
    for h, words in enumerate(_pack_rows(_ffn(h2, sg_ref[0], su_ref[0], sd_ref[0]))):
        o_ref[h] = words


def _shared_expert(layer, h2p, ws_gate, ws_up, ws_down):
    tm = SHARED_TILE
    row_spec = pl.BlockSpec((PACK_HALVES, tm, PACK_WORDS), lambda i: (0, i, 0))
    return pl.pallas_call(
        _shared_kernel,
        grid=(TOKENS // tm,),
        in_specs=[row_spec,
                  pl.BlockSpec((1, D_MODEL, D_EXPERT), lambda i: (layer, 0, 0)),
                  pl.BlockSpec((1, D_MODEL, D_EXPERT), lambda i: (layer, 0, 0)),
                  pl.BlockSpec((1, D_EXPERT, D_MODEL), lambda i: (layer, 0, 0))],
        out_specs=row_spec,
        out_shape=jax.ShapeDtypeStruct(h2p.shape, h2p.dtype),
        compiler_params=_params("parallel"),
        name="moe_shared_expert",
    )(h2p, ws_gate, ws_up, ws_down)


def _combine_kernel(*refs):
    g_refs = refs[:TOP_K]
    gk_ref, sh_ref, x_ref, mod_ref, gpost_ref, o_ref = refs[TOP_K:]
    acc = _unpack_rows([sh_ref[h] for h in range(PACK_HALVES)])
    gk = gk_ref[...].T
    for j in range(TOP_K):
        acc = acc + gk[:, j:j + 1] * _unpack_rows([g_refs[j][h] for h in range(PACK_HALVES)])
    gate_f = mod_ref[0][5:6]
    o_ref[...] = x_ref[...] + gate_f * _rms(acc, gpost_ref[...])


def _combine(picked, gk, shared, x1, mod, gpost):
    tm = ROW_TILE
    n_tiles = TOKENS // tm
    row_spec = pl.BlockSpec((tm, D_MODEL), lambda i: (i, 0))
    pick_spec = lambda j: pl.BlockSpec((PACK_HALVES, tm, PACK_WORDS), lambda i: (0, j * n_tiles + i, 0))
    return pl.pallas_call(
        _combine_kernel,
        grid=(n_tiles,),
        in_specs=[pick_spec(j) for j in range(TOP_K)] + [
            pl.BlockSpec((TOP_K, tm), lambda i: (0, i)),
            pl.BlockSpec((PACK_HALVES, tm, PACK_WORDS), lambda i: (0, i, 0)),
            row_spec,
            pl.BlockSpec((1, N_ADA, D_MODEL), lambda i: (i // (SEQ // tm), 0, 0)),
            pl.BlockSpec((1, D_MODEL), lambda i: (0, 0)),
        ],
        out_specs=row_spec,
        out_shape=jax.ShapeDtypeStruct((TOKENS, D_MODEL), F32),
        compiler_params=_params("parallel"),
        name="moe_combine",
    )(*([picked] * TOP_K), gk, shared, x1, mod, gpost.reshape(1, D_MODEL))


def _moe(layer, x1, h2p, gk, ek, rk, counts, mod, gpost, w_gate, w_up, w_down, ws_gate, ws_up, ws_down):
    cnt = counts[:, 0].astype(jnp.int32)
    padded = (cnt + EXPERT_BLOCK - 1) // EXPERT_BLOCK * EXPERT_BLOCK
    eid = jnp.arange(N_EXPERTS, dtype=jnp.int32)
    pend = jnp.sum(jnp.where(eid[:, None] <= eid[None, :], padded[:, None], 0), axis=0)
    pstart = pend - padded
    dest = rk + jnp.sum(jnp.where(ek[None] == eid[:, None, None], pstart[:, None, None], 0), axis=0)
    n_blocks = (TOKENS * TOP_K + N_EXPERTS * (EXPERT_BLOCK - 1)) // EXPERT_BLOCK + 1
    blk_start = jnp.arange(n_blocks, dtype=jnp.int32) * EXPERT_BLOCK
    blk_expert = jnp.minimum(jnp.sum((blk_start[None, :] >= pend[:, None]).astype(jnp.int32), axis=0),
                             N_EXPERTS - 1)
    n_used = pend[-1:] // EXPERT_BLOCK

    xs = _sc_scatter_rows(h2p, dest, n_blocks * EXPERT_BLOCK)
    shared = _shared_expert(layer, h2p, ws_gate, ws_up, ws_down)
    ys = _experts(layer, xs, blk_expert, n_used, w_gate, w_up, w_down)
    picked = _sc_gather_rows(ys, dest.reshape(-1))
    return _combine(picked, gk, shared, x1, mod, gpost)


def _sb_proj_kernel(x_ref, mod_ref, gpre_ref, w_ref, q_ref, k_ref, v_ref):
    mod = mod_ref[0]
    h = _bf(_rms(x_ref[...], gpre_ref[...]) * (1.0 + mod[1:2]) + mod[0:1])
    q_ref[...] = _bf(_nn(h, w_ref[:, :D_MODEL]) * (LOG2_E / 8.0))
    k_ref[...] = _bf(_nn(h, w_ref[:, D_MODEL:2 * D_MODEL]))
    v_ref[...] = _bf(_nn(h, w_ref[:, 2 * D_MODEL:]))


def _sb_proj(x, mod, gpre, w_qkv):
    tm = ROW_TILE
    full = lambda shape: pl.BlockSpec(shape, lambda i: (0,) * len(shape))
    row_spec = pl.BlockSpec((tm, D_MODEL), lambda i: (i, 0))
    out_sds = jax.ShapeDtypeStruct((TOKENS, D_MODEL), BF16)
    return pl.pallas_call(
        _sb_proj_kernel,
        grid=(TOKENS // tm,),
        in_specs=[row_spec,
                  pl.BlockSpec((1, N_ADA, D_MODEL), lambda i: (i // (SEQ // tm), 0, 0)),
                  full((1, D_MODEL)), full((D_MODEL, 3 * D_MODEL))],
        out_specs=[row_spec] * 3,
        out_shape=[out_sds] * 3,
        compiler_params=_params("parallel"),
        name="sb_qkv_proj",
    )(x, mod, gpre.reshape(1, D_MODEL), _bf(w_qkv))


def _sb_attn_kernel(q_ref, k_ref, v_ref, o_ref):
    blk = ATT_BLOCK
    n_heads = LANES // HEAD_DIM
    qb = pl.program_id(2)
    q = q_ref[...]
    lane = lax.broadcasted_iota(jnp.int32, (blk, LANES), 1)
    head0 = lane < HEAD_DIM
    qh = [jnp.where(head0, q, jnp.zeros_like(q)), jnp.where(head0, jnp.zeros_like(q), q)]
    rowi = lax.broadcasted_iota(jnp.int32, (blk, blk), 0)
    coli = lax.broadcasted_iota(jnp.int32, (blk, blk), 1)
    later = (rowi > coli).astype(BF16)
    ones = jnp.ones((blk, LANES), BF16)

    sub = ATT_SUB
    n_sub = blk // sub
    sub_row = lax.broadcasted_iota(jnp.int32, (sub, blk), 0)
    sub_col = lax.broadcasted_iota(jnp.int32, (sub, blk), 1)
    below = [sub_col < sub_row + r * sub for r in range(n_sub)]
    sub_head0 = lax.broadcasted_iota(jnp.int32, (sub, LANES), 1) < HEAD_DIM
    chains = [(hd, r) for hd in range(n_heads) for r in range(n_sub)]

    def chain(hd, r, ks, vs, get_carry, get_acc, mode, out):
        rows = slice(r * sub, (r + 1) * sub)
        u = _nt(qh[hd][rows], ks)
        yield
        neg_abs = lax.bitcast_convert_type(
            lax.bitcast_convert_type(u, jnp.uint32) | jnp.uint32(0x80000000), F32)
        lb = jnp.minimum(u, 0.0) - jnp.log2(1.0 + jnp.exp2(neg_abs))
        l1m = lb - u
        if mode == "diag":
            l1m = jnp.where(below[r], l1m, 0.0)
        l1b = _bf(l1m)
        yield
        carry = get_carry()
        tail = _nn(l1b, later) + jnp.concatenate([carry] * (blk // LANES), axis=1)
        out["carry"] = carry + _nn(l1b, ones)
        yield
        w = jnp.exp2(lb + tail)
        if mode == "diag":
            w = jnp.where(below[r], w, 0.0)
        elif mode == "prev":
            w = jnp.where(qb > 0, w, 0.0)
        wb = _bf(w)
        yield
        out["acc"] = get_acc() + _nn(wb, vs)

    def load_kv(kb):
        start = pl.multiple_of(kb * blk, blk)
        return k_ref[pl.ds(start, blk), :], v_ref[pl.ds(start, blk), :]

    def run(gens):
        while gens:
            gens = [g for g in gens if next(g, True) is None]

    def block(kb, carry, acc):
        ks, vs = load_kv(kb)
        outs = [{} for _ in chains]
        run([chain(hd, r, ks, vs, lambda c=c: carry[c], lambda c=c: acc[c], "plain", outs[c])
             for c, (hd, r) in enumerate(chains)])
        return [o["carry"] for o in outs], [o["acc"] for o in outs]

    zeros = jnp.zeros((sub, LANES), F32)
    ks_d, vs_d = load_kv(qb)
    ks_p, vs_p = load_kv(jnp.maximum(qb - 1, 0))
    outs_d = [{} for _ in chains]
    outs_p = [{} for _ in chains]
    run([chain(hd, r, ks_d, vs_d, lambda: zeros, lambda: zeros, "diag", outs_d[c])
         for c, (hd, r) in enumerate(chains)]
        + [chain(hd, r, ks_p, vs_p, lambda c=c: outs_d[c]["carry"], lambda c=c: outs_d[c]["acc"],
                 "prev", outs_p[c]) for c, (hd, r) in enumerate(chains)])
    carry = [o["carry"] for o in outs_p]
    acc = [o["acc"] for o in outs_p]

    def cmax_of(cr):
        return jnp.max(functools.reduce(jnp.maximum, cr))

    def cond(st):
        kb, _, _, cmax = st
        return jnp.logical_and(kb >= 0, cmax > EXP2_UNDERFLOW)

    def body(st):
        kb, cr, ac, _ = st
        cr, ac = block(kb, list(cr), list(ac))
        return kb - 1, tuple(cr), tuple(ac), cmax_of(cr)

    _, _, acc, _ = lax.while_loop(cond, body, (qb - 2, tuple(carry), tuple(acc), cmax_of(carry)))
    for r in range(n_sub):
        rows = slice(r * sub, (r + 1) * sub)
        o_ref[rows, :] = jnp.where(sub_head0, acc[r], acc[n_sub + r])


def _sb_attn(q, k, v):
    blk = ATT_BLOCK
    n_blk = SEQ // blk
    q_spec = pl.BlockSpec((blk, LANES), lambda b, p, i: (b * n_blk + i, p))
    kv_spec = pl.BlockSpec((SEQ, LANES), lambda b, p, i: (b, p))
    return pl.pallas_call(
        _sb_attn_kernel,
        grid=(BATCH, N_PAIRS, n_blk),
        in_specs=[q_spec, kv_spec, kv_spec],
        out_specs=q_spec,
        out_shape=jax.ShapeDtypeStruct((TOKENS, D_MODEL), F32),
        compiler_params=_params("parallel", "parallel", "arbitrary"),
        name="sb_attention",
    )(q, k, v)


def kernel(x, c, ada_w, ada_b, norm_pre_mix, norm_post_mix, norm_pre_ffn, norm_post_ffn, rwkv_mu, rwkv_w_rkv, rwkv_w_w1, rwkv_w_w2, rwkv_w0, rwkv_w_a1, rwkv_w_a2, rwkv_a0, rwkv_w_g1, rwkv_w_g2, rwkv_k_k, rwkv_k_a, rwkv_r_k, rwkv_lnx_g, rwkv_lnx_b, rwkv_w_o, sb_w_qkv, sb_w_o, moe_w_router, moe_router_bias, moe_w_gate, moe_w_up, moe_w_down, moe_ws_gate, moe_ws_up, moe_ws_down):
    mod_all = _ada_mod(c, ada_w, ada_b)
    xt = x.reshape(TOKENS, D_MODEL)
    for layer in range(DEPTH):
        mod = mod_all[layer]
        i = layer // 2
        if layer % 2 == 0:
            r, k, v, lw, a, g = _rwkv_proj(
                xt, mod, norm_pre_mix[layer], rwkv_mu[i], rwkv_w_rkv[i], rwkv_w_w1[i], rwkv_w_w2[i],
                rwkv_w0[i], rwkv_w_a1[i], rwkv_w_a2[i], rwkv_a0[i], rwkv_w_g1[i], rwkv_w_g2[i])
            o = _wkv(r, k, v, lw, a, rwkv_k_k[i], rwkv_k_a[i], rwkv_r_k[i], rwkv_lnx_g[i], rwkv_lnx_b[i])
            w_o = rwkv_w_o[i]
        else:
            q, k, v = _sb_proj(xt, mod, norm_pre_mix[layer], sb_w_qkv[i])
            o, g = _sb_attn(q, k, v), None
            w_o = sb_w_o[i]
        x1, h2p, gk, ek, rk, counts = _post(o, g, xt, mod, w_o, norm_post_mix[layer], norm_pre_ffn[layer],
                                            moe_w_router[layer], moe_router_bias[layer])
        xt = _moe(layer, x1, h2p, gk, ek, rk, counts, mod, norm_post_ffn[layer], moe_w_gate, moe_w_up,
                  moe_w_down, moe_ws_gate, moe_ws_up, moe_ws_down)
    return xt.reshape(BATCH, SEQ, D_MODEL)
```

```python
import functools

import jax
import jax.numpy as jnp
from jax import lax
from jax.experimental import pallas as pl
from jax.experimental.pallas import tpu as pltpu
from jax.experimental.pallas import tpu_sc as plsc

D_MODEL = 1024
BATCH = 2
SEQ = 8192
TOKENS = BATCH * SEQ
DEPTH = 2
HEAD_DIM = 64
N_EXPERTS = 64
TOP_K = 8
D_EXPERT = 256
ROUTED_SCALE = 2.5
RMS_EPS = 1e-6
LNX_EPS = 64e-5
N_ADA = 6

LANES = 128
SUBLANES = 8
N_PAIRS = D_MODEL // LANES
WKV_CHUNK = 64
WKV_PAIRS = 8
ROW_TILE = 512
SHARED_TILE = 512
EXPERT_BLOCK = 1024
EXPERT_SPLIT = 2
PACK_HALVES = 2
PACK_WORDS = D_MODEL // (2 * PACK_HALVES)
SC_WINDOW = 128
COMBINE_PARTS = 2
ATT_BLOCK = 256
ATT_SUB = 128
LOG2_E = 1.4426950408889634
EXP2_UNDERFLOW = -153.0
VMEM_LIMIT = 56 * 1024 * 1024

F32 = jnp.float32
BF16 = jnp.bfloat16


def _nn(a, b):
    return jnp.dot(a, b, preferred_element_type=F32)


def _nt(a, b):
    return lax.dot_general(a, b, (((1,), (1,)), ((), ())), preferred_element_type=F32)


def _tn(a, b):
    return lax.dot_general(a, b, (((0,), (0,)), ((), ())), preferred_element_type=F32)


def _bf(x):
    return x.astype(BF16)


def _sigmoid(x):
    return 1.0 / (1.0 + jnp.exp(-x))


def _rms(xv, g):
    ms = jnp.mean(xv * xv, axis=-1, keepdims=True)
    return xv * lax.rsqrt(ms + RMS_EPS) * g


def _split3(x):
    hi = x.astype(BF16)
    r1 = x - hi.astype(F32)
    mid = r1.astype(BF16)
    lo = (r1 - mid.astype(F32)).astype(BF16)
    return hi, mid, lo


def _params(*sem):
    return pltpu.CompilerParams(dimension_semantics=sem, vmem_limit_bytes=VMEM_LIMIT)


def _ada_kernel(c_ref, w_ref, b_ref, o_ref):
    cv = c_ref[...]
    o_ref[0] = _nn(cv * _sigmoid(cv), w_ref[0]) + b_ref[0]


def _ada_mod(c, ada_w, ada_b):
    tn = 1536
    c8 = jnp.pad(c, ((0, SUBLANES - BATCH), (0, 0)))
    out = pl.pallas_call(
        _ada_kernel,
        grid=(DEPTH, N_ADA * D_MODEL // tn),
        in_specs=[
            pl.BlockSpec((SUBLANES, D_MODEL), lambda l, j: (0, 0)),
            pl.BlockSpec((1, D_MODEL, tn), lambda l, j: (l, 0, j)),
            pl.BlockSpec((1, 1, tn), lambda l, j: (l, 0, j)),
        ],
        out_specs=pl.BlockSpec((1, SUBLANES, tn), lambda l, j: (l, 0, j)),
        out_shape=jax.ShapeDtypeStruct((DEPTH, SUBLANES, N_ADA * D_MODEL), F32),
        compiler_params=_params("parallel", "parallel"),
        name="ada_mod",
    )(c8, ada_w, ada_b.reshape(DEPTH, 1, N_ADA * D_MODEL))
    return out[:, :BATCH].reshape(DEPTH, BATCH, N_ADA, D_MODEL)


def _rwkv_proj_kernel(x_ref, xp_ref, mod_ref, gpre_ref, mu_ref, wrkv_ref, w1_ref, w2_ref, w0_ref,
                      a1_ref, a2_ref, a0_ref, g1_ref, g2_ref,
                      r_ref, k_ref, v_ref, lw_ref, a_ref, g_ref):
    i = pl.program_id(0)
    mod = mod_ref[0]
    shift, scale = mod[0:1], mod[1:2]
    gpre = gpre_ref[...]

    def modnorm(xv):
        return _rms(xv, gpre) * (1.0 + scale) + shift

    h = modnorm(x_ref[...])
    hp = modnorm(xp_ref[...])[SUBLANES - 1:SUBLANES]
    hp = jnp.where(i % (SEQ // ROW_TILE) == 0, 0.0, hp)
    row = lax.broadcasted_iota(jnp.int32, h.shape, 0)
    hs = jnp.where(row == 0, hp, pltpu.roll(h, 1, 0))
    xx = hs - h
    mu = mu_ref[...]

    def mix(n):
        return _bf(h + xx * mu[n:n + 1])

    r_ref[...] = _bf(_nn(mix(0), wrkv_ref[0]))
    k_ref[...] = _bf(_nn(mix(1), wrkv_ref[1]))
    v_ref[...] = _bf(_nn(mix(2), wrkv_ref[2]))
    wl = w0_ref[...] + _nn(_bf(jnp.tanh(_nn(mix(3), w1_ref[...]))), w2_ref[...])
    nwl = -wl
    sp = jnp.maximum(nwl, 0.0) + jnp.log(1.0 + jnp.exp(-jnp.abs(nwl)))
    lw_ref[...] = -jnp.exp(-sp - 0.5)
    a_ref[...] = _sigmoid(a0_ref[...] + _nn(_bf(_nn(mix(4), a1_ref[...])), a2_ref[...]))
    g_ref[...] = _bf(_nn(_bf(_sigmoid(_nn(mix(5), g1_ref[...]))), g2_ref[...]))


def _rwkv_proj(x, mod, gpre, mu, w_rkv, w1, w2, w0, a1, a2, a0, g1, g2):
    tm = ROW_TILE
    full = lambda shape: pl.BlockSpec(shape, lambda i: (0,) * len(shape))
    row_spec = pl.BlockSpec((tm, D_MODEL), lambda i: (i, 0))
    out_sds = [jax.ShapeDtypeStruct((TOKENS, D_MODEL), dt) for dt in (BF16, BF16, BF16, F32, F32, BF16)]
    return pl.pallas_call(
        _rwkv_proj_kernel,
        grid=(TOKENS // tm,),
        in_specs=[
            row_spec,
            pl.BlockSpec((SUBLANES, D_MODEL), lambda i: (jnp.maximum(i * (tm // SUBLANES) - 1, 0), 0)),
            pl.BlockSpec((1, N_ADA, D_MODEL), lambda i: (i // (SEQ // tm), 0, 0)),
            full((1, D_MODEL)), full((6, D_MODEL)), full((3, D_MODEL, D_MODEL)),
            full(w1.shape), full(w2.shape), full((1, D_MODEL)),
            full(a1.shape), full(a2.shape), full((1, D_MODEL)),
            full(g1.shape), full(g2.shape),
        ],
        out_specs=[row_spec] * 6,
        out_shape=out_sds,
        compiler_params=_params("parallel"),
        name="rwkv_proj",
    )(x, x, mod, gpre.reshape(1, D_MODEL), mu, _bf(w_rkv), _bf(w1), _bf(w2),
      w0.reshape(1, D_MODEL), _bf(a1), _bf(a2), a0.reshape(1, D_MODEL), _bf(g1), _bf(g2))


def _wkv_pair(r, k, v, lw, a, k_k, k_a, r_k, ln_g, ln_b, s_ref, o_ref):
    c = WKV_CHUNK
    lane = lax.broadcasted_iota(jnp.int32, (c, LANES), 1)
    head0 = lane < HEAD_DIM
    rowi = lax.broadcasted_iota(jnp.int32, (LANES, LANES), 0)
    coli = lax.broadcasted_iota(jnp.int32, (LANES, LANES), 1)
    rt, ct = rowi & (c - 1), coli & (c - 1)
    strict, incl, eye = rt > ct, rt >= ct, rowi == coli
    tril = (lax.broadcasted_iota(jnp.int32, (c, c), 0)
            >= lax.broadcasted_iota(jnp.int32, (c, c), 1)).astype(BF16)

    def headsum(x):
        s0 = jnp.sum(jnp.where(head0, x, 0.0), axis=-1, keepdims=True)
        s1 = jnp.sum(jnp.where(head0, 0.0, x), axis=-1, keepdims=True)
        return jnp.where(head0, s0, s1)

    def stack(x):
        return jnp.concatenate([jnp.where(head0, x, 0.0), jnp.where(head0, 0.0, x)], axis=0)

    kk = k * k_k
    kk = kk / jnp.maximum(jnp.sqrt(headsum(kk * kk)), 1e-12)
    kf = k * (1.0 + (a - 1.0) * k_a)
    bvec = kk * a
    yield

    cum = _nn(tril, jnp.concatenate(_split3(lw), axis=1))
    cum = cum[:, :LANES] + cum[:, LANES:2 * LANES] + cum[:, 2 * LANES:]
    cum_end = cum[c - 1:c]
    w_inv = jnp.exp(-cum)
    w_rem = jnp.exp(cum_end - cum)
    a_st = stack(-kk * jnp.exp(cum - lw))
    r_st = stack(r * jnp.exp(cum))
    b_st = stack(bvec * w_inv)
    k_st = stack(kf * w_inv)
    bp_st = _bf(stack(bvec * w_rem))
    kp_st = _bf(stack(kf * w_rem))
    v_st = _bf(stack(v))
    yield

    sc = _nt(_bf(jnp.concatenate([a_st, r_st], axis=0)), _bf(jnp.concatenate([b_st, k_st], axis=0)))
    l_ab = jnp.where(strict, sc[:LANES, :LANES], 0.0)
    l_ak = jnp.where(strict, sc[:LANES, LANES:], 0.0)
    l_rb = jnp.where(incl, sc[LANES:, :LANES], 0.0)
    l_rk = jnp.where(incl, sc[LANES:, LANES:], 0.0)
    yield

    tinv = jnp.where(eye, 1.0, l_ab)
    pw = l_ab
    for _ in range(c.bit_length() - 2):
        pwb = _bf(pw)
        pw = _nn(pwb, pwb)
        tinv = tinv + _nn(_bf(tinv), _bf(pw))
        yield

    au = _nn(_bf(tinv), _bf(jnp.concatenate([a_st, _nn(_bf(l_ak), v_st)], axis=1)))
    aub = _bf(au)
    yield
    rhs = jnp.concatenate([aub, jnp.concatenate([jnp.zeros_like(v_st), v_st], axis=1)], axis=0)
    rb = _nn(_bf(jnp.concatenate([l_rb, l_rk], axis=1)), rhs)
    gh = _tn(jnp.concatenate([bp_st, kp_st], axis=0), rhs)
    r_hat = r_st + rb[:, :LANES]
    o0 = rb[:, LANES:]
    g_mat = jnp.where(eye, jnp.exp(cum_end), 0.0) + gh[:, :LANES]
    h_mat = gh[:, LANES:]
    yield

    os = _nn(_bf(jnp.concatenate([r_hat, g_mat], axis=0)), _bf(s_ref[...]))
    s_ref[...] = os[LANES:] + h_mat
    o_st = os[:LANES] + o0
    o = o_st[:c] + o_st[c:]
    yield

    mean = headsum(o) * (1.0 / HEAD_DIM)
    oc = o - mean
    var = headsum(oc * oc) * (1.0 / HEAD_DIM)
    on = oc * lax.rsqrt(var + LNX_EPS) * ln_g + ln_b
    o_ref[...] = on + headsum(r * kf * r_k) * v


def _wkv_kernel(r_ref, k_ref, v_ref, lw_ref, a_ref, kk_ref, ka_ref, rk_ref, lg_ref, lb_ref,
                o_ref, s_ref):
    @pl.when(pl.program_id(2) == 0)
    def _():
        s_ref[...] = jnp.zeros_like(s_ref)

    def pair(p):
        sl = pl.ds(p * LANES, LANES)
        ins = [ref[:, sl].astype(F32)
               for ref in (r_ref, k_ref, v_ref, lw_ref, a_ref, kk_ref, ka_ref, rk_ref, lg_ref, lb_ref)]
        return _wkv_pair(*ins, s_ref.at[p], o_ref.at[:, sl])

    stages = [pair(p) for p in range(WKV_PAIRS)]
    while stages:
        stages = [g for g in stages if next(g, True) is None]


def _wkv(r, k, v, lw, a, k_k, k_a, r_k, lnx_g, lnx_b):
    n_chunks = SEQ // WKV_CHUNK
    width = WKV_PAIRS * LANES
    tok = pl.BlockSpec((WKV_CHUNK, width), lambda b, p, c: (b * n_chunks + c, p))
    par = pl.BlockSpec((1, width), lambda b, p, c: (0, p))
    vec = lambda t: t.reshape(1, D_MODEL)
    return pl.pallas_call(
        _wkv_kernel,
        grid=(BATCH, N_PAIRS // WKV_PAIRS, n_chunks),
        in_specs=[tok] * 5 + [par] * 5,
        out_specs=tok,
        out_shape=jax.ShapeDtypeStruct((TOKENS, D_MODEL), F32),
        scratch_shapes=[pltpu.VMEM((WKV_PAIRS, LANES, LANES), F32)],
        compiler_params=_params("parallel", "parallel", "arbitrary"),
        name="wkv_scan",
    )(r, k, v, lw, a, vec(k_k), vec(k_a), vec(r_k), vec(lnx_g), vec(lnx_b))


def _pack_rows(x):
    bits = lax.bitcast_convert_type(_bf(x).astype(F32), jnp.uint32)
    halves = []
    for h in range(PACK_HALVES):
        lo = bits[:, (2 * h) * PACK_WORDS:(2 * h + 1) * PACK_WORDS]
        hi = bits[:, (2 * h + 1) * PACK_WORDS:(2 * h + 2) * PACK_WORDS]
        halves.append(lax.bitcast_convert_type((lo >> 16) | (hi & jnp.uint32(0xFFFF0000)), jnp.int32))
    return halves


def _unpack_rows(halves):
    parts = []
    for w in halves:
        u = lax.bitcast_convert_type(w, jnp.uint32)
        parts.append(lax.bitcast_convert_type(u << 16, F32))
        parts.append(lax.bitcast_convert_type(u & jnp.uint32(0xFFFF0000), F32))
    return jnp.concatenate(parts, axis=1)


def _post_kernel(*refs, has_gate):
    if has_gate:
        o_ref, g_ref, *refs = refs
    else:
        o_ref, *refs = refs
    (x_ref, mod_ref, wo_ref, gpost_ref, gpre_ref, wr_hi_ref, wr_lo_ref, bias_ref,
     x1_ref, h2p_ref, gk_ref, ek_ref, rk_ref, cnt_ref, run_ref) = refs
    tm = ROW_TILE

    @pl.when(pl.program_id(0) == 0)
    def _():
        run_ref[...] = jnp.zeros_like(run_ref)

    mod = mod_ref[0]
    gate_m, shift_f, scale_f = mod[2:3], mod[3:4], mod[4:5]
    o = o_ref[...]
    if has_gate:
        o = o * g_ref[...].astype(F32)
    y = _nn(_bf(o), wo_ref[...])
    x1 = x_ref[...] + gate_m * _rms(y, gpost_ref[...])
    x1_ref[...] = x1
    h2 = _rms(x1, gpre_ref[...]) * (1.0 + scale_f) + shift_f
    for h, words in enumerate(_pack_rows(h2)):
        h2p_ref[h] = words

    hi = _bf(h2)
    lo = _bf(h2 - hi.astype(F32))
    wr_hi, wr_lo = wr_hi_ref[...], wr_lo_ref[...]
    scores = _sigmoid(_nt(wr_hi, hi) + (_nt(wr_hi, lo) + _nt(wr_lo, hi)))
    erow = lax.broadcasted_iota(jnp.int32, scores.shape, 0)
    work = scores + jnp.concatenate([bias_ref[...]] * (tm // LANES), axis=1)
    picked = jnp.zeros_like(scores)
    chosen = jnp.zeros_like(scores)
    hits, ids = [], []
    for _ in range(TOP_K):
        best = jnp.max(work, axis=0, keepdims=True)
        idx = jnp.min(jnp.where(work == best, erow, N_EXPERTS), axis=0, keepdims=True)
        hit = erow == idx
        picked = jnp.where(hit, scores, picked)
        chosen = jnp.where(hit, 1.0, chosen)
        work = jnp.where(hit, -jnp.inf, work)
        hits.append(hit)
        ids.append(idx)
    gates = ROUTED_SCALE * picked / jnp.sum(picked, axis=0, keepdims=True)

    upto = (lax.broadcasted_iota(jnp.int32, (tm, tm), 0)
            <= lax.broadcasted_iota(jnp.int32, (tm, tm), 1)).astype(BF16)
    chosen_b = _bf(chosen)
    run = run_ref[...]
    before = _nn(chosen_b, upto) - chosen + jnp.concatenate([run] * (tm // LANES), axis=1)
    run = run + _nn(chosen_b, jnp.ones((tm, LANES), BF16))
    run_ref[...] = run
    cnt_ref[...] = run

    krow = lax.broadcasted_iota(jnp.int32, (TOP_K, tm), 0)
    gk = jnp.zeros((TOP_K, tm), F32)
    ek = jnp.zeros((TOP_K, tm), jnp.int32)
    rk = jnp.zeros((TOP_K, tm), jnp.int32)
    for j in range(TOP_K):
        grow = jnp.sum(jnp.where(hits[j], gates, 0.0), axis=0, keepdims=True)
        rrow = jnp.sum(jnp.where(hits[j], before, 0.0), axis=0, keepdims=True)
        gk = jnp.where(krow == j, grow, gk)
        ek = jnp.where(krow == j, ids[j], ek)
        rk = jnp.where(krow == j, rrow.astype(jnp.int32), rk)
    gk_ref[...] = gk
    ek_ref[...] = ek
    rk_ref[...] = rk


def _post(o, g, x, mod, w_o, gpost, gpre, w_router, router_bias):
    tm = ROW_TILE
    full = lambda shape: pl.BlockSpec(shape, lambda i: (0,) * len(shape))
    row_spec = pl.BlockSpec((tm, D_MODEL), lambda i: (i, 0))
    k_spec = pl.BlockSpec((TOP_K, tm), lambda i: (0, i))
    wr = w_router.T
    wr_hi = _bf(wr)
    wr_lo = _bf(wr - wr_hi.astype(F32))
    bias = jnp.broadcast_to(router_bias[:, None], (N_EXPERTS, LANES))
    has_gate = g is not None
    acts = [o, g] if has_gate else [o]
    return pl.pallas_call(
        functools.partial(_post_kernel, has_gate=has_gate),
        grid=(TOKENS // tm,),
        in_specs=[row_spec] * (len(acts) + 1) + [
            pl.BlockSpec((1, N_ADA, D_MODEL), lambda i: (i // (SEQ // tm), 0, 0)),
            full((D_MODEL, D_MODEL)), full((1, D_MODEL)), full((1, D_MODEL)),
            full((N_EXPERTS, D_MODEL)), full((N_EXPERTS, D_MODEL)), full((N_EXPERTS, LANES)),
        ],
        out_specs=[row_spec, pl.BlockSpec((PACK_HALVES, tm, PACK_WORDS), lambda i: (0, i, 0)),
                   k_spec, k_spec, k_spec, full((N_EXPERTS, LANES))],
        out_shape=[jax.ShapeDtypeStruct((TOKENS, D_MODEL), F32),
                   jax.ShapeDtypeStruct((PACK_HALVES, TOKENS, PACK_WORDS), jnp.int32),
                   jax.ShapeDtypeStruct((TOP_K, TOKENS), F32),
                   jax.ShapeDtypeStruct((TOP_K, TOKENS), jnp.int32),
                   jax.ShapeDtypeStruct((TOP_K, TOKENS), jnp.int32),
                   jax.ShapeDtypeStruct((N_EXPERTS, LANES), F32)],
        scratch_shapes=[pltpu.VMEM((N_EXPERTS, LANES), F32)],
        compiler_params=_params("arbitrary"),
        name="mixer_out_router",
    )(*acts, x, mod, _bf(w_o), gpost.reshape(1, D_MODEL), gpre.reshape(1, D_MODEL),
      wr_hi, wr_lo, bias)


def _sc_mesh():
    return plsc.VectorSubcoreMesh(core_axis_name="core", subcore_axis_name="subcore")


def _sc_scatter_rows(rows, idx, n_out):
    n_copies = idx.shape[0]

    @pl.kernel(out_type=jax.ShapeDtypeStruct((PACK_HALVES, n_out, PACK_WORDS), rows.dtype),
               mesh=_sc_mesh(), scratch_types=[], name="sc_dispatch_rows")
    def scatter(x_hbm, i_hbm, o_hbm):
        for h in range(PACK_HALVES):
            def body(x_vmem, i_vmem, h=h):
                for k in range(n_copies):
                    pltpu.sync_copy(x_vmem, o_hbm.at[h].at[i_vmem.at[k]])

            pltpu.emit_pipeline(
                body, grid=(rows.shape[1] // SC_WINDOW,),
                in_specs=[pl.BlockSpec((SC_WINDOW, PACK_WORDS), index_map=lambda i: (i, 0)),
                          pl.BlockSpec((n_copies, SC_WINDOW), index_map=lambda i: (0, i))],
                out_specs=[],
                core_axis_name=("core", "subcore"),
                dimension_semantics=(pltpu.PARALLEL,),
            )(x_hbm.at[h], i_hbm)

    return scatter(rows, idx)


def _sc_gather_rows(tables, idx):
    n = idx.shape[0]

    @pl.kernel(out_type=jax.ShapeDtypeStruct((PACK_HALVES, n, PACK_WORDS), tables[0].dtype),
               mesh=_sc_mesh(), scratch_types=[], name="sc_collect_rows")
    def gather(*refs):
        x_hbm, (i_hbm, o_hbm) = refs[:PACK_HALVES], refs[PACK_HALVES:]
        for h in range(PACK_HALVES):
            def body(i_vmem, o_vmem, h=h):
                pltpu.sync_copy(x_hbm[h].at[i_vmem.at[0]], o_vmem)

            pltpu.emit_pipeline(
                body, grid=(n // SC_WINDOW,),
                in_specs=[pl.BlockSpec((1, SC_WINDOW), index_map=lambda i: (0, i))],
                out_specs=[pl.BlockSpec((SC_WINDOW, PACK_WORDS), index_map=lambda i: (i, 0))],
                core_axis_name=("core", "subcore"),
                dimension_semantics=(pltpu.PARALLEL,),
            )(i_hbm, o_hbm.at[h])

    return gather(*tables, idx.reshape(1, n))


def _ffn(x, wg, wu, wd):
    up = _nn(x, _bf(wu))
    gt = _nn(x, _bf(wg))
    return _nn(_bf(gt * _sigmoid(gt) * up), _bf(wd))


def _expert_kernel(be_ref, nu_ref, *refs):
    n_in = PACK_HALVES * EXPERT_SPLIT
    x_refs = refs[:n_in]
    wg_ref, wu_ref, wd_ref = refs[n_in:n_in + 3]
    y_refs = refs[n_in + 3:n_in + 3 + PACK_HALVES]
    wgb_ref, wub_ref, wdb_ref = refs[n_in + 3 + PACK_HALVES:]
    b = pl.program_id(0)

    @pl.when(jnp.logical_or(b == 0, be_ref[b] != be_ref[jnp.maximum(b - 1, 0)]))
    def _():
        wgb_ref[...] = _bf(wg_ref[0, 0])
        wub_ref[...] = _bf(wu_ref[0, 0])
        wdb_ref[...] = _bf(wd_ref[0, 0])

    @pl.when(b < nu_ref[0])
    def _():
        halves = [jnp.concatenate([x_refs[h * EXPERT_SPLIT + j][0] for j in range(EXPERT_SPLIT)], axis=0)
                  for h in range(PACK_HALVES)]
        y = _ffn(_bf(_unpack_rows(halves)), wgb_ref[...], wub_ref[...], wdb_ref[...])
        for h, words in enumerate(_pack_rows(y)):
            y_refs[h][...] = words


def _experts(layer, xs, blk_expert, n_used, w_gate, w_up, w_down):
    n_rows = xs.shape[1]
    n_blocks = n_rows // EXPERT_BLOCK
    part = EXPERT_BLOCK // EXPERT_SPLIT
    x_spec = lambda h, j: pl.BlockSpec((1, part, PACK_WORDS), lambda b, be, nu: (h, b * EXPERT_SPLIT + j, 0))
    w_spec = lambda shape: pl.BlockSpec((1, 1) + shape, lambda b, be, nu: (layer, be[b], 0, 0))
    y_spec = pl.BlockSpec((EXPERT_BLOCK, PACK_WORDS), lambda b, be, nu: (b, 0))
    n_in = PACK_HALVES * EXPERT_SPLIT
    return pl.pallas_call(
        _expert_kernel,
        grid_spec=pltpu.PrefetchScalarGridSpec(
            num_scalar_prefetch=2,
            grid=(n_blocks,),
            in_specs=[x_spec(h, j) for h in range(PACK_HALVES) for j in range(EXPERT_SPLIT)]
                     + [w_spec((D_MODEL, D_EXPERT)), w_spec((D_MODEL, D_EXPERT)), w_spec((D_EXPERT, D_MODEL))],
            out_specs=[y_spec] * PACK_HALVES,
            scratch_shapes=[pltpu.VMEM((D_MODEL, D_EXPERT), BF16), pltpu.VMEM((D_MODEL, D_EXPERT), BF16),
                            pltpu.VMEM((D_EXPERT, D_MODEL), BF16)],
        ),
        out_shape=[jax.ShapeDtypeStruct((n_rows, PACK_WORDS), xs.dtype)] * PACK_HALVES,
        compiler_params=_params("arbitrary"),
        name="moe_experts",
    )(blk_expert, n_used, *([xs] * n_in), w_gate, w_up, w_down)


def _shared_kernel(h2p_ref, sg_ref, su_ref, sd_ref, o_ref):
    h2 = _bf(_unpack_rows([h2p_ref[h] for h in range(PACK_HALVES)]))
    for h, words in enumerate(_pack_rows(_ffn(h2, sg_ref[0], su_ref[0], sd_ref[0]))):
        o_ref[h] = words


def _shared_expert(layer, h2p, ws_gate, ws_up, ws_down):
    tm = SHARED_TILE
    row_spec = pl.BlockSpec((PACK_HALVES, tm, PACK_WORDS), lambda i: (0, i, 0))
    return pl.pallas_call(
        _shared_kernel,
        grid=(TOKENS // tm,),
        in_specs=[row_spec,
                  pl.BlockSpec((1, D_MODEL, D_EXPERT), lambda i: (layer, 0, 0)),
                  pl.BlockSpec((1, D_MODEL, D_EXPERT), lambda i: (layer, 0, 0)),
                  pl.BlockSpec((1, D_EXPERT, D_MODEL), lambda i: (layer, 0, 0))],
        out_specs=row_spec,
        out_shape=jax.ShapeDtypeStruct(h2p.shape, h2p.dtype),
        compiler_params=_params("parallel"),
        name="moe_shared_expert",
    )(h2p, ws_gate, ws_up, ws_down)


def _combine_kernel(*refs, has_out):
    g_refs = refs[:TOP_K]
    gk_ref, sh_ref, x_ref, mod_ref, gpost_ref = refs[TOP_K:TOP_K + 5]
    o_ref = refs[-1]
    acc = _unpack_rows([sh_ref[h] for h in range(PACK_HALVES)])
    gk = gk_ref[...].T
    for j in range(TOP_K):
        acc = acc + gk[:, j:j + 1] * _unpack_rows([g_refs[j][h] for h in range(PACK_HALVES)])
    gate_f = mod_ref[0][5:6]
    o_ref[...] = x_ref[...] + gate_f * _rms(acc, gpost_ref[...])


def _combine(picked, part, gk, shared, x1, mod, gpost, out):
    tm = ROW_TILE
    n_tiles = TOKENS // tm // COMBINE_PARTS
    first = part * n_tiles
    row_spec = pl.BlockSpec((tm, D_MODEL), lambda i: (first + i, 0))
    pick_spec = lambda j: pl.BlockSpec((PACK_HALVES, tm, PACK_WORDS), lambda i: (0, j * n_tiles + i, 0))
    operands = [*([picked] * TOP_K), gk, shared, x1, mod, gpost.reshape(1, D_MODEL)]
    in_specs = [pick_spec(j) for j in range(TOP_K)] + [
        pl.BlockSpec((TOP_K, tm), lambda i: (0, first + i)),
        pl.BlockSpec((PACK_HALVES, tm, PACK_WORDS), lambda i: (0, first + i, 0)),
        row_spec,
        pl.BlockSpec((1, N_ADA, D_MODEL), lambda i: ((first + i) // (SEQ // tm), 0, 0)),
        pl.BlockSpec((1, D_MODEL), lambda i: (0, 0)),
    ]
    aliases = {}
    if out is not None:
        aliases = {len(operands): 0}
        operands.append(out)
        in_specs.append(pl.BlockSpec(memory_space=pl.ANY))
    return pl.pallas_call(
        functools.partial(_combine_kernel, has_out=out is not None),
        grid=(n_tiles,),
        in_specs=in_specs,
        out_specs=row_spec,
        out_shape=jax.ShapeDtypeStruct((TOKENS, D_MODEL), F32),
        input_output_aliases=aliases,
        compiler_params=_params("parallel"),
        name="moe_combine",
    )(*operands)


def _moe(layer, x1, h2p, gk, ek, rk, counts, mod, gpost, w_gate, w_up, w_down, ws_gate, ws_up, ws_down):
    cnt = counts[:, 0].astype(jnp.int32)
    padded = (cnt + EXPERT_BLOCK - 1) // EXPERT_BLOCK * EXPERT_BLOCK
    eid = jnp.arange(N_EXPERTS, dtype=jnp.int32)
    pend = jnp.sum(jnp.where(eid[:, None] <= eid[None, :], padded[:, None], 0), axis=0)
    pstart = pend - padded
    dest = rk + jnp.sum(jnp.where(ek[None] == eid[:, None, None], pstart[:, None, None], 0), axis=0)
    n_blocks = (TOKENS * TOP_K + N_EXPERTS * (EXPERT_BLOCK - 1)) // EXPERT_BLOCK + 1
    blk_start = jnp.arange(n_blocks, dtype=jnp.int32) * EXPERT_BLOCK
    blk_expert = jnp.minimum(jnp.sum((blk_start[None, :] >= pend[:, None]).astype(jnp.int32), axis=0),
                             N_EXPERTS - 1)
    n_used = pend[-1:] // EXPERT_BLOCK

    xs = _sc_scatter_rows(h2p, dest, n_blocks * EXPERT_BLOCK)
    shared = _shared_expert(layer, h2p, ws_gate, ws_up, ws_down)
    ys = _experts(layer, xs, blk_expert, n_used, w_gate, w_up, w_down)
    out = None
    part_len = TOKENS // COMBINE_PARTS
    for part in range(COMBINE_PARTS):
        idx = dest[:, part * part_len:(part + 1) * part_len].reshape(-1)
        out = _combine(_sc_gather_rows(ys, idx), part, gk, shared, x1, mod, gpost, out)
    return out


def _sb_proj_kernel(x_ref, mod_ref, gpre_ref, w_ref, q_ref, k_ref, v_ref):
    mod = mod_ref[0]
    h = _bf(_rms(x_ref[...], gpre_ref[...]) * (1.0 + mod[1:2]) + mod[0:1])
    q_ref[...] = _bf(_nn(h, w_ref[:, :D_MODEL]) * (LOG2_E / 8.0))
    k_ref[...] = _bf(_nn(h, w_ref[:, D_MODEL:2 * D_MODEL]))
    v_ref[...] = _bf(_nn(h, w_ref[:, 2 * D_MODEL:]))


def _sb_proj(x, mod, gpre, w_qkv):
    tm = ROW_TILE
    full = lambda shape: pl.BlockSpec(shape, lambda i: (0,) * len(shape))
    row_spec = pl.BlockSpec((tm, D_MODEL), lambda i: (i, 0))
    out_sds = jax.ShapeDtypeStruct((TOKENS, D_MODEL), BF16)
    return pl.pallas_call(
        _sb_proj_kernel,
        grid=(TOKENS // tm,),
        in_specs=[row_spec,
                  pl.BlockSpec((1, N_ADA, D_MODEL), lambda i: (i // (SEQ // tm), 0, 0)),
                  full((1, D_MODEL)), full((D_MODEL, 3 * D_MODEL))],
        out_specs=[row_spec] * 3,
        out_shape=[out_sds] * 3,
        compiler_params=_params("parallel"),
        name="sb_qkv_proj",
    )(x, mod, gpre.reshape(1, D_MODEL), _bf(w_qkv))


def _sb_attn_kernel(q_ref, k_ref, v_ref, o_ref):
    blk = ATT_BLOCK
    n_heads = LANES // HEAD_DIM
    qb = pl.program_id(2)
    q = q_ref[...]
    lane = lax.broadcasted_iota(jnp.int32, (blk, LANES), 1)
    head0 = lane < HEAD_DIM
    qh = [jnp.where(head0, q, jnp.zeros_like(q)), jnp.where(head0, jnp.zeros_like(q), q)]
    rowi = lax.broadcasted_iota(jnp.int32, (blk, blk), 0)
    coli = lax.broadcasted_iota(jnp.int32, (blk, blk), 1)
    later = (rowi > coli).astype(BF16)
    ones = jnp.ones((blk, LANES), BF16)

    sub = ATT_SUB
    n_sub = blk // sub
    sub_row = lax.broadcasted_iota(jnp.int32, (sub, blk), 0)
    sub_col = lax.broadcasted_iota(jnp.int32, (sub, blk), 1)
    below = [sub_col < sub_row + r * sub for r in range(n_sub)]
    sub_head0 = lax.broadcasted_iota(jnp.int32, (sub, LANES), 1) < HEAD_DIM
    chains = [(hd, r) for hd in range(n_heads) for r in range(n_sub)]

    def chain(hd, r, ks, vs, get_carry, get_acc, mode, out):
        rows = slice(r * sub, (r + 1) * sub)
        u = _nt(qh[hd][rows], ks)
        yield
        neg_abs = lax.bitcast_convert_type(
            lax.bitcast_convert_type(u, jnp.uint32) | jnp.uint32(0x80000000), F32)
        lb = jnp.minimum(u, 0.0) - jnp.log2(1.0 + jnp.exp2(neg_abs))
        l1m = lb - u
        if mode == "diag":
            l1m = jnp.where(below[r], l1m, 0.0)
        l1b = _bf(l1m)
        yield
        carry = get_carry()
        tail = _nn(l1b, later) + jnp.concatenate([carry] * (blk // LANES), axis=1)
        out["carry"] = carry + _nn(l1b, ones)
        yield
        w = jnp.exp2(lb + tail)
        if mode == "diag":
            w = jnp.where(below[r], w, 0.0)
        elif mode == "prev":
            w = jnp.where(qb > 0, w, 0.0)
        wb = _bf(w)
        yield
        out["acc"] = get_acc() + _nn(wb, vs)

    def load_kv(kb):
        start = pl.multiple_of(kb * blk, blk)
        return k_ref[pl.ds(start, blk), :], v_ref[pl.ds(start, blk), :]

    def run(gens):
        while gens:
            gens = [g for g in gens if next(g, True) is None]

    def block(kb, carry, acc):
        ks, vs = load_kv(kb)
        outs = [{} for _ in chains]
        run([chain(hd, r, ks, vs, lambda c=c: carry[c], lambda c=c: acc[c], "plain", outs[c])
             for c, (hd, r) in enumerate(chains)])
        return [o["carry"] for o in outs], [o["acc"] for o in outs]

    zeros = jnp.zeros((sub, LANES), F32)
    ks_d, vs_d = load_kv(qb)
    ks_p, vs_p = load_kv(jnp.maximum(qb - 1, 0))
    outs_d = [{} for _ in chains]
    outs_p = [{} for _ in chains]
    run([chain(hd, r, ks_d, vs_d, lambda: zeros, lambda: zeros, "diag", outs_d[c])
         for c, (hd, r) in enumerate(chains)]
        + [chain(hd, r, ks_p, vs_p, lambda c=c: outs_d[c]["carry"], lambda c=c: outs_d[c]["acc"],
                 "prev", outs_p[c]) for c, (hd, r) in enumerate(chains)])
    carry = [o["carry"] for o in outs_p]
    acc = [o["acc"] for o in outs_p]

    def cmax_of(cr):
        return jnp.max(functools.reduce(jnp.maximum, cr))

    def cond(st):
        kb, _, _, cmax = st
        return jnp.logical_and(kb >= 0, cmax > EXP2_UNDERFLOW)

    def body(st):
        kb, cr, ac, _ = st
        cr, ac = block(kb, list(cr), list(ac))
        return kb - 1, tuple(cr), tuple(ac), cmax_of(cr)

    _, _, acc, _ = lax.while_loop(cond, body, (qb - 2, tuple(carry), tuple(acc), cmax_of(carry)))
    for r in range(n_sub):
        rows = slice(r * sub, (r + 1) * sub)
        o_ref[rows, :] = jnp.where(sub_head0, acc[r], acc[n_sub + r])


def _sb_attn(q, k, v):
    blk = ATT_BLOCK
    n_blk = SEQ // blk
    q_spec = pl.BlockSpec((blk, LANES), lambda b, p, i: (b * n_blk + i, p))
    kv_spec = pl.BlockSpec((SEQ, LANES), lambda b, p, i: (b, p))
    return pl.pallas_call(
        _sb_attn_kernel,
        grid=(BATCH, N_PAIRS, n_blk),
        in_specs=[q_spec, kv_spec, kv_spec],
        out_specs=q_spec,
        out_shape=jax.ShapeDtypeStruct((TOKENS, D_MODEL), F32),
        compiler_params=_params("parallel", "parallel", "arbitrary"),
        name="sb_attention",
    )(q, k, v)


def kernel(x, c, ada_w, ada_b, norm_pre_mix, norm_post_mix, norm_pre_ffn, norm_post_ffn, rwkv_mu, rwkv_w_rkv, rwkv_w_w1, rwkv_w_w2, rwkv_w0, rwkv_w_a1, rwkv_w_a2, rwkv_a0, rwkv_w_g1, rwkv_w_g2, rwkv_k_k, rwkv_k_a, rwkv_r_k, rwkv_lnx_g, rwkv_lnx_b, rwkv_w_o, sb_w_qkv, sb_w_o, moe_w_router, moe_router_bias, moe_w_gate, moe_w_up, moe_w_down, moe_ws_gate, moe_ws_up, moe_ws_down):
    mod_all = _ada_mod(c, ada_w, ada_b)
    xt = x.reshape(TOKENS, D_MODEL)
    for layer in range(DEPTH):
        mod = mod_all[layer]
        i = layer // 2
        if layer % 2 == 0:
            r, k, v, lw, a, g = _rwkv_proj(
                xt, mod, norm_pre_mix[layer], rwkv_mu[i], rwkv_w_rkv[i], rwkv_w_w1[i], rwkv_w_w2[i],
                rwkv_w0[i], rwkv_w_a1[i], rwkv_w_a2[i], rwkv_a0[i], rwkv_w_g1[i], rwkv_w_g2[i])
            o = _wkv(r, k, v, lw, a, rwkv_k_k[i], rwkv_k_a[i], rwkv_r_k[i], rwkv_lnx_g[i], rwkv_lnx_b[i])
            w_o = rwkv_w_o[i]
        else:
            q, k, v = _sb_proj(xt, mod, norm_pre_mix[layer], sb_w_qkv[i])
            o, g = _sb_attn(q, k, v), None
            w_o = sb_w_o[i]
        x1, h2p, gk, ek, rk, counts = _post(o, g, xt, mod, w_o, norm_post_mix[layer], norm_pre_ffn[layer],
                                            moe_w_router[layer], moe_router_bias[layer])
        xt = _moe(layer, x1, h2p, gk, ek, rk, counts, mod, norm_post_ffn[layer], moe_w_gate, moe_w_up,
                  moe_w_down, moe_ws_gate, moe_ws_up, moe_ws_down)
    return xt.reshape(BATCH, SEQ, D_MODEL)
```

```python
import functools

import jax
import jax.numpy as jnp
from jax import lax
from jax.experimental import pallas as pl
from jax.experimental.pallas import tpu as pltpu
from jax.experimental.pallas import tpu_sc as plsc

D_MODEL = 1024
BATCH = 2
SEQ = 8192
TOKENS = BATCH * SEQ
DEPTH = 2
HEAD_DIM = 64
N_EXPERTS = 64
TOP_K = 8
D_EXPERT = 256
ROUTED_SCALE = 2.5
RMS_EPS = 1e-6
LNX_EPS = 64e-5
N_ADA = 6

LANES = 128
SUBLANES = 8
N_PAIRS = D_MODEL // LANES
WKV_CHUNK = 64
WKV_PAIRS = 8
WKV_SEQS = 2
ROW_TILE = 512
SHARED_TILE = 512
EXPERT_BLOCK = 1024
EXPERT_SPLIT = 2
PACK_HALVES = 2
PACK_WORDS = D_MODEL // (2 * PACK_HALVES)
SC_WINDOW = 128
ATT_BLOCK = 256
ATT_SUB = 128
LOG2_E = 1.4426950408889634
EXP2_UNDERFLOW = -153.0
VMEM_LIMIT = 56 * 1024 * 1024

F32 = jnp.float32
BF16 = jnp.bfloat16


def _nn(a, b):
    return jnp.dot(a, b, preferred_element_type=F32)


def _nt(a, b):
    return lax.dot_general(a, b, (((1,), (1,)), ((), ())), preferred_element_type=F32)


def _tn(a, b):
    return lax.dot_general(a, b, (((0,), (0,)), ((), ())), preferred_element_type=F32)


def _bf(x):
    return x.astype(BF16)


def _sigmoid(x):
    return 1.0 / (1.0 + jnp.exp(-x))


def _rms(xv, g):
    ms = jnp.mean(xv * xv, axis=-1, keepdims=True)
    return xv * lax.rsqrt(ms + RMS_EPS) * g


def _split3(x):
    hi = x.astype(BF16)
    r1 = x - hi.astype(F32)
    mid = r1.astype(BF16)
    lo = (r1 - mid.astype(F32)).astype(BF16)
    return hi, mid, lo


def _params(*sem):
    return pltpu.CompilerParams(dimension_semantics=sem, vmem_limit_bytes=VMEM_LIMIT)


def _ada_kernel(c_ref, w_ref, b_ref, o_ref):
    cv = c_ref[...]
    o_ref[0] = _nn(cv * _sigmoid(cv), w_ref[0]) + b_ref[0]


def _ada_mod(c, ada_w, ada_b):
    tn = 1536
    c8 = jnp.pad(c, ((0, SUBLANES - BATCH), (0, 0)))
    out = pl.pallas_call(
        _ada_kernel,
        grid=(DEPTH, N_ADA * D_MODEL // tn),
        in_specs=[
            pl.BlockSpec((SUBLANES, D_MODEL), lambda l, j: (0, 0)),
            pl.BlockSpec((1, D_MODEL, tn), lambda l, j: (l, 0, j)),
            pl.BlockSpec((1, 1, tn), lambda l, j: (l, 0, j)),
        ],
        out_specs=pl.BlockSpec((1, SUBLANES, tn), lambda l, j: (l, 0, j)),
        out_shape=jax.ShapeDtypeStruct((DEPTH, SUBLANES, N_ADA * D_MODEL), F32),
        compiler_params=_params("parallel", "parallel"),
        name="ada_mod",
    )(c8, ada_w, ada_b.reshape(DEPTH, 1, N_ADA * D_MODEL))
    return out[:, :BATCH].reshape(DEPTH, BATCH, N_ADA, D_MODEL)


def _rwkv_proj_kernel(x_ref, xp_ref, mod_ref, gpre_ref, mu_ref, wrkv_ref, w1_ref, w2_ref, w0_ref,
                      a1_ref, a2_ref, a0_ref, g1_ref, g2_ref,
                      r_ref, k_ref, v_ref, lw_ref, a_ref, g_ref):
    i = pl.program_id(0)
    mod = mod_ref[0]
    shift, scale = mod[0:1], mod[1:2]
    gpre = gpre_ref[...]

    def modnorm(xv):
        return _rms(xv, gpre) * (1.0 + scale) + shift

    h = modnorm(x_ref[...])
    hp = modnorm(xp_ref[...])[SUBLANES - 1:SUBLANES]
    hp = jnp.where(i % (SEQ // ROW_TILE) == 0, 0.0, hp)
    row = lax.broadcasted_iota(jnp.int32, h.shape, 0)
    hs = jnp.where(row == 0, hp, pltpu.roll(h, 1, 0))
    xx = hs - h
    mu = mu_ref[...]

    def mix(n):
        return _bf(h + xx * mu[n:n + 1])

    r_ref[...] = _bf(_nn(mix(0), wrkv_ref[0]))
    k_ref[...] = _bf(_nn(mix(1), wrkv_ref[1]))
    v_ref[...] = _bf(_nn(mix(2), wrkv_ref[2]))
    wl = w0_ref[...] + _nn(_bf(jnp.tanh(_nn(mix(3), w1_ref[...]))), w2_ref[...])
    nwl = -wl
    sp = jnp.maximum(nwl, 0.0) + jnp.log(1.0 + jnp.exp(-jnp.abs(nwl)))
    lw_ref[...] = -jnp.exp(-sp - 0.5)
    a_ref[...] = _sigmoid(a0_ref[...] + _nn(_bf(_nn(mix(4), a1_ref[...])), a2_ref[...]))
    g_ref[...] = _bf(_nn(_bf(_sigmoid(_nn(mix(5), g1_ref[...]))), g2_ref[...]))


def _rwkv_proj(x, mod, gpre, mu, w_rkv, w1, w2, w0, a1, a2, a0, g1, g2):
    tm = ROW_TILE
    full = lambda shape: pl.BlockSpec(shape, lambda i: (0,) * len(shape))
    row_spec = pl.BlockSpec((tm, D_MODEL), lambda i: (i, 0))
    out_sds = [jax.ShapeDtypeStruct((TOKENS, D_MODEL), dt) for dt in (BF16, BF16, BF16, F32, F32, BF16)]
    return pl.pallas_call(
        _rwkv_proj_kernel,
        grid=(TOKENS // tm,),
        in_specs=[
            row_spec,
            pl.BlockSpec((SUBLANES, D_MODEL), lambda i: (jnp.maximum(i * (tm // SUBLANES) - 1, 0), 0)),
            pl.BlockSpec((1, N_ADA, D_MODEL), lambda i: (i // (SEQ // tm), 0, 0)),
            full((1, D_MODEL)), full((6, D_MODEL)), full((3, D_MODEL, D_MODEL)),
            full(w1.shape), full(w2.shape), full((1, D_MODEL)),
            full(a1.shape), full(a2.shape), full((1, D_MODEL)),
            full(g1.shape), full(g2.shape),
        ],
        out_specs=[row_spec] * 6,
        out_shape=out_sds,
        compiler_params=_params("parallel"),
        name="rwkv_proj",
    )(x, x, mod, gpre.reshape(1, D_MODEL), mu, _bf(w_rkv), _bf(w1), _bf(w2),
      w0.reshape(1, D_MODEL), _bf(a1), _bf(a2), a0.reshape(1, D_MODEL), _bf(g1), _bf(g2))


def _wkv_pair(r, k, v, lw, a, k_k, k_a, r_k, ln_g, ln_b, s_ref, o_ref):
    c = WKV_CHUNK
    lane = lax.broadcasted_iota(jnp.int32, (c, LANES), 1)
    head0 = lane < HEAD_DIM
    rowi = lax.broadcasted_iota(jnp.int32, (LANES, LANES), 0)
    coli = lax.broadcasted_iota(jnp.int32, (LANES, LANES), 1)
    rt, ct = rowi & (c - 1), coli & (c - 1)
    strict, incl, eye = rt > ct, rt >= ct, rowi == coli
    tril = (lax.broadcasted_iota(jnp.int32, (c, c), 0)
            >= lax.broadcasted_iota(jnp.int32, (c, c), 1)).astype(BF16)

    def headsum(x):
        s0 = jnp.sum(jnp.where(head0, x, 0.0), axis=-1, keepdims=True)
        s1 = jnp.sum(jnp.where(head0, 0.0, x), axis=-1, keepdims=True)
        return jnp.where(head0, s0, s1)

    def stack(x):
        return jnp.concatenate([jnp.where(head0, x, 0.0), jnp.where(head0, 0.0, x)], axis=0)

    kk = k * k_k
    kk = kk / jnp.maximum(jnp.sqrt(headsum(kk * kk)), 1e-12)
    kf = k * (1.0 + (a - 1.0) * k_a)
    bvec = kk * a
    yield

    cum = _nn(tril, jnp.concatenate(_split3(lw), axis=1))
    cum = cum[:, :LANES] + cum[:, LANES:2 * LANES] + cum[:, 2 * LANES:]
    cum_end = cum[c - 1:c]
    w_inv = jnp.exp(-cum)
    w_rem = jnp.exp(cum_end - cum)
    a_st = stack(-kk * jnp.exp(cum - lw))
    r_st = stack(r * jnp.exp(cum))
    b_st = stack(bvec * w_inv)
    k_st = stack(kf * w_inv)
    bp_st = _bf(stack(bvec * w_rem))
    kp_st = _bf(stack(kf * w_rem))
    v_st = _bf(stack(v))
    yield

    sc = _nt(_bf(jnp.concatenate([a_st, r_st], axis=0)), _bf(jnp.concatenate([b_st, k_st], axis=0)))
    l_ab = jnp.where(strict, sc[:LANES, :LANES], 0.0)
    l_ak = jnp.where(strict, sc[:LANES, LANES:], 0.0)
    l_rb = jnp.where(incl, sc[LANES:, :LANES], 0.0)
    l_rk = jnp.where(incl, sc[LANES:, LANES:], 0.0)
    yield

    tinv = jnp.where(eye, 1.0, l_ab)
    pw = l_ab
    for _ in range(c.bit_length() - 2):
        pwb = _bf(pw)
        pw = _nn(pwb, pwb)
        tinv = tinv + _nn(_bf(tinv), _bf(pw))
        yield

    au = _nn(_bf(tinv), _bf(jnp.concatenate([a_st, _nn(_bf(l_ak), v_st)], axis=1)))
    aub = _bf(au)
    yield
    rhs = jnp.concatenate([aub, jnp.concatenate([jnp.zeros_like(v_st), v_st], axis=1)], axis=0)
    rb = _nn(_bf(jnp.concatenate([l_rb, l_rk], axis=1)), rhs)
    gh = _tn(jnp.concatenate([bp_st, kp_st], axis=0), rhs)
    r_hat = r_st + rb[:, :LANES]
    o0 = rb[:, LANES:]
    g_mat = jnp.where(eye, jnp.exp(cum_end), 0.0) + gh[:, :LANES]
    h_mat = gh[:, LANES:]
    yield

    os = _nn(_bf(jnp.concatenate([r_hat, g_mat], axis=0)), _bf(s_ref[...]))
    s_ref[...] = os[LANES:] + h_mat
    o_st = os[:LANES] + o0
    o = o_st[:c] + o_st[c:]
    yield

    mean = headsum(o) * (1.0 / HEAD_DIM)
    oc = o - mean
    var = headsum(oc * oc) * (1.0 / HEAD_DIM)
    on = oc * lax.rsqrt(var + LNX_EPS) * ln_g + ln_b
    o_ref[...] = _bf(on + headsum(r * kf * r_k) * v)


def _wkv_kernel(r_ref, k_ref, v_ref, lw_ref, a_ref, kk_ref, ka_ref, rk_ref, lg_ref, lb_ref,
                o_ref, s_ref):
    @pl.when(pl.program_id(2) == 0)
    def _():
        s_ref[...] = jnp.zeros_like(s_ref)

    def pair(b, p):
        sl = pl.ds(p * LANES, LANES)
        acts = [ref[b, :, sl].astype(F32) for ref in (r_ref, k_ref, v_ref, lw_ref, a_ref)]
        pars = [ref[:, sl] for ref in (kk_ref, ka_ref, rk_ref, lg_ref, lb_ref)]
        return _wkv_pair(*acts, *pars, s_ref.at[b * WKV_PAIRS + p], o_ref.at[b, :, sl])

    stages = [pair(b, p) for b in range(WKV_SEQS) for p in range(WKV_PAIRS)]
    while stages:
        stages = [g for g in stages if next(g, True) is None]


def _wkv(r, k, v, lw, a, k_k, k_a, r_k, lnx_g, lnx_b):
    n_chunks = SEQ // WKV_CHUNK
    width = WKV_PAIRS * LANES
    tok = pl.BlockSpec((WKV_SEQS, WKV_CHUNK, width), lambda b, p, c: (b, c, p))
    par = pl.BlockSpec((1, width), lambda b, p, c: (0, p))
    vec = lambda t: t.reshape(1, D_MODEL)
    seq = lambda t: t.reshape(BATCH, SEQ, D_MODEL)
    out = pl.pallas_call(
        _wkv_kernel,
        grid=(BATCH // WKV_SEQS, N_PAIRS // WKV_PAIRS, n_chunks),
        in_specs=[tok] * 5 + [par] * 5,
        out_specs=tok,
        out_shape=jax.ShapeDtypeStruct((BATCH, SEQ, D_MODEL), BF16),
        scratch_shapes=[pltpu.VMEM((WKV_SEQS * WKV_PAIRS, LANES, LANES), F32)],
        compiler_params=_params("parallel", "parallel", "arbitrary"),
        name="wkv_scan",
    )(seq(r), seq(k), seq(v), seq(lw), seq(a), vec(k_k), vec(k_a), vec(r_k), vec(lnx_g), vec(lnx_b))
    return out.reshape(TOKENS, D_MODEL)


def _pack_rows(x):
    bits = lax.bitcast_convert_type(_bf(x).astype(F32), jnp.uint32)
    halves = []
    for h in range(PACK_HALVES):
        lo = bits[:, (2 * h) * PACK_WORDS:(2 * h + 1) * PACK_WORDS]
        hi = bits[:, (2 * h + 1) * PACK_WORDS:(2 * h + 2) * PACK_WORDS]
        halves.append(lax.bitcast_convert_type((lo >> 16) | (hi & jnp.uint32(0xFFFF0000)), jnp.int32))
    return halves


def _unpack_rows(halves):
    parts = []
    for w in halves:
        u = lax.bitcast_convert_type(w, jnp.uint32)
        parts.append(lax.bitcast_convert_type(u << 16, F32))
        parts.append(lax.bitcast_convert_type(u & jnp.uint32(0xFFFF0000), F32))
    return jnp.concatenate(parts, axis=1)


def _post_kernel(*refs, has_gate):
    if has_gate:
        o_ref, g_ref, *refs = refs
    else:
        o_ref, *refs = refs
    (x_ref, mod_ref, wo_ref, gpost_ref, gpre_ref, wr_hi_ref, wr_lo_ref, bias_ref,
     x1_ref, h2p_ref, gk_ref, ek_ref, rk_ref, cnt_ref, run_ref) = refs
    tm = ROW_TILE

    @pl.when(pl.program_id(0) == 0)
    def _():
        run_ref[...] = jnp.zeros_like(run_ref)

    mod = mod_ref[0]
    gate_m, shift_f, scale_f = mod[2:3], mod[3:4], mod[4:5]
    o = o_ref[...]
    if has_gate:
        o = o.astype(F32) * g_ref[...].astype(F32)
    y = _nn(_bf(o), wo_ref[...])
    x1 = x_ref[...] + gate_m * _rms(y, gpost_ref[...])
    x1_ref[...] = x1
    h2 = _rms(x1, gpre_ref[...]) * (1.0 + scale_f) + shift_f
    for h, words in enumerate(_pack_rows(h2)):
        h2p_ref[h] = words

    hi = _bf(h2)
    lo = _bf(h2 - hi.astype(F32))
    wr_hi, wr_lo = wr_hi_ref[...], wr_lo_ref[...]
    scores = _sigmoid(_nt(wr_hi, hi) + (_nt(wr_hi, lo) + _nt(wr_lo, hi)))
    erow = lax.broadcasted_iota(jnp.int32, scores.shape, 0)
    work = scores + jnp.concatenate([bias_ref[...]] * (tm // LANES), axis=1)
    picked = jnp.zeros_like(scores)
    chosen = jnp.zeros_like(scores)
    hits, ids = [], []
    for _ in range(TOP_K):
        best = jnp.max(work, axis=0, keepdims=True)
        idx = jnp.min(jnp.where(work == best, erow, N_EXPERTS), axis=0, keepdims=True)
        hit = erow == idx
        picked = jnp.where(hit, scores, picked)
        chosen = jnp.where(hit, 1.0, chosen)
        work = jnp.where(hit, -jnp.inf, work)
        hits.append(hit)
        ids.append(idx)
    gates = ROUTED_SCALE * picked / jnp.sum(picked, axis=0, keepdims=True)

    upto = (lax.broadcasted_iota(jnp.int32, (tm, tm), 0)
            <= lax.broadcasted_iota(jnp.int32, (tm, tm), 1)).astype(BF16)
    chosen_b = _bf(chosen)
    run = run_ref[...]
    before = _nn(chosen_b, upto) - chosen + jnp.concatenate([run] * (tm // LANES), axis=1)
    run = run + _nn(chosen_b, jnp.ones((tm, LANES), BF16))
    run_ref[...] = run
    cnt_ref[...] = run

    krow = lax.broadcasted_iota(jnp.int32, (TOP_K, tm), 0)
    gk = jnp.zeros((TOP_K, tm), F32)
    ek = jnp.zeros((TOP_K, tm), jnp.int32)
    rk = jnp.zeros((TOP_K, tm), jnp.int32)
    for j in range(TOP_K):
        grow = jnp.sum(jnp.where(hits[j], gates, 0.0), axis=0, keepdims=True)
        rrow = jnp.sum(jnp.where(hits[j], before, 0.0), axis=0, keepdims=True)
        gk = jnp.where(krow == j, grow, gk)
        ek = jnp.where(krow == j, ids[j], ek)
        rk = jnp.where(krow == j, rrow.astype(jnp.int32), rk)
    gk_ref[...] = gk
    ek_ref[...] = ek
    rk_ref[...] = rk


def _post(o, g, x, mod, w_o, gpost, gpre, w_router, router_bias):
    tm = ROW_TILE
    full = lambda shape: pl.BlockSpec(shape, lambda i: (0,) * len(shape))
    row_spec = pl.BlockSpec((tm, D_MODEL), lambda i: (i, 0))
    k_spec = pl.BlockSpec((TOP_K, tm), lambda i: (0, i))
    wr = w_router.T
    wr_hi = _bf(wr)
    wr_lo = _bf(wr - wr_hi.astype(F32))
    bias = jnp.broadcast_to(router_bias[:, None], (N_EXPERTS, LANES))
    has_gate = g is not None
    acts = [o, g] if has_gate else [o]
    return pl.pallas_call(
        functools.partial(_post_kernel, has_gate=has_gate),
        grid=(TOKENS // tm,),
        in_specs=[row_spec] * (len(acts) + 1) + [
            pl.BlockSpec((1, N_ADA, D_MODEL), lambda i: (i // (SEQ // tm), 0, 0)),
            full((D_MODEL, D_MODEL)), full((1, D_MODEL)), full((1, D_MODEL)),
            full((N_EXPERTS, D_MODEL)), full((N_EXPERTS, D_MODEL)), full((N_EXPERTS, LANES)),
        ],
        out_specs=[row_spec, pl.BlockSpec((PACK_HALVES, tm, PACK_WORDS), lambda i: (0, i, 0)),
                   k_spec, k_spec, k_spec, full((N_EXPERTS, LANES))],
        out_shape=[jax.ShapeDtypeStruct((TOKENS, D_MODEL), F32),
                   jax.ShapeDtypeStruct((PACK_HALVES, TOKENS, PACK_WORDS), jnp.int32),
                   jax.ShapeDtypeStruct((TOP_K, TOKENS), F32),
                   jax.ShapeDtypeStruct((TOP_K, TOKENS), jnp.int32),
                   jax.ShapeDtypeStruct((TOP_K, TOKENS), jnp.int32),
                   jax.ShapeDtypeStruct((N_EXPERTS, LANES), F32)],
        scratch_shapes=[pltpu.VMEM((N_EXPERTS, LANES), F32)],
        compiler_params=_params("arbitrary"),
        name="mixer_out_router",
    )(*acts, x, mod, _bf(w_o), gpost.reshape(1, D_MODEL), gpre.reshape(1, D_MODEL),
      wr_hi, wr_lo, bias)


def _sc_mesh():
    return plsc.VectorSubcoreMesh(core_axis_name="core", subcore_axis_name="subcore")


def _sc_scatter_rows(rows, idx, n_out):
    n_copies = idx.shape[0]

    @pl.kernel(out_type=jax.ShapeDtypeStruct((PACK_HALVES, n_out, PACK_WORDS), rows.dtype),
               mesh=_sc_mesh(), scratch_types=[], name="sc_dispatch_rows")
    def scatter(x_hbm, i_hbm, o_hbm):
        for h in range(PACK_HALVES):
            def body(x_vmem, i_vmem, h=h):
                for k in range(n_copies):
                    pltpu.sync_copy(x_vmem, o_hbm.at[h].at[i_vmem.at[k]])

            pltpu.emit_pipeline(
                body, grid=(rows.shape[1] // SC_WINDOW,),
                in_specs=[pl.BlockSpec((SC_WINDOW, PACK_WORDS), index_map=lambda i: (i, 0)),
                          pl.BlockSpec((n_copies, SC_WINDOW), index_map=lambda i: (0, i))],
                out_specs=[],
                core_axis_name=("core", "subcore"),
                dimension_semantics=(pltpu.PARALLEL,),
            )(x_hbm.at[h], i_hbm)

    return scatter(rows, idx)


def _sc_gather_rows(tables, idx):
    n = idx.shape[0]

    @pl.kernel(out_type=jax.ShapeDtypeStruct((PACK_HALVES, n, PACK_WORDS), tables[0].dtype),
               mesh=_sc_mesh(), scratch_types=[], name="sc_collect_rows")
    def gather(*refs):
        x_hbm, (i_hbm, o_hbm) = refs[:PACK_HALVES], refs[PACK_HALVES:]
        for h in range(PACK_HALVES):
            def body(i_vmem, o_vmem, h=h):
                pltpu.sync_copy(x_hbm[h].at[i_vmem.at[0]], o_vmem)

            pltpu.emit_pipeline(
                body, grid=(n // SC_WINDOW,),
                in_specs=[pl.BlockSpec((1, SC_WINDOW), index_map=lambda i: (0, i))],
                out_specs=[pl.BlockSpec((SC_WINDOW, PACK_WORDS), index_map=lambda i: (i, 0))],
                core_axis_name=("core", "subcore"),
                dimension_semantics=(pltpu.PARALLEL,),
            )(i_hbm, o_hbm.at[h])

    return gather(*tables, idx.reshape(1, n))


def _ffn(x, wg, wu, wd):
    up = _nn(x, _bf(wu))
    gt = _nn(x, _bf(wg))
    return _nn(_bf(gt * _sigmoid(gt) * up), _bf(wd))


def _expert_kernel(be_ref, nu_ref, *refs):
    n_in = PACK_HALVES * EXPERT_SPLIT
    x_refs = refs[:n_in]
    wg_ref, wu_ref, wd_ref = refs[n_in:n_in + 3]
    y_refs = refs[n_in + 3:n_in + 3 + PACK_HALVES]
    wgb_ref, wub_ref, wdb_ref = refs[n_in + 3 + PACK_HALVES:]
    b = pl.program_id(0)

    @pl.when(jnp.logical_or(b == 0, be_ref[b] != be_ref[jnp.maximum(b - 1, 0)]))
    def _():
        wgb_ref[...] = _bf(wg_ref[0, 0])
        wub_ref[...] = _bf(wu_ref[0, 0])
        wdb_ref[...] = _bf(wd_ref[0, 0])

    @pl.when(b < nu_ref[0])
    def _():
        halves = [jnp.concatenate([x_refs[h * EXPERT_SPLIT + j][0] for j in range(EXPERT_SPLIT)], axis=0)
                  for h in range(PACK_HALVES)]
        y = _ffn(_bf(_unpack_rows(halves)), wgb_ref[...], wub_ref[...], wdb_ref[...])
        for h, words in enumerate(_pack_rows(y)):
            y_refs[h][...] = words


def _experts(layer, xs, blk_expert, n_used, w_gate, w_up, w_down):
    n_rows = xs.shape[1]
    n_blocks = n_rows // EXPERT_BLOCK
    part = EXPERT_BLOCK // EXPERT_SPLIT
    x_spec = lambda h, j: pl.BlockSpec((1, part, PACK_WORDS), lambda b, be, nu: (h, b * EXPERT_SPLIT + j, 0))
    w_spec = lambda shape: pl.BlockSpec((1, 1) + shape, lambda b, be, nu: (layer, be[b], 0, 0))
    y_spec = pl.BlockSpec((EXPERT_BLOCK, PACK_WORDS), lambda b, be, nu: (b, 0))
    n_in = PACK_HALVES * EXPERT_SPLIT
    return pl.pallas_call(
        _expert_kernel,
        grid_spec=pltpu.PrefetchScalarGridSpec(
            num_scalar_prefetch=2,
            grid=(n_blocks,),
            in_specs=[x_spec(h, j) for h in range(PACK_HALVES) for j in range(EXPERT_SPLIT)]
                     + [w_spec((D_MODEL, D_EXPERT)), w_spec((D_MODEL, D_EXPERT)), w_spec((D_EXPERT, D_MODEL))],
            out_specs=[y_spec] * PACK_HALVES,
            scratch_shapes=[pltpu.VMEM((D_MODEL, D_EXPERT), BF16), pltpu.VMEM((D_MODEL, D_EXPERT), BF16),
                            pltpu.VMEM((D_EXPERT, D_MODEL), BF16)],
        ),
        out_shape=[jax.ShapeDtypeStruct((n_rows, PACK_WORDS), xs.dtype)] * PACK_HALVES,
        compiler_params=_params("arbitrary"),
        name="moe_experts",
    )(blk_expert, n_used, *([xs] * n_in), w_gate, w_up, w_down)


def _shared_kernel(h2p_ref, sg_ref, su_ref, sd_ref, o_ref):
    h2 = _bf(_unpack_rows([h2p_ref[h] for h in range(PACK_HALVES)]))
    for h, words in enumerate(_pack_rows(_ffn(h2, sg_ref[0], su_ref[0], sd_ref[0]))):
        o_ref[h] = words


def _shared_expert(layer, h2p, ws_gate, ws_up, ws_down):
    tm = SHARED_TILE
    row_spec = pl.BlockSpec((PACK_HALVES, tm, PACK_WORDS), lambda i: (0, i, 0))
    return pl.pallas_call(
        _shared_kernel,
        grid=(TOKENS // tm,),
        in_specs=[row_spec,
                  pl.BlockSpec((1, D_MODEL, D_EXPERT), lambda i: (layer, 0, 0)),
                  pl.BlockSpec((1, D_MODEL, D_EXPERT), lambda i: (layer, 0, 0)),
                  pl.BlockSpec((1, D_EXPERT, D_MODEL), lambda i: (layer, 0, 0))],
        out_specs=row_spec,
        out_shape=jax.ShapeDtypeStruct(h2p.shape, h2p.dtype),
        compiler_params=_params("parallel"),
        name="moe_shared_expert",
    )(h2p, ws_gate, ws_up, ws_down)


def _combine_kernel(*refs):
    g_refs = refs[:TOP_K]
    gk_ref, sh_ref, x_ref, mod_ref, gpost_ref, o_ref = refs[TOP_K:]
    acc = _unpack_rows([sh_ref[h] for h in range(PACK_HALVES)])
    gk = gk_ref[...].T
    for j in range(TOP_K):
        acc = acc + gk[:, j:j + 1] * _unpack_rows([g_refs[j][h] for h in range(PACK_HALVES)])
    gate_f = mod_ref[0][5:6]
    o_ref[...] = x_ref[...] + gate_f * _rms(acc, gpost_ref[...])


def _combine(picked, gk, shared, x1, mod, gpost):
    tm = ROW_TILE
    n_tiles = TOKENS // tm
    row_spec = pl.BlockSpec((tm, D_MODEL), lambda i: (i, 0))
    pick_spec = lambda j: pl.BlockSpec((PACK_HALVES, tm, PACK_WORDS), lambda i: (0, j * n_tiles + i, 0))
    return pl.pallas_call(
        _combine_kernel,
        grid=(n_tiles,),
        in_specs=[pick_spec(j) for j in range(TOP_K)] + [
            pl.BlockSpec((TOP_K, tm), lambda i: (0, i)),
            pl.BlockSpec((PACK_HALVES, tm, PACK_WORDS), lambda i: (0, i, 0)),
            row_spec,
            pl.BlockSpec((1, N_ADA, D_MODEL), lambda i: (i // (SEQ // tm), 0, 0)),
            pl.BlockSpec((1, D_MODEL), lambda i: (0, 0)),
        ],
        out_specs=row_spec,
        out_shape=jax.ShapeDtypeStruct((TOKENS, D_MODEL), F32),
        compiler_params=_params("parallel"),
        name="moe_combine",
    )(*([picked] * TOP_K), gk, shared, x1, mod, gpost.reshape(1, D_MODEL))


def _moe(layer, x1, h2p, gk, ek, rk, counts, mod, gpost, w_gate, w_up, w_down, ws_gate, ws_up, ws_down):
    cnt = counts[:, 0].astype(jnp.int32)
    padded = (cnt + EXPERT_BLOCK - 1) // EXPERT_BLOCK * EXPERT_BLOCK
    eid = jnp.arange(N_EXPERTS, dtype=jnp.int32)
    pend = jnp.sum(jnp.where(eid[:, None] <= eid[None, :], padded[:, None], 0), axis=0)
    pstart = pend - padded
    dest = rk + jnp.sum(jnp.where(ek[None] == eid[:, None, None], pstart[:, None, None], 0), axis=0)
    n_blocks = (TOKENS * TOP_K + N_EXPERTS * (EXPERT_BLOCK - 1)) // EXPERT_BLOCK + 1
    blk_start = jnp.arange(n_blocks, dtype=jnp.int32) * EXPERT_BLOCK
    blk_expert = jnp.minimum(jnp.sum((blk_start[None, :] >= pend[:, None]).astype(jnp.int32), axis=0),
                             N_EXPERTS - 1)
    n_used = pend[-1:] // EXPERT_BLOCK

    xs = _sc_scatter_rows(h2p, dest, n_blocks * EXPERT_BLOCK)
    shared = _shared_expert(layer, h2p, ws_gate, ws_up, ws_down)
    ys = _experts(layer, xs, blk_expert, n_used, w_gate, w_up, w_down)
    picked = _sc_gather_rows(ys, dest.reshape(-1))
    return _combine(picked, gk, shared, x1, mod, gpost)


def _sb_proj_kernel(x_ref, mod_ref, gpre_ref, w_ref, q_ref, k_ref, v_ref):
    mod = mod_ref[0]
    h = _bf(_rms(x_ref[...], gpre_ref[...]) * (1.0 + mod[1:2]) + mod[0:1])
    q_ref[...] = _bf(_nn(h, w_ref[:, :D_MODEL]) * (LOG2_E / 8.0))
    k_ref[...] = _bf(_nn(h, w_ref[:, D_MODEL:2 * D_MODEL]))
    v_ref[...] = _bf(_nn(h, w_ref[:, 2 * D_MODEL:]))


def _sb_proj(x, mod, gpre, w_qkv):
    tm = ROW_TILE
    full = lambda shape: pl.BlockSpec(shape, lambda i: (0,) * len(shape))
    row_spec = pl.BlockSpec((tm, D_MODEL), lambda i: (i, 0))
    out_sds = jax.ShapeDtypeStruct((TOKENS, D_MODEL), BF16)
    return pl.pallas_call(
        _sb_proj_kernel,
        grid=(TOKENS // tm,),
        in_specs=[row_spec,
                  pl.BlockSpec((1, N_ADA, D_MODEL), lambda i: (i // (SEQ // tm), 0, 0)),
                  full((1, D_MODEL)), full((D_MODEL, 3 * D_MODEL))],
        out_specs=[row_spec] * 3,
        out_shape=[out_sds] * 3,
        compiler_params=_params("parallel"),
        name="sb_qkv_proj",
    )(x, mod, gpre.reshape(1, D_MODEL), _bf(w_qkv))


def _sb_attn_kernel(q_ref, k_ref, v_ref, o_ref):
    blk = ATT_BLOCK
    n_heads = LANES // HEAD_DIM
    qb = pl.program_id(2)
    q = q_ref[...]
    lane = lax.broadcasted_iota(jnp.int32, (blk, LANES), 1)
    head0 = lane < HEAD_DIM
    qh = [jnp.where(head0, q, jnp.zeros_like(q)), jnp.where(head0, jnp.zeros_like(q), q)]
    rowi = lax.broadcasted_iota(jnp.int32, (blk, blk), 0)
    coli = lax.broadcasted_iota(jnp.int32, (blk, blk), 1)
    later = (rowi > coli).astype(BF16)

    sub = ATT_SUB
    n_sub = blk // sub
    sub_row = lax.broadcasted_iota(jnp.int32, (sub, blk), 0)
    sub_col = lax.broadcasted_iota(jnp.int32, (sub, blk), 1)
    below = [sub_col < sub_row + r * sub for r in range(n_sub)]
    sub_head0 = lax.broadcasted_iota(jnp.int32, (sub, LANES), 1) < HEAD_DIM
    chains = [(hd, r) for hd in range(n_heads) for r in range(n_sub)]

    def chain(hd, r, ks, vs, get_carry, get_acc, mode, out):
        rows = slice(r * sub, (r + 1) * sub)
        u = _nt(qh[hd][rows], ks)
        yield
        neg_abs = lax.bitcast_convert_type(
            lax.bitcast_convert_type(u, jnp.uint32) | jnp.uint32(0x80000000), F32)
        lb = jnp.minimum(u, 0.0) - jnp.log2(1.0 + jnp.exp2(neg_abs))
        l1m = lb - u
        if mode == "diag":
            l1m = jnp.where(below[r], l1m, 0.0)
        l1b = _bf(l1m)
        yield
        carry = get_carry()
        tail = _nn(l1b, later) + jnp.concatenate([carry] * (blk // LANES), axis=1)
        out["carry"] = carry + jnp.sum(l1m, axis=-1, keepdims=True)
        yield
        w = jnp.exp2(lb + tail)
        if mode == "diag":
            w = jnp.where(below[r], w, 0.0)
        elif mode == "prev":
            w = jnp.where(qb > 0, w, 0.0)
        wb = _bf(w)
        yield
        out["acc"] = get_acc() + _nn(wb, vs)

    def load_kv(kb):
        start = pl.multiple_of(kb * blk, blk)
        return k_ref[pl.ds(start, blk), :], v_ref[pl.ds(start, blk), :]

    def run(gens):
        while gens:
            gens = [g for g in gens if next(g, True) is None]

    def block(kb, carry, acc):
        ks, vs = load_kv(kb)
        outs = [{} for _ in chains]
        run([chain(hd, r, ks, vs, lambda c=c: carry[c], lambda c=c: acc[c], "plain", outs[c])
             for c, (hd, r) in enumerate(chains)])
        return [o["carry"] for o in outs], [o["acc"] for o in outs]

    zeros = jnp.zeros((sub, LANES), F32)
    ks_d, vs_d = load_kv(qb)
    ks_p, vs_p = load_kv(jnp.maximum(qb - 1, 0))
    outs_d = [{} for _ in chains]
    outs_p = [{} for _ in chains]
    run([chain(hd, r, ks_d, vs_d, lambda: zeros, lambda: zeros, "diag", outs_d[c])
         for c, (hd, r) in enumerate(chains)]
        + [chain(hd, r, ks_p, vs_p, lambda c=c: outs_d[c]["carry"], lambda c=c: outs_d[c]["acc"],
                 "prev", outs_p[c]) for c, (hd, r) in enumerate(chains)])
    carry = [o["carry"] for o in outs_p]
    acc = [o["acc"] for o in outs_p]

    def cmax_of(cr):
        return jnp.max(functools.reduce(jnp.maximum, cr))

    def cond(st):
        kb, _, _, cmax = st
        return jnp.logical_and(kb >= 0, cmax > EXP2_UNDERFLOW)

    def body(st):
        kb, cr, ac, _ = st
        cr, ac = block(kb, list(cr), list(ac))
        return kb - 1, tuple(cr), tuple(ac), cmax_of(cr)

    _, _, acc, _ = lax.while_loop(cond, body, (qb - 2, tuple(carry), tuple(acc), cmax_of(carry)))
    for r in range(n_sub):
        rows = slice(r * sub, (r + 1) * sub)
        o_ref[rows, :] = _bf(jnp.where(sub_head0, acc[r], acc[n_sub + r]))


def _sb_attn(q, k, v):
    blk = ATT_BLOCK
    n_blk = SEQ // blk
    q_spec = pl.BlockSpec((blk, LANES), lambda b, p, i: (b * n_blk + i, p))
    kv_spec = pl.BlockSpec((SEQ, LANES), lambda b, p, i: (b, p))
    return pl.pallas_call(
        _sb_attn_kernel,
        grid=(BATCH, N_PAIRS, n_blk),
        in_specs=[q_spec, kv_spec, kv_spec],
        out_specs=q_spec,
        out_shape=jax.ShapeDtypeStruct((TOKENS, D_MODEL), BF16),
        compiler_params=_params("parallel", "parallel", "arbitrary"),
        name="sb_attention",
    )(q, k, v)


def kernel(x, c, ada_w, ada_b, norm_pre_mix, norm_post_mix, norm_pre_ffn, norm_post_ffn, rwkv_mu, rwkv_w_rkv, rwkv_w_w1, rwkv_w_w2, rwkv_w0, rwkv_w_a1, rwkv_w_a2, rwkv_a0, rwkv_w_g1, rwkv_w_g2, rwkv_k_k, rwkv_k_a, rwkv_r_k, rwkv_lnx_g, rwkv_lnx_b, rwkv_w_o, sb_w_qkv, sb_w_o, moe_w_router, moe_router_bias, moe_w_gate, moe_w_up, moe_w_down, moe_ws_gate, moe_ws_up, moe_ws_down):
    mod_all = _ada_mod(c, ada_w, ada_b)
    xt = x.reshape(TOKENS, D_MODEL)
    for layer in range(DEPTH):
        mod = mod_all[layer]
        i = layer // 2
        if layer % 2 == 0:
            r, k, v, lw, a, g = _rwkv_proj(
                xt, mod, norm_pre_mix[layer], rwkv_mu[i], rwkv_w_rkv[i], rwkv_w_w1[i], rwkv_w_w2[i],
                rwkv_w0[i], rwkv_w_a1[i], rwkv_w_a2[i], rwkv_a0[i], rwkv_w_g1[i], rwkv_w_g2[i])
            o = _wkv(r, k, v, lw, a, rwkv_k_k[i], rwkv_k_a[i], rwkv_r_k[i], rwkv_lnx_g[i], rwkv_lnx_b[i])
            w_o = rwkv_w_o[i]
        else:
            q, k, v = _sb_proj(xt, mod, norm_pre_mix[layer], sb_w_qkv[i])
            o, g = _sb_attn(q, k, v), None
            w_o = sb_w_o[i]
        x1, h2p, gk, ek, rk, counts = _post(o, g, xt, mod, w_o, norm_post_mix[layer], norm_pre_ffn[layer],
                                            moe_w_router[layer], moe_router_bias[layer])
        xt = _moe(layer, x1, h2p, gk, ek, rk, counts, mod, norm_post_ffn[layer], moe_w_gate, moe_w_up,
                  moe_w_down, moe_ws_gate, moe_ws_up, moe_ws_down)
    return xt.reshape(BATCH, SEQ, D_MODEL)
```

```python
import functools

import jax
import jax.numpy as jnp
from jax import lax
from jax.experimental import pallas as pl
from jax.experimental.pallas import tpu as pltpu
from jax.experimental.pallas import tpu_sc as plsc

D_MODEL = 1024
BATCH = 2
SEQ = 8192
TOKENS = BATCH * SEQ
DEPTH = 2
HEAD_DIM = 64
N_EXPERTS = 64
TOP_K = 8
D_EXPERT = 256
ROUTED_SCALE = 2.5
RMS_EPS = 1e-6
LNX_EPS = 64e-5
N_ADA = 6

LANES = 128
SUBLANES = 8
N_PAIRS = D_MODEL // LANES
WKV_CHUNK = 64
WKV_PAIRS = 8
WKV_SEQS = 2
ROW_TILE = 512
SHARED_TILE = 512
EXPERT_BLOCK = 1024
EXPERT_SPLIT = 2
PACK_HALVES = 2
PACK_WORDS = D_MODEL // (2 * PACK_HALVES)
SC_WINDOW = 128
ATT_BLOCK = 256
ATT_SUB = 128
LOG2_E = 1.4426950408889634
EXP2_UNDERFLOW = -153.0
VMEM_LIMIT = 56 * 1024 * 1024

F32 = jnp.float32
BF16 = jnp.bfloat16


def _nn(a, b):
    return jnp.dot(a, b, preferred_element_type=F32)


def _nt(a, b):
    return lax.dot_general(a, b, (((1,), (1,)), ((), ())), preferred_element_type=F32)


def _tn(a, b):
    return lax.dot_general(a, b, (((0,), (0,)), ((), ())), preferred_element_type=F32)


def _bf(x):
    return x.astype(BF16)


def _sigmoid(x):
    return 1.0 / (1.0 + jnp.exp(-x))


def _rms(xv, g):
    ms = jnp.mean(xv * xv, axis=-1, keepdims=True)
    return xv * lax.rsqrt(ms + RMS_EPS) * g


def _split3(x):
    hi = x.astype(BF16)
    r1 = x - hi.astype(F32)
    mid = r1.astype(BF16)
    lo = (r1 - mid.astype(F32)).astype(BF16)
    return hi, mid, lo


def _params(*sem):
    return pltpu.CompilerParams(dimension_semantics=sem, vmem_limit_bytes=VMEM_LIMIT)


def _ada_kernel(c_ref, w_ref, b_ref, o_ref):
    cv = c_ref[...]
    o_ref[0] = _nn(cv * _sigmoid(cv), w_ref[0]) + b_ref[0]


def _ada_mod(c, ada_w, ada_b):
    tn = 1536
    c8 = jnp.pad(c, ((0, SUBLANES - BATCH), (0, 0)))
    out = pl.pallas_call(
        _ada_kernel,
        grid=(DEPTH, N_ADA * D_MODEL // tn),
        in_specs=[
            pl.BlockSpec((SUBLANES, D_MODEL), lambda l, j: (0, 0)),
            pl.BlockSpec((1, D_MODEL, tn), lambda l, j: (l, 0, j)),
            pl.BlockSpec((1, 1, tn), lambda l, j: (l, 0, j)),
        ],
        out_specs=pl.BlockSpec((1, SUBLANES, tn), lambda l, j: (l, 0, j)),
        out_shape=jax.ShapeDtypeStruct((DEPTH, SUBLANES, N_ADA * D_MODEL), F32),
        compiler_params=_params("parallel", "parallel"),
        name="ada_mod",
    )(c8, ada_w, ada_b.reshape(DEPTH, 1, N_ADA * D_MODEL))
    return out[:, :BATCH].reshape(DEPTH, BATCH, N_ADA, D_MODEL)


def _rwkv_proj_kernel(x_ref, xp_ref, mod_ref, gpre_ref, mu_ref, wrkv_ref, w1_ref, w2_ref, w0_ref,
                      a1_ref, a2_ref, a0_ref, g1_ref, g2_ref,
                      r_ref, k_ref, v_ref, lw_ref, a_ref, g_ref):
    i = pl.program_id(0)
    mod = mod_ref[0]
    shift, scale = mod[0:1], mod[1:2]
    gpre = gpre_ref[...]

    def modnorm(xv):
        return _rms(xv, gpre) * (1.0 + scale) + shift

    h = modnorm(x_ref[...])
    hp = modnorm(xp_ref[...])[SUBLANES - 1:SUBLANES]
    hp = jnp.where(i % (SEQ // ROW_TILE) == 0, 0.0, hp)
    row = lax.broadcasted_iota(jnp.int32, h.shape, 0)
    hs = jnp.where(row == 0, hp, pltpu.roll(h, 1, 0))
    xx = hs - h
    mu = mu_ref[...]

    def mix(n):
        return _bf(h + xx * mu[n:n + 1])

    r_ref[...] = _bf(_nn(mix(0), wrkv_ref[0]))
    k_ref[...] = _bf(_nn(mix(1), wrkv_ref[1]))
    v_ref[...] = _bf(_nn(mix(2), wrkv_ref[2]))
    wl = w0_ref[...] + _nn(_bf(jnp.tanh(_nn(mix(3), w1_ref[...]))), w2_ref[...])
    nwl = -wl
    sp = jnp.maximum(nwl, 0.0) + jnp.log(1.0 + jnp.exp(-jnp.abs(nwl)))
    lw_ref[...] = -jnp.exp(-sp - 0.5)
    a_ref[...] = _sigmoid(a0_ref[...] + _nn(_bf(_nn(mix(4), a1_ref[...])), a2_ref[...]))
    g_ref[...] = _bf(_nn(_bf(_sigmoid(_nn(mix(5), g1_ref[...]))), g2_ref[...]))


def _rwkv_proj(x, mod, gpre, mu, w_rkv, w1, w2, w0, a1, a2, a0, g1, g2):
    tm = ROW_TILE
    full = lambda shape: pl.BlockSpec(shape, lambda i: (0,) * len(shape))
    row_spec = pl.BlockSpec((tm, D_MODEL), lambda i: (i, 0))
    out_sds = [jax.ShapeDtypeStruct((TOKENS, D_MODEL), dt) for dt in (BF16, BF16, BF16, F32, F32, BF16)]
    return pl.pallas_call(
        _rwkv_proj_kernel,
        grid=(TOKENS // tm,),
        in_specs=[
            row_spec,
            pl.BlockSpec((SUBLANES, D_MODEL), lambda i: (jnp.maximum(i * (tm // SUBLANES) - 1, 0), 0)),
            pl.BlockSpec((1, N_ADA, D_MODEL), lambda i: (i // (SEQ // tm), 0, 0)),
            full((1, D_MODEL)), full((6, D_MODEL)), full((3, D_MODEL, D_MODEL)),
            full(w1.shape), full(w2.shape), full((1, D_MODEL)),
            full(a1.shape), full(a2.shape), full((1, D_MODEL)),
            full(g1.shape), full(g2.shape),
        ],
        out_specs=[row_spec] * 6,
        out_shape=out_sds,
        compiler_params=_params("parallel"),
        name="rwkv_proj",
    )(x, x, mod, gpre.reshape(1, D_MODEL), mu, _bf(w_rkv), _bf(w1), _bf(w2),
      w0.reshape(1, D_MODEL), _bf(a1), _bf(a2), a0.reshape(1, D_MODEL), _bf(g1), _bf(g2))


def _wkv_pair(r, k, v, lw, a, k_k, k_a, r_k, ln_g, ln_b, s_ref, o_ref):
    c = WKV_CHUNK
    lane = lax.broadcasted_iota(jnp.int32, (c, LANES), 1)
    head0 = lane < HEAD_DIM
    rowi = lax.broadcasted_iota(jnp.int32, (LANES, LANES), 0)
    coli = lax.broadcasted_iota(jnp.int32, (LANES, LANES), 1)
    rt, ct = rowi & (c - 1), coli & (c - 1)
    strict, incl, eye = rt > ct, rt >= ct, rowi == coli
    tril = (lax.broadcasted_iota(jnp.int32, (c, c), 0)
            >= lax.broadcasted_iota(jnp.int32, (c, c), 1)).astype(BF16)

    def headsum(x):
        s0 = jnp.sum(jnp.where(head0, x, 0.0), axis=-1, keepdims=True)
        s1 = jnp.sum(jnp.where(head0, 0.0, x), axis=-1, keepdims=True)
        return jnp.where(head0, s0, s1)

    def stack(x):
        return jnp.concatenate([jnp.where(head0, x, 0.0), jnp.where(head0, 0.0, x)], axis=0)

    kk = k * k_k
    kk = kk / jnp.maximum(jnp.sqrt(headsum(kk * kk)), 1e-12)
    kf = k * (1.0 + (a - 1.0) * k_a)
    bvec = kk * a
    yield

    cum = _nn(tril, jnp.concatenate(_split3(lw), axis=1))
    cum = cum[:, :LANES] + cum[:, LANES:2 * LANES] + cum[:, 2 * LANES:]
    cum_end = cum[c - 1:c]
    w_inv = jnp.exp(-cum)
    w_rem = jnp.exp(cum_end - cum)
    a_st = stack(-kk * jnp.exp(cum - lw))
    r_st = stack(r * jnp.exp(cum))
    b_st = stack(bvec * w_inv)
    k_st = stack(kf * w_inv)
    bp_st = _bf(stack(bvec * w_rem))
    kp_st = _bf(stack(kf * w_rem))
    v_st = _bf(stack(v))
    yield

    sc = _nt(_bf(jnp.concatenate([a_st, r_st], axis=0)), _bf(jnp.concatenate([b_st, k_st], axis=0)))
    l_ab = jnp.where(strict, sc[:LANES, :LANES], 0.0)
    l_ak = jnp.where(strict, sc[:LANES, LANES:], 0.0)
    l_rb = jnp.where(incl, sc[LANES:, :LANES], 0.0)
    l_rk = jnp.where(incl, sc[LANES:, LANES:], 0.0)
    yield

    tinv = jnp.where(eye, 1.0, l_ab)
    pw = l_ab
    for _ in range(c.bit_length() - 2):
        pwb = _bf(pw)
        pw = _nn(pwb, pwb)
        tinv = tinv + _nn(_bf(tinv), _bf(pw))
        yield

    au = _nn(_bf(tinv), _bf(jnp.concatenate([a_st, _nn(_bf(l_ak), v_st)], axis=1)))
    aub = _bf(au)
    yield
    rhs = jnp.concatenate([aub, jnp.concatenate([jnp.zeros_like(v_st), v_st], axis=1)], axis=0)
    rb = _nn(_bf(jnp.concatenate([l_rb, l_rk], axis=1)), rhs)
    gh = _tn(jnp.concatenate([bp_st, kp_st], axis=0), rhs)
    r_hat = r_st + rb[:, :LANES]
    o0 = rb[:, LANES:]
    g_mat = jnp.where(eye, jnp.exp(cum_end), 0.0) + gh[:, :LANES]
    h_mat = gh[:, LANES:]
    yield

    os = _nn(_bf(jnp.concatenate([r_hat, g_mat], axis=0)), _bf(s_ref[...]))
    s_ref[...] = os[LANES:] + h_mat
    o_st = os[:LANES] + o0
    o = o_st[:c] + o_st[c:]
    yield

    mean = headsum(o) * (1.0 / HEAD_DIM)
    oc = o - mean
    var = headsum(oc * oc) * (1.0 / HEAD_DIM)
    on = oc * lax.rsqrt(var + LNX_EPS) * ln_g + ln_b
    o_ref[...] = on + headsum(r * kf * r_k) * v


def _wkv_kernel(r_ref, k_ref, v_ref, lw_ref, a_ref, kk_ref, ka_ref, rk_ref, lg_ref, lb_ref,
                o_ref, s_ref):
    @pl.when(pl.program_id(2) == 0)
    def _():
        s_ref[...] = jnp.zeros_like(s_ref)

    def pair(b, p):
        sl = pl.ds(p * LANES, LANES)
        acts = [ref[b, :, sl].astype(F32) for ref in (r_ref, k_ref, v_ref, lw_ref, a_ref)]
        pars = [ref[:, sl] for ref in (kk_ref, ka_ref, rk_ref, lg_ref, lb_ref)]
        return _wkv_pair(*acts, *pars, s_ref.at[b * WKV_PAIRS + p], o_ref.at[b, :, sl])

    stages = [pair(b, p) for b in range(WKV_SEQS) for p in range(WKV_PAIRS)]
    while stages:
        stages = [g for g in stages if next(g, True) is None]


def _wkv(r, k, v, lw, a, k_k, k_a, r_k, lnx_g, lnx_b):
    n_chunks = SEQ // WKV_CHUNK
    width = WKV_PAIRS * LANES
    tok = pl.BlockSpec((WKV_SEQS, WKV_CHUNK, width), lambda b, p, c: (b, c, p))
    par = pl.BlockSpec((1, width), lambda b, p, c: (0, p))
    vec = lambda t: t.reshape(1, D_MODEL)
    seq = lambda t: t.reshape(BATCH, SEQ, D_MODEL)
    out = pl.pallas_call(
        _wkv_kernel,
        grid=(BATCH // WKV_SEQS, N_PAIRS // WKV_PAIRS, n_chunks),
        in_specs=[tok] * 5 + [par] * 5,
        out_specs=tok,
        out_shape=jax.ShapeDtypeStruct((BATCH, SEQ, D_MODEL), F32),
        scratch_shapes=[pltpu.VMEM((WKV_SEQS * WKV_PAIRS, LANES, LANES), F32)],
        compiler_params=_params("parallel", "parallel", "arbitrary"),
        name="wkv_scan",
    )(seq(r), seq(k), seq(v), seq(lw), seq(a), vec(k_k), vec(k_a), vec(r_k), vec(lnx_g), vec(lnx_b))
    return out.reshape(TOKENS, D_MODEL)


def _pack_rows(x):
    bits = lax.bitcast_convert_type(_bf(x).astype(F32), jnp.uint32)
    halves = []
    for h in range(PACK_HALVES):
        lo = bits[:, (2 * h) * PACK_WORDS:(2 * h + 1) * PACK_WORDS]
        hi = bits[:, (2 * h + 1) * PACK_WORDS:(2 * h + 2) * PACK_WORDS]
        halves.append(lax.bitcast_convert_type((lo >> 16) | (hi & jnp.uint32(0xFFFF0000)), jnp.int32))
    return halves


def _unpack_rows(halves):
    parts = []
    for w in halves:
        u = lax.bitcast_convert_type(w, jnp.uint32)
        parts.append(lax.bitcast_convert_type(u << 16, F32))
        parts.append(lax.bitcast_convert_type(u & jnp.uint32(0xFFFF0000), F32))
    return jnp.concatenate(parts, axis=1)


def _post_kernel(*refs, has_gate):
    if has_gate:
        o_ref, g_ref, *refs = refs
    else:
        o_ref, *refs = refs
    (x_ref, mod_ref, wo_ref, gpost_ref, gpre_ref, wr_hi_ref, wr_lo_ref, bias_ref,
     x1_ref, h2p_ref, gk_ref, ek_ref, rk_ref, cnt_ref, run_ref) = refs
    tm = ROW_TILE

    @pl.when(pl.program_id(0) == 0)
    def _():
        run_ref[...] = jnp.zeros_like(run_ref)

    mod = mod_ref[0]
    gate_m, shift_f, scale_f = mod[2:3], mod[3:4], mod[4:5]
    o = o_ref[...]
    if has_gate:
        o = o * g_ref[...].astype(F32)
    y = _nn(_bf(o), wo_ref[...])
    x1 = x_ref[...] + gate_m * _rms(y, gpost_ref[...])
    x1_ref[...] = x1
    h2 = _rms(x1, gpre_ref[...]) * (1.0 + scale_f) + shift_f
    for h, words in enumerate(_pack_rows(h2)):
        h2p_ref[h] = words

    hi = _bf(h2)
    lo = _bf(h2 - hi.astype(F32))
    wr_hi, wr_lo = wr_hi_ref[...], wr_lo_ref[...]
    scores = _sigmoid(_nt(wr_hi, hi) + (_nt(wr_hi, lo) + _nt(wr_lo, hi)))
    erow = lax.broadcasted_iota(jnp.int32, scores.shape, 0)
    work = scores + jnp.concatenate([bias_ref[...]] * (tm // LANES), axis=1)
    picked = jnp.zeros_like(scores)
    chosen = jnp.zeros_like(scores)
    hits, ids = [], []
    for _ in range(TOP_K):
        best = jnp.max(work, axis=0, keepdims=True)
        idx = jnp.min(jnp.where(work == best, erow, N_EXPERTS), axis=0, keepdims=True)
        hit = erow == idx
        picked = jnp.where(hit, scores, picked)
        chosen = jnp.where(hit, 1.0, chosen)
        work = jnp.where(hit, -jnp.inf, work)
        hits.append(hit)
        ids.append(idx)
    gates = ROUTED_SCALE * picked / jnp.sum(picked, axis=0, keepdims=True)

    upto = (lax.broadcasted_iota(jnp.int32, (tm, tm), 0)
            <= lax.broadcasted_iota(jnp.int32, (tm, tm), 1)).astype(BF16)
    chosen_b = _bf(chosen)
    run = run_ref[...]
    before = _nn(chosen_b, upto) - chosen + jnp.concatenate([run] * (tm // LANES), axis=1)
    run = run + _nn(chosen_b, jnp.ones((tm, LANES), BF16))
    run_ref[...] = run
    cnt_ref[...] = run

    krow = lax.broadcasted_iota(jnp.int32, (TOP_K, tm), 0)
    gk = jnp.zeros((TOP_K, tm), F32)
    ek = jnp.zeros((TOP_K, tm), jnp.int32)
    rk = jnp.zeros((TOP_K, tm), jnp.int32)
    for j in range(TOP_K):
        grow = jnp.sum(jnp.where(hits[j], gates, 0.0), axis=0, keepdims=True)
        rrow = jnp.sum(jnp.where(hits[j], before, 0.0), axis=0, keepdims=True)
        gk = jnp.where(krow == j, grow, gk)
        ek = jnp.where(krow == j, ids[j], ek)
        rk = jnp.where(krow == j, rrow.astype(jnp.int32), rk)
    gk_ref[...] = gk
    ek_ref[...] = ek
    rk_ref[...] = rk


def _post(o, g, x, mod, w_o, gpost, gpre, w_router, router_bias):
    tm = ROW_TILE
    full = lambda shape: pl.BlockSpec(shape, lambda i: (0,) * len(shape))
    row_spec = pl.BlockSpec((tm, D_MODEL), lambda i: (i, 0))
    k_spec = pl.BlockSpec((TOP_K, tm), lambda i: (0, i))
    wr = w_router.T
    wr_hi = _bf(wr)
    wr_lo = _bf(wr - wr_hi.astype(F32))
    bias = jnp.broadcast_to(router_bias[:, None], (N_EXPERTS, LANES))
    has_gate = g is not None
    acts = [o, g] if has_gate else [o]
    return pl.pallas_call(
        functools.partial(_post_kernel, has_gate=has_gate),
        grid=(TOKENS // tm,),
        in_specs=[row_spec] * (len(acts) + 1) + [
            pl.BlockSpec((1, N_ADA, D_MODEL), lambda i: (i // (SEQ // tm), 0, 0)),
            full((D_MODEL, D_MODEL)), full((1, D_MODEL)), full((1, D_MODEL)),
            full((N_EXPERTS, D_MODEL)), full((N_EXPERTS, D_MODEL)), full((N_EXPERTS, LANES)),
        ],
        out_specs=[row_spec, pl.BlockSpec((PACK_HALVES, tm, PACK_WORDS), lambda i: (0, i, 0)),
                   k_spec, k_spec, k_spec, full((N_EXPERTS, LANES))],
        out_shape=[jax.ShapeDtypeStruct((TOKENS, D_MODEL), F32),
                   jax.ShapeDtypeStruct((PACK_HALVES, TOKENS, PACK_WORDS), jnp.int32),
                   jax.ShapeDtypeStruct((TOP_K, TOKENS), F32),
                   jax.ShapeDtypeStruct((TOP_K, TOKENS), jnp.int32),
                   jax.ShapeDtypeStruct((TOP_K, TOKENS), jnp.int32),
                   jax.ShapeDtypeStruct((N_EXPERTS, LANES), F32)],
        scratch_shapes=[pltpu.VMEM((N_EXPERTS, LANES), F32)],
        compiler_params=_params("arbitrary"),
        name="mixer_out_router",
    )(*acts, x, mod, _bf(w_o), gpost.reshape(1, D_MODEL), gpre.reshape(1, D_MODEL),
      wr_hi, wr_lo, bias)


def _sc_mesh():
    return plsc.VectorSubcoreMesh(core_axis_name="core", subcore_axis_name="subcore")


def _sc_scatter_rows(rows, idx, n_out):
    n_copies = idx.shape[0]

    @pl.kernel(out_type=jax.ShapeDtypeStruct((PACK_HALVES, n_out, PACK_WORDS), rows.dtype),
               mesh=_sc_mesh(), scratch_types=[], name="sc_dispatch_rows")
    def scatter(x_hbm, i_hbm, o_hbm):
        for h in range(PACK_HALVES):
            def body(x_vmem, i_vmem, h=h):
                for k in range(n_copies):
                    pltpu.sync_copy(x_vmem, o_hbm.at[h].at[i_vmem.at[k]])

            pltpu.emit_pipeline(
                body, grid=(rows.shape[1] // SC_WINDOW,),
                in_specs=[pl.BlockSpec((SC_WINDOW, PACK_WORDS), index_map=lambda i: (i, 0)),
                          pl.BlockSpec((n_copies, SC_WINDOW), index_map=lambda i: (0, i))],
                out_specs=[],
                core_axis_name=("core", "subcore"),
                dimension_semantics=(pltpu.PARALLEL,),
            )(x_hbm.at[h], i_hbm)

    return scatter(rows, idx)


def _sc_gather_rows(tables, idx):
    n = idx.shape[0]

    @pl.kernel(out_type=jax.ShapeDtypeStruct((PACK_HALVES, n, PACK_WORDS), tables[0].dtype),
               mesh=_sc_mesh(), scratch_types=[], name="sc_collect_rows")
    def gather(*refs):
        x_hbm, (i_hbm, o_hbm) = refs[:PACK_HALVES], refs[PACK_HALVES:]
        for h in range(PACK_HALVES):
            def body(i_vmem, o_vmem, h=h):
                pltpu.sync_copy(x_hbm[h].at[i_vmem.at[0]], o_vmem)

            pltpu.emit_pipeline(
                body, grid=(n // SC_WINDOW,),
                in_specs=[pl.BlockSpec((1, SC_WINDOW), index_map=lambda i: (0, i))],
                out_specs=[pl.BlockSpec((SC_WINDOW, PACK_WORDS), index_map=lambda i: (i, 0))],
                core_axis_name=("core", "subcore"),
                dimension_semantics=(pltpu.PARALLEL,),
            )(i_hbm, o_hbm.at[h])

    return gather(*tables, idx.reshape(1, n))


def _ffn(x, wg, wu, wd):
    up = _nn(x, _bf(wu))
    gt = _nn(x, _bf(wg))
    return _nn(_bf(gt * _sigmoid(gt) * up), _bf(wd))


def _expert_kernel(be_ref, nu_ref, *refs):
    n_in = PACK_HALVES * EXPERT_SPLIT
    x_refs = refs[:n_in]
    wg_ref, wu_ref, wd_ref = refs[n_in:n_in + 3]
    y_refs = refs[n_in + 3:]

    @pl.when(pl.program_id(0) < nu_ref[0])
    def _():
        halves = [jnp.concatenate([x_refs[h * EXPERT_SPLIT + j][0] for j in range(EXPERT_SPLIT)], axis=0)
                  for h in range(PACK_HALVES)]
        y = _ffn(_bf(_unpack_rows(halves)), wg_ref[0], wu_ref[0], wd_ref[0])
        for h, words in enumerate(_pack_rows(y)):
            y_refs[h][...] = words


def _experts(layer, xs, blk_expert, n_used, w_gate, w_up, w_down):
    n_rows = xs.shape[1]
    n_blocks = n_rows // EXPERT_BLOCK
    part = EXPERT_BLOCK // EXPERT_SPLIT
    x_spec = lambda h, j: pl.BlockSpec((1, part, PACK_WORDS), lambda b, be, nu: (h, b * EXPERT_SPLIT + j, 0))
    w_spec = lambda shape: pl.BlockSpec((1,) + shape, lambda b, be, nu: (be[b], 0, 0))
    y_spec =pl.BlockSpec((EXPERT_BLOCK, PACK_WORDS), lambda b, be, nu: (b, 0))
    n_in = PACK_HALVES * EXPERT_SPLIT
    return pl.pallas_call(
        _expert_kernel,
        grid_spec=pltpu.PrefetchScalarGridSpec(
            num_scalar_prefetch=2,
            grid=(n_blocks,),
            in_specs=[x_spec(h, j) for h in range(PACK_HALVES) for j in range(EXPERT_SPLIT)]
                     + [w_spec((D_MODEL, D_EXPERT)), w_spec((D_MODEL, D_EXPERT)), w_spec((D_EXPERT, D_MODEL))],
            out_specs=[y_spec] * PACK_HALVES,
        ),
        out_shape=[jax.ShapeDtypeStruct((n_rows, PACK_WORDS), xs.dtype)] * PACK_HALVES,
        compiler_params=_params("arbitrary"),
        name="moe_experts",
    )(blk_expert, n_used, *([xs] * n_in), _bf(w_gate[layer]), _bf(w_up[layer]), _bf(w_down[layer]))


def _shared_kernel(h2p_ref, sg_ref, su_ref, sd_ref, o_ref):
    h2 = _bf(_unpack_rows([h2p_ref[h] for h in range(PACK_HALVES)]))
    for h, words in enumerate(_pack_rows(_ffn(h2, sg_ref[0], su_ref[0], sd_ref[0]))):
        o_ref[h] = words


def _shared_expert(layer, h2p, ws_gate, ws_up, ws_down):
    tm = SHARED_TILE
    row_spec = pl.BlockSpec((PACK_HALVES, tm, PACK_WORDS), lambda i: (0, i, 0))
    return pl.pallas_call(
        _shared_kernel,
        grid=(TOKENS // tm,),
        in_specs=[row_spec,
                  pl.BlockSpec((1, D_MODEL, D_EXPERT), lambda i: (layer, 0, 0)),
                  pl.BlockSpec((1, D_MODEL, D_EXPERT), lambda i: (layer, 0, 0)),
                  pl.BlockSpec((1, D_EXPERT, D_MODEL), lambda i: (layer, 0, 0))],
        out_specs=row_spec,
        out_shape=jax.ShapeDtypeStruct(h2p.shape, h2p.dtype),
        compiler_params=_params("parallel"),
        name="moe_shared_expert",
    )(h2p, ws_gate, ws_up, ws_down)


def _combine_kernel(*refs):
    g_refs = refs[:TOP_K]
    gk_ref, sh_ref, x_ref, mod_ref, gpost_ref, o_ref = refs[TOP_K:]
    acc = _unpack_rows([sh_ref[h] for h in range(PACK_HALVES)])
    gk = gk_ref[...].T
    for j in range(TOP_K):
        acc = acc + gk[:, j:j + 1] * _unpack_rows([g_refs[j][h] for h in range(PACK_HALVES)])
    gate_f = mod_ref[0][5:6]
    o_ref[...] = x_ref[...] + gate_f * _rms(acc, gpost_ref[...])


def _combine(picked, gk, shared, x1, mod, gpost):
    tm = ROW_TILE
    n_tiles = TOKENS // tm
    row_spec = pl.BlockSpec((tm, D_MODEL), lambda i: (i, 0))
    pick_spec = lambda j: pl.BlockSpec((PACK_HALVES, tm, PACK_WORDS), lambda i: (0, j * n_tiles + i, 0))
    return pl.pallas_call(
        _combine_kernel,
        grid=(n_tiles,),
        in_specs=[pick_spec(j) for j in range(TOP_K)] + [
            pl.BlockSpec((TOP_K, tm), lambda i: (0, i)),
            pl.BlockSpec((PACK_HALVES, tm, PACK_WORDS), lambda i: (0, i, 0)),
            row_spec,
            pl.BlockSpec((1, N_ADA, D_MODEL), lambda i: (i // (SEQ // tm), 0, 0)),
            pl.BlockSpec((1, D_MODEL), lambda i: (0, 0)),
        ],
        out_specs=row_spec,
        out_shape=jax.ShapeDtypeStruct((TOKENS, D_MODEL), F32),
        compiler_params=_params("parallel"),
        name="moe_combine",
    )(*([picked] * TOP_K), gk, shared, x1, mod, gpost.reshape(1, D_MODEL))


def _moe(layer, x1, h2p, gk, ek, rk, counts, mod, gpost, w_gate, w_up, w_down, ws_gate, ws_up, ws_down):
    cnt = counts[:, 0].astype(jnp.int32)
    padded = (cnt + EXPERT_BLOCK - 1) // EXPERT_BLOCK * EXPERT_BLOCK
    eid = jnp.arange(N_EXPERTS, dtype=jnp.int32)
    pend = jnp.sum(jnp.where(eid[:, None] <= eid[None, :], padded[:, None], 0), axis=0)
    pstart = pend - padded
    dest = rk + jnp.sum(jnp.where(ek[None] == eid[:, None, None], pstart[:, None, None], 0), axis=0)
    n_blocks = (TOKENS * TOP_K + N_EXPERTS * (EXPERT_BLOCK - 1)) // EXPERT_BLOCK + 1
    blk_start = jnp.arange(n_blocks, dtype=jnp.int32) * EXPERT_BLOCK
    blk_expert = jnp.minimum(jnp.sum((blk_start[None, :] >= pend[:, None]).astype(jnp.int32), axis=0),
                             N_EXPERTS - 1)
    n_used = pend[-1:] // EXPERT_BLOCK

    xs = _sc_scatter_rows(h2p, dest, n_blocks * EXPERT_BLOCK)
    shared = _shared_expert(layer, h2p, ws_gate, ws_up, ws_down)
    ys = _experts(layer, xs, blk_expert, n_used, w_gate, w_up, w_down)
    picked = _sc_gather_rows(ys, dest.reshape(-1))
    return _combine(picked, gk, shared, x1, mod, gpost)


def _sb_proj_kernel(x_ref, mod_ref, gpre_ref, w_ref, q_ref, k_ref, v_ref):
    mod = mod_ref[0]
    h = _bf(_rms(x_ref[...], gpre_ref[...]) * (1.0 + mod[1:2]) + mod[0:1])
    q_ref[...] = _bf(_nn(h, w_ref[:, :D_MODEL]) * (LOG2_E / 8.0))
    k_ref[...] = _bf(_nn(h, w_ref[:, D_MODEL:2 * D_MODEL]))
    v_ref[...] = _bf(_nn(h, w_ref[:, 2 * D_MODEL:]))


def _sb_proj(x, mod, gpre, w_qkv):
    tm = ROW_TILE
    full = lambda shape: pl.BlockSpec(shape, lambda i: (0,) * len(shape))
    row_spec = pl.BlockSpec((tm, D_MODEL), lambda i: (i, 0))
    out_sds = jax.ShapeDtypeStruct((TOKENS, D_MODEL), BF16)
    return pl.pallas_call(
        _sb_proj_kernel,
        grid=(TOKENS // tm,),
        in_specs=[row_spec,
                  pl.BlockSpec((1, N_ADA, D_MODEL), lambda i: (i // (SEQ // tm), 0, 0)),
                  full((1, D_MODEL)), full((D_MODEL, 3 * D_MODEL))],
        out_specs=[row_spec] * 3,
        out_shape=[out_sds] * 3,
        compiler_params=_params("parallel"),
        name="sb_qkv_proj",
    )(x, mod, gpre.reshape(1, D_MODEL), _bf(w_qkv))


def _sb_attn_kernel(q_ref, k_ref, v_ref, o_ref):
    blk = ATT_BLOCK
    n_heads = LANES // HEAD_DIM
    qb = pl.program_id(2)
    q = q_ref[...]
    lane = lax.broadcasted_iota(jnp.int32, (blk, LANES), 1)
    head0 = lane < HEAD_DIM
    qh = [jnp.where(head0, q, jnp.zeros_like(q)), jnp.where(head0, jnp.zeros_like(q), q)]
    rowi = lax.broadcasted_iota(jnp.int32, (blk, blk), 0)
    coli = lax.broadcasted_iota(jnp.int32, (blk, blk), 1)
    later = (rowi > coli).astype(BF16)

    sub = ATT_SUB
    n_sub = blk // sub
    sub_row = lax.broadcasted_iota(jnp.int32, (sub, blk), 0)
    sub_col = lax.broadcasted_iota(jnp.int32, (sub, blk), 1)
    below = [sub_col < sub_row + r * sub for r in range(n_sub)]
    sub_head0 = lax.broadcasted_iota(jnp.int32, (sub, LANES), 1) < HEAD_DIM
    chains = [(hd, r) for hd in range(n_heads) for r in range(n_sub)]

    def chain(hd, r, ks, vs, get_carry, get_acc, mode, out):
        rows = slice(r * sub, (r + 1) * sub)
        u = _nt(qh[hd][rows], ks)
        yield
        neg_abs = lax.bitcast_convert_type(
            lax.bitcast_convert_type(u, jnp.uint32) | jnp.uint32(0x80000000), F32)
        lb = jnp.minimum(u, 0.0) - jnp.log2(1.0 + jnp.exp2(neg_abs))
        l1m = lb - u
        if mode == "diag":
            l1m = jnp.where(below[r], l1m, 0.0)
        l1b = _bf(l1m)
        yield
        carry = get_carry()
        tail = _nn(l1b, later) + jnp.concatenate([carry] * (blk // LANES), axis=1)
        out["carry"] = carry + jnp.sum(l1m, axis=-1, keepdims=True)
        yield
        w = jnp.exp2(lb + tail)
        if mode == "diag":
            w = jnp.where(below[r], w, 0.0)
        elif mode == "prev":
            w = jnp.where(qb > 0, w, 0.0)
        wb = _bf(w)
        yield
        out["acc"] = get_acc() + _nn(wb, vs)

    def load_kv(kb):
        start = pl.multiple_of(kb * blk, blk)
        return k_ref[pl.ds(start, blk), :], v_ref[pl.ds(start, blk), :]

    def run(gens):
        while gens:
            gens = [g for g in gens if next(g, True) is None]

    def block(kb, carry, acc):
        ks, vs = load_kv(kb)
        outs = [{} for _ in chains]
        run([chain(hd, r, ks, vs, lambda c=c: carry[c], lambda c=c: acc[c], "plain", outs[c])
             for c, (hd, r) in enumerate(chains)])
        return [o["carry"] for o in outs], [o["acc"] for o in outs]

    zeros = jnp.zeros((sub, LANES), F32)
    ks_d, vs_d = load_kv(qb)
    ks_p, vs_p = load_kv(jnp.maximum(qb - 1, 0))
    outs_d = [{} for _ in chains]
    outs_p = [{} for _ in chains]
    run([chain(hd, r, ks_d, vs_d, lambda: zeros, lambda: zeros, "diag", outs_d[c])
         for c, (hd, r) in enumerate(chains)]
        + [chain(hd, r, ks_p, vs_p, lambda c=c: outs_d[c]["carry"], lambda c=c: outs_d[c]["acc"],
                 "prev", outs_p[c]) for c, (hd, r) in enumerate(chains)])
    carry = [o["carry"] for o in outs_p]
    acc = [o["acc"] for o in outs_p]

    def cmax_of(cr):
        return jnp.max(functools.reduce(jnp.maximum, cr))

    def cond(st):
        kb, _, _, cmax = st
        return jnp.logical_and(kb >= 0, cmax > EXP2_UNDERFLOW)

    def body(st):
        kb, cr, ac, _ = st
        cr, ac = block(kb, list(cr), list(ac))
        return kb - 1, tuple(cr), tuple(ac), cmax_of(cr)

    _, _, acc, _ = lax.while_loop(cond, body, (qb - 2, tuple(carry), tuple(acc), cmax_of(carry)))
    for r in range(n_sub):
        rows = slice(r * sub, (r + 1) * sub)
        o_ref[rows, :] = jnp.where(sub_head0, acc[r], acc[n_sub + r])


def _sb_attn(q, k, v):
    blk = ATT_BLOCK
    n_blk = SEQ // blk
    q_spec = pl.BlockSpec((blk, LANES), lambda b, p, i: (b * n_blk + i, p))
    kv_spec = pl.BlockSpec((SEQ, LANES), lambda b, p, i: (b, p))
    return pl.pallas_call(
        _sb_attn_kernel,
        grid=(BATCH, N_PAIRS, n_blk),
        in_specs=[q_spec, kv_spec, kv_spec],
        out_specs=q_spec,
        out_shape=jax.ShapeDtypeStruct((TOKENS, D_MODEL), F32),
        compiler_params=_params("parallel", "parallel", "arbitrary"),
        name="sb_attention",
    )(q, k, v)


def kernel(x, c, ada_w, ada_b, norm_pre_mix, norm_post_mix, norm_pre_ffn, norm_post_ffn, rwkv_mu, rwkv_w_rkv, rwkv_w_w1, rwkv_w_w2, rwkv_w0, rwkv_w_a1, rwkv_w_a2, rwkv_a0, rwkv_w_g1, rwkv_w_g2, rwkv_k_k, rwkv_k_a, rwkv_r_k, rwkv_lnx_g, rwkv_lnx_b, rwkv_w_o, sb_w_qkv, sb_w_o, moe_w_router, moe_router_bias, moe_w_gate, moe_w_up, moe_w_down, moe_ws_gate, moe_ws_up, moe_ws_down):
    mod_all = _ada_mod(c, ada_w, ada_b)
    xt = x.reshape(TOKENS, D_MODEL)
    for layer in range(DEPTH):
        mod = mod_all[layer]
        i = layer // 2
        if layer % 2 == 0:
            r, k, v, lw, a, g = _rwkv_proj(
                xt, mod, norm_pre_mix[layer], rwkv_mu[i], rwkv_w_rkv[i], rwkv_w_w1[i], rwkv_w_w2[i],
                rwkv_w0[i], rwkv_w_a1[i], rwkv_w_a2[i], rwkv_a0[i], rwkv_w_g1[i], rwkv_w_g2[i])
            o = _wkv(r, k, v, lw, a, rwkv_k_k[i], rwkv_k_a[i], rwkv_r_k[i], rwkv_lnx_g[i], rwkv_lnx_b[i])
            w_o = rwkv_w_o[i]
        else:
            q, k, v = _sb_proj(xt, mod, norm_pre_mix[layer], sb_w_qkv[i])
            o, g = _sb_attn(q, k, v), None
            w_o = sb_w_o[i]
        x1, h2p, gk, ek, rk, counts = _post(o, g, xt, mod, w_o, norm_post_mix[layer], norm_pre_ffn[layer],
                                            moe_w_router[layer], moe_router_bias[layer])
        xt = _moe(layer, x1, h2p, gk, ek, rk, counts, mod, norm_post_ffn[layer], moe_w_gate, moe_w_up,
                  moe_w_down, moe_ws_gate, moe_ws_up, moe_ws_down)
    return xt.reshape(BATCH, SEQ, D_MODEL)
```

```python
import functools

import jax
import jax.numpy as jnp
from jax import lax
from jax.experimental import pallas as pl
from jax.experimental.pallas import tpu as pltpu
from jax.experimental.pallas import tpu_sc as plsc

D_MODEL = 1024
BATCH = 2
SEQ = 8192
TOKENS = BATCH * SEQ
DEPTH = 2
HEAD_DIM = 64
N_EXPERTS = 64
TOP_K = 8
D_EXPERT = 256
ROUTED_SCALE = 2.5
RMS_EPS = 1e-6
LNX_EPS = 64e-5
N_ADA = 6

LANES = 128
SUBLANES = 8
N_PAIRS = D_MODEL // LANES
WKV_CHUNK = 64
WKV_PAIRS = 8
WKV_SEQS = 2
ROW_TILE = 512
SHARED_TILE = 512
EXPERT_BLOCK = 1024
EXPERT_SPLIT = 2
PACK_HALVES = 2
PACK_WORDS = D_MODEL // (2 * PACK_HALVES)
SC_WINDOW = 128
ATT_BLOCK = 256
ATT_SUB = 128
LOG2_E = 1.4426950408889634
EXP2_UNDERFLOW = -153.0
VMEM_LIMIT = 56 * 1024 * 1024

F32 = jnp.float32
BF16 = jnp.bfloat16


def _nn(a, b):
    return jnp.dot(a, b, preferred_element_type=F32)


def _nt(a, b):
    return lax.dot_general(a, b, (((1,), (1,)), ((), ())), preferred_element_type=F32)


def _tn(a, b):
    return lax.dot_general(a, b, (((0,), (0,)), ((), ())), preferred_element_type=F32)


def _bf(x):
    return x.astype(BF16)


def _sigmoid(x):
    return 1.0 / (1.0 + jnp.exp(-x))


def _rms(xv, g):
    ms = jnp.mean(xv * xv, axis=-1, keepdims=True)
    return xv * lax.rsqrt(ms + RMS_EPS) * g


def _split3(x):
    hi = x.astype(BF16)
    r1 = x - hi.astype(F32)
    mid = r1.astype(BF16)
    lo = (r1 - mid.astype(F32)).astype(BF16)
    return hi, mid, lo


def _params(*sem):
    return pltpu.CompilerParams(dimension_semantics=sem, vmem_limit_bytes=VMEM_LIMIT)


def _ada_kernel(c_ref, w_ref, b_ref, o_ref):
    cv = c_ref[...]
    o_ref[0] = _nn(cv * _sigmoid(cv), w_ref[0]) + b_ref[0]


def _ada_mod(c, ada_w, ada_b):
    tn = 1536
    c8 = jnp.pad(c, ((0, SUBLANES - BATCH), (0, 0)))
    out = pl.pallas_call(
        _ada_kernel,
        grid=(DEPTH, N_ADA * D_MODEL // tn),
        in_specs=[
            pl.BlockSpec((SUBLANES, D_MODEL), lambda l, j: (0, 0)),
            pl.BlockSpec((1, D_MODEL, tn), lambda l, j: (l, 0, j)),
            pl.BlockSpec((1, 1, tn), lambda l, j: (l, 0, j)),
        ],
        out_specs=pl.BlockSpec((1, SUBLANES, tn), lambda l, j: (l, 0, j)),
        out_shape=jax.ShapeDtypeStruct((DEPTH, SUBLANES, N_ADA * D_MODEL), F32),
        compiler_params=_params("parallel", "parallel"),
        name="ada_mod",
    )(c8, ada_w, ada_b.reshape(DEPTH, 1, N_ADA * D_MODEL))
    return out[:, :BATCH].reshape(DEPTH, BATCH, N_ADA, D_MODEL)


def _rwkv_proj_kernel(x_ref, xp_ref, mod_ref, gpre_ref, mu_ref, wrkv_ref, w1_ref, w2_ref, w0_ref,
                      a1_ref, a2_ref, a0_ref, g1_ref, g2_ref,
                      r_ref, k_ref, v_ref, lw_ref, a_ref, g_ref):
    i = pl.program_id(0)
    mod = mod_ref[0]
    shift, scale = mod[0:1], mod[1:2]
    gpre = gpre_ref[...]

    def modnorm(xv):
        return _rms(xv, gpre) * (1.0 + scale) + shift

    h = modnorm(x_ref[...])
    hp = modnorm(xp_ref[...])[SUBLANES - 1:SUBLANES]
    hp = jnp.where(i % (SEQ // ROW_TILE) == 0, 0.0, hp)
    row = lax.broadcasted_iota(jnp.int32, h.shape, 0)
    hs = jnp.where(row == 0, hp, pltpu.roll(h, 1, 0))
    xx = hs - h
    mu = mu_ref[...]

    def mix(n):
        return _bf(h + xx * mu[n:n + 1])

    r_ref[...] = _bf(_nn(mix(0), wrkv_ref[0]))
    k_ref[...] = _bf(_nn(mix(1), wrkv_ref[1]))
    v_ref[...] = _bf(_nn(mix(2), wrkv_ref[2]))
    wl = w0_ref[...] + _nn(_bf(jnp.tanh(_nn(mix(3), w1_ref[...]))), w2_ref[...])
    nwl = -wl
    sp = jnp.maximum(nwl, 0.0) + jnp.log(1.0 + jnp.exp(-jnp.abs(nwl)))
    lw_ref[...] = -jnp.exp(-sp - 0.5)
    a_ref[...] = _sigmoid(a0_ref[...] + _nn(_bf(_nn(mix(4), a1_ref[...])), a2_ref[...]))
    g_ref[...] = _bf(_nn(_bf(_sigmoid(_nn(mix(5), g1_ref[...]))), g2_ref[...]))


def _rwkv_proj(x, mod, gpre, mu, w_rkv, w1, w2, w0, a1, a2, a0, g1, g2):
    tm = ROW_TILE
    full = lambda shape: pl.BlockSpec(shape, lambda i: (0,) * len(shape))
    row_spec = pl.BlockSpec((tm, D_MODEL), lambda i: (i, 0))
    out_sds = [jax.ShapeDtypeStruct((TOKENS, D_MODEL), dt) for dt in (BF16, BF16, BF16, F32, F32, BF16)]
    return pl.pallas_call(
        _rwkv_proj_kernel,
        grid=(TOKENS // tm,),
        in_specs=[
            row_spec,
            pl.BlockSpec((SUBLANES, D_MODEL), lambda i: (jnp.maximum(i * (tm // SUBLANES) - 1, 0), 0)),
            pl.BlockSpec((1, N_ADA, D_MODEL), lambda i: (i // (SEQ // tm), 0, 0)),
            full((1, D_MODEL)), full((6, D_MODEL)), full((3, D_MODEL, D_MODEL)),
            full(w1.shape), full(w2.shape), full((1, D_MODEL)),
            full(a1.shape), full(a2.shape), full((1, D_MODEL)),
            full(g1.shape), full(g2.shape),
        ],
        out_specs=[row_spec] * 6,
        out_shape=out_sds,
        compiler_params=_params("parallel"),
        name="rwkv_proj",
    )(x, x, mod, gpre.reshape(1, D_MODEL), mu, _bf(w_rkv), _bf(w1), _bf(w2),
      w0.reshape(1, D_MODEL), _bf(a1), _bf(a2), a0.reshape(1, D_MODEL), _bf(g1), _bf(g2))


def _wkv_pair(r, k, v, lw, a, k_k, k_a, r_k, ln_g, ln_b, s_ref, o_ref):
    c = WKV_CHUNK
    lane = lax.broadcasted_iota(jnp.int32, (c, LANES), 1)
    head0 = lane < HEAD_DIM
    rowi = lax.broadcasted_iota(jnp.int32, (LANES, LANES), 0)
    coli = lax.broadcasted_iota(jnp.int32, (LANES, LANES), 1)
    rt, ct = rowi & (c - 1), coli & (c - 1)
    strict, incl, eye = rt > ct, rt >= ct, rowi == coli
    tril = (lax.broadcasted_iota(jnp.int32, (c, c), 0)
            >= lax.broadcasted_iota(jnp.int32, (c, c), 1)).astype(BF16)

    def headsum(x):
        s0 = jnp.sum(jnp.where(head0, x, 0.0), axis=-1, keepdims=True)
        s1 = jnp.sum(jnp.where(head0, 0.0, x), axis=-1, keepdims=True)
        return jnp.where(head0, s0, s1)

    def stack(x):
        return jnp.concatenate([jnp.where(head0, x, 0.0), jnp.where(head0, 0.0, x)], axis=0)

    kk = k * k_k
    kk = kk / jnp.maximum(jnp.sqrt(headsum(kk * kk)), 1e-12)
    kf = k * (1.0 + (a - 1.0) * k_a)
    bvec = kk * a
    yield

    cum = _nn(tril, jnp.concatenate(_split3(lw), axis=1))
    cum = cum[:, :LANES] + cum[:, LANES:2 * LANES] + cum[:, 2 * LANES:]
    cum_end = cum[c - 1:c]
    w_inv = jnp.exp(-cum)
    w_rem = jnp.exp(cum_end - cum)
    a_st = stack(-kk * jnp.exp(cum - lw))
    r_st = stack(r * jnp.exp(cum))
    b_st = stack(bvec * w_inv)
    k_st = stack(kf * w_inv)
    bp_st = _bf(stack(bvec * w_rem))
    kp_st = _bf(stack(kf * w_rem))
    v_st = _bf(stack(v))
    yield

    sc = _nt(_bf(jnp.concatenate([a_st, r_st], axis=0)), _bf(jnp.concatenate([b_st, k_st], axis=0)))
    l_ab = jnp.where(strict, sc[:LANES, :LANES], 0.0)
    l_ak = jnp.where(strict, sc[:LANES, LANES:], 0.0)
    l_rb = jnp.where(incl, sc[LANES:, :LANES], 0.0)
    l_rk = jnp.where(incl, sc[LANES:, LANES:], 0.0)
    yield

    tinv = jnp.where(eye, 1.0, l_ab)
    pw = l_ab
    for _ in range(c.bit_length() - 2):
        pwb = _bf(pw)
        pw = _nn(pwb, pwb)
        tinv = tinv + _nn(_bf(tinv), _bf(pw))
        yield

    au = _nn(_bf(tinv), _bf(jnp.concatenate([a_st, _nn(_bf(l_ak), v_st)], axis=1)))
    aub = _bf(au)
    yield
    rhs = jnp.concatenate([aub, jnp.concatenate([jnp.zeros_like(v_st), v_st], axis=1)], axis=0)
    rb = _nn(_bf(jnp.concatenate([l_rb, l_rk], axis=1)), rhs)
    gh = _tn(jnp.concatenate([bp_st, kp_st], axis=0), rhs)
    r_hat = r_st + rb[:, :LANES]
    o0 = rb[:, LANES:]
    g_mat = jnp.where(eye, jnp.exp(cum_end), 0.0) + gh[:, :LANES]
    h_mat = gh[:, LANES:]
    yield

    os = _nn(_bf(jnp.concatenate([r_hat, g_mat], axis=0)), _bf(s_ref[...]))
    s_ref[...] = os[LANES:] + h_mat
    o_st = os[:LANES] + o0
    o = o_st[:c] + o_st[c:]
    yield

    mean = headsum(o) * (1.0 / HEAD_DIM)
    oc = o - mean
    var = headsum(oc * oc) * (1.0 / HEAD_DIM)
    on = oc * lax.rsqrt(var + LNX_EPS) * ln_g + ln_b
    o_ref[...] = on + headsum(r * kf * r_k) * v


def _wkv_kernel(r_ref, k_ref, v_ref, lw_ref, a_ref, kk_ref, ka_ref, rk_ref, lg_ref, lb_ref,
                o_ref, s_ref):
    @pl.when(pl.program_id(2) == 0)
    def _():
        s_ref[...] = jnp.zeros_like(s_ref)

    def pair(b, p):
        sl = pl.ds(p * LANES, LANES)
        acts = [ref[b, :, sl].astype(F32) for ref in (r_ref, k_ref, v_ref, lw_ref, a_ref)]
        pars = [ref[:, sl] for ref in (kk_ref, ka_ref, rk_ref, lg_ref, lb_ref)]
        return _wkv_pair(*acts, *pars, s_ref.at[b * WKV_PAIRS + p], o_ref.at[b, :, sl])

    stages = [pair(b, p) for b in range(WKV_SEQS) for p in range(WKV_PAIRS)]
    while stages:
        stages = [g for g in stages if next(g, True) is None]


def _wkv(r, k, v, lw, a, k_k, k_a, r_k, lnx_g, lnx_b):
    n_chunks = SEQ // WKV_CHUNK
    width = WKV_PAIRS * LANES
    tok = pl.BlockSpec((WKV_SEQS, WKV_CHUNK, width), lambda b, p, c: (b, c, p))
    par = pl.BlockSpec((1, width), lambda b, p, c: (0, p))
    vec = lambda t: t.reshape(1, D_MODEL)
    seq = lambda t: t.reshape(BATCH, SEQ, D_MODEL)
    out = pl.pallas_call(
        _wkv_kernel,
        grid=(BATCH // WKV_SEQS, N_PAIRS // WKV_PAIRS, n_chunks),
        in_specs=[tok] * 5 + [par] * 5,
        out_specs=tok,
        out_shape=jax.ShapeDtypeStruct((BATCH, SEQ, D_MODEL), F32),
        scratch_shapes=[pltpu.VMEM((WKV_SEQS * WKV_PAIRS, LANES, LANES), F32)],
        compiler_params=_params("parallel", "parallel", "arbitrary"),
        name="wkv_scan",
    )(seq(r), seq(k), seq(v), seq(lw), seq(a), vec(k_k), vec(k_a), vec(r_k), vec(lnx_g), vec(lnx_b))
    return out.reshape(TOKENS, D_MODEL)


def _pack_rows(x):
    bits = lax.bitcast_convert_type(_bf(x).astype(F32), jnp.uint32)
    halves = []
    for h in range(PACK_HALVES):
        lo = bits[:, (2 * h) * PACK_WORDS:(2 * h + 1) * PACK_WORDS]
        hi = bits[:, (2 * h + 1) * PACK_WORDS:(2 * h + 2) * PACK_WORDS]
        halves.append(lax.bitcast_convert_type((lo >> 16) | (hi & jnp.uint32(0xFFFF0000)), jnp.int32))
    return halves


def _unpack_rows(halves):
    parts = []
    for w in halves:
        u = lax.bitcast_convert_type(w, jnp.uint32)
        parts.append(lax.bitcast_convert_type(u << 16, F32))
        parts.append(lax.bitcast_convert_type(u & jnp.uint32(0xFFFF0000), F32))
    return jnp.concatenate(parts, axis=1)


def _post_kernel(*refs, has_gate):
    if has_gate:
        o_ref, g_ref, *refs = refs
    else:
        o_ref, *refs = refs
    (x_ref, mod_ref, wo_ref, gpost_ref, gpre_ref, wr_hi_ref, wr_lo_ref, bias_ref,
     x1_ref, h2p_ref, gk_ref, ek_ref, rk_ref, cnt_ref, run_ref) = refs
    tm = ROW_TILE

    @pl.when(pl.program_id(0) == 0)
    def _():
        run_ref[...] = jnp.zeros_like(run_ref)

    mod = mod_ref[0]
    gate_m, shift_f, scale_f = mod[2:3], mod[3:4], mod[4:5]
    o = o_ref[...]
    if has_gate:
        o = o * g_ref[...].astype(F32)
    y = _nn(_bf(o), wo_ref[...])
    x1 = x_ref[...] + gate_m * _rms(y, gpost_ref[...])
    x1_ref[...] = x1
    h2 = _rms(x1, gpre_ref[...]) * (1.0 + scale_f) + shift_f
    for h, words in enumerate(_pack_rows(h2)):
        h2p_ref[h] = words

    hi = _bf(h2)
    lo = _bf(h2 - hi.astype(F32))
    wr_hi, wr_lo = wr_hi_ref[...], wr_lo_ref[...]
    scores = _sigmoid(_nt(wr_hi, hi) + (_nt(wr_hi, lo) + _nt(wr_lo, hi)))
    erow = lax.broadcasted_iota(jnp.int32, scores.shape, 0)
    work = scores + jnp.concatenate([bias_ref[...]] * (tm // LANES), axis=1)
    picked = jnp.zeros_like(scores)
    chosen = jnp.zeros_like(scores)
    hits, ids = [], []
    for _ in range(TOP_K):
        best = jnp.max(work, axis=0, keepdims=True)
        idx = jnp.min(jnp.where(work == best, erow, N_EXPERTS), axis=0, keepdims=True)
        hit = erow == idx
        picked = jnp.where(hit, scores, picked)
        chosen = jnp.where(hit, 1.0, chosen)
        work = jnp.where(hit, -jnp.inf, work)
        hits.append(hit)
        ids.append(idx)
    gates = ROUTED_SCALE * picked / jnp.sum(picked, axis=0, keepdims=True)

    upto = (lax.broadcasted_iota(jnp.int32, (tm, tm), 0)
            <= lax.broadcasted_iota(jnp.int32, (tm, tm), 1)).astype(BF16)
    chosen_b = _bf(chosen)
    run = run_ref[...]
    before = _nn(chosen_b, upto) - chosen + jnp.concatenate([run] * (tm // LANES), axis=1)
    run = run + _nn(chosen_b, jnp.ones((tm, LANES), BF16))
    run_ref[...] = run
    cnt_ref[...] = run

    krow = lax.broadcasted_iota(jnp.int32, (TOP_K, tm), 0)
    gk = jnp.zeros((TOP_K, tm), F32)
    ek = jnp.zeros((TOP_K, tm), jnp.int32)
    rk = jnp.zeros((TOP_K, tm), jnp.int32)
    for j in range(TOP_K):
        grow = jnp.sum(jnp.where(hits[j], gates, 0.0), axis=0, keepdims=True)
        rrow = jnp.sum(jnp.where(hits[j], before, 0.0), axis=0, keepdims=True)
        gk = jnp.where(krow == j, grow, gk)
        ek = jnp.where(krow == j, ids[j], ek)
        rk = jnp.where(krow == j, rrow.astype(jnp.int32), rk)
    gk_ref[...] = gk
    ek_ref[...] = ek
    rk_ref[...] = rk


def _post(o, g, x, mod, w_o, gpost, gpre, w_router, router_bias):
    tm = ROW_TILE
    full = lambda shape: pl.BlockSpec(shape, lambda i: (0,) * len(shape))
    row_spec = pl.BlockSpec((tm, D_MODEL), lambda i: (i, 0))
    k_spec = pl.BlockSpec((TOP_K, tm), lambda i: (0, i))
    wr = w_router.T
    wr_hi = _bf(wr)
    wr_lo = _bf(wr - wr_hi.astype(F32))
    bias = jnp.broadcast_to(router_bias[:, None], (N_EXPERTS, LANES))
    has_gate = g is not None
    acts = [o, g] if has_gate else [o]
    return pl.pallas_call(
        functools.partial(_post_kernel, has_gate=has_gate),
        grid=(TOKENS // tm,),
        in_specs=[row_spec] * (len(acts) + 1) + [
            pl.BlockSpec((1, N_ADA, D_MODEL), lambda i: (i // (SEQ // tm), 0, 0)),
            full((D_MODEL, D_MODEL)), full((1, D_MODEL)), full((1, D_MODEL)),
            full((N_EXPERTS, D_MODEL)), full((N_EXPERTS, D_MODEL)), full((N_EXPERTS, LANES)),
        ],
        out_specs=[row_spec, pl.BlockSpec((PACK_HALVES, tm, PACK_WORDS), lambda i: (0, i, 0)),
                   k_spec, k_spec, k_spec, full((N_EXPERTS, LANES))],
        out_shape=[jax.ShapeDtypeStruct((TOKENS, D_MODEL), F32),
                   jax.ShapeDtypeStruct((PACK_HALVES, TOKENS, PACK_WORDS), jnp.int32),
                   jax.ShapeDtypeStruct((TOP_K, TOKENS), F32),
                   jax.ShapeDtypeStruct((TOP_K, TOKENS), jnp.int32),
                   jax.ShapeDtypeStruct((TOP_K, TOKENS), jnp.int32),
                   jax.ShapeDtypeStruct((N_EXPERTS, LANES), F32)],
        scratch_shapes=[pltpu.VMEM((N_EXPERTS, LANES), F32)],
        compiler_params=_params("arbitrary"),
        name="mixer_out_router",
    )(*acts, x, mod, _bf(w_o), gpost.reshape(1, D_MODEL), gpre.reshape(1, D_MODEL),
      wr_hi, wr_lo, bias)


def _sc_mesh():
    return plsc.VectorSubcoreMesh(core_axis_name="core", subcore_axis_name="subcore")


def _sc_scatter_rows(rows, idx, n_out):
    n_copies = idx.shape[0]

    @pl.kernel(out_type=jax.ShapeDtypeStruct((PACK_HALVES, n_out, PACK_WORDS), rows.dtype),
               mesh=_sc_mesh(), scratch_types=[], name="sc_dispatch_rows")
    def scatter(x_hbm, i_hbm, o_hbm):
        for h in range(PACK_HALVES):
            def body(x_vmem, i_vmem, h=h):
                for k in range(n_copies):
                    pltpu.sync_copy(x_vmem, o_hbm.at[h].at[i_vmem.at[k]])

            pltpu.emit_pipeline(
                body, grid=(rows.shape[1] // SC_WINDOW,),
                in_specs=[pl.BlockSpec((SC_WINDOW, PACK_WORDS), index_map=lambda i: (i, 0)),
                          pl.BlockSpec((n_copies, SC_WINDOW), index_map=lambda i: (0, i))],
                out_specs=[],
                core_axis_name=("core", "subcore"),
                dimension_semantics=(pltpu.PARALLEL,),
            )(x_hbm.at[h], i_hbm)

    return scatter(rows, idx)


def _sc_gather_rows(tables, idx):
    n = idx.shape[0]

    @pl.kernel(out_type=jax.ShapeDtypeStruct((PACK_HALVES, n, PACK_WORDS), tables[0].dtype),
               mesh=_sc_mesh(), scratch_types=[], name="sc_collect_rows")
    def gather(*refs):
        x_hbm, (i_hbm, o_hbm) = refs[:PACK_HALVES], refs[PACK_HALVES:]
        for h in range(PACK_HALVES):
            def body(i_vmem, o_vmem, h=h):
                pltpu.sync_copy(x_hbm[h].at[i_vmem.at[0]], o_vmem)

            pltpu.emit_pipeline(
                body, grid=(n // SC_WINDOW,),
                in_specs=[pl.BlockSpec((1, SC_WINDOW), index_map=lambda i: (0, i))],
                out_specs=[pl.BlockSpec((SC_WINDOW, PACK_WORDS), index_map=lambda i: (i, 0))],
                core_axis_name=("core", "subcore"),
                dimension_semantics=(pltpu.PARALLEL,),
            )(i_hbm, o_hbm.at[h])

    return gather(*tables, idx.reshape(1, n))


def _ffn(x, wg, wu, wd):
    up = _nn(x, _bf(wu))
    gt = _nn(x, _bf(wg))
    return _nn(_bf(gt * _sigmoid(gt) * up), _bf(wd))


def _expert_kernel(be_ref, nu_ref, *refs):
    n_in = PACK_HALVES * EXPERT_SPLIT
    x_refs = refs[:n_in]
    wg_ref, wu_ref, wd_ref = refs[n_in:n_in + 3]
    y_refs = refs[n_in + 3:n_in + 3 + PACK_HALVES]
    wgb_ref, wub_ref, wdb_ref = refs[n_in + 3 + PACK_HALVES:]
    b = pl.program_id(0)

    @pl.when(jnp.logical_or(b == 0, be_ref[b] != be_ref[jnp.maximum(b - 1, 0)]))
    def _():
        wgb_ref[...] = _bf(wg_ref[0, 0])
        wub_ref[...] = _bf(wu_ref[0, 0])
        wdb_ref[...] = _bf(wd_ref[0, 0])

    @pl.when(b < nu_ref[0])
    def _():
        halves = [jnp.concatenate([x_refs[h * EXPERT_SPLIT + j][0] for j in range(EXPERT_SPLIT)], axis=0)
                  for h in range(PACK_HALVES)]
        y = _ffn(_bf(_unpack_rows(halves)), wgb_ref[...], wub_ref[...], wdb_ref[...])
        for h, words in enumerate(_pack_rows(y)):
            y_refs[h][...] = words


def _experts(layer, xs, blk_expert, n_used, w_gate, w_up, w_down):
    n_rows = xs.shape[1]
    n_blocks = n_rows // EXPERT_BLOCK
    part = EXPERT_BLOCK // EXPERT_SPLIT
    x_spec = lambda h, j: pl.BlockSpec((1, part, PACK_WORDS), lambda b, be, nu: (h, b * EXPERT_SPLIT + j, 0))
    w_spec = lambda shape: pl.BlockSpec((1, 1) + shape, lambda b, be, nu: (layer, be[b], 0, 0))
    y_spec = pl.BlockSpec((EXPERT_BLOCK, PACK_WORDS), lambda b, be, nu: (b, 0))
    n_in = PACK_HALVES * EXPERT_SPLIT
    return pl.pallas_call(
        _expert_kernel,
        grid_spec=pltpu.PrefetchScalarGridSpec(
            num_scalar_prefetch=2,
            grid=(n_blocks,),
            in_specs=[x_spec(h, j) for h in range(PACK_HALVES) for j in range(EXPERT_SPLIT)]
                     + [w_spec((D_MODEL, D_EXPERT)), w_spec((D_MODEL, D_EXPERT)), w_spec((D_EXPERT, D_MODEL))],
            out_specs=[y_spec] * PACK_HALVES,
            scratch_shapes=[pltpu.VMEM((D_MODEL, D_EXPERT), BF16), pltpu.VMEM((D_MODEL, D_EXPERT), BF16),
                            pltpu.VMEM((D_EXPERT, D_MODEL), BF16)],
        ),
        out_shape=[jax.ShapeDtypeStruct((n_rows, PACK_WORDS), xs.dtype)] * PACK_HALVES,
        compiler_params=_params("arbitrary"),
        name="moe_experts",
    )(blk_expert, n_used, *([xs] * n_in), w_gate, w_up, w_down)


def _shared_kernel(h2p_ref, sg_ref, su_ref, sd_ref, o_ref):
    h2 = _bf(_unpack_rows([h2p_ref[h] for h in range(PACK_HALVES)]))
    for h, words in enumerate(_pack_rows(_ffn(h2, sg_ref[0], su_ref[0], sd_ref[0]))):
        o_ref[h] = words


def _shared_expert(layer, h2p, ws_gate, ws_up, ws_down):
    tm = SHARED_TILE
    row_spec = pl.BlockSpec((PACK_HALVES, tm, PACK_WORDS), lambda i: (0, i, 0))
    return pl.pallas_call(
        _shared_kernel,
        grid=(TOKENS // tm,),
        in_specs=[row_spec,
                  pl.BlockSpec((1, D_MODEL, D_EXPERT), lambda i: (layer, 0, 0)),
                  pl.BlockSpec((1, D_MODEL, D_EXPERT), lambda i: (layer, 0, 0)),
                  pl.BlockSpec((1, D_EXPERT, D_MODEL), lambda i: (layer, 0, 0))],
        out_specs=row_spec,
        out_shape=jax.ShapeDtypeStruct(h2p.shape, h2p.dtype),
        compiler_params=_params("parallel"),
        name="moe_shared_expert",
    )(h2p, ws_gate, ws_up, ws_down)


def _combine_kernel(*refs, with_qkv):
    g_refs = refs[:TOP_K]
    gk_ref, sh_ref, x_ref, mod_ref, gpost_ref = refs[TOP_K:TOP_K + 5]
    acc = _unpack_rows([sh_ref[h] for h in range(PACK_HALVES)])
    gk = gk_ref[...].T
    for j in range(TOP_K):
        acc = acc + gk[:, j:j + 1] * _unpack_rows([g_refs[j][h] for h in range(PACK_HALVES)])
    gate_f = mod_ref[0][5:6]
    x_new = x_ref[...] + gate_f * _rms(acc, gpost_ref[...])
    if not with_qkv:
        refs[-1][...] = x_new
        return
    modn_ref, gpre_ref, w_ref, o_ref, q_ref, k_ref, v_ref = refs[TOP_K + 5:]
    o_ref[...] = x_new
    modn = modn_ref[0]
    h = _bf(_rms(x_new, gpre_ref[...]) * (1.0 + modn[1:2]) + modn[0:1])
    q_ref[...] = _bf(_nn(h, w_ref[:, :D_MODEL]) * (LOG2_E / 8.0))
    k_ref[...] = _bf(_nn(h, w_ref[:, D_MODEL:2 * D_MODEL]))
    v_ref[...] = _bf(_nn(h, w_ref[:, 2 * D_MODEL:]))


def _combine(picked, gk, shared, x1, mod, gpost, next_qkv):
    tm = ROW_TILE
    n_tiles = TOKENS // tm
    row_spec = pl.BlockSpec((tm, D_MODEL), lambda i: (i, 0))
    mod_spec = pl.BlockSpec((1, N_ADA, D_MODEL), lambda i: (i // (SEQ // tm), 0, 0))
    vec_spec = pl.BlockSpec((1, D_MODEL), lambda i: (0, 0))
    pick_spec = lambda j: pl.BlockSpec((PACK_HALVES, tm, PACK_WORDS), lambda i: (0, j * n_tiles + i, 0))
    operands = [*([picked] * TOP_K), gk, shared, x1, mod, gpost.reshape(1, D_MODEL)]
    in_specs = [pick_spec(j) for j in range(TOP_K)] + [
        pl.BlockSpec((TOP_K, tm), lambda i: (0, i)),
        pl.BlockSpec((PACK_HALVES, tm, PACK_WORDS), lambda i: (0, i, 0)),
        row_spec, mod_spec, vec_spec,
    ]
    out_specs = [row_spec]
    out_shape = [jax.ShapeDtypeStruct((TOKENS, D_MODEL), F32)]
    if next_qkv is not None:
        mod_next, gpre_next, w_qkv = next_qkv
        operands += [mod_next, gpre_next.reshape(1, D_MODEL), _bf(w_qkv)]
        in_specs += [mod_spec, vec_spec, pl.BlockSpec((D_MODEL, 3 * D_MODEL), lambda i: (0, 0))]
        out_specs += [row_spec] * 3
        out_shape += [jax.ShapeDtypeStruct((TOKENS, D_MODEL), BF16)] * 3
    out = pl.pallas_call(
        functools.partial(_combine_kernel, with_qkv=next_qkv is not None),
        grid=(n_tiles,),
        in_specs=in_specs,
        out_specs=out_specs,
        out_shape=out_shape,
        compiler_params=_params("parallel"),
        name="moe_combine",
    )(*operands)
    return out if next_qkv is not None else out[0]


def _moe(layer, x1, h2p, gk, ek, rk, counts, mod, gpost, w_gate, w_up, w_down, ws_gate, ws_up, ws_down,
         next_qkv):
    cnt = counts[:, 0].astype(jnp.int32)
    padded = (cnt + EXPERT_BLOCK - 1) // EXPERT_BLOCK * EXPERT_BLOCK
    eid = jnp.arange(N_EXPERTS, dtype=jnp.int32)
    pend = jnp.sum(jnp.where(eid[:, None] <= eid[None, :], padded[:, None], 0), axis=0)
    pstart = pend - padded
    dest = rk + jnp.sum(jnp.where(ek[None] == eid[:, None, None], pstart[:, None, None], 0), axis=0)
    n_blocks = (TOKENS * TOP_K + N_EXPERTS * (EXPERT_BLOCK - 1)) // EXPERT_BLOCK + 1
    blk_start = jnp.arange(n_blocks, dtype=jnp.int32) * EXPERT_BLOCK
    blk_expert = jnp.minimum(jnp.sum((blk_start[None, :] >= pend[:, None]).astype(jnp.int32), axis=0),
                             N_EXPERTS - 1)
    n_used = pend[-1:] // EXPERT_BLOCK

    xs = _sc_scatter_rows(h2p, dest, n_blocks * EXPERT_BLOCK)
    shared = _shared_expert(layer, h2p, ws_gate, ws_up, ws_down)
    ys = _experts(layer, xs, blk_expert, n_used, w_gate, w_up, w_down)
    picked = _sc_gather_rows(ys, dest.reshape(-1))
    return _combine(picked, gk, shared, x1, mod, gpost, next_qkv)


def _sb_attn_kernel(q_ref, k_ref, v_ref, o_ref):
    blk = ATT_BLOCK
    n_heads = LANES // HEAD_DIM
    qb = pl.program_id(2)
    q = q_ref[...]
    lane = lax.broadcasted_iota(jnp.int32, (blk, LANES), 1)
    head0 = lane < HEAD_DIM
    qh = [jnp.where(head0, q, jnp.zeros_like(q)), jnp.where(head0, jnp.zeros_like(q), q)]
    rowi = lax.broadcasted_iota(jnp.int32, (blk, blk), 0)
    coli = lax.broadcasted_iota(jnp.int32, (blk, blk), 1)
    later = (rowi > coli).astype(BF16)

    sub = ATT_SUB
    n_sub = blk // sub
    sub_row = lax.broadcasted_iota(jnp.int32, (sub, blk), 0)
    sub_col = lax.broadcasted_iota(jnp.int32, (sub, blk), 1)
    below = [sub_col < sub_row + r * sub for r in range(n_sub)]
    sub_head0 = lax.broadcasted_iota(jnp.int32, (sub, LANES), 1) < HEAD_DIM
    chains = [(hd, r) for hd in range(n_heads) for r in range(n_sub)]

    def chain(hd, r, ks, vs, get_carry, get_acc, mode, out):
        rows = slice(r * sub, (r + 1) * sub)
        u = _nt(qh[hd][rows], ks)
        yield
        neg_abs = lax.bitcast_convert_type(
            lax.bitcast_convert_type(u, jnp.uint32) | jnp.uint32(0x80000000), F32)
        lb = jnp.minimum(u, 0.0) - jnp.log2(1.0 + jnp.exp2(neg_abs))
        l1m = lb - u
        if mode == "diag":
            l1m = jnp.where(below[r], l1m, 0.0)
        l1b = _bf(l1m)
        yield
        carry = get_carry()
        tail = _nn(l1b, later) + jnp.concatenate([carry] * (blk // LANES), axis=1)
        out["carry"] = carry + jnp.sum(l1m, axis=-1, keepdims=True)
        yield
        w = jnp.exp2(lb + tail)
        if mode == "diag":
            w = jnp.where(below[r], w, 0.0)
        elif mode == "prev":
            w = jnp.where(qb > 0, w, 0.0)
        wb = _bf(w)
        yield
        out["acc"] = get_acc() + _nn(wb, vs)

    def load_kv(kb):
        start = pl.multiple_of(kb * blk, blk)
        return k_ref[pl.ds(start, blk), :], v_ref[pl.ds(start, blk), :]

    def run(gens):
        while gens:
            gens = [g for g in gens if next(g, True) is None]

    def block(kb, carry, acc):
        ks, vs = load_kv(kb)
        outs = [{} for _ in chains]
        run([chain(hd, r, ks, vs, lambda c=c: carry[c], lambda c=c: acc[c], "plain", outs[c])
             for c, (hd, r) in enumerate(chains)])
        return [o["carry"] for o in outs], [o["acc"] for o in outs]

    zeros = jnp.zeros((sub, LANES), F32)
    ks_d, vs_d = load_kv(qb)
    ks_p, vs_p = load_kv(jnp.maximum(qb - 1, 0))
    outs_d = [{} for _ in chains]
    outs_p = [{} for _ in chains]
    run([chain(hd, r, ks_d, vs_d, lambda: zeros, lambda: zeros, "diag", outs_d[c])
         for c, (hd, r) in enumerate(chains)]
        + [chain(hd, r, ks_p, vs_p, lambda c=c: outs_d[c]["carry"], lambda c=c: outs_d[c]["acc"],
                 "prev", outs_p[c]) for c, (hd, r) in enumerate(chains)])
    carry = [o["carry"] for o in outs_p]
    acc = [o["acc"] for o in outs_p]

    def cmax_of(cr):
        return jnp.max(functools.reduce(jnp.maximum, cr))

    def cond(st):
        kb, _, _, cmax = st
        return jnp.logical_and(kb >= 0, cmax > EXP2_UNDERFLOW)

    def body(st):
        kb, cr, ac, _ = st
        cr, ac = block(kb, list(cr), list(ac))
        return kb - 1, tuple(cr), tuple(ac), cmax_of(cr)

    _, _, acc, _ = lax.while_loop(cond, body, (qb - 2, tuple(carry), tuple(acc), cmax_of(carry)))
    for r in range(n_sub):
        rows = slice(r * sub, (r + 1) * sub)
        o_ref[rows, :] = jnp.where(sub_head0, acc[r], acc[n_sub + r])


def _sb_attn(q, k, v):
    blk = ATT_BLOCK
    n_blk = SEQ // blk
    q_spec = pl.BlockSpec((blk, LANES), lambda b, p, i: (b * n_blk + i, p))
    kv_spec = pl.BlockSpec((SEQ, LANES), lambda b, p, i: (b, p))
    return pl.pallas_call(
        _sb_attn_kernel,
        grid=(BATCH, N_PAIRS, n_blk),
        in_specs=[q_spec, kv_spec, kv_spec],
        out_specs=q_spec,
        out_shape=jax.ShapeDtypeStruct((TOKENS, D_MODEL), F32),
        compiler_params=_params("parallel", "parallel", "arbitrary"),
        name="sb_attention",
    )(q, k, v)


def kernel(x, c, ada_w, ada_b, norm_pre_mix, norm_post_mix, norm_pre_ffn, norm_post_ffn, rwkv_mu, rwkv_w_rkv, rwkv_w_w1, rwkv_w_w2, rwkv_w0, rwkv_w_a1, rwkv_w_a2, rwkv_a0, rwkv_w_g1, rwkv_w_g2, rwkv_k_k, rwkv_k_a, rwkv_r_k, rwkv_lnx_g, rwkv_lnx_b, rwkv_w_o, sb_w_qkv, sb_w_o, moe_w_router, moe_router_bias, moe_w_gate, moe_w_up, moe_w_down, moe_ws_gate, moe_ws_up, moe_ws_down):
    mod_all = _ada_mod(c, ada_w, ada_b)
    xt = x.reshape(TOKENS, D_MODEL)
    for layer in range(DEPTH):
        mod = mod_all[layer]
        i = layer // 2
        if layer % 2 == 0:
            r, k, v, lw, a, g = _rwkv_proj(
                xt, mod, norm_pre_mix[layer], rwkv_mu[i], rwkv_w_rkv[i], rwkv_w_w1[i], rwkv_w_w2[i],
                rwkv_w0[i], rwkv_w_a1[i], rwkv_w_a2[i], rwkv_a0[i], rwkv_w_g1[i], rwkv_w_g2[i])
            o = _wkv(r, k, v, lw, a, rwkv_k_k[i], rwkv_k_a[i], rwkv_r_k[i], rwkv_lnx_g[i], rwkv_lnx_b[i])
            w_o = rwkv_w_o[i]
        else:
            o, g = _sb_attn(*qkv), None
            w_o = sb_w_o[i]
        x1, h2p, gk, ek, rk, counts = _post(o, g, xt, mod, w_o, norm_post_mix[layer], norm_pre_ffn[layer],
                                            moe_w_router[layer], moe_router_bias[layer])
        nxt = layer + 1
        next_qkv = ((mod_all[nxt], norm_pre_mix[nxt], sb_w_qkv[nxt // 2])
                    if nxt < DEPTH and nxt % 2 == 1 else None)
        out = _moe(layer, x1, h2p, gk, ek, rk, counts, mod, norm_post_ffn[layer], moe_w_gate, moe_w_up,
                   moe_w_down, moe_ws_gate, moe_ws_up, moe_ws_down, next_qkv)
        xt, *qkv = out if next_qkv is not None else (out,)
    return xt.reshape(BATCH, SEQ, D_MODEL)
```

```python
import functools

import jax
import jax.numpy as jnp
from jax import lax
from jax.experimental import pallas as pl
from jax.experimental.pallas import tpu as pltpu
from jax.experimental.pallas import tpu_sc as plsc

D_MODEL = 1024
BATCH = 2
SEQ = 8192
TOKENS = BATCH * SEQ
DEPTH = 2
HEAD_DIM = 64
N_EXPERTS = 64
TOP_K = 8
D_EXPERT = 256
ROUTED_SCALE = 2.5
RMS_EPS = 1e-6
LNX_EPS = 64e-5
N_ADA = 6

LANES = 128
SUBLANES = 8
N_PAIRS = D_MODEL // LANES
WKV_CHUNK = 64
WKV_PAIRS = 8
WKV_SEQS = 2
ROW_TILE = 512
SHARED_TILE = 512
EXPERT_BLOCK = 1024
EXPERT_SPLIT = 2
PACK_HALVES = 2
PACK_WORDS = D_MODEL // (2 * PACK_HALVES)
SC_WINDOW = 128
ATT_BLOCK = 256
ATT_SUB = 128
LOG2_E = 1.4426950408889634
EXP2_UNDERFLOW = -153.0
VMEM_LIMIT = 56 * 1024 * 1024

F32 = jnp.float32
BF16 = jnp.bfloat16


def _nn(a, b):
    return jnp.dot(a, b, preferred_element_type=F32)


def _nt(a, b):
    return lax.dot_general(a, b, (((1,), (1,)), ((), ())), preferred_element_type=F32)


def _tn(a, b):
    return lax.dot_general(a, b, (((0,), (0,)), ((), ())), preferred_element_type=F32)


def _bf(x):
    return x.astype(BF16)


def _sigmoid(x):
    return 1.0 / (1.0 + jnp.exp(-x))


def _rms(xv, g):
    ms = jnp.mean(xv * xv, axis=-1, keepdims=True)
    return xv * lax.rsqrt(ms + RMS_EPS) * g


def _split2(x):
    hi = x.astype(BF16)
    return hi, (x - hi.astype(F32)).astype(BF16)


def _params(*sem):
    return pltpu.CompilerParams(dimension_semantics=sem, vmem_limit_bytes=VMEM_LIMIT)


def _ada_kernel(c_ref, w_ref, b_ref, o_ref):
    cv = c_ref[...]
    o_ref[0] = _nn(cv * _sigmoid(cv), w_ref[0]) + b_ref[0]


def _ada_mod(c, ada_w, ada_b):
    tn = 1536
    c8 = jnp.pad(c, ((0, SUBLANES - BATCH), (0, 0)))
    out = pl.pallas_call(
        _ada_kernel,
        grid=(DEPTH, N_ADA * D_MODEL // tn),
        in_specs=[
            pl.BlockSpec((SUBLANES, D_MODEL), lambda l, j: (0, 0)),
            pl.BlockSpec((1, D_MODEL, tn), lambda l, j: (l, 0, j)),
            pl.BlockSpec((1, 1, tn), lambda l, j: (l, 0, j)),
        ],
        out_specs=pl.BlockSpec((1, SUBLANES, tn), lambda l, j: (l, 0, j)),
        out_shape=jax.ShapeDtypeStruct((DEPTH, SUBLANES, N_ADA * D_MODEL), F32),
        compiler_params=_params("parallel", "parallel"),
        name="ada_mod",
    )(c8, ada_w, ada_b.reshape(DEPTH, 1, N_ADA * D_MODEL))
    return out[:, :BATCH].reshape(DEPTH, BATCH, N_ADA, D_MODEL)


def _rwkv_proj_kernel(x_ref, xp_ref, mod_ref, gpre_ref, mu_ref, wrkv_ref, w1_ref, w2_ref, w0_ref,
                      a1_ref, a2_ref, a0_ref, g1_ref, g2_ref,
                      r_ref, k_ref, v_ref, lw_ref, a_ref, g_ref):
    i = pl.program_id(0)
    mod = mod_ref[0]
    shift, scale = mod[0:1], mod[1:2]
    gpre = gpre_ref[...]

    def modnorm(xv):
        return _rms(xv, gpre) * (1.0 + scale) + shift

    h = modnorm(x_ref[...])
    hp = modnorm(xp_ref[...])[SUBLANES - 1:SUBLANES]
    hp = jnp.where(i % (SEQ // ROW_TILE) == 0, 0.0, hp)
    row = lax.broadcasted_iota(jnp.int32, h.shape, 0)
    hs = jnp.where(row == 0, hp, pltpu.roll(h, 1, 0))
    xx = hs - h
    mu = mu_ref[...]

    def mix(n):
        return _bf(h + xx * mu[n:n + 1])

    r_ref[...] = _bf(_nn(mix(0), wrkv_ref[0]))
    k_ref[...] = _bf(_nn(mix(1), wrkv_ref[1]))
    v_ref[...] = _bf(_nn(mix(2), wrkv_ref[2]))
    wl = w0_ref[...] + _nn(_bf(jnp.tanh(_nn(mix(3), w1_ref[...]))), w2_ref[...])
    nwl = -wl
    sp = jnp.maximum(nwl, 0.0) + jnp.log(1.0 + jnp.exp(-jnp.abs(nwl)))
    lw_ref[...] = -jnp.exp(-sp - 0.5)
    a_ref[...] = _sigmoid(a0_ref[...] + _nn(_bf(_nn(mix(4), a1_ref[...])), a2_ref[...]))
    g_ref[...] = _bf(_nn(_bf(_sigmoid(_nn(mix(5), g1_ref[...]))), g2_ref[...]))


def _rwkv_proj(x, mod, gpre, mu, w_rkv, w1, w2, w0, a1, a2, a0, g1, g2):
    tm = ROW_TILE
    full = lambda shape: pl.BlockSpec(shape, lambda i: (0,) * len(shape))
    row_spec = pl.BlockSpec((tm, D_MODEL), lambda i: (i, 0))
    out_sds = [jax.ShapeDtypeStruct((TOKENS, D_MODEL), dt) for dt in (BF16, BF16, BF16, F32, F32, BF16)]
    return pl.pallas_call(
        _rwkv_proj_kernel,
        grid=(TOKENS // tm,),
        in_specs=[
            row_spec,
            pl.BlockSpec((SUBLANES, D_MODEL), lambda i: (jnp.maximum(i * (tm // SUBLANES) - 1, 0), 0)),
            pl.BlockSpec((1, N_ADA, D_MODEL), lambda i: (i // (SEQ // tm), 0, 0)),
            full((1, D_MODEL)), full((6, D_MODEL)), full((3, D_MODEL, D_MODEL)),
            full(w1.shape), full(w2.shape), full((1, D_MODEL)),
            full(a1.shape), full(a2.shape), full((1, D_MODEL)),
            full(g1.shape), full(g2.shape),
        ],
        out_specs=[row_spec] * 6,
        out_shape=out_sds,
        compiler_params=_params("parallel"),
        name="rwkv_proj",
    )(x, x, mod, gpre.reshape(1, D_MODEL), mu, _bf(w_rkv), _bf(w1), _bf(w2),
      w0.reshape(1, D_MODEL), _bf(a1), _bf(a2), a0.reshape(1, D_MODEL), _bf(g1), _bf(g2))


def _wkv_pair(r, k, v, lw, a, k_k, k_a, r_k, ln_g, ln_b, s_ref, o_ref):
    c = WKV_CHUNK
    lane = lax.broadcasted_iota(jnp.int32, (c, LANES), 1)
    head0 = lane < HEAD_DIM
    rowi = lax.broadcasted_iota(jnp.int32, (LANES, LANES), 0)
    coli = lax.broadcasted_iota(jnp.int32, (LANES, LANES), 1)
    rt, ct = rowi & (c - 1), coli & (c - 1)
    strict, incl, eye = rt > ct, rt >= ct, rowi == coli
    tril = (lax.broadcasted_iota(jnp.int32, (c, c), 0)
            >= lax.broadcasted_iota(jnp.int32, (c, c), 1)).astype(BF16)

    def headsum(x):
        s0 = jnp.sum(jnp.where(head0, x, 0.0), axis=-1, keepdims=True)
        s1 = jnp.sum(jnp.where(head0, 0.0, x), axis=-1, keepdims=True)
        return jnp.where(head0, s0, s1)

    def stack(x):
        return jnp.concatenate([jnp.where(head0, x, 0.0), jnp.where(head0, 0.0, x)], axis=0)

    kk = k * k_k
    kk = kk / jnp.maximum(jnp.sqrt(headsum(kk * kk)), 1e-12)
    kf = k * (1.0 + (a - 1.0) * k_a)
    bvec = kk * a
    yield

    cum = _nn(tril, jnp.concatenate(_split2(lw), axis=1))
    cum = cum[:, :LANES] + cum[:, LANES:]
    cum_end = cum[c - 1:c]
    w_inv = jnp.exp(-cum)
    w_rem = jnp.exp(cum_end - cum)
    a_st = stack(-kk * jnp.exp(cum - lw))
    r_st = stack(r * jnp.exp(cum))
    b_st = stack(bvec * w_inv)
    k_st = stack(kf * w_inv)
    bp_st = _bf(stack(bvec * w_rem))
    kp_st = _bf(stack(kf * w_rem))
    v_st = _bf(stack(v))
    yield

    sc = _nt(_bf(jnp.concatenate([a_st, r_st], axis=0)), _bf(jnp.concatenate([b_st, k_st], axis=0)))
    l_ab = jnp.where(strict, sc[:LANES, :LANES], 0.0)
    l_ak = jnp.where(strict, sc[:LANES, LANES:], 0.0)
    l_rb = jnp.where(incl, sc[LANES:, :LANES], 0.0)
    l_rk = jnp.where(incl, sc[LANES:, LANES:], 0.0)
    yield

    tinv = jnp.where(eye, 1.0, l_ab)
    pw = l_ab
    for _ in range(c.bit_length() - 2):
        pwb = _bf(pw)
        pw = _nn(pwb, pwb)
        tinv = tinv + _nn(_bf(tinv), _bf(pw))
        yield

    au = _nn(_bf(tinv), _bf(jnp.concatenate([a_st, _nn(_bf(l_ak), v_st)], axis=1)))
    aub = _bf(au)
    yield
    rhs = jnp.concatenate([aub, jnp.concatenate([jnp.zeros_like(v_st), v_st], axis=1)], axis=0)
    rb = _nn(_bf(jnp.concatenate([l_rb, l_rk], axis=1)), rhs)
    gh = _tn(jnp.concatenate([bp_st, kp_st], axis=0), rhs)
    r_hat = r_st + rb[:, :LANES]
    o0 = rb[:, LANES:]
    g_mat = jnp.where(eye, jnp.exp(cum_end), 0.0) + gh[:, :LANES]
    h_mat = gh[:, LANES:]
    yield

    os = _nn(_bf(jnp.concatenate([r_hat, g_mat], axis=0)), _bf(s_ref[...]))
    s_ref[...] = os[LANES:] + h_mat
    o_st = os[:LANES] + o0
    o = o_st[:c] + o_st[c:]
    yield

    mean = headsum(o) * (1.0 / HEAD_DIM)
    oc = o - mean
    var = headsum(oc * oc) * (1.0 / HEAD_DIM)
    on = oc * lax.rsqrt(var + LNX_EPS) * ln_g + ln_b
    o_ref[...] = on + headsum(r * kf * r_k) * v


def _wkv_kernel(r_ref, k_ref, v_ref, lw_ref, a_ref, kk_ref, ka_ref, rk_ref, lg_ref, lb_ref,
                o_ref, s_ref):
    @pl.when(pl.program_id(2) == 0)
    def _():
        s_ref[...] = jnp.zeros_like(s_ref)

    def pair(b, p):
        sl = pl.ds(p * LANES, LANES)
        acts = [ref[b, :, sl].astype(F32) for ref in (r_ref, k_ref, v_ref, lw_ref, a_ref)]
        pars = [ref[:, sl] for ref in (kk_ref, ka_ref, rk_ref, lg_ref, lb_ref)]
        return _wkv_pair(*acts, *pars, s_ref.at[b * WKV_PAIRS + p], o_ref.at[b, :, sl])

    stages = [pair(b, p) for b in range(WKV_SEQS) for p in range(WKV_PAIRS)]
    while stages:
        stages = [g for g in stages if next(g, True) is None]


def _wkv(r, k, v, lw, a, k_k, k_a, r_k, lnx_g, lnx_b):
    n_chunks = SEQ // WKV_CHUNK
    width = WKV_PAIRS * LANES
    tok = pl.BlockSpec((WKV_SEQS, WKV_CHUNK, width), lambda b, p, c: (b, c, p))
    par = pl.BlockSpec((1, width), lambda b, p, c: (0, p))
    vec = lambda t: t.reshape(1, D_MODEL)
    seq = lambda t: t.reshape(BATCH, SEQ, D_MODEL)
    out = pl.pallas_call(
        _wkv_kernel,
        grid=(BATCH // WKV_SEQS, N_PAIRS // WKV_PAIRS, n_chunks),
        in_specs=[tok] * 5 + [par] * 5,
        out_specs=tok,
        out_shape=jax.ShapeDtypeStruct((BATCH, SEQ, D_MODEL), F32),
        scratch_shapes=[pltpu.VMEM((WKV_SEQS * WKV_PAIRS, LANES, LANES), F32)],
        compiler_params=_params("parallel", "parallel", "arbitrary"),
        name="wkv_scan",
    )(seq(r), seq(k), seq(v), seq(lw), seq(a), vec(k_k), vec(k_a), vec(r_k), vec(lnx_g), vec(lnx_b))
    return out.reshape(TOKENS, D_MODEL)


def _pack_rows(x):
    bits = lax.bitcast_convert_type(_bf(x).astype(F32), jnp.uint32)
    halves = []
    for h in range(PACK_HALVES):
        lo = bits[:, (2 * h) * PACK_WORDS:(2 * h + 1) * PACK_WORDS]
        hi = bits[:, (2 * h + 1) * PACK_WORDS:(2 * h + 2) * PACK_WORDS]
        halves.append(lax.bitcast_convert_type((lo >> 16) | (hi & jnp.uint32(0xFFFF0000)), jnp.int32))
    return halves


def _unpack_rows(halves):
    parts = []
    for w in halves:
        u = lax.bitcast_convert_type(w, jnp.uint32)
        parts.append(lax.bitcast_convert_type(u << 16, F32))
        parts.append(lax.bitcast_convert_type(u & jnp.uint32(0xFFFF0000), F32))
    return jnp.concatenate(parts, axis=1)


def _post_kernel(*refs, has_gate):
    if has_gate:
        o_ref, g_ref, *refs = refs
    else:
        o_ref, *refs = refs
    (x_ref, mod_ref, wo_ref, gpost_ref, gpre_ref, wr_hi_ref, wr_lo_ref, bias_ref,
     x1_ref, h2p_ref, gk_ref, ek_ref, rk_ref, cnt_ref, run_ref) = refs
    tm = ROW_TILE

    @pl.when(pl.program_id(0) == 0)
    def _():
        run_ref[...] = jnp.zeros_like(run_ref)

    mod = mod_ref[0]
    gate_m, shift_f, scale_f = mod[2:3], mod[3:4], mod[4:5]
    o = o_ref[...]
    if has_gate:
        o = o * g_ref[...].astype(F32)
    y = _nn(_bf(o), wo_ref[...])
    x1 = x_ref[...] + _rms(y, gate_m * gpost_ref[...])
    x1_ref[...] = x1
    h2 = _rms(x1, gpre_ref[...] * (1.0 + scale_f)) + shift_f
    for h, words in enumerate(_pack_rows(h2)):
        h2p_ref[h] = words

    hi = _bf(h2)
    lo = _bf(h2 - hi.astype(F32))
    wr_hi, wr_lo = wr_hi_ref[...], wr_lo_ref[...]
    scores = _sigmoid(_nt(wr_hi, hi) + (_nt(wr_hi, lo) + _nt(wr_lo, hi)))
    erow = lax.broadcasted_iota(jnp.int32, scores.shape, 0)
    work = scores + jnp.concatenate([bias_ref[...]] * (tm // LANES), axis=1)
    picked = jnp.zeros_like(scores)
    chosen = jnp.zeros_like(scores)
    hits, ids = [], []
    for _ in range(TOP_K):
        best = jnp.max(work, axis=0, keepdims=True)
        idx = jnp.min(jnp.where(work == best, erow, N_EXPERTS), axis=0, keepdims=True)
        hit = erow == idx
        picked = jnp.where(hit, scores, picked)
        chosen = jnp.where(hit, 1.0, chosen)
        work = jnp.where(hit, -jnp.inf, work)
        hits.append(hit)
        ids.append(idx)
    gates = ROUTED_SCALE * picked / jnp.sum(picked, axis=0, keepdims=True)

    upto = (lax.broadcasted_iota(jnp.int32, (tm, tm), 0)
            <= lax.broadcasted_iota(jnp.int32, (tm, tm), 1)).astype(BF16)
    chosen_b = _bf(chosen)
    run = run_ref[...]
    before = _nn(chosen_b, upto) - chosen + jnp.concatenate([run] * (tm // LANES), axis=1)
    run = run + _nn(chosen_b, jnp.ones((tm, LANES), BF16))
    run_ref[...] = run
    cnt_ref[...] = run

    krow = lax.broadcasted_iota(jnp.int32, (TOP_K, tm), 0)
    gk = jnp.zeros((TOP_K, tm), F32)
    ek = jnp.zeros((TOP_K, tm), jnp.int32)
    rk = jnp.zeros((TOP_K, tm), jnp.int32)
    for j in range(TOP_K):
        grow = jnp.sum(jnp.where(hits[j], gates, 0.0), axis=0, keepdims=True)
        rrow = jnp.sum(jnp.where(hits[j], before, 0.0), axis=0, keepdims=True)
        gk = jnp.where(krow == j, grow, gk)
        ek = jnp.where(krow == j, ids[j], ek)
        rk = jnp.where(krow == j, rrow.astype(jnp.int32), rk)
    gk_ref[...] = gk
    ek_ref[...] = ek
    rk_ref[...] = rk


def _post(o, g, x, mod, w_o, gpost, gpre, w_router, router_bias):
    tm = ROW_TILE
    full = lambda shape: pl.BlockSpec(shape, lambda i: (0,) * len(shape))
    row_spec = pl.BlockSpec((tm, D_MODEL), lambda i: (i, 0))
    k_spec = pl.BlockSpec((TOP_K, tm), lambda i: (0, i))
    wr = w_router.T
    wr_hi = _bf(wr)
    wr_lo = _bf(wr - wr_hi.astype(F32))
    bias = jnp.broadcast_to(router_bias[:, None], (N_EXPERTS, LANES))
    has_gate = g is not None
    acts = [o, g] if has_gate else [o]
    return pl.pallas_call(
        functools.partial(_post_kernel, has_gate=has_gate),
        grid=(TOKENS // tm,),
        in_specs=[row_spec] * (len(acts) + 1) + [
            pl.BlockSpec((1, N_ADA, D_MODEL), lambda i: (i // (SEQ // tm), 0, 0)),
            full((D_MODEL, D_MODEL)), full((1, D_MODEL)), full((1, D_MODEL)),
            full((N_EXPERTS, D_MODEL)), full((N_EXPERTS, D_MODEL)), full((N_EXPERTS, LANES)),
        ],
        out_specs=[row_spec, pl.BlockSpec((PACK_HALVES, tm, PACK_WORDS), lambda i: (0, i, 0)),
                   k_spec, k_spec, k_spec, full((N_EXPERTS, LANES))],
        out_shape=[jax.ShapeDtypeStruct((TOKENS, D_MODEL), F32),
                   jax.ShapeDtypeStruct((PACK_HALVES, TOKENS, PACK_WORDS), jnp.int32),
                   jax.ShapeDtypeStruct((TOP_K, TOKENS), F32),
                   jax.ShapeDtypeStruct((TOP_K, TOKENS), jnp.int32),
                   jax.ShapeDtypeStruct((TOP_K, TOKENS), jnp.int32),
                   jax.ShapeDtypeStruct((N_EXPERTS, LANES), F32)],
        scratch_shapes=[pltpu.VMEM((N_EXPERTS, LANES), F32)],
        compiler_params=_params("arbitrary"),
        name="mixer_out_router",
    )(*acts, x, mod, _bf(w_o), gpost.reshape(1, D_MODEL), gpre.reshape(1, D_MODEL),
      wr_hi, wr_lo, bias)


def _sc_mesh():
    return plsc.VectorSubcoreMesh(core_axis_name="core", subcore_axis_name="subcore")


def _sc_scatter_rows(rows, idx, n_out):
    n_copies = idx.shape[0]

    @pl.kernel(out_type=jax.ShapeDtypeStruct((PACK_HALVES, n_out, PACK_WORDS), rows.dtype),
               mesh=_sc_mesh(), scratch_types=[], name="sc_dispatch_rows")
    def scatter(x_hbm, i_hbm, o_hbm):
        for h in range(PACK_HALVES):
            def body(x_vmem, i_vmem, h=h):
                for k in range(n_copies):
                    pltpu.sync_copy(x_vmem, o_hbm.at[h].at[i_vmem.at[k]])

            pltpu.emit_pipeline(
                body, grid=(rows.shape[1] // SC_WINDOW,),
                in_specs=[pl.BlockSpec((SC_WINDOW, PACK_WORDS), index_map=lambda i: (i, 0)),
                          pl.BlockSpec((n_copies, SC_WINDOW), index_map=lambda i: (0, i))],
                out_specs=[],
                core_axis_name=("core", "subcore"),
                dimension_semantics=(pltpu.PARALLEL,),
            )(x_hbm.at[h], i_hbm)

    return scatter(rows, idx)


def _sc_gather_rows(tables, idx):
    n = idx.shape[0]

    @pl.kernel(out_type=jax.ShapeDtypeStruct((PACK_HALVES, n, PACK_WORDS), tables[0].dtype),
               mesh=_sc_mesh(), scratch_types=[], name="sc_collect_rows")
    def gather(*refs):
        x_hbm, (i_hbm, o_hbm) = refs[:PACK_HALVES], refs[PACK_HALVES:]
        for h in range(PACK_HALVES):
            def body(i_vmem, o_vmem, h=h):
                pltpu.sync_copy(x_hbm[h].at[i_vmem.at[0]], o_vmem)

            pltpu.emit_pipeline(
                body, grid=(n // SC_WINDOW,),
                in_specs=[pl.BlockSpec((1, SC_WINDOW), index_map=lambda i: (0, i))],
                out_specs=[pl.BlockSpec((SC_WINDOW, PACK_WORDS), index_map=lambda i: (i, 0))],
                core_axis_name=("core", "subcore"),
                dimension_semantics=(pltpu.PARALLEL,),
            )(i_hbm, o_hbm.at[h])

    return gather(*tables, idx.reshape(1, n))


def _ffn(x, wg, wu, wd):
    up = _nn(x, _bf(wu))
    gt = _nn(x, _bf(wg))
    return _nn(_bf(gt * _sigmoid(gt) * up), _bf(wd))


def _expert_kernel(be_ref, nu_ref, *refs):
    n_in = PACK_HALVES * EXPERT_SPLIT
    x_refs = refs[:n_in]
    wg_ref, wu_ref, wd_ref = refs[n_in:n_in + 3]
    y_refs = refs[n_in + 3:n_in + 3 + PACK_HALVES]
    wgb_ref, wub_ref, wdb_ref = refs[n_in + 3 + PACK_HALVES:]
    b = pl.program_id(0)

    @pl.when(jnp.logical_or(b == 0, be_ref[b] != be_ref[jnp.maximum(b - 1, 0)]))
    def _():
        wgb_ref[...] = _bf(wg_ref[0, 0])
        wub_ref[...] = _bf(wu_ref[0, 0])
        wdb_ref[...] = _bf(wd_ref[0, 0])

    @pl.when(b < nu_ref[0])
    def _():
        halves = [jnp.concatenate([x_refs[h * EXPERT_SPLIT + j][0] for j in range(EXPERT_SPLIT)], axis=0)
                  for h in range(PACK_HALVES)]
        y = _ffn(_bf(_unpack_rows(halves)), wgb_ref[...], wub_ref[...], wdb_ref[...])
        for h, words in enumerate(_pack_rows(y)):
            y_refs[h][...] = words


def _experts(layer, xs, blk_expert, n_used, w_gate, w_up, w_down):
    n_rows = xs.shape[1]
    n_blocks = n_rows // EXPERT_BLOCK
    part = EXPERT_BLOCK // EXPERT_SPLIT
    x_spec = lambda h, j: pl.BlockSpec((1, part, PACK_WORDS), lambda b, be, nu: (h, b * EXPERT_SPLIT + j, 0))
    w_spec = lambda shape: pl.BlockSpec((1, 1) + shape, lambda b, be, nu: (layer, be[b], 0, 0))
    y_spec = pl.BlockSpec((EXPERT_BLOCK, PACK_WORDS), lambda b, be, nu: (b, 0))
    n_in = PACK_HALVES * EXPERT_SPLIT
    return pl.pallas_call(
        _expert_kernel,
        grid_spec=pltpu.PrefetchScalarGridSpec(
            num_scalar_prefetch=2,
            grid=(n_blocks,),
            in_specs=[x_spec(h, j) for h in range(PACK_HALVES) for j in range(EXPERT_SPLIT)]
                     + [w_spec((D_MODEL, D_EXPERT)), w_spec((D_MODEL, D_EXPERT)), w_spec((D_EXPERT, D_MODEL))],
            out_specs=[y_spec] * PACK_HALVES,
            scratch_shapes=[pltpu.VMEM((D_MODEL, D_EXPERT), BF16), pltpu.VMEM((D_MODEL, D_EXPERT), BF16),
                            pltpu.VMEM((D_EXPERT, D_MODEL), BF16)],
        ),
        out_shape=[jax.ShapeDtypeStruct((n_rows, PACK_WORDS), xs.dtype)] * PACK_HALVES,
        compiler_params=_params("arbitrary"),
        name="moe_experts",
    )(blk_expert, n_used, *([xs] * n_in), w_gate, w_up, w_down)


def _shared_kernel(h2p_ref, sg_ref, su_ref, sd_ref, o_ref):
    h2 = _bf(_unpack_rows([h2p_ref[h] for h in range(PACK_HALVES)]))
    for h, words in enumerate(_pack_rows(_ffn(h2, sg_ref[0], su_ref[0], sd_ref[0]))):
        o_ref[h] = words


def _shared_expert(layer, h2p, ws_gate, ws_up, ws_down):
    tm = SHARED_TILE
    row_spec = pl.BlockSpec((PACK_HALVES, tm, PACK_WORDS), lambda i: (0, i, 0))
    return pl.pallas_call(
        _shared_kernel,
        grid=(TOKENS // tm,),
        in_specs=[row_spec,
                  pl.BlockSpec((1, D_MODEL, D_EXPERT), lambda i: (layer, 0, 0)),
                  pl.BlockSpec((1, D_MODEL, D_EXPERT), lambda i: (layer, 0, 0)),
                  pl.BlockSpec((1, D_EXPERT, D_MODEL), lambda i: (layer, 0, 0))],
        out_specs=row_spec,
        out_shape=jax.ShapeDtypeStruct(h2p.shape, h2p.dtype),
        compiler_params=_params("parallel"),
        name="moe_shared_expert",
    )(h2p, ws_gate, ws_up, ws_down)


def _combine_kernel(*refs, with_qkv):
    g_refs = refs[:TOP_K]
    gk_ref, sh_ref, x_ref, mod_ref, gpost_ref = refs[TOP_K:TOP_K + 5]
    acc = _unpack_rows([sh_ref[h] for h in range(PACK_HALVES)])
    gk = gk_ref[...].T
    for j in range(TOP_K):
        acc = acc + gk[:, j:j + 1] * _unpack_rows([g_refs[j][h] for h in range(PACK_HALVES)])
    gate_f = mod_ref[0][5:6]
    x_new = x_ref[...] + gate_f * _rms(acc, gpost_ref[...])
    if not with_qkv:
        refs[-1][...] = x_new
        return
    modn_ref, gpre_ref, w_ref, o_ref, q_ref, k_ref, v_ref = refs[TOP_K + 5:]
    o_ref[...] = x_new
    modn = modn_ref[0]
    h = _bf(_rms(x_new, gpre_ref[...]) * (1.0 + modn[1:2]) + modn[0:1])
    q_ref[...] = _bf(_nn(h, w_ref[:, :D_MODEL]) * (LOG2_E / 8.0))
    k_ref[...] = _bf(_nn(h, w_ref[:, D_MODEL:2 * D_MODEL]))
    v_ref[...] = _bf(_nn(h, w_ref[:, 2 * D_MODEL:]))


def _combine(picked, gk, shared, x1, mod, gpost, next_qkv):
    tm = ROW_TILE
    n_tiles = TOKENS // tm
    row_spec = pl.BlockSpec((tm, D_MODEL), lambda i: (i, 0))
    mod_spec = pl.BlockSpec((1, N_ADA, D_MODEL), lambda i: (i // (SEQ // tm), 0, 0))
    vec_spec = pl.BlockSpec((1, D_MODEL), lambda i: (0, 0))
    pick_spec = lambda j: pl.BlockSpec((PACK_HALVES, tm, PACK_WORDS), lambda i: (0, j * n_tiles + i, 0))
    operands = [*([picked] * TOP_K), gk, shared, x1, mod, gpost.reshape(1, D_MODEL)]
    in_specs = [pick_spec(j) for j in range(TOP_K)] + [
        pl.BlockSpec((TOP_K, tm), lambda i: (0, i)),
        pl.BlockSpec((PACK_HALVES, tm, PACK_WORDS), lambda i: (0, i, 0)),
        row_spec, mod_spec, vec_spec,
    ]
    out_specs = [row_spec]
    out_shape = [jax.ShapeDtypeStruct((TOKENS, D_MODEL), F32)]
    if next_qkv is not None:
        mod_next, gpre_next, w_qkv = next_qkv
        operands += [mod_next, gpre_next.reshape(1, D_MODEL), _bf(w_qkv)]
        in_specs += [mod_spec, vec_spec, pl.BlockSpec((D_MODEL, 3 * D_MODEL), lambda i: (0, 0))]
        out_specs += [row_spec] * 3
        out_shape += [jax.ShapeDtypeStruct((TOKENS, D_MODEL), BF16)] * 3
    out = pl.pallas_call(
        functools.partial(_combine_kernel, with_qkv=next_qkv is not None),
        grid=(n_tiles,),
        in_specs=in_specs,
        out_specs=out_specs,
        out_shape=out_shape,
        compiler_params=_params("parallel"),
        name="moe_combine",
    )(*operands)
    return out if next_qkv is not None else out[0]


def _moe(layer, x1, h2p, gk, ek, rk, counts, mod, gpost, w_gate, w_up, w_down, ws_gate, ws_up, ws_down,
         next_qkv):
    cnt = counts[:, 0].astype(jnp.int32)
    padded = (cnt + EXPERT_BLOCK - 1) // EXPERT_BLOCK * EXPERT_BLOCK
    eid = jnp.arange(N_EXPERTS, dtype=jnp.int32)
    pend = jnp.sum(jnp.where(eid[:, None] <= eid[None, :], padded[:, None], 0), axis=0)
    pstart = pend - padded
    dest = rk + jnp.sum(jnp.where(ek[None] == eid[:, None, None], pstart[:, None, None], 0), axis=0)
    n_blocks = (TOKENS * TOP_K + N_EXPERTS * (EXPERT_BLOCK - 1)) // EXPERT_BLOCK + 1
    blk_start = jnp.arange(n_blocks, dtype=jnp.int32) * EXPERT_BLOCK
    blk_expert = jnp.minimum(jnp.sum((blk_start[None, :] >= pend[:, None]).astype(jnp.int32), axis=0),
                             N_EXPERTS - 1)
    n_used = pend[-1:] // EXPERT_BLOCK

    xs = _sc_scatter_rows(h2p, dest, n_blocks * EXPERT_BLOCK)
    shared = _shared_expert(layer, h2p, ws_gate, ws_up, ws_down)
    ys = _experts(layer, xs, blk_expert, n_used, w_gate, w_up, w_down)
    picked = _sc_gather_rows(ys, dest.reshape(-1))
    return _combine(picked, gk, shared, x1, mod, gpost, next_qkv)


def _sb_attn_kernel(q_ref, k_ref, v_ref, o_ref):
    blk = ATT_BLOCK
    n_heads = LANES // HEAD_DIM
    qb = pl.program_id(2)
    q = q_ref[...]
    lane = lax.broadcasted_iota(jnp.int32, (blk, LANES), 1)
    head0 = lane < HEAD_DIM
    qh = [jnp.where(head0, q, jnp.zeros_like(q)), jnp.where(head0, jnp.zeros_like(q), q)]
    rowi = lax.broadcasted_iota(jnp.int32, (blk, blk), 0)
    coli = lax.broadcasted_iota(jnp.int32, (blk, blk), 1)
    later = (rowi > coli).astype(BF16)

    sub = ATT_SUB
    n_sub = blk // sub
    sub_row = lax.broadcasted_iota(jnp.int32, (sub, blk), 0)
    sub_col = lax.broadcasted_iota(jnp.int32, (sub, blk), 1)
    below = [sub_col < sub_row + r * sub for r in range(n_sub)]
    sub_head0 = lax.broadcasted_iota(jnp.int32, (sub, LANES), 1) < HEAD_DIM
    chains = [(hd, r) for hd in range(n_heads) for r in range(n_sub)]

    def chain(hd, r, ks, vs, get_carry, get_acc, mode, out):
        rows = slice(r * sub, (r + 1) * sub)
        u = _nt(qh[hd][rows], ks)
        yield
        neg_abs = lax.bitcast_convert_type(
            lax.bitcast_convert_type(u, jnp.uint32) | jnp.uint32(0x80000000), F32)
        lb = jnp.minimum(u, 0.0) - jnp.log2(1.0 + jnp.exp2(neg_abs))
        l1m = lb - u
        if mode == "diag":
            l1m = jnp.where(below[r], l1m, 0.0)
        l1b = _bf(l1m)
        yield
        carry = get_carry()
        tail = _nn(l1b, later) + jnp.concatenate([carry] * (blk // LANES), axis=1)
        out["carry"] = carry + jnp.sum(l1m, axis=-1, keepdims=True)
        yield
        w = jnp.exp2(lb + tail)
        if mode == "diag":
            w = jnp.where(below[r], w, 0.0)
        elif mode == "prev":
            w = jnp.where(qb > 0, w, 0.0)
        wb = _bf(w)
        yield
        out["acc"] = get_acc() + _nn(wb, vs)

    def load_kv(kb):
        start = pl.multiple_of(kb * blk, blk)
        return k_ref[pl.ds(start, blk), :], v_ref[pl.ds(start, blk), :]

    def run(gens):
        while gens:
            gens = [g for g in gens if next(g, True) is None]

    def block(kb, carry, acc):
        ks, vs = load_kv(kb)
        outs = [{} for _ in chains]
        run([chain(hd, r, ks, vs, lambda c=c: carry[c], lambda c=c: acc[c], "plain", outs[c])
             for c, (hd, r) in enumerate(chains)])
        return [o["carry"] for o in outs], [o["acc"] for o in outs]

    zeros = jnp.zeros((sub, LANES), F32)
    ks_d, vs_d = load_kv(qb)
    ks_p, vs_p = load_kv(jnp.maximum(qb - 1, 0))
    outs_d = [{} for _ in chains]
    outs_p = [{} for _ in chains]
    run([chain(hd, r, ks_d, vs_d, lambda: zeros, lambda: zeros, "diag", outs_d[c])
         for c, (hd, r) in enumerate(chains)]
        + [chain(hd, r, ks_p, vs_p, lambda c=c: outs_d[c]["carry"], lambda c=c: outs_d[c]["acc"],
                 "prev", outs_p[c]) for c, (hd, r) in enumerate(chains)])
    carry = [o["carry"] for o in outs_p]
    acc = [o["acc"] for o in outs_p]

    def cmax_of(cr):
        return jnp.max(functools.reduce(jnp.maximum, cr))

    def cond(st):
        kb, _, _, cmax = st
        return jnp.logical_and(kb >= 0, cmax > EXP2_UNDERFLOW)

    def body(st):
        kb, cr, ac, _ = st
        cr, ac = block(kb, list(cr), list(ac))
        return kb - 1, tuple(cr), tuple(ac), cmax_of(cr)

    _, _, acc, _ = lax.while_loop(cond, body, (qb - 2, tuple(carry), tuple(acc), cmax_of(carry)))
    for r in range(n_sub):
        rows = slice(r * sub, (r + 1) * sub)
        o_ref[rows, :] = jnp.where(sub_head0, acc[r], acc[n_sub + r])


def _sb_attn(q, k, v):
    blk = ATT_BLOCK
    n_blk = SEQ // blk
    q_spec = pl.BlockSpec((blk, LANES), lambda b, p, i: (b * n_blk + i, p))
    kv_spec = pl.BlockSpec((SEQ, LANES), lambda b, p, i: (b, p))
    return pl.pallas_call(
        _sb_attn_kernel,
        grid=(BATCH, N_PAIRS, n_blk),
        in_specs=[q_spec, kv_spec, kv_spec],
        out_specs=q_spec,
        out_shape=jax.ShapeDtypeStruct((TOKENS, D_MODEL), F32),
        compiler_params=_params("parallel", "parallel", "arbitrary"),
        name="sb_attention",
    )(q, k, v)


def kernel(x, c, ada_w, ada_b, norm_pre_mix, norm_post_mix, norm_pre_ffn, norm_post_ffn, rwkv_mu, rwkv_w_rkv, rwkv_w_w1, rwkv_w_w2, rwkv_w0, rwkv_w_a1, rwkv_w_a2, rwkv_a0, rwkv_w_g1, rwkv_w_g2, rwkv_k_k, rwkv_k_a, rwkv_r_k, rwkv_lnx_g, rwkv_lnx_b, rwkv_w_o, sb_w_qkv, sb_w_o, moe_w_router, moe_router_bias, moe_w_gate, moe_w_up, moe_w_down, moe_ws_gate, moe_ws_up, moe_ws_down):
    mod_all = _ada_mod(c, ada_w, ada_b)
    xt = x.reshape(TOKENS, D_MODEL)
    for layer in range(DEPTH):
        mod = mod_all[layer]
        i = layer // 2
        if layer % 2 == 0:
            r, k, v, lw, a, g = _rwkv_proj(
                xt, mod, norm_pre_mix[layer], rwkv_mu[i], rwkv_w_rkv[i], rwkv_w_w1[i], rwkv_w_w2[i],
                rwkv_w0[i], rwkv_w_a1[i], rwkv_w_a2[i], rwkv_a0[i], rwkv_w_g1[i], rwkv_w_g2[i])
            o = _wkv(r, k, v, lw, a, rwkv_k_k[i], rwkv_k_a[i], rwkv_r_k[i], rwkv_lnx_g[i], rwkv_lnx_b[i])
            w_o = rwkv_w_o[i]
        else:
            o, g = _sb_attn(*qkv), None
            w_o = sb_w_o[i]
        x1, h2p, gk, ek, rk, counts = _post(o, g, xt, mod, w_o, norm_post_mix[layer], norm_pre_ffn[layer],
                                            moe_w_router[layer], moe_router_bias[layer])
        nxt = layer + 1
        next_qkv = ((mod_all[nxt], norm_pre_mix[nxt], sb_w_qkv[nxt // 2])
                    if nxt < DEPTH and nxt % 2 == 1 else None)
        out = _moe(layer, x1, h2p, gk, ek, rk, counts, mod, norm_post_ffn[layer], moe_w_gate, moe_w_up,
                   moe_w_down, moe_ws_gate, moe_ws_up, moe_ws_down, next_qkv)
        xt, *qkv = out if next_qkv is not None else (out,)
    return xt.reshape(BATCH, SEQ, D_MODEL)
```
